```python
import math
import jax, jax.numpy as jnp
from jax import lax
import numpy as np

D_MODEL = 2048
BATCH = 16
SEQ = 256
DEPTH = 2
DEC_BATCH = 8
DEC_SEQ = 1024
PAST_LEN = 256

GRID_W = 64
EPS = 1e-6
HY_CH = 512
HY_ORDER = 2
HY_SHORT = 3
HY_BANDS = 16
HY_EMB = 1 + 2 * HY_BANDS
HY_FF = 64
HY_FAST_DECAY = 0.3
HY_SLOW_DECAY = 1.5
HY_TARGET = 1e-2
HY_W3_STD = 0.1 * HY_FF ** -0.5
HY_IN = 3 * HY_CH
GLA_HEADS = 4
GLA_DK = 64
GLA_DV = 128
GLA_LOWRANK = 16
GLA_TAU = 16.0
GLA_CHUNK = 64
GLA_QK = GLA_HEADS * GLA_DK
GLA_VW = GLA_HEADS * GLA_DV
GLA_IN = 2 * GLA_QK + 2 * GLA_VW + 2 * GLA_LOWRANK
MLA_HEADS = 8
MLA_Q_RANK = 384
MLA_KV_RANK = 256
MLA_NOPE = 128
MLA_ROPE = 64
MLA_V = 128
MLA_IN = MLA_Q_RANK + MLA_KV_RANK + MLA_ROPE
ROPE_THETA = 10000.0
Q_BLOCK = 128
D_IN = HY_IN + GLA_IN + MLA_IN
MIX_W = HY_CH + GLA_VW + MLA_HEADS * MLA_V
D_FF = -(-8 * D_MODEL // (3 * 256)) * 256

kernel_name = 'hybrid_hyena_gla_mla_prefix_dit_step'


def _split(t, sizes):
    pts = [int(p) for p in np.cumsum(sizes)[:-1]]
    return jnp.split(t, pts, axis=-1)


def rms_norm(x, g):
    xf = x.astype(jnp.float32)
    y = xf * lax.rsqrt(jnp.mean(xf * xf, axis=-1, keepdims=True) + EPS)
    return (y * g.astype(jnp.float32)).astype(x.dtype)


def short_conv(u, w, b):
    L = u.shape[1]
    p = HY_SHORT // 2
    up = jnp.pad(u, ((0, 0), (p, p), (0, 0)))
    return sum(up[:, i:i + L] * w[i] for i in range(HY_SHORT)) + b


def hyena_filters(L, w1, b1, freq, w2, b2, w3):
    t = jnp.linspace(0.0, 1.0, L, dtype=jnp.float32)[:, None]
    w = 2.0 * math.pi * jnp.arange(L, dtype=jnp.float32)[:, None] / L
    f = jnp.linspace(1e-4, HY_BANDS - 1, HY_BANDS, dtype=jnp.float32)
    z = jnp.concatenate([t, jnp.cos(f * w), -jnp.sin(f * w)], axis=-1)
    h = jnp.sin(freq[0] * (z @ w1 + b1))
    h = jnp.sin(freq[1] * (h @ w2 + b2))
    h = (h @ w3).astype(jnp.float32).reshape(L, 2, HY_ORDER, HY_CH)
    min_decay = math.log(HY_TARGET) / HY_SLOW_DECAY
    max_decay = math.log(HY_TARGET) / HY_FAST_DECAY
    delta = jnp.abs(jnp.linspace(min_decay, max_decay, HY_CH, dtype=jnp.float32))
    h = h * jnp.exp(-t[:, :, None, None] * delta)
    k_fwd, k_bwd = h[:, 0], h[:, 1]
    zero = jnp.zeros((1, HY_ORDER, HY_CH), jnp.float32)
    return jnp.concatenate([k_fwd, zero, k_bwd[1:][::-1]], axis=0)


def hyena_mixer(u, conv_w, conv_b, w1, b1, freq, w2, b2, w3, skip):
    L = u.shape[1]
    uc = short_conv(u, conv_w, conv_b).astype(jnp.float32)
    v, x1, x2 = jnp.split(uc, 3, axis=-1)
    k_f = jnp.fft.rfft(hyena_filters(L, w1, b1, freq, w2, b2, w3), n=2 * L, axis=0)
    z = v
    for o, gate in enumerate((x1, x2)):
        z_f = jnp.fft.rfft(z, n=2 * L, axis=1)
        conv = jnp.fft.irfft(z_f * k_f[:, o], n=2 * L, axis=1)[:, :L]
        z = gate * (conv + z * skip[o].astype(jnp.float32))
    return z.astype(u.dtype)


def gla_chunked(q, k, v, log_a, s0):
    B, H, L, dk = q.shape
    dv = v.shape[-1]
    n = L // GLA_CHUNK
    def chunks(t):
        return t.reshape(B, H, n, GLA_CHUNK, t.shape[-1])
    q, k, v, log_a = chunks(q), chunks(k), chunks(v), chunks(log_a)
    cum = jnp.cumsum(log_a, axis=3)
    total = cum[:, :, :, -1]
    q_dec = q * jnp.exp(cum)
    k_inv = k * jnp.exp(-cum)
    k_end = k * jnp.exp(total[:, :, :, None] - cum)
    mask = jnp.tril(jnp.ones((GLA_CHUNK, GLA_CHUNK), dtype=bool))
    att = jnp.where(mask, jnp.einsum('bhncd,bhnsd->bhncs', q_dec, k_inv), 0.0)
    o_intra = jnp.einsum('bhncs,bhnse->bhnce', att, v)
    kv_chunk = jnp.einsum('bhncd,bhnce->bhnde', k_end, v)
    def step(s, inp):
        qd, dec, kvc = inp
        o = jnp.einsum('bhcd,bhde->bhce', qd, s)
        return dec[..., None] * s + kvc, o
    xs = (jnp.moveaxis(q_dec, 2, 0), jnp.moveaxis(jnp.exp(total), 2, 0), jnp.moveaxis(kv_chunk, 2, 0))
    s_fin, o_inter = lax.scan(step, s0, xs)
    o = o_intra + jnp.moveaxis(o_inter, 0, 2)
    return o.reshape(B, H, L, dv), s_fin


def gla_mixer(u, s0f, s0b, wa_f, ba_f, wa_b, ba_b, norm_g):
    B, L, _ = u.shape
    q, k, v, g, af, ab = _split(u, (GLA_QK, GLA_QK, GLA_VW, GLA_VW, GLA_LOWRANK, GLA_LOWRANK))
    def heads(t, d):
        return t.reshape(B, L, GLA_HEADS, d).transpose(0, 2, 1, 3).astype(jnp.float32)
    q = heads(q, GLA_DK) * GLA_DK ** -0.5
    k = heads(k, GLA_DK)
    v = heads(v, GLA_DV)
    la_f = heads(jax.nn.log_sigmoid((af @ wa_f + ba_f).astype(jnp.float32)), GLA_DK) / GLA_TAU
    la_b = heads(jax.nn.log_sigmoid((ab @ wa_b + ba_b).astype(jnp.float32)), GLA_DK) / GLA_TAU
    o_f, s_f = gla_chunked(q, k, v, la_f, s0f.astype(jnp.float32))
    def flip(t):
        return t[:, :, ::-1]
    o_b, s_b = gla_chunked(flip(q), flip(k), flip(v), flip(la_b), s0b.astype(jnp.float32))
    o = rms_norm(o_f + flip(o_b), norm_g)
    o = o.transpose(0, 2, 1, 3).reshape(B, L, GLA_VW) * jax.nn.silu(g.astype(jnp.float32))
    return o.astype(u.dtype), s_f, s_b


def axial_rope(x):
    L = x.shape[1]
    rows = L // GRID_W
    row = jnp.repeat(jnp.arange(rows), GRID_W)
    col = jnp.tile(jnp.arange(GRID_W), rows)
    half = MLA_ROPE // 2
    inv = ROPE_THETA ** (-jnp.arange(0, half, 2, dtype=jnp.float32) / half)
    extra = (1,) * (x.ndim - 3)
    def rot(xa, pos):
        ang = (pos.astype(jnp.float32)[:, None] * inv).reshape((L,) + extra + (half // 2,))
        cos, sin = jnp.cos(ang), jnp.sin(ang)
        x1, x2 = jnp.split(xa.astype(jnp.float32), 2, axis=-1)
        return jnp.concatenate([x1 * cos - x2 * sin, x1 * sin + x2 * cos], axis=-1)
    return jnp.concatenate([rot(x[..., :half], row), rot(x[..., half:], col)], axis=-1).astype(x.dtype)


def mla_attend(q_nope, q_rope, k_nope, k_rope, v):
    B, Lq, H, _ = q_nope.shape
    nb = Lq // Q_BLOCK
    scale = (MLA_NOPE + MLA_ROPE) ** -0.5
    def blocks(t):
        return jnp.moveaxis(t.reshape((B, nb, Q_BLOCK) + t.shape[2:]), 1, 0)
    def one_block(qs):
        qn, qr = qs
        s = jnp.einsum('bqhd,bkhd->bhqk', qn, k_nope) + jnp.einsum('bqhd,bkd->bhqk', qr, k_rope)
        p = jax.nn.softmax(s.astype(jnp.float32) * scale, axis=-1).astype(v.dtype)
        return jnp.einsum('bhqk,bkhd->bqhd', p, v)
    out = lax.map(one_block, (blocks(q_nope), blocks(q_rope)))
    return jnp.moveaxis(out, 0, 1).reshape(B, Lq, H * MLA_V)


def mla_mixer(u, q_norm, w_uq, kv_norm, w_ukv, ctx=None):
    B, L, _ = u.shape
    cq, ckv, krope = _split(u, (MLA_Q_RANK, MLA_KV_RANK, MLA_ROPE))
    q = (rms_norm(cq, q_norm) @ w_uq).reshape(B, L, MLA_HEADS, MLA_NOPE + MLA_ROPE)
    q_nope, q_rope = q[..., :MLA_NOPE], q[..., MLA_NOPE:]
    ckv = rms_norm(ckv, kv_norm)
    if ctx is None:
        keys_ckv, keys_rope = ckv, krope
    else:
        ctx_ckv, ctx_krope = ctx
        q_rope = axial_rope(q_rope)
        keys_ckv = jnp.concatenate([ckv, ctx_ckv.astype(ckv.dtype)], axis=1)
        keys_rope = jnp.concatenate([axial_rope(krope), ctx_krope.astype(krope.dtype)], axis=1)
    Lk = keys_ckv.shape[1]
    kv = (keys_ckv @ w_ukv).reshape(B, Lk, MLA_HEADS, MLA_NOPE + MLA_V)
    k_nope, v = kv[..., :MLA_NOPE], kv[..., MLA_NOPE:]
    return mla_attend(q_nope, q_rope, k_nope, keys_rope, v), ckv, krope


def trunk_layer(x, mod, lp, ctx=None):
    B = x.shape[0]
    sh1, sc1, g1, sh2, sc2, g2 = [m[:, None] for m in jnp.split(mod, 6, axis=-1)]
    h = rms_norm(x, lp['ln_mix']) * (1.0 + sc1) + sh1
    u_hy, u_gla, u_mla = _split(h @ lp['w_in'], (HY_IN, GLA_IN, MLA_IN))
    y_hy = hyena_mixer(u_hy, lp['hy_conv_w'], lp['hy_conv_b'], lp['hy_filt_w1'], lp['hy_filt_b1'],
                       lp['hy_filt_freq'], lp['hy_filt_w2'], lp['hy_filt_b2'], lp['hy_filt_w3'], lp['hy_skip'])
    if ctx is None:
        s0 = jnp.zeros((B, GLA_HEADS, GLA_DK, GLA_DV), jnp.float32)
        s0f, s0b, mla_ctx = s0, s0, None
    else:
        ctx_ckv, ctx_krope, s0f, s0b = ctx
        mla_ctx = (ctx_ckv, ctx_krope)
    y_gla, s_f, s_b = gla_mixer(u_gla, s0f, s0b, lp['gla_wa_f'], lp['gla_ba_f'], lp['gla_wa_b'],
                                lp['gla_ba_b'], lp['gla_norm'])
    y_mla, ckv, krope = mla_mixer(u_mla, lp['mla_q_norm'], lp['mla_w_uq'], lp['mla_kv_norm'],
                                  lp['mla_w_ukv'], mla_ctx)
    y = jnp.concatenate([y_hy, y_gla.astype(y_hy.dtype), y_mla.astype(y_hy.dtype)], axis=-1) @ lp['w_out']
    x = x + g1 * y
    h = rms_norm(x, lp['ln_ffn']) * (1.0 + sc2) + sh2
    gate, up = jnp.split(h @ lp['w_ffn_in'], 2, axis=-1)
    x = x + g2 * ((jax.nn.silu(gate) * up) @ lp['w_ffn_out'])
    return x, ((ckv, krope, s_f, s_b) if ctx is None else None)


def setup_inputs(seed: int = 0) -> dict:
    key = jax.random.key(seed)
    ks = iter(jax.random.split(key, 48))
    def nrm(shape, s):
        return jax.random.normal(next(ks), shape, jnp.float32) * s
    def gain(shape):
        return 1.0 + nrm(shape, 0.02)
    return {
        'x_prompt': nrm((BATCH, SEQ, D_MODEL), 1.0),
        'x_sample': nrm((DEC_BATCH, DEC_SEQ, D_MODEL), 1.0),
        'cache_mla_ckv': nrm((DEC_BATCH, DEPTH, PAST_LEN, MLA_KV_RANK), 1.0),
        'cache_mla_krope': nrm((DEC_BATCH, DEPTH, PAST_LEN, MLA_ROPE), 1.0),
        'state_gla_fwd': nrm((DEC_BATCH, DEPTH, GLA_HEADS, GLA_DK, GLA_DV), 2.0),
        'state_gla_bwd': nrm((DEC_BATCH, DEPTH, GLA_HEADS, GLA_DK, GLA_DV), 2.0),
        'c': nrm((DEC_BATCH, D_MODEL), 1.0),
        'c_ctx': nrm((D_MODEL,), 1.0),
        'w_mod': nrm((DEPTH, D_MODEL, 6 * D_MODEL), 0.5 * D_MODEL ** -0.5),
        'b_mod': nrm((DEPTH, 6 * D_MODEL), 0.02),
        'ln_mix': gain((DEPTH, D_MODEL)),
        'w_in': nrm((DEPTH, D_MODEL, D_IN), D_MODEL ** -0.5),
        'hy_conv_w': nrm((DEPTH, HY_SHORT, HY_IN), HY_SHORT ** -0.5),
        'hy_conv_b': nrm((DEPTH, HY_IN), 0.02),
        'hy_filt_w1': nrm((DEPTH, HY_EMB, HY_FF), HY_EMB ** -0.5),
        'hy_filt_b1': nrm((DEPTH, HY_FF), 0.1),
        'hy_filt_freq': gain((DEPTH, 2, HY_FF)),
        'hy_filt_w2': nrm((DEPTH, HY_FF, HY_FF), HY_FF ** -0.5),
        'hy_filt_b2': nrm((DEPTH, HY_FF), 0.1),
        'hy_filt_w3': nrm((DEPTH, HY_FF, 2 * HY_ORDER * HY_CH), HY_W3_STD),
        'hy_skip': nrm((DEPTH, HY_ORDER, HY_CH), 0.5),
        'gla_wa_f': nrm((DEPTH, GLA_LOWRANK, GLA_QK), GLA_LOWRANK ** -0.5),
        'gla_ba_f': nrm((DEPTH, GLA_QK), 0.02),
        'gla_wa_b': nrm((DEPTH, GLA_LOWRANK, GLA_QK), GLA_LOWRANK ** -0.5),
        'gla_ba_b': nrm((DEPTH, GLA_QK), 0.02),
        'gla_norm': gain((DEPTH, GLA_DV)),
        'mla_q_norm': gain((DEPTH, MLA_Q_RANK)),
        'mla_w_uq': nrm((DEPTH, MLA_Q_RANK, MLA_HEADS * (MLA_NOPE + MLA_ROPE)), MLA_Q_RANK ** -0.5),
        'mla_kv_norm': gain((DEPTH, MLA_KV_RANK)),
        'mla_w_ukv': nrm((DEPTH, MLA_KV_RANK, MLA_HEADS * (MLA_NOPE + MLA_V)), MLA_KV_RANK ** -0.5),
        'w_out': nrm((DEPTH, MIX_W, D_MODEL), MIX_W ** -0.5),
        'ln_ffn': gain((DEPTH, D_MODEL)),
        'w_ffn_in': nrm((DEPTH, D_MODEL, 2 * D_FF), D_MODEL ** -0.5),
        'w_ffn_out': nrm((DEPTH, D_FF, D_MODEL), D_FF ** -0.5),
        'ln_final': gain((D_MODEL,)),
    }


def reference(x_prompt, x_sample, cache_mla_ckv, cache_mla_krope, state_gla_fwd, state_gla_bwd, c, c_ctx,
              w_mod, b_mod, ln_mix, w_in, hy_conv_w, hy_conv_b, hy_filt_w1, hy_filt_b1, hy_filt_freq,
              hy_filt_w2, hy_filt_b2, hy_filt_w3, hy_skip, gla_wa_f, gla_ba_f, gla_wa_b, gla_ba_b, gla_norm,
              mla_q_norm, mla_w_uq, mla_kv_norm, mla_w_ukv, w_out, ln_ffn, w_ffn_in, w_ffn_out, ln_final):
    x_ctx, x_lat = x_prompt, x_sample
    ckv_l, krope_l, sf_l, sb_l = [], [], [], []
    for l in range(DEPTH):
        lp = {
            'ln_mix': ln_mix[l], 'w_in': w_in[l],
            'hy_conv_w': hy_conv_w[l], 'hy_conv_b': hy_conv_b[l],
            'hy_filt_w1': hy_filt_w1[l], 'hy_filt_b1': hy_filt_b1[l], 'hy_filt_freq': hy_filt_freq[l],
            'hy_filt_w2': hy_filt_w2[l], 'hy_filt_b2': hy_filt_b2[l], 'hy_filt_w3': hy_filt_w3[l],
            'hy_skip': hy_skip[l],
            'gla_wa_f': gla_wa_f[l], 'gla_ba_f': gla_ba_f[l], 'gla_wa_b': gla_wa_b[l], 'gla_ba_b': gla_ba_b[l],
            'gla_norm': gla_norm[l],
            'mla_q_norm': mla_q_norm[l], 'mla_w_uq': mla_w_uq[l], 'mla_kv_norm': mla_kv_norm[l],
            'mla_w_ukv': mla_w_ukv[l],
            'w_out': w_out[l], 'ln_ffn': ln_ffn[l], 'w_ffn_in': w_ffn_in[l], 'w_ffn_out': w_ffn_out[l],
        }
        mod_ctx = jax.nn.silu(c_ctx)[None] @ w_mod[l] + b_mod[l]
        mod_lat = jax.nn.silu(c) @ w_mod[l] + b_mod[l]
        x_ctx, (ckv, krope, s_f, s_b) = trunk_layer(x_ctx, mod_ctx, lp)
        ckv_l.append(ckv)
        krope_l.append(krope)
        sf_l.append(s_f)
        sb_l.append(s_b)
        ctx = (cache_mla_ckv[:, l], cache_mla_krope[:, l], state_gla_fwd[:, l], state_gla_bwd[:, l])
        x_lat, _ = trunk_layer(x_lat, mod_lat, lp, ctx)
    y_prompt = rms_norm(x_ctx, ln_final)
    y_sample = rms_norm(x_lat, ln_final)
    new_mla_ckv = jnp.stack(ckv_l, axis=1)
    new_mla_krope = jnp.stack(krope_l, axis=1)
    new_gla_fwd = jnp.stack(sf_l, axis=1)
    new_gla_bwd = jnp.stack(sb_l, axis=1)
    return (y_prompt, y_sample, new_mla_ckv, new_mla_krope, new_gla_fwd, new_gla_bwd)
```

```python
import functools
import math

import numpy as np
import jax
import jax.numpy as jnp
from jax import lax
from jax.experimental import pallas as pl
from jax.experimental.pallas import tpu as pltpu

F32 = jnp.float32
BF16 = jnp.bfloat16

D_MODEL = 2048
DEPTH = 2
GRID_W = 64
EPS = 1e-6
HY_CH = 512
HY_ORDER = 2
HY_BANDS = 16
HY_EMB = 1 + 2 * HY_BANDS
HY_FF = 64
HY_FAST_DECAY = 0.3
HY_SLOW_DECAY = 1.5
HY_TARGET = 1e-2
HY_IN = 3 * HY_CH
GLA_HEADS = 4
GLA_DK = 64
GLA_DV = 128
GLA_LOWRANK = 16
GLA_TAU = 16.0
GLA_CHUNK = 64
GLA_QK = GLA_HEADS * GLA_DK
GLA_VW = GLA_HEADS * GLA_DV
MLA_HEADS = 8
MLA_Q_RANK = 384
MLA_KV_RANK = 256
MLA_NOPE = 128
MLA_ROPE = 64
MLA_V = 128
ROPE_THETA = 10000.0
D_FF = -(-8 * D_MODEL // (3 * 256)) * 256

U_HY = 0
U_GLA = 1536
U_MLA = 3072
U_GLA_A = 3840
U_W = 4096
MOD_ROWS = 16

VMEM_LIMIT_V7X = 48 * 1024 * 1024


def _cparams(sem):
    return pltpu.CompilerParams(dimension_semantics=sem, vmem_limit_bytes=VMEM_LIMIT_V7X)


def _dot(a, b):
    return jnp.dot(a, b, preferred_element_type=F32)


def _dot_nt(a, b):
    return lax.dot_general(a, b, (((1,), (1,)), ((), ())), preferred_element_type=F32)


def _split(x):
    hi = x.astype(BF16)
    lo = (x - hi.astype(F32)).astype(BF16)
    return hi, lo


def _dot3(a, b):
    a_hi, a_lo = _split(a)
    b_hi, b_lo = _split(b)
    return _dot(a_hi, b_hi) + (_dot(a_lo, b_hi) + _dot(a_hi, b_lo))


def _rms(x, g):
    ms = jnp.mean(x * x, axis=-1, keepdims=True)
    return x * lax.rsqrt(ms + EPS) * g


def _mod_kernel(c_ref, w_ref, b_ref, o_ref):
    c = c_ref[...]
    s = (c * jax.nn.sigmoid(c)).astype(BF16)
    o_ref[...] = _dot(s, w_ref[...].astype(BF16)) + b_ref[...]


def _modulation(c_all, w_mod, b_mod):
    tn = 1024
    n6 = 6 * D_MODEL
    return pl.pallas_call(
        _mod_kernel,
        grid=(DEPTH, n6 // tn),
        in_specs=[pl.BlockSpec((MOD_ROWS, D_MODEL), lambda l, j: (0, 0)),
                  pl.BlockSpec((None, D_MODEL, tn), lambda l, j: (l, 0, j)),
                  pl.BlockSpec((None, 1, tn), lambda l, j: (l, 0, j))],
        out_specs=pl.BlockSpec((None, MOD_ROWS, tn), lambda l, j: (l, 0, j)),
        out_shape=jax.ShapeDtypeStruct((DEPTH, MOD_ROWS, n6), F32),
        compiler_params=_cparams(("arbitrary", "arbitrary")),
        name="modulation",
    )(c_all, w_mod, b_mod.reshape(DEPTH, 1, n6))


def _mod_spec(chunk, row_of_tile):
    return pl.BlockSpec((None, None, 1, D_MODEL), lambda i, *_: (row_of_tile(i), chunk, 0, 0))


def _inproj_kernel(x_ref, ln_ref, sh_ref, sc_ref, w_ref, o_ref, h_ref):
    @pl.when(pl.program_id(1) == 0)
    def _():
        y = _rms(x_ref[...], ln_ref[...])
        h_ref[...] = (y * (1.0 + sc_ref[...]) + sh_ref[...]).astype(BF16)

    o_ref[...] = _dot(h_ref[...], w_ref[...])


def _in_projection(x2d, ln, mod, w_in_p, row_of_tile, tm):
    rows = x2d.shape[0]
    tn = 1024
    return pl.pallas_call(
        _inproj_kernel,
        grid=(rows // tm, U_W // tn),
        in_specs=[pl.BlockSpec((tm, D_MODEL), lambda i, j: (i, 0)),
                  pl.BlockSpec((1, D_MODEL), lambda i, j: (0, 0)),
                  _mod_spec(0, row_of_tile),
                  _mod_spec(1, row_of_tile),
                  pl.BlockSpec((D_MODEL, tn), lambda i, j: (0, j))],
        out_specs=pl.BlockSpec((tm, tn), lambda i, j: (i, j)),
        out_shape=jax.ShapeDtypeStruct((rows, U_W), F32),
        scratch_shapes=[pltpu.VMEM((tm, D_MODEL), BF16)],
        compiler_params=_cparams(("arbitrary", "arbitrary")),
        name="in_projection",
    )(x2d, ln, mod, mod, w_in_p)


def _filter_kernel(z_ref, w1_ref, b1_ref, fr_ref, w2_ref, b2_ref, w3f_ref, w3b_ref, dec_ref,
                   chi_ref, clo_ref, shi_ref, slo_ref, kr_ref, ki_ref, kn_ref, *, L):
    h = jnp.sin(fr_ref[0:1, :] * (_dot3(z_ref[...], w1_ref[...]) + b1_ref[...]))
    h = jnp.sin(fr_ref[1:2, :] * (_dot3(h, w2_ref[...]) + b2_ref[...]))
    dec = dec_ref[...]
    row = lax.broadcasted_iota(jnp.int32, (L, 1), 0)
    kf = _dot3(h, w3f_ref[...]) * dec
    kb = jnp.where(row == 0, 0.0, _dot3(h, w3b_ref[...]) * dec)
    p_hi, p_lo = _split(kf + kb)
    m_hi, m_lo = _split(kf - kb)
    chi, clo = chi_ref[...], clo_ref[...]
    shi, slo = shi_ref[...], slo_ref[...]
    kc = _dot(chi, p_hi) + (_dot(clo, p_hi) + _dot(chi, p_lo))
    ks = _dot(shi, m_hi) + (_dot(slo, m_hi) + _dot(shi, m_lo))
    sign = jnp.where((row & 1) == 0, 1.0, -1.0)
    kn = jnp.sum((kf + kb) * sign, axis=0, keepdims=True)
    kr_ref[...] = kc * jnp.where(row == 0, 0.5 / L, 1.0 / L)
    ki_ref[...] = ks * (-1.0 / L)
    kn_ref[...] = kn * (0.5 / L)


def _hyena_filters(L, consts, w1p, b1, freq, w2, b2, w3):
    zpos, decay, c_hi, c_lo, s_hi, s_lo = consts
    full = lambda shape: pl.BlockSpec(shape, lambda o: (0,) * len(shape))
    return pl.pallas_call(
        functools.partial(_filter_kernel, L=L),
        grid=(HY_ORDER,),
        in_specs=[full((L, 64)), full((64, HY_FF)), full((1, HY_FF)), full((2, HY_FF)),
                  full((HY_FF, HY_FF)), full((1, HY_FF)),
                  pl.BlockSpec((HY_FF, HY_CH), lambda o: (0, o)),
                  pl.BlockSpec((HY_FF, HY_CH), lambda o: (0, HY_ORDER + o)),
                  full((L, HY_CH)), full((L, L)), full((L, L)), full((L, L)), full((L, L))],
        out_specs=[pl.BlockSpec((None, L, HY_CH), lambda o: (o, 0, 0)),
                   pl.BlockSpec((None, L, HY_CH), lambda o: (o, 0, 0)),
                   pl.BlockSpec((None, 1, HY_CH), lambda o: (o, 0, 0))],
        out_shape=[jax.ShapeDtypeStruct((HY_ORDER, L, HY_CH), F32),
                   jax.ShapeDtypeStruct((HY_ORDER, L, HY_CH), F32),
                   jax.ShapeDtypeStruct((HY_ORDER, 1, HY_CH), F32)],
        compiler_params=_cparams(("arbitrary",)),
        name="hyena_filters",
    )(zpos, w1p, b1, freq, w2, b2, w3, w3, decay, c_hi, c_lo, s_hi, s_lo)


HY_CG = 256


def _hyena_kernel(v_ref, x1_ref, x2_ref, wv_ref, w1_ref, w2_ref, bv_ref, b1_ref, b2_ref, skip_ref,
                  kr_ref, ki_ref, kn_ref, c_ref, sp_ref, spt_ref, o_ref, *, L):
    row = lax.broadcasted_iota(jnp.int32, (L, 1), 0)
    first = row == 0
    last = row == L - 1

    def short_conv(u_ref, w_ref, b_ref):
        u = u_ref[...]
        prev = jnp.where(first, 0.0, pltpu.roll(u, 1, axis=0))
        nxt = jnp.where(last, 0.0, pltpu.roll(u, L - 1, axis=0))
        return prev * w_ref[0:1, :] + u * w_ref[1:2, :] + nxt * w_ref[2:3, :] + b_ref[...]

    z = short_conv(v_ref, wv_ref, bv_ref)
    gates = (short_conv(x1_ref, w1_ref, b1_ref), short_conv(x2_ref, w2_ref, b2_ref))
    c = c_ref[...]
    sp = sp_ref[...]
    spt = spt_ref[...]
    for o in range(HY_ORDER):
        zb = z.astype(BF16)
        a = _dot(c, zb)
        s = _dot(sp, zb)
        kr = kr_ref[o]
        ki = ki_ref[o]
        qt = a * kr + s * ki
        qb = s * jnp.where(first, kn_ref[o], kr) - a * ki
        conv = _dot(c, qt.astype(BF16)) + _dot(spt, qb.astype(BF16))
        z = gates[o] * (conv + z * skip_ref[o:o + 1, :])
    o_ref[...] = z.astype(BF16)


def _hyena_mixer(u3, conv_w, conv_b, skip, filt, dft, L):
    B = u3.shape[0]
    kr, ki, kn = filt
    c_b, sp_b, spt_b = dft
    ng = HY_CH // HY_CG
    ublk = lambda part: pl.BlockSpec((None, L, HY_CG), lambda g, b: (b, 0, part * ng + g))
    wblk = lambda part: pl.BlockSpec((3, HY_CG), lambda g, b: (0, part * ng + g))
    bblk = lambda part: pl.BlockSpec((1, HY_CG), lambda g, b: (0, part * ng + g))
    fblk = lambda rows: pl.BlockSpec((HY_ORDER, rows, HY_CG), lambda g, b: (0, 0, g))
    mat = pl.BlockSpec((L, L), lambda g, b: (0, 0))
    return pl.pallas_call(
        functools.partial(_hyena_kernel, L=L),
        grid=(ng, B),
        in_specs=[ublk(0), ublk(1), ublk(2), wblk(0), wblk(1), wblk(2), bblk(0), bblk(1), bblk(2),
                  pl.BlockSpec((HY_ORDER, HY_CG), lambda g, b: (0, g)),
                  fblk(L), fblk(L), fblk(1), mat, mat, mat],
        out_specs=pl.BlockSpec((None, L, HY_CG), lambda g, b: (b, 0, g)),
        out_shape=jax.ShapeDtypeStruct((B, L, HY_CH), BF16),
        compiler_params=_cparams(("arbitrary", "arbitrary")),
        name="hyena_mixer",
    )(u3, u3, u3, conv_w, conv_w, conv_w, conv_b, conv_b, conv_b, skip, kr, ki, kn, c_b, sp_b, spt_b)


GLA_PAIR = 2 * GLA_CHUNK
GLA_STATES = 2 * GLA_HEADS


def _gla_kernel(*refs, L, has_state, emit_state):
    it = iter(refs)
    qkvg_ref, a_ref, wa_ref, ba_ref, ng_ref = next(it), next(it), next(it), next(it), next(it)
    st0_ref = next(it) if has_state else None
    y_ref = next(it)
    stout_ref = next(it) if emit_state else None
    qd_s, ki_s, ke_s, ec_s, vb_s, vt_s, o_s, st_s = (next(it) for _ in range(8))

    npair = L // GLA_PAIR
    rc = lax.broadcasted_iota(jnp.int32, (L, 1), 0) & (GLA_CHUNK - 1)
    steps = (1, 2, 4, 8, 16, 32)

    def prefix(x):
        for d in steps:
            x = x + jnp.where(rc >= d, pltpu.roll(x, d, axis=0), 0.0)
        return x

    def suffix(x):
        for d in steps:
            x = x + jnp.where(rc < GLA_CHUNK - d, pltpu.roll(x, L - d, axis=0), 0.0)
        return x

    pre = _dot3(a_ref[...], wa_ref[...]) + ba_ref[...]
    la = (jnp.minimum(pre, 0.0) - jnp.log(1.0 + jnp.exp(-jnp.abs(pre)))) * (1.0 / GLA_TAU)
    q = qkvg_ref[:, 0:GLA_QK] * (GLA_DK ** -0.5)
    k = qkvg_ref[:, GLA_QK:2 * GLA_QK]
    for dirn in range(2):
        la_d = la[:, dirn * GLA_QK:(dirn + 1) * GLA_QK]
        if dirn == 0:
            cum = prefix(la_d)
            rest = suffix(la_d) - la_d
        else:
            cum = suffix(la_d)
            rest = prefix(la_d) - la_d
        ec = jnp.exp(cum)
        ec_s[dirn] = ec
        qd_s[dirn] = (q * ec).astype(BF16)
        ki_s[dirn] = (k * jnp.exp(-cum)).astype(BF16)
        ke_s[dirn] = (k * jnp.exp(rest)).astype(BF16)
    vb_s[...] = qkvg_ref[:, 2 * GLA_QK:2 * GLA_QK + GLA_VW].astype(BF16)

    def transpose_v(n2, carry):
        r0 = pl.multiple_of(n2 * GLA_PAIR, GLA_PAIR)
        for h in range(GLA_HEADS):
            vp = qkvg_ref[pl.ds(r0, GLA_PAIR), 2 * GLA_QK + h * GLA_DV:2 * GLA_QK + (h + 1) * GLA_DV]
            vt_s[n2, h] = vp.T.astype(BF16)
        return carry

    lax.fori_loop(0, npair, transpose_v, 0)

    if has_state:
        st_s[...] = st0_ref[...]
    else:
        st_s[...] = jnp.zeros_like(st_s)

    lane = lax.broadcasted_iota(jnp.int32, (1, 2 * GLA_DK), 1)
    head_lanes = (lane < GLA_DK, lane >= GLA_DK)
    ci = lax.broadcasted_iota(jnp.int32, (GLA_CHUNK, GLA_CHUNK), 0)
    si = lax.broadcasted_iota(jnp.int32, (GLA_CHUNK, GLA_CHUNK), 1)
    tri = (si <= ci, si >= ci)

    def pair_step(n, carry):
        for dirn in range(2):
            n2 = n if dirn == 0 else npair - 1 - n
            r0 = pl.multiple_of(n2 * GLA_PAIR, GLA_PAIR)
            for j in ((0, 1) if dirn == 0 else (1, 0)):
                rows = pl.ds(r0 + j * GLA_CHUNK, GLA_CHUNK)
                dec_grp = pl.multiple_of(r0 + j * GLA_CHUNK + (GLA_CHUNK - 8 if dirn == 0 else 0), 8)
                dec_sub = 7 if dirn == 0 else 0
                for p in range(GLA_HEADS // 2):
                    lanes = slice(p * 2 * GLA_DK, (p + 1) * 2 * GLA_DK)
                    qd = qd_s[dirn, rows, lanes]
                    ki = ki_s[dirn, rows, lanes]
                    ke = ke_s[dirn, pl.ds(r0, GLA_PAIR), lanes]
                    dec = ec_s[dirn, pl.ds(dec_grp, 8), lanes][dec_sub:dec_sub + 1, :]
                    for hh in range(2):
                        h = 2 * p + hh
                        qm = jnp.where(head_lanes[hh], qd, jnp.zeros_like(qd))
                        att = jnp.where(tri[dirn], _dot_nt(qm, ki), 0.0)
                        vh = vb_s[rows, h * GLA_DV:(h + 1) * GLA_DV]
                        st = st_s[dirn * GLA_HEADS + h]
                        o = _dot(att.astype(BF16), vh) + _dot_nt(qm, st.astype(BF16))
                        vt = vt_s[n2, h]
                        vt = jnp.where(head_lanes[j], vt, jnp.zeros_like(vt))
                        st_s[dirn * GLA_HEADS + h] = st * dec + _dot(vt, ke)
                        o_s[dirn, rows, h * GLA_DV:(h + 1) * GLA_DV] = o
        return carry

    lax.fori_loop(0, npair, pair_step, 0)

    for h in range(GLA_HEADS):
        cols = slice(h * GLA_DV, (h + 1) * GLA_DV)
        o = _rms(o_s[0, :, cols] + o_s[1, :, cols], ng_ref[...])
        g = qkvg_ref[:, 2 * GLA_QK + GLA_VW + h * GLA_DV:2 * GLA_QK + GLA_VW + (h + 1) * GLA_DV]
        y_ref[:, cols] = (o * (g * jax.nn.sigmoid(g))).astype(BF16)
    if emit_state:
        stout_ref[...] = st_s[...]


def _gla_mixer(u3, wa_p, ba_p, norm_g, st0, L, emit_state):
    B = u3.shape[0]
    has_state = st0 is not None
    st_spec = pl.BlockSpec((None, GLA_STATES, GLA_DV, 2 * GLA_DK), lambda b: (b, 0, 0, 0))
    in_specs = [pl.BlockSpec((None, L, 1536), lambda b: (b, 0, U_GLA // 1536)),
                pl.BlockSpec((None, L, 256), lambda b: (b, 0, U_GLA_A // 256)),
                pl.BlockSpec((256, 2 * GLA_QK), lambda b: (0, 0)),
                pl.BlockSpec((1, 2 * GLA_QK), lambda b: (0, 0)),
                pl.BlockSpec((1, GLA_DV), lambda b: (0, 0))]
    args = [u3, u3, wa_p, ba_p, norm_g]
    if has_state:
        in_specs.append(st_spec)
        args.append(st0)
    out_specs = [pl.BlockSpec((None, L, GLA_VW), lambda b: (b, 0, 0))]
    out_shape = [jax.ShapeDtypeStruct((B, L, GLA_VW), BF16)]
    if emit_state:
        out_specs.append(st_spec)
        out_shape.append(jax.ShapeDtypeStruct((B, GLA_STATES, GLA_DV, 2 * GLA_DK), F32))
    npair = L // GLA_PAIR
    res = pl.pallas_call(
        functools.partial(_gla_kernel, L=L, has_state=has_state, emit_state=emit_state),
        grid=(B,),
        in_specs=in_specs,
        out_specs=out_specs,
        out_shape=out_shape,
        scratch_shapes=[pltpu.VMEM((2, L, GLA_QK), BF16),
                        pltpu.VMEM((2, L, GLA_QK), BF16),
                        pltpu.VMEM((2, L, GLA_QK), BF16),
                        pltpu.VMEM((2, L, GLA_QK), F32),
                        pltpu.VMEM((L, GLA_VW), BF16),
                        pltpu.VMEM((npair, GLA_HEADS, GLA_DV, GLA_PAIR), BF16),
                        pltpu.VMEM((2, L, GLA_VW), F32),
                        pltpu.VMEM((GLA_STATES, GLA_DV, 2 * GLA_DK), F32)],
        compiler_params=_cparams(("arbitrary",)),
        name="gla_mixer",
    )(*args)
    return (res[0], res[1]) if emit_state else (res[0], None)


MLA_HW = 256
MLA_TQ = 256
MLA_KCH = 256


def _rope(x, cos, sin):
    lane = lax.broadcasted_iota(jnp.int32, x.shape, 1)
    partner = jnp.where((lane & 16) == 0, pltpu.roll(x, 112, axis=1), pltpu.roll(x, 16, axis=1))
    return x * cos + partner * sin


def _mla_kernel(*refs, L, Lk, rope, emit):
    it = iter(refs)
    u_ref, qn_ref, wq_ref, kvn_ref, wkv_ref = (next(it) for _ in range(5))
    cos_ref, sin_ref, cckv_ref, ckr_ref = (next(it) for _ in range(4)) if rope else (None,) * 4
    y_ref = next(it)
    ckv_out, kr_out = (next(it), next(it)) if emit else (None, None)
    kf_s, v_s = next(it), next(it)

    qi = pl.program_id(1)

    def project(keys, rope_part, r0):
        kv = _dot(keys, wkv_ref[...])
        for h in range(MLA_HEADS):
            kf_s[h, r0:r0 + MLA_KCH, 0:MLA_NOPE] = kv[:, h * 256:h * 256 + MLA_NOPE].astype(BF16)
            kf_s[h, r0:r0 + MLA_KCH, MLA_NOPE:MLA_HW] = rope_part
            v_s[h, r0:r0 + MLA_KCH, :] = kv[:, h * 256 + MLA_NOPE:(h + 1) * 256].astype(BF16)

    @pl.when(qi == 0)
    def _():
        for r0 in range(0, L, MLA_KCH):
            ckvn = _rms(u_ref[r0:r0 + MLA_KCH, MLA_Q_RANK:MLA_Q_RANK + MLA_KV_RANK], kvn_ref[...])
            kr = u_ref[r0:r0 + MLA_KCH, 640:768]
            if emit:
                ckv_out[r0:r0 + MLA_KCH, :] = ckvn
                kr_out[r0:r0 + MLA_KCH, :] = kr[:, 0:MLA_ROPE]
            if rope:
                kr = _rope(kr, cos_ref[r0:r0 + MLA_KCH, :], sin_ref[r0:r0 + MLA_KCH, :])
            project(ckvn.astype(BF16), kr.astype(BF16), r0)
        if rope:
            for r0 in range(0, Lk - L, MLA_KCH):
                project(cckv_ref[r0:r0 + MLA_KCH, :].astype(BF16),
                        ckr_ref[r0:r0 + MLA_KCH, :].astype(BF16), L + r0)

    q0 = pl.multiple_of(qi * MLA_TQ, MLA_TQ)
    cqn = _rms(u_ref[pl.ds(q0, MLA_TQ), 0:MLA_Q_RANK], qn_ref[...]).astype(BF16)
    q = _dot(cqn, wq_ref[...]) * ((MLA_NOPE + MLA_ROPE) ** -0.5)
    if rope:
        cos = cos_ref[pl.ds(q0, MLA_TQ), :]
        sin = sin_ref[pl.ds(q0, MLA_TQ), :]
    for h in range(MLA_HEADS):
        q_nope = q[:, h * MLA_HW:h * MLA_HW + MLA_NOPE]
        q_rope = q[:, h * MLA_HW + MLA_NOPE:(h + 1) * MLA_HW]
        if rope:
            q_rope = _rope(q_rope, cos, sin)
        qh = jnp.concatenate([q_nope, q_rope], axis=1).astype(BF16)
        s = _dot_nt(qh, kf_s[h])
        p = jnp.exp(s - jnp.max(s, axis=-1, keepdims=True))
        denom = jnp.sum(p, axis=-1, keepdims=True)
        o = _dot(p.astype(BF16), v_s[h])
        y_ref[:, h * MLA_V:(h + 1) * MLA_V] = (o / denom).astype(BF16)


def _mla_mixer(u3, q_norm, wq_p, kv_norm, wkv, rope_args, L, emit):
    B = u3.shape[0]
    rope = rope_args is not None
    Lk = L + (rope_args[2].shape[1] if rope else 0)
    const = lambda shape: pl.BlockSpec(shape, lambda b, i: (0,) * len(shape))
    in_specs = [pl.BlockSpec((None, L, 768), lambda b, i: (b, 0, U_MLA // 768)),
                const((1, MLA_Q_RANK)), const((MLA_Q_RANK, MLA_HEADS * MLA_HW)),
                const((1, MLA_KV_RANK)), const((MLA_KV_RANK, MLA_HEADS * 256))]
    args = [u3, q_norm, wq_p, kv_norm, wkv]
    if rope:
        cos, sin, cckv, ckr = rope_args
        in_specs += [const((L, 128)), const((L, 128)),
                     pl.BlockSpec((None, Lk - L, MLA_KV_RANK), lambda b, i: (b, 0, 0)),
                     pl.BlockSpec((None, Lk - L, 128), lambda b, i: (b, 0, 0))]
        args += [cos, sin, cckv, ckr]
    out_specs = [pl.BlockSpec((None, MLA_TQ, MLA_HEADS * MLA_V), lambda b, i: (b, i, 0))]
    out_shape = [jax.ShapeDtypeStruct((B, L, MLA_HEADS * MLA_V), BF16)]
    if emit:
        out_specs += [pl.BlockSpec((None, L, MLA_KV_RANK), lambda b, i: (b, 0, 0)),
                      pl.BlockSpec((None, L, MLA_ROPE), lambda b, i: (b, 0, 0))]
        out_shape += [jax.ShapeDtypeStruct((B, L, MLA_KV_RANK), F32),
                      jax.ShapeDtypeStruct((B, L, MLA_ROPE), F32)]
    res = pl.pallas_call(
        functools.partial(_mla_kernel, L=L, Lk=Lk, rope=rope, emit=emit),
        grid=(B, L // MLA_TQ),
        in_specs=in_specs,
        out_specs=out_specs,
        out_shape=out_shape,
        scratch_shapes=[pltpu.VMEM((MLA_HEADS, Lk, MLA_HW), BF16),
                        pltpu.VMEM((MLA_HEADS, Lk, MLA_V), BF16)],
        compiler_params=_cparams(("arbitrary", "arbitrary")),
        name="mla_mixer",
    )(*args)
    return res if emit else (res[0], None, None)


def _outproj_kernel(x_ref, g_ref, yh_ref, yg_ref, ym_ref, wh_ref, wg_ref, wm_ref, o_ref):
    y = _dot(yh_ref[...], wh_ref[...]) + _dot(yg_ref[...], wg_ref[...]) + _dot(ym_ref[...], wm_ref[...])
    o_ref[...] = x_ref[...] + g_ref[...] * y


def _out_projection(x2d, mod, y_hy, y_gla, y_mla, w_out_b, row_of_tile, tm):
    rows = x2d.shape[0]
    return pl.pallas_call(
        _outproj_kernel,
        grid=(rows // tm,),
        in_specs=[pl.BlockSpec((tm, D_MODEL), lambda i: (i, 0)),
                  _mod_spec(2, row_of_tile),
                  pl.BlockSpec((tm, HY_CH), lambda i: (i, 0)),
                  pl.BlockSpec((tm, GLA_VW), lambda i: (i, 0)),
                  pl.BlockSpec((tm, MLA_HEADS * MLA_V), lambda i: (i, 0)),
                  pl.BlockSpec((HY_CH, D_MODEL), lambda i: (0, 0)),
                  pl.BlockSpec((GLA_VW, D_MODEL), lambda i: (1, 0)),
                  pl.BlockSpec((MLA_HEADS * MLA_V, D_MODEL), lambda i: (1, 0))],
        out_specs=pl.BlockSpec((tm, D_MODEL), lambda i: (i, 0)),
        out_shape=jax.ShapeDtypeStruct((rows, D_MODEL), F32),
        compiler_params=_cparams(("arbitrary",)),
        name="out_projection",
    )(x2d, mod, y_hy, y_gla, y_mla, w_out_b, w_out_b, w_out_b)


FFN_TF = 512


def _ffn_kernel(x_ref, ln_ref, sh_ref, sc_ref, g_ref, wg_ref, wu_ref, wo_ref, lnf_ref, o_ref,
                h_ref, acc_ref, *, final):
    f = pl.program_id(1)

    @pl.when(f == 0)
    def _():
        y = _rms(x_ref[...], ln_ref[...])
        h_ref[...] = (y * (1.0 + sc_ref[...]) + sh_ref[...]).astype(BF16)
        acc_ref[...] = jnp.zeros_like(acc_ref)

    h = h_ref[...]
    gate = _dot(h, wg_ref[...])
    up = _dot(h, wu_ref[...])
    act = (gate * jax.nn.sigmoid(gate) * up).astype(BF16)
    acc_ref[...] += _dot(act, wo_ref[...])

    @pl.when(f == pl.num_programs(1) - 1)
    def _():
        x = x_ref[...] + g_ref[...] * acc_ref[...]
        o_ref[...] = _rms(x, lnf_ref[...]) if final else x


def _ffn(x2d, ln, mod, w_in_b, w_out_b, ln_final, row_of_tile, tm, final):
    rows = x2d.shape[0]
    nf = D_FF // FFN_TF
    return pl.pallas_call(
        functools.partial(_ffn_kernel, final=final),
        grid=(rows // tm, nf),
        in_specs=[pl.BlockSpec((tm, D_MODEL), lambda i, f: (i, 0)),
                  pl.BlockSpec((1, D_MODEL), lambda i, f: (0, 0)),
                  _mod_spec(3, row_of_tile),
                  _mod_spec(4, row_of_tile),
                  _mod_spec(5, row_of_tile),
                  pl.BlockSpec((D_MODEL, FFN_TF), lambda i, f: (0, f)),
                  pl.BlockSpec((D_MODEL, FFN_TF), lambda i, f: (0, nf + f)),
                  pl.BlockSpec((FFN_TF, D_MODEL), lambda i, f: (f, 0)),
                  pl.BlockSpec((1, D_MODEL), lambda i, f: (0, 0))],
        out_specs=pl.BlockSpec((tm, D_MODEL), lambda i, f: (i, 0)),
        out_shape=jax.ShapeDtypeStruct((rows, D_MODEL), F32),
        scratch_shapes=[pltpu.VMEM((tm, D_MODEL), BF16), pltpu.VMEM((tm, D_MODEL), F32)],
        compiler_params=_cparams(("arbitrary", "arbitrary")),
        name="ffn",
    )(x2d, ln, mod, mod, mod, w_in_b, w_in_b, w_out_b, ln_final)


def _dft_tables(L):
    k = np.arange(L)
    ang = np.pi * ((k[:, None] * k[None, :]) % (2 * L)) / L
    c = np.cos(ang)
    s = np.sin(ang)
    sp = s.copy()
    sp[0, :] = 1.0 - 2.0 * (k % 2)
    as32 = lambda a: jnp.asarray(a.astype(np.float32))
    return as32(c), as32(s), as32(sp), as32(sp.T.copy())


def _hyena_consts(L):
    c, s, sp, spt = _dft_tables(L)
    c_hi, c_lo = _split(c)
    s_hi, s_lo = _split(s)
    t = jnp.linspace(0.0, 1.0, L, dtype=F32)[:, None]
    w = 2.0 * math.pi * jnp.arange(L, dtype=F32)[:, None] / L
    f = jnp.linspace(1e-4, HY_BANDS - 1, HY_BANDS, dtype=F32)
    zpos = jnp.concatenate([t, jnp.cos(f * w), -jnp.sin(f * w)], axis=-1)
    zpos = jnp.pad(zpos, ((0, 0), (0, 64 - HY_EMB)))
    min_decay = math.log(HY_TARGET) / HY_SLOW_DECAY
    max_decay = math.log(HY_TARGET) / HY_FAST_DECAY
    delta = jnp.abs(jnp.linspace(min_decay, max_decay, HY_CH, dtype=F32))
    decay = jnp.exp(-t * delta)
    filt_consts = (zpos, decay, c_hi, c_lo, s_hi, s_lo)
    main_consts = (c_hi, sp.astype(BF16), spt.astype(BF16))
    return filt_consts, main_consts


def _rope_tables(L):
    t = jnp.arange(L)
    half = MLA_ROPE // 2
    inv = ROPE_THETA ** (-jnp.arange(0, half, 2, dtype=F32) / half)
    ang_r = (t // GRID_W).astype(F32)[:, None] * inv
    ang_c = (t % GRID_W).astype(F32)[:, None] * inv
    cr, sr, cc, sc = jnp.cos(ang_r), jnp.sin(ang_r), jnp.cos(ang_c), jnp.sin(ang_c)
    cos = jnp.concatenate([cr, cr, cc, cc, jnp.ones((L, 128 - MLA_ROPE), F32)], axis=1)
    sin = jnp.concatenate([-sr, sr, -sc, sc, jnp.zeros((L, 128 - MLA_ROPE), F32)], axis=1)
    return cos, sin


def _prep_w_in(w):
    z = lambda n: jnp.zeros((D_MODEL, n), w.dtype)
    gla0 = HY_IN
    a0 = gla0 + 2 * GLA_QK + 2 * GLA_VW
    mla0 = a0 + 2 * GLA_LOWRANK
    return jnp.concatenate([w[:, :a0], w[:, mla0:], z(64), w[:, a0:mla0], z(256 - 2 * GLA_LOWRANK)],
                           axis=1).astype(BF16)


def _prep_w_uq(w):
    w = w.reshape(MLA_Q_RANK, MLA_HEADS, MLA_NOPE + MLA_ROPE)
    w = jnp.pad(w, ((0, 0), (0, 0), (0, MLA_HW - MLA_NOPE - MLA_ROPE)))
    return w.reshape(MLA_Q_RANK, MLA_HEADS * MLA_HW).astype(BF16)


def _prep_gla_decay(wa_f, ba_f, wa_b, ba_b):
    wa = jnp.zeros((256, 2 * GLA_QK), F32)
    wa = wa.at[0:GLA_LOWRANK, 0:GLA_QK].set(wa_f)
    wa = wa.at[GLA_LOWRANK:2 * GLA_LOWRANK, GLA_QK:].set(wa_b)
    return wa, jnp.concatenate([ba_f, ba_b])[None, :]


def _pack_states(s_f, s_b):
    def one(s):
        st = jnp.swapaxes(s, -1, -2)
        z = jnp.zeros_like(st)
        even = jnp.concatenate([st, z], axis=-1)
        odd = jnp.concatenate([z, st], axis=-1)
        sel = (jnp.arange(GLA_HEADS) % 2 == 0)[None, :, None, None]
        return jnp.where(sel, even, odd)
    return jnp.concatenate([one(s_f), one(s_b)], axis=1).astype(F32)


def _unpack_states(st):
    def one(s):
        sel = (jnp.arange(GLA_HEADS) % 2 == 0)[None, :, None, None]
        s = jnp.where(sel, s[..., :GLA_DK], s[..., GLA_DK:])
        return jnp.swapaxes(s, -1, -2)
    return one(st[:, :GLA_HEADS]), one(st[:, GLA_HEADS:])


def _trunk_layer(x2d, B, L, mod, row_of_tile, lw, consts, ctx, final, ln_final):
    tm_proj = 1024
    tm_ffn = 512
    u = _in_projection(x2d, lw['ln_mix'], mod, lw['w_in'], row_of_tile(tm_proj), tm_proj)
    u3 = u.reshape(B, L, U_W)
    filt_consts, main_consts = consts['hyena']
    filt = _hyena_filters(L, filt_consts, lw['hy_w1'], lw['hy_b1'], lw['hy_freq'], lw['hy_w2'],
                          lw['hy_b2'], lw['hy_w3'])
    y_hy = _hyena_mixer(u3, lw['hy_conv_w'], lw['hy_conv_b'], lw['hy_skip'], filt, main_consts, L)
    if ctx is None:
        y_gla, st = _gla_mixer(u3, lw['gla_wa'], lw['gla_ba'], lw['gla_norm'], None, L, True)
        y_mla, ckv, krope = _mla_mixer(u3, lw['mla_q_norm'], lw['mla_w_uq'], lw['mla_kv_norm'],
                                       lw['mla_w_ukv'], None, L, True)
        extras = (ckv, krope) + _unpack_states(st)
    else:
        ctx_ckv, ctx_krope, s0f, s0b = ctx
        y_gla, _ = _gla_mixer(u3, lw['gla_wa'], lw['gla_ba'], lw['gla_norm'], _pack_states(s0f, s0b), L, False)
        cos, sin = consts['rope']
        ckr = jnp.pad(ctx_krope, ((0, 0), (0, 0), (0, 128 - MLA_ROPE)))
        y_mla, _, _ = _mla_mixer(u3, lw['mla_q_norm'], lw['mla_w_uq'], lw['mla_kv_norm'], lw['mla_w_ukv'],
                                 (cos, sin, ctx_ckv, ckr), L, False)
        extras = None
    rows = B * L
    x2d = _out_projection(x2d, mod, y_hy.reshape(rows, -1), y_gla.reshape(rows, -1), y_mla.reshape(rows, -1),
                          lw['w_out'], row_of_tile(tm_ffn), tm_ffn)
    x2d = _ffn(x2d, lw['ln_ffn'], mod, lw['w_ffn_in'], lw['w_ffn_out'], ln_final, row_of_tile(tm_ffn),
               tm_ffn, final)
    return x2d, extras


def kernel(x_prompt, x_sample, cache_mla_ckv, cache_mla_krope, state_gla_fwd, state_gla_bwd, c, c_ctx, w_mod, b_mod, ln_mix, w_in, hy_conv_w, hy_conv_b, hy_filt_w1, hy_filt_b1, hy_filt_freq, hy_filt_w2, hy_filt_b2, hy_filt_w3, hy_skip, gla_wa_f, gla_ba_f, gla_wa_b, gla_ba_b, gla_norm, mla_q_norm, mla_w_uq, mla_kv_norm, mla_w_ukv, w_out, ln_ffn, w_ffn_in, w_ffn_out, ln_final):
    Bc, Lc, _ = x_prompt.shape
    Bl, Ll, _ = x_sample.shape
    assert 1 + Bl <= MOD_ROWS

    c_all = jnp.concatenate([c_ctx[None, :], c, jnp.zeros((MOD_ROWS - 1 - Bl, D_MODEL), F32)], axis=0)
    mod_all = _modulation(c_all, w_mod, b_mod).reshape(DEPTH, MOD_ROWS, 6, 1, D_MODEL)

    consts_ctx = {'hyena': _hyena_consts(Lc)}
    consts_lat = {'hyena': _hyena_consts(Ll), 'rope': _rope_tables(Ll)}
    ctx_rows = lambda tm: (lambda i: 0)
    lat_rows = lambda tm: (lambda i: 1 + (i * tm) // Ll)
    lnf = ln_final[None, :]

    x_ctx = x_prompt.reshape(Bc * Lc, D_MODEL)
    x_lat = x_sample.reshape(Bl * Ll, D_MODEL)
    ckv_l, krope_l, sf_l, sb_l = [], [], [], []
    for l in range(DEPTH):
        wa, ba = _prep_gla_decay(gla_wa_f[l], gla_ba_f[l], gla_wa_b[l], gla_ba_b[l])
        lw = {
            'ln_mix': ln_mix[l][None, :], 'w_in': _prep_w_in(w_in[l]),
            'hy_conv_w': hy_conv_w[l], 'hy_conv_b': hy_conv_b[l][None, :],
            'hy_w1': jnp.pad(hy_filt_w1[l], ((0, 64 - HY_EMB), (0, 0))), 'hy_b1': hy_filt_b1[l][None, :],
            'hy_freq': hy_filt_freq[l], 'hy_w2': hy_filt_w2[l], 'hy_b2': hy_filt_b2[l][None, :],
            'hy_w3': hy_filt_w3[l], 'hy_skip': hy_skip[l],
            'gla_wa': wa, 'gla_ba': ba, 'gla_norm': gla_norm[l][None, :],
            'mla_q_norm': mla_q_norm[l][None, :], 'mla_w_uq': _prep_w_uq(mla_w_uq[l]),
            'mla_kv_norm': mla_kv_norm[l][None, :], 'mla_w_ukv': mla_w_ukv[l].astype(BF16),
            'w_out': w_out[l].astype(BF16), 'ln_ffn': ln_ffn[l][None, :],
            'w_ffn_in': w_ffn_in[l].astype(BF16), 'w_ffn_out': w_ffn_out[l].astype(BF16),
        }
        final = l == DEPTH - 1
        mod = mod_all[l]
        x_ctx, (ckv, krope, s_f, s_b) = _trunk_layer(x_ctx, Bc, Lc, mod, ctx_rows, lw, consts_ctx, None,
                                                     final, lnf)
        ckv_l.append(ckv)
        krope_l.append(krope)
        sf_l.append(s_f)
        sb_l.append(s_b)
        ctx = (cache_mla_ckv[:, l], cache_mla_krope[:, l], state_gla_fwd[:, l], state_gla_bwd[:, l])
        x_lat, _ = _trunk_layer(x_lat, Bl, Ll, mod, lat_rows, lw, consts_lat, ctx, final, lnf)
    return (x_ctx.reshape(Bc, Lc, D_MODEL), x_lat.reshape(Bl, Ll, D_MODEL),
            jnp.stack(ckv_l, axis=1), jnp.stack(krope_l, axis=1),
            jnp.stack(sf_l, axis=1), jnp.stack(sb_l, axis=1))
```

```python
import functools
import math

import numpy as np
import jax
import jax.numpy as jnp
from jax import lax
from jax.experimental import pallas as pl
from jax.experimental.pallas import tpu as pltpu

F32 = jnp.float32
BF16 = jnp.bfloat16

D_MODEL = 2048
DEPTH = 2
GRID_W = 64
EPS = 1e-6
HY_CH = 512
HY_ORDER = 2
HY_BANDS = 16
HY_EMB = 1 + 2 * HY_BANDS
HY_FF = 64
HY_FAST_DECAY = 0.3
HY_SLOW_DECAY = 1.5
HY_TARGET = 1e-2
HY_IN = 3 * HY_CH
GLA_HEADS = 4
GLA_DK = 64
GLA_DV = 128
GLA_LOWRANK = 16
GLA_TAU = 16.0
GLA_CHUNK = 64
GLA_QK = GLA_HEADS * GLA_DK
GLA_VW = GLA_HEADS * GLA_DV
MLA_HEADS = 8
MLA_Q_RANK = 384
MLA_KV_RANK = 256
MLA_NOPE = 128
MLA_ROPE = 64
MLA_V = 128
ROPE_THETA = 10000.0
D_FF = -(-8 * D_MODEL // (3 * 256)) * 256

U_HY = 0
U_GLA = 1536
U_MLA = 3072
U_GLA_A = 3840
U_W = 4096
MOD_ROWS = 16

VMEM_LIMIT_V7X = 48 * 1024 * 1024
VMEM_LIMIT_FFN_V7X = 54 * 1024 * 1024


def _cparams(sem, vmem_limit=VMEM_LIMIT_V7X):
    return pltpu.CompilerParams(dimension_semantics=sem, vmem_limit_bytes=vmem_limit)


def _dot(a, b):
    return jnp.dot(a, b, preferred_element_type=F32)


def _dot_nt(a, b):
    return lax.dot_general(a, b, (((1,), (1,)), ((), ())), preferred_element_type=F32)


def _split(x):
    hi = x.astype(BF16)
    lo = (x - hi.astype(F32)).astype(BF16)
    return hi, lo


def _dot3(a, b):
    a_hi, a_lo = _split(a)
    b_hi, b_lo = _split(b)
    return _dot(a_hi, b_hi) + (_dot(a_lo, b_hi) + _dot(a_hi, b_lo))


def _rms(x, g):
    ms = jnp.mean(x * x, axis=-1, keepdims=True)
    return x * lax.rsqrt(ms + EPS) * g


def _mod_kernel(c_ref, w_ref, b_ref, o_ref):
    c = c_ref[...]
    s = (c * jax.nn.sigmoid(c)).astype(BF16)
    o_ref[...] = _dot(s, w_ref[...].astype(BF16)) + b_ref[...]


def _modulation(c_all, w_mod, b_mod):
    tn = 1024
    n6 = 6 * D_MODEL
    return pl.pallas_call(
        _mod_kernel,
        grid=(DEPTH, n6 // tn),
        in_specs=[pl.BlockSpec((MOD_ROWS, D_MODEL), lambda l, j: (0, 0)),
                  pl.BlockSpec((None, D_MODEL, tn), lambda l, j: (l, 0, j)),
                  pl.BlockSpec((None, 1, tn), lambda l, j: (l, 0, j))],
        out_specs=pl.BlockSpec((None, MOD_ROWS, tn), lambda l, j: (l, 0, j)),
        out_shape=jax.ShapeDtypeStruct((DEPTH, MOD_ROWS, n6), F32),
        compiler_params=_cparams(("arbitrary", "arbitrary")),
        name="modulation",
    )(c_all, w_mod, b_mod.reshape(DEPTH, 1, n6))


def _mod_spec(chunk, row_of_tile, tn=None):
    if tn is None:
        return pl.BlockSpec((None, None, 1, D_MODEL), lambda i, *_: (row_of_tile(i), chunk, 0, 0))
    return pl.BlockSpec((None, None, 1, tn), lambda i, j: (row_of_tile(i), chunk, 0, j))


IN_TN = 512
IN_MAIN = 3072


def _inproj_kernel(x_ref, ln_ref, sh_ref, sc_ref, w_ref, wt_ref, o_ref, h_ref):
    j = pl.program_id(1)

    @pl.when(j == 0)
    def _():
        y = _rms(x_ref[...], ln_ref[...])
        h_ref[...] = (y * (1.0 + sc_ref[...]) + sh_ref[...]).astype(BF16)

    @pl.when(j < IN_MAIN // IN_TN)
    def _():
        o_ref[...] = _dot(h_ref[...], w_ref[...].astype(BF16))

    @pl.when(j >= IN_MAIN // IN_TN)
    def _():
        o_ref[...] = _dot(h_ref[...], wt_ref[...])


def _in_projection(x2d, ln, mod, w_in, layer, w_tail, row_of_tile, tm):
    rows = x2d.shape[0]
    n_main = IN_MAIN // IN_TN
    return pl.pallas_call(
        _inproj_kernel,
        grid=(rows // tm, U_W // IN_TN),
        in_specs=[pl.BlockSpec((tm, D_MODEL), lambda i, j: (i, 0)),
                  pl.BlockSpec((1, D_MODEL), lambda i, j: (0, 0)),
                  _mod_spec(0, row_of_tile),
                  _mod_spec(1, row_of_tile),
                  pl.BlockSpec((None, D_MODEL, IN_TN), lambda i, j: (layer, 0, jnp.minimum(j, n_main - 1))),
                  pl.BlockSpec((D_MODEL, IN_TN), lambda i, j: (0, jnp.maximum(j - n_main, 0)))],
        out_specs=pl.BlockSpec((tm, IN_TN), lambda i, j: (i, j)),
        out_shape=jax.ShapeDtypeStruct((rows, U_W), F32),
        scratch_shapes=[pltpu.VMEM((tm, D_MODEL), BF16)],
        compiler_params=_cparams(("arbitrary", "arbitrary")),
        name="in_projection",
    )(x2d, ln, mod, mod, w_in, w_tail)


def _filter_kernel(z_ref, w1_ref, b1_ref, fr_ref, w2_ref, b2_ref, w3f_ref, w3b_ref, dec_ref,
                   chi_ref, clo_ref, shi_ref, slo_ref, kr_ref, ki_ref, kn_ref, *, L):
    h = jnp.sin(fr_ref[0:1, :] * (_dot3(z_ref[...], w1_ref[...]) + b1_ref[...]))
    h = jnp.sin(fr_ref[1:2, :] * (_dot3(h, w2_ref[...]) + b2_ref[...]))
    dec = dec_ref[...]
    row = lax.broadcasted_iota(jnp.int32, (L, 1), 0)
    kf = _dot3(h, w3f_ref[...]) * dec
    kb = jnp.where(row == 0, 0.0, _dot3(h, w3b_ref[...]) * dec)
    p_hi, p_lo = _split(kf + kb)
    m_hi, m_lo = _split(kf - kb)
    chi, clo = chi_ref[...], clo_ref[...]
    shi, slo = shi_ref[...], slo_ref[...]
    kc = _dot(chi, p_hi) + (_dot(clo, p_hi) + _dot(chi, p_lo))
    ks = _dot(shi, m_hi) + (_dot(slo, m_hi) + _dot(shi, m_lo))
    sign = jnp.where((row & 1) == 0, 1.0, -1.0)
    kn = jnp.sum((kf + kb) * sign, axis=0, keepdims=True)
    kr_ref[...] = kc * jnp.where(row == 0, 0.5 / L, 1.0 / L)
    ki_ref[...] = ks * (-1.0 / L)
    kn_ref[...] = kn * (0.5 / L)


def _hyena_filters(L, consts, w1p, b1, freq, w2, b2, w3):
    zpos, decay, c_hi, c_lo, s_hi, s_lo = consts
    full = lambda shape: pl.BlockSpec(shape, lambda o: (0,) * len(shape))
    return pl.pallas_call(
        functools.partial(_filter_kernel, L=L),
        grid=(HY_ORDER,),
        in_specs=[full((L, 64)), full((64, HY_FF)), full((1, HY_FF)), full((2, HY_FF)),
                  full((HY_FF, HY_FF)), full((1, HY_FF)),
                  pl.BlockSpec((HY_FF, HY_CH), lambda o: (0, o)),
                  pl.BlockSpec((HY_FF, HY_CH), lambda o: (0, HY_ORDER + o)),
                  full((L, HY_CH)), full((L, L)), full((L, L)), full((L, L)), full((L, L))],
        out_specs=[pl.BlockSpec((None, L, HY_CH), lambda o: (o, 0, 0)),
                   pl.BlockSpec((None, L, HY_CH), lambda o: (o, 0, 0)),
                   pl.BlockSpec((None, 1, HY_CH), lambda o: (o, 0, 0))],
        out_shape=[jax.ShapeDtypeStruct((HY_ORDER, L, HY_CH), F32),
                   jax.ShapeDtypeStruct((HY_ORDER, L, HY_CH), F32),
                   jax.ShapeDtypeStruct((HY_ORDER, 1, HY_CH), F32)],
        compiler_params=_cparams(("arbitrary",)),
        name="hyena_filters",
    )(zpos, w1p, b1, freq, w2, b2, w3, w3, decay, c_hi, c_lo, s_hi, s_lo)


HY_CG = 256


def _hyena_kernel(v_ref, x1_ref, x2_ref, wv_ref, w1_ref, w2_ref, bv_ref, b1_ref, b2_ref, skip_ref,
                  kr_ref, ki_ref, kn_ref, c_ref, sp_ref, spt_ref, o_ref, *, L):
    row = lax.broadcasted_iota(jnp.int32, (L, 1), 0)
    first = row == 0
    last = row == L - 1

    def short_conv(u_ref, w_ref, b_ref):
        u = u_ref[...]
        prev = jnp.where(first, 0.0, pltpu.roll(u, 1, axis=0))
        nxt = jnp.where(last, 0.0, pltpu.roll(u, L - 1, axis=0))
        return prev * w_ref[0:1, :] + u * w_ref[1:2, :] + nxt * w_ref[2:3, :] + b_ref[...]

    z = short_conv(v_ref, wv_ref, bv_ref)
    gates = (short_conv(x1_ref, w1_ref, b1_ref), short_conv(x2_ref, w2_ref, b2_ref))
    c = c_ref[...]
    sp = sp_ref[...]
    spt = spt_ref[...]
    for o in range(HY_ORDER):
        zb = z.astype(BF16)
        a = _dot(c, zb)
        s = _dot(sp, zb)
        kr = kr_ref[o]
        ki = ki_ref[o]
        qt = a * kr + s * ki
        qb = s * jnp.where(first, kn_ref[o], kr) - a * ki
        conv = _dot(c, qt.astype(BF16)) + _dot(spt, qb.astype(BF16))
        z = gates[o] * (conv + z * skip_ref[o:o + 1, :])
    o_ref[...] = z.astype(BF16)


def _hyena_mixer(u3, conv_w, conv_b, skip, filt, dft, L):
    B = u3.shape[0]
    kr, ki, kn = filt
    c_b, sp_b, spt_b = dft
    ng = HY_CH // HY_CG
    ublk = lambda part: pl.BlockSpec((None, L, HY_CG), lambda g, b: (b, 0, part * ng + g))
    wblk = lambda part: pl.BlockSpec((3, HY_CG), lambda g, b: (0, part * ng + g))
    bblk = lambda part: pl.BlockSpec((1, HY_CG), lambda g, b: (0, part * ng + g))
    fblk = lambda rows: pl.BlockSpec((HY_ORDER, rows, HY_CG), lambda g, b: (0, 0, g))
    mat = pl.BlockSpec((L, L), lambda g, b: (0, 0))
    return pl.pallas_call(
        functools.partial(_hyena_kernel, L=L),
        grid=(ng, B),
        in_specs=[ublk(0), ublk(1), ublk(2), wblk(0), wblk(1), wblk(2), bblk(0), bblk(1), bblk(2),
                  pl.BlockSpec((HY_ORDER, HY_CG), lambda g, b: (0, g)),
                  fblk(L), fblk(L), fblk(1), mat, mat, mat],
        out_specs=pl.BlockSpec((None, L, HY_CG), lambda g, b: (b, 0, g)),
        out_shape=jax.ShapeDtypeStruct((B, L, HY_CH), BF16),
        compiler_params=_cparams(("arbitrary", "arbitrary")),
        name="hyena_mixer",
    )(u3, u3, u3, conv_w, conv_w, conv_w, conv_b, conv_b, conv_b, skip, kr, ki, kn, c_b, sp_b, spt_b)


GLA_PAIR = 2 * GLA_CHUNK
GLA_STATES = 2 * GLA_HEADS


def _gla_kernel(*refs, L, has_state, emit_state):
    it = iter(refs)
    qkvg_ref, a_ref, wa_ref, ba_ref, ng_ref = next(it), next(it), next(it), next(it), next(it)
    st0_ref = next(it) if has_state else None
    y_ref = next(it)
    stout_ref = next(it) if emit_state else None
    qd_s, ki_s, ke_s, ec_s, vb_s, vt_s, o_s, st_s = (next(it) for _ in range(8))

    npair = L // GLA_PAIR
    rc = lax.broadcasted_iota(jnp.int32, (L, 1), 0) & (GLA_CHUNK - 1)
    steps = (1, 2, 4, 8, 16, 32)

    def prefix(x):
        for d in steps:
            x = x + jnp.where(rc >= d, pltpu.roll(x, d, axis=0), 0.0)
        return x

    def suffix(x):
        for d in steps:
            x = x + jnp.where(rc < GLA_CHUNK - d, pltpu.roll(x, L - d, axis=0), 0.0)
        return x

    pre = _dot3(a_ref[...], wa_ref[...]) + ba_ref[...]
    la = (jnp.minimum(pre, 0.0) - jnp.log(1.0 + jnp.exp(-jnp.abs(pre)))) * (1.0 / GLA_TAU)
    q = qkvg_ref[:, 0:GLA_QK] * (GLA_DK ** -0.5)
    k = qkvg_ref[:, GLA_QK:2 * GLA_QK]
    for dirn in range(2):
        la_d = la[:, dirn * GLA_QK:(dirn + 1) * GLA_QK]
        if dirn == 0:
            cum = prefix(la_d)
            rest = suffix(la_d) - la_d
        else:
            cum = suffix(la_d)
            rest = prefix(la_d) - la_d
        ec = jnp.exp(cum)
        ec_s[dirn] = ec
        qd_s[dirn] = (q * ec).astype(BF16)
        ki_s[dirn] = (k * jnp.exp(-cum)).astype(BF16)
        ke_s[dirn] = (k * jnp.exp(rest)).astype(BF16)
    vb_s[...] = qkvg_ref[:, 2 * GLA_QK:2 * GLA_QK + GLA_VW].astype(BF16)

    def transpose_v(n2, carry):
        r0 = pl.multiple_of(n2 * GLA_PAIR, GLA_PAIR)
        for h in range(GLA_HEADS):
            vp = qkvg_ref[pl.ds(r0, GLA_PAIR), 2 * GLA_QK + h * GLA_DV:2 * GLA_QK + (h + 1) * GLA_DV]
            vt_s[n2, h] = vp.T.astype(BF16)
        return carry

    lax.fori_loop(0, npair, transpose_v, 0)

    if has_state:
        st_s[...] = st0_ref[...]
    else:
        st_s[...] = jnp.zeros_like(st_s)

    lane = lax.broadcasted_iota(jnp.int32, (1, 2 * GLA_DK), 1)
    head_lanes = (lane < GLA_DK, lane >= GLA_DK)
    ci = lax.broadcasted_iota(jnp.int32, (GLA_CHUNK, GLA_CHUNK), 0)
    si = lax.broadcasted_iota(jnp.int32, (GLA_CHUNK, GLA_CHUNK), 1)
    tri = (si <= ci, si >= ci)

    def pair_step(n, carry):
        for dirn in range(2):
            n2 = n if dirn == 0 else npair - 1 - n
            r0 = pl.multiple_of(n2 * GLA_PAIR, GLA_PAIR)
            for j in ((0, 1) if dirn == 0 else (1, 0)):
                rows = pl.ds(r0 + j * GLA_CHUNK, GLA_CHUNK)
                dec_grp = pl.multiple_of(r0 + j * GLA_CHUNK + (GLA_CHUNK - 8 if dirn == 0 else 0), 8)
                dec_sub = 7 if dirn == 0 else 0
                for p in range(GLA_HEADS // 2):
                    lanes = slice(p * 2 * GLA_DK, (p + 1) * 2 * GLA_DK)
                    qd = qd_s[dirn, rows, lanes]
                    ki = ki_s[dirn, rows, lanes]
                    ke = ke_s[dirn, pl.ds(r0, GLA_PAIR), lanes]
                    dec = ec_s[dirn, pl.ds(dec_grp, 8), lanes][dec_sub:dec_sub + 1, :]
                    for hh in range(2):
                        h = 2 * p + hh
                        qm = jnp.where(head_lanes[hh], qd, jnp.zeros_like(qd))
                        att = jnp.where(tri[dirn], _dot_nt(qm, ki), 0.0)
                        vh = vb_s[rows, h * GLA_DV:(h + 1) * GLA_DV]
                        st = st_s[dirn * GLA_HEADS + h]
                        o = _dot(att.astype(BF16), vh) + _dot_nt(qm, st.astype(BF16))
                        vt = vt_s[n2, h]
                        vt = jnp.where(head_lanes[j], vt, jnp.zeros_like(vt))
                        st_s[dirn * GLA_HEADS + h] = st * dec + _dot(vt, ke)
                        o_s[dirn, rows, h * GLA_DV:(h + 1) * GLA_DV] = o
        return carry

    lax.fori_loop(0, npair, pair_step, 0)

    for h in range(GLA_HEADS):
        cols = slice(h * GLA_DV, (h + 1) * GLA_DV)
        o = _rms(o_s[0, :, cols] + o_s[1, :, cols], ng_ref[...])
        g = qkvg_ref[:, 2 * GLA_QK + GLA_VW + h * GLA_DV:2 * GLA_QK + GLA_VW + (h + 1) * GLA_DV]
        y_ref[:, cols] = (o * (g * jax.nn.sigmoid(g))).astype(BF16)
    if emit_state:
        stout_ref[...] = st_s[...]


def _gla_mixer(u3, wa_p, ba_p, norm_g, st0, L, emit_state):
    B = u3.shape[0]
    has_state = st0 is not None
    st_spec = pl.BlockSpec((None, GLA_STATES, GLA_DV, 2 * GLA_DK), lambda b: (b, 0, 0, 0))
    in_specs = [pl.BlockSpec((None, L, 1536), lambda b: (b, 0, U_GLA // 1536)),
                pl.BlockSpec((None, L, 256), lambda b: (b, 0, U_GLA_A // 256)),
                pl.BlockSpec((256, 2 * GLA_QK), lambda b: (0, 0)),
                pl.BlockSpec((1, 2 * GLA_QK), lambda b: (0, 0)),
                pl.BlockSpec((1, GLA_DV), lambda b: (0, 0))]
    args = [u3, u3, wa_p, ba_p, norm_g]
    if has_state:
        in_specs.append(st_spec)
        args.append(st0)
    out_specs = [pl.BlockSpec((None, L, GLA_VW), lambda b: (b, 0, 0))]
    out_shape = [jax.ShapeDtypeStruct((B, L, GLA_VW), BF16)]
    if emit_state:
        out_specs.append(st_spec)
        out_shape.append(jax.ShapeDtypeStruct((B, GLA_STATES, GLA_DV, 2 * GLA_DK), F32))
    npair = L // GLA_PAIR
    res = pl.pallas_call(
        functools.partial(_gla_kernel, L=L, has_state=has_state, emit_state=emit_state),
        grid=(B,),
        in_specs=in_specs,
        out_specs=out_specs,
        out_shape=out_shape,
        scratch_shapes=[pltpu.VMEM((2, L, GLA_QK), BF16),
                        pltpu.VMEM((2, L, GLA_QK), BF16),
                        pltpu.VMEM((2, L, GLA_QK), BF16),
                        pltpu.VMEM((2, L, GLA_QK), F32),
                        pltpu.VMEM((L, GLA_VW), BF16),
                        pltpu.VMEM((npair, GLA_HEADS, GLA_DV, GLA_PAIR), BF16),
                        pltpu.VMEM((2, L, GLA_VW), F32),
                        pltpu.VMEM((GLA_STATES, GLA_DV, 2 * GLA_DK), F32)],
        compiler_params=_cparams(("arbitrary",)),
        name="gla_mixer",
    )(*args)
    return (res[0], res[1]) if emit_state else (res[0], None)


MLA_HW = 256
MLA_TQ = 256
MLA_KCH = 256


def _rope(x, cos, sin):
    lane = lax.broadcasted_iota(jnp.int32, x.shape, 1)
    partner = jnp.where((lane & 16) == 0, pltpu.roll(x, 112, axis=1), pltpu.roll(x, 16, axis=1))
    return x * cos + partner * sin


def _mla_kernel(*refs, L, Lk, rope, emit):
    it = iter(refs)
    u_ref, qn_ref, wq_ref, kvn_ref, wkv_ref = (next(it) for _ in range(5))
    cos_ref, sin_ref, cckv_ref, ckr_ref = (next(it) for _ in range(4)) if rope else (None,) * 4
    y_ref = next(it)
    ckv_out, kr_out = (next(it), next(it)) if emit else (None, None)
    kf_s, v_s = next(it), next(it)

    qi = pl.program_id(1)

    def project(keys, rope_part, r0):
        kv = _dot(keys, wkv_ref[...])
        for h in range(MLA_HEADS):
            kf_s[h, r0:r0 + MLA_KCH, 0:MLA_NOPE] = kv[:, h * 256:h * 256 + MLA_NOPE].astype(BF16)
            kf_s[h, r0:r0 + MLA_KCH, MLA_NOPE:MLA_HW] = rope_part
            v_s[h, r0:r0 + MLA_KCH, :] = kv[:, h * 256 + MLA_NOPE:(h + 1) * 256].astype(BF16)

    @pl.when(qi == 0)
    def _():
        for r0 in range(0, L, MLA_KCH):
            ckvn = _rms(u_ref[r0:r0 + MLA_KCH, MLA_Q_RANK:MLA_Q_RANK + MLA_KV_RANK], kvn_ref[...])
            kr = u_ref[r0:r0 + MLA_KCH, 640:768]
            if emit:
                ckv_out[r0:r0 + MLA_KCH, :] = ckvn
                kr_out[r0:r0 + MLA_KCH, :] = kr[:, 0:MLA_ROPE]
            if rope:
                kr = _rope(kr, cos_ref[r0:r0 + MLA_KCH, :], sin_ref[r0:r0 + MLA_KCH, :])
            project(ckvn.astype(BF16), kr.astype(BF16), r0)
        if rope:
            for r0 in range(0, Lk - L, MLA_KCH):
                project(cckv_ref[r0:r0 + MLA_KCH, :].astype(BF16),
                        ckr_ref[r0:r0 + MLA_KCH, :].astype(BF16), L + r0)

    q0 = pl.multiple_of(qi * MLA_TQ, MLA_TQ)
    cqn = _rms(u_ref[pl.ds(q0, MLA_TQ), 0:MLA_Q_RANK], qn_ref[...]).astype(BF16)
    q = _dot(cqn, wq_ref[...]) * ((MLA_NOPE + MLA_ROPE) ** -0.5)
    if rope:
        cos = cos_ref[pl.ds(q0, MLA_TQ), :]
        sin = sin_ref[pl.ds(q0, MLA_TQ), :]
    for h in range(MLA_HEADS):
        q_nope = q[:, h * MLA_HW:h * MLA_HW + MLA_NOPE]
        q_rope = q[:, h * MLA_HW + MLA_NOPE:(h + 1) * MLA_HW]
        if rope:
            q_rope = _rope(q_rope, cos, sin)
        qh = jnp.concatenate([q_nope, q_rope], axis=1).astype(BF16)
        s = _dot_nt(qh, kf_s[h])
        p = jnp.exp(s - jnp.max(s, axis=-1, keepdims=True))
        denom = jnp.sum(p, axis=-1, keepdims=True)
        o = _dot(p.astype(BF16), v_s[h])
        y_ref[:, h * MLA_V:(h + 1) * MLA_V] = (o / denom).astype(BF16)


def _mla_mixer(u3, q_norm, wq_p, kv_norm, wkv, rope_args, L, emit):
    B = u3.shape[0]
    rope = rope_args is not None
    Lk = L + (rope_args[2].shape[1] if rope else 0)
    const = lambda shape: pl.BlockSpec(shape, lambda b, i: (0,) * len(shape))
    in_specs = [pl.BlockSpec((None, L, 768), lambda b, i: (b, 0, U_MLA // 768)),
                const((1, MLA_Q_RANK)), const((MLA_Q_RANK, MLA_HEADS * MLA_HW)),
                const((1, MLA_KV_RANK)), const((MLA_KV_RANK, MLA_HEADS * 256))]
    args = [u3, q_norm, wq_p, kv_norm, wkv]
    if rope:
        cos, sin, cckv, ckr = rope_args
        in_specs += [const((L, 128)), const((L, 128)),
                     pl.BlockSpec((None, Lk - L, MLA_KV_RANK), lambda b, i: (b, 0, 0)),
                     pl.BlockSpec((None, Lk - L, 128), lambda b, i: (b, 0, 0))]
        args += [cos, sin, cckv, ckr]
    out_specs = [pl.BlockSpec((None, MLA_TQ, MLA_HEADS * MLA_V), lambda b, i: (b, i, 0))]
    out_shape = [jax.ShapeDtypeStruct((B, L, MLA_HEADS * MLA_V), BF16)]
    if emit:
        out_specs += [pl.BlockSpec((None, L, MLA_KV_RANK), lambda b, i: (b, 0, 0)),
                      pl.BlockSpec((None, L, MLA_ROPE), lambda b, i: (b, 0, 0))]
        out_shape += [jax.ShapeDtypeStruct((B, L, MLA_KV_RANK), F32),
                      jax.ShapeDtypeStruct((B, L, MLA_ROPE), F32)]
    res = pl.pallas_call(
        functools.partial(_mla_kernel, L=L, Lk=Lk, rope=rope, emit=emit),
        grid=(B, L // MLA_TQ),
        in_specs=in_specs,
        out_specs=out_specs,
        out_shape=out_shape,
        scratch_shapes=[pltpu.VMEM((MLA_HEADS, Lk, MLA_HW), BF16),
                        pltpu.VMEM((MLA_HEADS, Lk, MLA_V), BF16)],
        compiler_params=_cparams(("arbitrary", "arbitrary")),
        name="mla_mixer",
    )(*args)
    return res if emit else (res[0], None, None)


OUT_TN = 512


def _outproj_kernel(x_ref, g_ref, yh_ref, yg_ref, ym_ref, wh_ref, wg_ref, wm_ref, o_ref):
    y = (_dot(yh_ref[...], wh_ref[...].astype(BF16)) + _dot(yg_ref[...], wg_ref[...].astype(BF16))
         + _dot(ym_ref[...], wm_ref[...].astype(BF16)))
    o_ref[...] = x_ref[...] + g_ref[...] * y


def _out_projection(x2d, mod, y_hy, y_gla, y_mla, w_out, layer, row_of_tile, tm):
    rows = x2d.shape[0]
    return pl.pallas_call(
        _outproj_kernel,
        grid=(rows // tm, D_MODEL // OUT_TN),
        in_specs=[pl.BlockSpec((tm, OUT_TN), lambda i, j: (i, j)),
                  _mod_spec(2, row_of_tile, OUT_TN),
                  pl.BlockSpec((tm, HY_CH), lambda i, j: (i, 0)),
                  pl.BlockSpec((tm, GLA_VW), lambda i, j: (i, 0)),
                  pl.BlockSpec((tm, MLA_HEADS * MLA_V), lambda i, j: (i, 0)),
                  pl.BlockSpec((None, HY_CH, OUT_TN), lambda i, j: (layer, 0, j)),
                  pl.BlockSpec((None, GLA_VW, OUT_TN), lambda i, j: (layer, 1, j)),
                  pl.BlockSpec((None, MLA_HEADS * MLA_V, OUT_TN), lambda i, j: (layer, 1, j))],
        out_specs=pl.BlockSpec((tm, OUT_TN), lambda i, j: (i, j)),
        out_shape=jax.ShapeDtypeStruct((rows, D_MODEL), F32),
        compiler_params=_cparams(("arbitrary", "arbitrary")),
        name="out_projection",
    )(x2d, mod, y_hy, y_gla, y_mla, w_out, w_out, w_out)


FFN_TF = 256


def _ffn_kernel(x_ref, ln_ref, sh_ref, sc_ref, g_ref, wg_ref, wu_ref, wo_ref, lnf_ref, o_ref, h_ref, *, final):
    f = pl.program_id(1)

    @pl.when(f == 0)
    def _():
        y = _rms(x_ref[...], ln_ref[...])
        h_ref[...] = (y * (1.0 + sc_ref[...]) + sh_ref[...]).astype(BF16)
        o_ref[...] = jnp.zeros_like(o_ref)

    h = h_ref[...]
    gate = _dot(h, wg_ref[...].astype(BF16))
    up = _dot(h, wu_ref[...].astype(BF16))
    act = (gate * jax.nn.sigmoid(gate) * up).astype(BF16)
    o_ref[...] += _dot(act, wo_ref[...].astype(BF16))

    @pl.when(f == pl.num_programs(1) - 1)
    def _():
        x = x_ref[...] + g_ref[...] * o_ref[...]
        o_ref[...] = _rms(x, lnf_ref[...]) if final else x


def _ffn(x2d, ln, mod, w_in, w_out, layer, ln_final, row_of_tile, tm, final):
    rows = x2d.shape[0]
    nf = D_FF // FFN_TF
    return pl.pallas_call(
        functools.partial(_ffn_kernel, final=final),
        grid=(rows // tm, nf),
        in_specs=[pl.BlockSpec((tm, D_MODEL), lambda i, f: (i, 0), pipeline_mode=pl.Buffered(1)),
                  pl.BlockSpec((1, D_MODEL), lambda i, f: (0, 0)),
                  _mod_spec(3, row_of_tile),
                  _mod_spec(4, row_of_tile),
                  _mod_spec(5, row_of_tile),
                  pl.BlockSpec((None, D_MODEL, FFN_TF), lambda i, f: (layer, 0, f)),
                  pl.BlockSpec((None, D_MODEL, FFN_TF), lambda i, f: (layer, 0, nf + f)),
                  pl.BlockSpec((None, FFN_TF, D_MODEL), lambda i, f: (layer, f, 0)),
                  pl.BlockSpec((1, D_MODEL), lambda i, f: (0, 0))],
        out_specs=pl.BlockSpec((tm, D_MODEL), lambda i, f: (i, 0)),
        out_shape=jax.ShapeDtypeStruct((rows, D_MODEL), F32),
        scratch_shapes=[pltpu.VMEM((tm, D_MODEL), BF16)],
        compiler_params=_cparams(("arbitrary", "arbitrary"), VMEM_LIMIT_FFN_V7X),
        name="ffn",
    )(x2d, ln, mod, mod, mod, w_in, w_in, w_out, ln_final)


def _dft_tables(L):
    k = np.arange(L)
    ang = np.pi * ((k[:, None] * k[None, :]) % (2 * L)) / L
    c = np.cos(ang)
    s = np.sin(ang)
    sp = s.copy()
    sp[0, :] = 1.0 - 2.0 * (k % 2)
    as32 = lambda a: jnp.asarray(a.astype(np.float32))
    return as32(c), as32(s), as32(sp), as32(sp.T.copy())


def _hyena_consts(L):
    c, s, sp, spt = _dft_tables(L)
    c_hi, c_lo = _split(c)
    s_hi, s_lo = _split(s)
    t = jnp.linspace(0.0, 1.0, L, dtype=F32)[:, None]
    w = 2.0 * math.pi * jnp.arange(L, dtype=F32)[:, None] / L
    f = jnp.linspace(1e-4, HY_BANDS - 1, HY_BANDS, dtype=F32)
    zpos = jnp.concatenate([t, jnp.cos(f * w), -jnp.sin(f * w)], axis=-1)
    zpos = jnp.pad(zpos, ((0, 0), (0, 64 - HY_EMB)))
    min_decay = math.log(HY_TARGET) / HY_SLOW_DECAY
    max_decay = math.log(HY_TARGET) / HY_FAST_DECAY
    delta = jnp.abs(jnp.linspace(min_decay, max_decay, HY_CH, dtype=F32))
    decay = jnp.exp(-t * delta)
    filt_consts = (zpos, decay, c_hi, c_lo, s_hi, s_lo)
    main_consts = (c_hi, sp.astype(BF16), spt.astype(BF16))
    return filt_consts, main_consts


def _rope_tables(L):
    t = jnp.arange(L)
    half = MLA_ROPE // 2
    inv = ROPE_THETA ** (-jnp.arange(0, half, 2, dtype=F32) / half)
    ang_r = (t // GRID_W).astype(F32)[:, None] * inv
    ang_c = (t % GRID_W).astype(F32)[:, None] * inv
    cr, sr, cc, sc = jnp.cos(ang_r), jnp.sin(ang_r), jnp.cos(ang_c), jnp.sin(ang_c)
    cos = jnp.concatenate([cr, cr, cc, cc, jnp.ones((L, 128 - MLA_ROPE), F32)], axis=1)
    sin = jnp.concatenate([-sr, sr, -sc, sc, jnp.zeros((L, 128 - MLA_ROPE), F32)], axis=1)
    return cos, sin


def _prep_w_in_tail(w):
    z = lambda n: jnp.zeros((D_MODEL, n), w.dtype)
    mla0 = IN_MAIN + 2 * GLA_LOWRANK
    return jnp.concatenate([w[:, mla0:], z(64), w[:, IN_MAIN:mla0], z(256 - 2 * GLA_LOWRANK)], axis=1).astype(BF16)


def _prep_w_uq(w):
    w = w.reshape(MLA_Q_RANK, MLA_HEADS, MLA_NOPE + MLA_ROPE)
    w = jnp.pad(w, ((0, 0), (0, 0), (0, MLA_HW - MLA_NOPE - MLA_ROPE)))
    return w.reshape(MLA_Q_RANK, MLA_HEADS * MLA_HW).astype(BF16)


def _prep_gla_decay(wa_f, ba_f, wa_b, ba_b):
    wa = jnp.zeros((256, 2 * GLA_QK), F32)
    wa = wa.at[0:GLA_LOWRANK, 0:GLA_QK].set(wa_f)
    wa = wa.at[GLA_LOWRANK:2 * GLA_LOWRANK, GLA_QK:].set(wa_b)
    return wa, jnp.concatenate([ba_f, ba_b])[None, :]


def _pack_states(s_f, s_b):
    def one(s):
        st = jnp.swapaxes(s, -1, -2)
        z = jnp.zeros_like(st)
        even = jnp.concatenate([st, z], axis=-1)
        odd = jnp.concatenate([z, st], axis=-1)
        sel = (jnp.arange(GLA_HEADS) % 2 == 0)[None, :, None, None]
        return jnp.where(sel, even, odd)
    return jnp.concatenate([one(s_f), one(s_b)], axis=1).astype(F32)


def _unpack_states(st):
    def one(s):
        sel = (jnp.arange(GLA_HEADS) % 2 == 0)[None, :, None, None]
        s = jnp.where(sel, s[..., :GLA_DK], s[..., GLA_DK:])
        return jnp.swapaxes(s, -1, -2)
    return one(st[:, :GLA_HEADS]), one(st[:, GLA_HEADS:])


def _trunk_layer(x2d, B, L, mod, row_of_tile, lw, consts, ctx, final, ln_final):
    tm = 1024
    u = _in_projection(x2d, lw['ln_mix'], mod, lw['w_in'], lw['layer'], lw['w_in_tail'], row_of_tile(tm), tm)
    u3 = u.reshape(B, L, U_W)
    filt_consts, main_consts = consts['hyena']
    filt = _hyena_filters(L, filt_consts, lw['hy_w1'], lw['hy_b1'], lw['hy_freq'], lw['hy_w2'],
                          lw['hy_b2'], lw['hy_w3'])
    y_hy = _hyena_mixer(u3, lw['hy_conv_w'], lw['hy_conv_b'], lw['hy_skip'], filt, main_consts, L)
    if ctx is None:
        y_gla, st = _gla_mixer(u3, lw['gla_wa'], lw['gla_ba'], lw['gla_norm'], None, L, True)
        y_mla, ckv, krope = _mla_mixer(u3, lw['mla_q_norm'], lw['mla_w_uq'], lw['mla_kv_norm'],
                                       lw['mla_w_ukv'], None, L, True)
        extras = (ckv, krope) + _unpack_states(st)
    else:
        ctx_ckv, ctx_krope, s0f, s0b = ctx
        y_gla, _ = _gla_mixer(u3, lw['gla_wa'], lw['gla_ba'], lw['gla_norm'], _pack_states(s0f, s0b), L, False)
        cos, sin = consts['rope']
        ckr = jnp.pad(ctx_krope, ((0, 0), (0, 0), (0, 128 - MLA_ROPE)))
        y_mla, _, _ = _mla_mixer(u3, lw['mla_q_norm'], lw['mla_w_uq'], lw['mla_kv_norm'], lw['mla_w_ukv'],
                                 (cos, sin, ctx_ckv, ckr), L, False)
        extras = None
    rows = B * L
    x2d = _out_projection(x2d, mod, y_hy.reshape(rows, -1), y_gla.reshape(rows, -1), y_mla.reshape(rows, -1),
                          lw['w_out'], lw['layer'], row_of_tile(tm), tm)
    x2d = _ffn(x2d, lw['ln_ffn'], mod, lw['w_ffn_in'], lw['w_ffn_out'], lw['layer'], ln_final, row_of_tile(tm), tm,
               final)
    return x2d, extras


def kernel(x_prompt, x_sample, cache_mla_ckv, cache_mla_krope, state_gla_fwd, state_gla_bwd, c, c_ctx, w_mod, b_mod, ln_mix, w_in, hy_conv_w, hy_conv_b, hy_filt_w1, hy_filt_b1, hy_filt_freq, hy_filt_w2, hy_filt_b2, hy_filt_w3, hy_skip, gla_wa_f, gla_ba_f, gla_wa_b, gla_ba_b, gla_norm, mla_q_norm, mla_w_uq, mla_kv_norm, mla_w_ukv, w_out, ln_ffn, w_ffn_in, w_ffn_out, ln_final):
    Bc, Lc, _ = x_prompt.shape
    Bl, Ll, _ = x_sample.shape
    assert 1 + Bl <= MOD_ROWS

    c_all = jnp.concatenate([c_ctx[None, :], c, jnp.zeros((MOD_ROWS - 1 - Bl, D_MODEL), F32)], axis=0)
    mod_all = _modulation(c_all, w_mod, b_mod).reshape(DEPTH, MOD_ROWS, 6, 1, D_MODEL)

    consts_ctx = {'hyena': _hyena_consts(Lc)}
    consts_lat = {'hyena': _hyena_consts(Ll), 'rope': _rope_tables(Ll)}
    ctx_rows = lambda tm: (lambda i: 0)
    lat_rows = lambda tm: (lambda i: 1 + (i * tm) // Ll)
    lnf = ln_final[None, :]

    x_ctx = x_prompt.reshape(Bc * Lc, D_MODEL)
    x_lat = x_sample.reshape(Bl * Ll, D_MODEL)
    ckv_l, krope_l, sf_l, sb_l = [], [], [], []
    for l in range(DEPTH):
        wa, ba = _prep_gla_decay(gla_wa_f[l], gla_ba_f[l], gla_wa_b[l], gla_ba_b[l])
        lw = {
            'layer': l, 'ln_mix': ln_mix[l][None, :], 'w_in': w_in, 'w_in_tail': _prep_w_in_tail(w_in[l]),
            'hy_conv_w': hy_conv_w[l], 'hy_conv_b': hy_conv_b[l][None, :],
            'hy_w1': jnp.pad(hy_filt_w1[l], ((0, 64 - HY_EMB), (0, 0))), 'hy_b1': hy_filt_b1[l][None, :],
            'hy_freq': hy_filt_freq[l], 'hy_w2': hy_filt_w2[l], 'hy_b2': hy_filt_b2[l][None, :],
            'hy_w3': hy_filt_w3[l], 'hy_skip': hy_skip[l],
            'gla_wa': wa, 'gla_ba': ba, 'gla_norm': gla_norm[l][None, :],
            'mla_q_norm': mla_q_norm[l][None, :], 'mla_w_uq': _prep_w_uq(mla_w_uq[l]),
            'mla_kv_norm': mla_kv_norm[l][None, :], 'mla_w_ukv': mla_w_ukv[l].astype(BF16),
            'w_out': w_out, 'ln_ffn': ln_ffn[l][None, :],
            'w_ffn_in': w_ffn_in, 'w_ffn_out': w_ffn_out,
        }
        final = l == DEPTH - 1
        mod = mod_all[l]
        x_ctx, (ckv, krope, s_f, s_b) = _trunk_layer(x_ctx, Bc, Lc, mod, ctx_rows, lw, consts_ctx, None,
                                                     final, lnf)
        ckv_l.append(ckv)
        krope_l.append(krope)
        sf_l.append(s_f)
        sb_l.append(s_b)
        ctx = (cache_mla_ckv[:, l], cache_mla_krope[:, l], state_gla_fwd[:, l], state_gla_bwd[:, l])
        x_lat, _ = _trunk_layer(x_lat, Bl, Ll, mod, lat_rows, lw, consts_lat, ctx, final, lnf)
    return (x_ctx.reshape(Bc, Lc, D_MODEL), x_lat.reshape(Bl, Ll, D_MODEL),
            jnp.stack(ckv_l, axis=1), jnp.stack(krope_l, axis=1),
            jnp.stack(sf_l, axis=1), jnp.stack(sb_l, axis=1))
```

```python
import functools
import math

import numpy as np
import jax
import jax.numpy as jnp
from jax import lax
from jax.experimental import pallas as pl
from jax.experimental.pallas import tpu as pltpu

F32 = jnp.float32
BF16 = jnp.bfloat16

D_MODEL = 2048
DEPTH = 2
GRID_W = 64
EPS = 1e-6
HY_CH = 512
HY_ORDER = 2
HY_BANDS = 16
HY_EMB = 1 + 2 * HY_BANDS
HY_FF = 64
HY_FAST_DECAY = 0.3
HY_SLOW_DECAY = 1.5
HY_TARGET = 1e-2
HY_IN = 3 * HY_CH
GLA_HEADS = 4
GLA_DK = 64
GLA_DV = 128
GLA_LOWRANK = 16
GLA_TAU = 16.0
GLA_CHUNK = 64
GLA_QK = GLA_HEADS * GLA_DK
GLA_VW = GLA_HEADS * GLA_DV
MLA_HEADS = 8
MLA_Q_RANK = 384
MLA_KV_RANK = 256
MLA_NOPE = 128
MLA_ROPE = 64
MLA_V = 128
ROPE_THETA = 10000.0
D_FF = -(-8 * D_MODEL // (3 * 256)) * 256

U_HY = 0
U_GLA = 1536
U_MLA = 3072
U_GLA_A = 3840
U_W = 4096
MOD_ROWS = 16

VMEM_LIMIT_V7X = 48 * 1024 * 1024
VMEM_LIMIT_FFN_V7X = 54 * 1024 * 1024


def _cparams(sem, vmem_limit=VMEM_LIMIT_V7X):
    return pltpu.CompilerParams(dimension_semantics=sem, vmem_limit_bytes=vmem_limit)


def _dot(a, b):
    return jnp.dot(a, b, preferred_element_type=F32)


def _dot_nt(a, b):
    return lax.dot_general(a, b, (((1,), (1,)), ((), ())), preferred_element_type=F32)


def _split(x):
    hi = x.astype(BF16)
    lo = (x - hi.astype(F32)).astype(BF16)
    return hi, lo


def _dot3(a, b):
    a_hi, a_lo = _split(a)
    b_hi, b_lo = _split(b)
    return _dot(a_hi, b_hi) + (_dot(a_lo, b_hi) + _dot(a_hi, b_lo))


def _rms(x, g):
    ms = jnp.mean(x * x, axis=-1, keepdims=True)
    return x * lax.rsqrt(ms + EPS) * g


ADALN_ROWS = 16


def _adaln_to_bf16(x_ref, ln_ref, sc_ref, sh_ref, h_ref):
    gain = ln_ref[...] * (1.0 + sc_ref[...])
    shift = sh_ref[...]

    def body(i, carry):
        r = pl.multiple_of(i * ADALN_ROWS, ADALN_ROWS)
        x = x_ref[pl.ds(r, ADALN_ROWS), :]
        ms = jnp.mean(x * x, axis=-1, keepdims=True)
        h_ref[pl.ds(r, ADALN_ROWS), :] = (x * lax.rsqrt(ms + EPS) * gain + shift).astype(BF16)
        return carry

    lax.fori_loop(0, x_ref.shape[0] // ADALN_ROWS, body, 0, unroll=4)


def _mod_kernel(c_ref, w_ref, b_ref, o_ref):
    c = c_ref[...]
    s = (c * jax.nn.sigmoid(c)).astype(BF16)
    o_ref[...] = _dot(s, w_ref[...].astype(BF16)) + b_ref[...]


def _modulation(c_all, w_mod, b_mod):
    tn = 1024
    n6 = 6 * D_MODEL
    return pl.pallas_call(
        _mod_kernel,
        grid=(DEPTH, n6 // tn),
        in_specs=[pl.BlockSpec((MOD_ROWS, D_MODEL), lambda l, j: (0, 0)),
                  pl.BlockSpec((None, D_MODEL, tn), lambda l, j: (l, 0, j)),
                  pl.BlockSpec((None, 1, tn), lambda l, j: (l, 0, j))],
        out_specs=pl.BlockSpec((None, MOD_ROWS, tn), lambda l, j: (l, 0, j)),
        out_shape=jax.ShapeDtypeStruct((DEPTH, MOD_ROWS, n6), F32),
        compiler_params=_cparams(("arbitrary", "arbitrary")),
        name="modulation",
    )(c_all, w_mod, b_mod.reshape(DEPTH, 1, n6))


def _mod_spec(chunk, row_of_tile, tn=None):
    if tn is None:
        return pl.BlockSpec((None, None, 1, D_MODEL), lambda i, *_: (row_of_tile(i), chunk, 0, 0))
    return pl.BlockSpec((None, None, 1, tn), lambda i, j: (row_of_tile(i), chunk, 0, j))


IN_TN = 1024


def _inproj_kernel(x_ref, ln_ref, sh_ref, sc_ref, w_ref, o_ref, h_ref):
    @pl.when(pl.program_id(1) == 0)
    def _():
        _adaln_to_bf16(x_ref, ln_ref, sc_ref, sh_ref, h_ref)

    o_ref[...] = _dot(h_ref[...], w_ref[...])


def _in_projection(x2d, ln, mod, w_in_p, layer, row_of_tile, tm):
    rows = x2d.shape[0]
    return pl.pallas_call(
        _inproj_kernel,
        grid=(rows // tm, U_W // IN_TN),
        in_specs=[pl.BlockSpec((tm, D_MODEL), lambda i, j: (i, 0)),
                  pl.BlockSpec((1, D_MODEL), lambda i, j: (0, 0)),
                  _mod_spec(0, row_of_tile),
                  _mod_spec(1, row_of_tile),
                  pl.BlockSpec((None, D_MODEL, IN_TN), lambda i, j: (layer, 0, j))],
        out_specs=pl.BlockSpec((tm, IN_TN), lambda i, j: (i, j)),
        out_shape=jax.ShapeDtypeStruct((rows, U_W), F32),
        scratch_shapes=[pltpu.VMEM((tm, D_MODEL), BF16)],
        compiler_params=_cparams(("arbitrary", "arbitrary")),
        name="in_projection",
    )(x2d, ln, mod, mod, w_in_p)


def _filter_kernel(z_ref, w1_ref, b1_ref, fr_ref, w2_ref, b2_ref, w3f_ref, w3b_ref, dec_ref,
                   chi_ref, clo_ref, shi_ref, slo_ref, kr_ref, ki_ref, kn_ref, *, L):
    h = jnp.sin(fr_ref[0:1, :] * (_dot3(z_ref[...], w1_ref[...]) + b1_ref[...]))
    h = jnp.sin(fr_ref[1:2, :] * (_dot3(h, w2_ref[...]) + b2_ref[...]))
    dec = dec_ref[...]
    row = lax.broadcasted_iota(jnp.int32, (L, 1), 0)
    kf = _dot3(h, w3f_ref[...]) * dec
    kb = jnp.where(row == 0, 0.0, _dot3(h, w3b_ref[...]) * dec)
    p_hi, p_lo = _split(kf + kb)
    m_hi, m_lo = _split(kf - kb)
    chi, clo = chi_ref[...], clo_ref[...]
    shi, slo = shi_ref[...], slo_ref[...]
    kc = _dot(chi, p_hi) + (_dot(clo, p_hi) + _dot(chi, p_lo))
    ks = _dot(shi, m_hi) + (_dot(slo, m_hi) + _dot(shi, m_lo))
    sign = jnp.where((row & 1) == 0, 1.0, -1.0)
    kn = jnp.sum((kf + kb) * sign, axis=0, keepdims=True)
    kr_ref[...] = kc * jnp.where(row == 0, 0.5 / L, 1.0 / L)
    ki_ref[...] = ks * (-1.0 / L)
    kn_ref[...] = kn * (0.5 / L)


def _hyena_filters(L, consts, w1p, b1, freq, w2, b2, w3):
    zpos, decay, c_hi, c_lo, s_hi, s_lo = consts
    full = lambda shape: pl.BlockSpec(shape, lambda o: (0,) * len(shape))
    return pl.pallas_call(
        functools.partial(_filter_kernel, L=L),
        grid=(HY_ORDER,),
        in_specs=[full((L, 64)), full((64, HY_FF)), full((1, HY_FF)), full((2, HY_FF)),
                  full((HY_FF, HY_FF)), full((1, HY_FF)),
                  pl.BlockSpec((HY_FF, HY_CH), lambda o: (0, o)),
                  pl.BlockSpec((HY_FF, HY_CH), lambda o: (0, HY_ORDER + o)),
                  full((L, HY_CH)), full((L, L)), full((L, L)), full((L, L)), full((L, L))],
        out_specs=[pl.BlockSpec((None, L, HY_CH), lambda o: (o, 0, 0)),
                   pl.BlockSpec((None, L, HY_CH), lambda o: (o, 0, 0)),
                   pl.BlockSpec((None, 1, HY_CH), lambda o: (o, 0, 0))],
        out_shape=[jax.ShapeDtypeStruct((HY_ORDER, L, HY_CH), F32),
                   jax.ShapeDtypeStruct((HY_ORDER, L, HY_CH), F32),
                   jax.ShapeDtypeStruct((HY_ORDER, 1, HY_CH), F32)],
        compiler_params=_cparams(("arbitrary",)),
        name="hyena_filters",
    )(zpos, w1p, b1, freq, w2, b2, w3, w3, decay, c_hi, c_lo, s_hi, s_lo)


HY_CG = 256


def _hyena_kernel(v_ref, x1_ref, x2_ref, wv_ref, w1_ref, w2_ref, bv_ref, b1_ref, b2_ref, skip_ref,
                  kr_ref, ki_ref, kn_ref, c_ref, sp_ref, spt_ref, o_ref, *, L):
    row = lax.broadcasted_iota(jnp.int32, (L, 1), 0)
    first = row == 0
    last = row == L - 1
    groups = [slice(g * HY_CG, (g + 1) * HY_CG) for g in range(HY_CH // HY_CG)]

    def short_conv(u_ref, w_ref, b_ref, cols):
        u = u_ref[:, cols]
        prev = jnp.where(first, 0.0, pltpu.roll(u, 1, axis=0))
        nxt = jnp.where(last, 0.0, pltpu.roll(u, L - 1, axis=0))
        return prev * w_ref[0:1, cols] + u * w_ref[1:2, cols] + nxt * w_ref[2:3, cols] + b_ref[:, cols]

    gate_refs = ((x1_ref, w1_ref, b1_ref), (x2_ref, w2_ref, b2_ref))
    z = [short_conv(v_ref, wv_ref, bv_ref, cols) for cols in groups]
    for o in range(HY_ORDER):
        zb = [zz.astype(BF16) for zz in z]
        a = [_dot(c_ref[...], x) for x in zb]
        s = [_dot(sp_ref[...], x) for x in zb]
        conv = []
        for g, cols in enumerate(groups):
            kr = kr_ref[o, :, cols]
            ki = ki_ref[o, :, cols]
            qt = a[g] * kr + s[g] * ki
            qb = s[g] * jnp.where(first, kn_ref[o, :, cols], kr) - a[g] * ki
            conv.append(_dot(c_ref[...], qt.astype(BF16)) + _dot(spt_ref[...], qb.astype(BF16)))
        z = [short_conv(*gate_refs[o], cols) * (conv[g] + z[g] * skip_ref[o:o + 1, cols])
             for g, cols in enumerate(groups)]
    for g, cols in enumerate(groups):
        o_ref[:, cols] = z[g].astype(BF16)


def _hyena_mixer(u3, conv_w, conv_b, skip, filt, dft, L):
    B = u3.shape[0]
    kr, ki, kn = filt
    c_b, sp_b, spt_b = dft
    once = pl.Buffered(1)
    ublk = lambda part: pl.BlockSpec((None, L, HY_CH), lambda b: (b, 0, part))
    wblk = lambda part: pl.BlockSpec((3, HY_CH), lambda b: (0, part))
    bblk = lambda part: pl.BlockSpec((1, HY_CH), lambda b: (0, part))
    fblk = lambda rows: pl.BlockSpec((HY_ORDER, rows, HY_CH), lambda b: (0, 0, 0), pipeline_mode=once)
    mat = pl.BlockSpec((L, L), lambda b: (0, 0), pipeline_mode=once)
    return pl.pallas_call(
        functools.partial(_hyena_kernel, L=L),
        grid=(B,),
        in_specs=[ublk(0), ublk(1), ublk(2), wblk(0), wblk(1), wblk(2), bblk(0), bblk(1), bblk(2),
                  pl.BlockSpec((HY_ORDER, HY_CH), lambda b: (0, 0)),
                  fblk(L), fblk(L), fblk(1), mat, mat, mat],
        out_specs=pl.BlockSpec((None, L, HY_CH), lambda b: (b, 0, 0)),
        out_shape=jax.ShapeDtypeStruct((B, L, HY_CH), BF16),
        compiler_params=_cparams(("arbitrary",)),
        name="hyena_mixer",
    )(u3, u3, u3, conv_w, conv_w, conv_w, conv_b, conv_b, conv_b, skip, kr, ki, kn, c_b, sp_b, spt_b)


GLA_PAIR = 2 * GLA_CHUNK
GLA_STATES = 2 * GLA_HEADS


def _gla_kernel(*refs, L, has_state, emit_state):
    it = iter(refs)
    qkvg_ref, a_ref, wa_ref, ba_ref, ng_ref = next(it), next(it), next(it), next(it), next(it)
    st0_ref = next(it) if has_state else None
    y_ref = next(it)
    stout_ref = next(it) if emit_state else None
    qd_s, ki_s, ke_s, ec_s, vb_s, vt_s, o_s, st_s = (next(it) for _ in range(8))

    npair = L // GLA_PAIR
    rc = lax.broadcasted_iota(jnp.int32, (L, 1), 0) & (GLA_CHUNK - 1)
    steps = (1, 2, 4, 8, 16, 32)

    def prefix(x):
        for d in steps:
            x = x + jnp.where(rc >= d, pltpu.roll(x, d, axis=0), 0.0)
        return x

    def suffix(x):
        for d in steps:
            x = x + jnp.where(rc < GLA_CHUNK - d, pltpu.roll(x, L - d, axis=0), 0.0)
        return x

    pre = _dot3(a_ref[...], wa_ref[...]) + ba_ref[...]
    la = (jnp.minimum(pre, 0.0) - jnp.log(1.0 + jnp.exp(-jnp.abs(pre)))) * (1.0 / GLA_TAU)
    q = qkvg_ref[:, 0:GLA_QK] * (GLA_DK ** -0.5)
    k = qkvg_ref[:, GLA_QK:2 * GLA_QK]
    for dirn in range(2):
        la_d = la[:, dirn * GLA_QK:(dirn + 1) * GLA_QK]
        if dirn == 0:
            cum = prefix(la_d)
            rest = suffix(la_d) - la_d
        else:
            cum = suffix(la_d)
            rest = prefix(la_d) - la_d
        ec = jnp.exp(cum)
        ec_s[dirn] = ec
        qd_s[dirn] = (q * ec).astype(BF16)
        ki_s[dirn] = (k * jnp.exp(-cum)).astype(BF16)
        ke_s[dirn] = (k * jnp.exp(rest)).astype(BF16)
    vb_s[...] = qkvg_ref[:, 2 * GLA_QK:2 * GLA_QK + GLA_VW].astype(BF16)

    def transpose_v(n2, carry):
        r0 = pl.multiple_of(n2 * GLA_PAIR, GLA_PAIR)
        for h in range(GLA_HEADS):
            vp = qkvg_ref[pl.ds(r0, GLA_PAIR), 2 * GLA_QK + h * GLA_DV:2 * GLA_QK + (h + 1) * GLA_DV]
            vt_s[n2, h] = vp.T.astype(BF16)
        return carry

    lax.fori_loop(0, npair, transpose_v, 0)

    if has_state:
        st_s[...] = st0_ref[...]
    else:
        st_s[...] = jnp.zeros_like(st_s)

    lane = lax.broadcasted_iota(jnp.int32, (1, 2 * GLA_DK), 1)
    head_lanes = (lane < GLA_DK, lane >= GLA_DK)
    ci = lax.broadcasted_iota(jnp.int32, (GLA_CHUNK, GLA_CHUNK), 0)
    si = lax.broadcasted_iota(jnp.int32, (GLA_CHUNK, GLA_CHUNK), 1)
    tri = (si <= ci, si >= ci)

    def pair_step(n, carry):
        for dirn in range(2):
            n2 = n if dirn == 0 else npair - 1 - n
            r0 = pl.multiple_of(n2 * GLA_PAIR, GLA_PAIR)
            for j in ((0, 1) if dirn == 0 else (1, 0)):
                rows = pl.ds(r0 + j * GLA_CHUNK, GLA_CHUNK)
                dec_grp = pl.multiple_of(r0 + j * GLA_CHUNK + (GLA_CHUNK - 8 if dirn == 0 else 0), 8)
                dec_sub = 7 if dirn == 0 else 0
                for p in range(GLA_HEADS // 2):
                    lanes = slice(p * 2 * GLA_DK, (p + 1) * 2 * GLA_DK)
                    qd = qd_s[dirn, rows, lanes]
                    ki = ki_s[dirn, rows, lanes]
                    ke = ke_s[dirn, pl.ds(r0, GLA_PAIR), lanes]
                    dec = ec_s[dirn, pl.ds(dec_grp, 8), lanes][dec_sub:dec_sub + 1, :]
                    for hh in range(2):
                        h = 2 * p + hh
                        qm = jnp.where(head_lanes[hh], qd, jnp.zeros_like(qd))
                        att = jnp.where(tri[dirn], _dot_nt(qm, ki), 0.0)
                        vh = vb_s[rows, h * GLA_DV:(h + 1) * GLA_DV]
                        st = st_s[dirn * GLA_HEADS + h]
                        o = _dot(att.astype(BF16), vh) + _dot_nt(qm, st.astype(BF16))
                        vt = vt_s[n2, h]
                        vt = jnp.where(head_lanes[j], vt, jnp.zeros_like(vt))
                        st_s[dirn * GLA_HEADS + h] = st * dec + _dot(vt, ke)
                        o_s[dirn, rows, h * GLA_DV:(h + 1) * GLA_DV] = o
        return carry

    lax.fori_loop(0, npair, pair_step, 0)

    for h in range(GLA_HEADS):
        cols = slice(h * GLA_DV, (h + 1) * GLA_DV)
        o = _rms(o_s[0, :, cols] + o_s[1, :, cols], ng_ref[...])
        g = qkvg_ref[:, 2 * GLA_QK + GLA_VW + h * GLA_DV:2 * GLA_QK + GLA_VW + (h + 1) * GLA_DV]
        y_ref[:, cols] = (o * (g * jax.nn.sigmoid(g))).astype(BF16)
    if emit_state:
        stout_ref[...] = st_s[...]


def _gla_mixer(u3, wa_p, ba_p, norm_g, st0, L, emit_state):
    B = u3.shape[0]
    has_state = st0 is not None
    st_spec = pl.BlockSpec((None, GLA_STATES, GLA_DV, 2 * GLA_DK), lambda b: (b, 0, 0, 0))
    in_specs = [pl.BlockSpec((None, L, 1536), lambda b: (b, 0, U_GLA // 1536)),
                pl.BlockSpec((None, L, 256), lambda b: (b, 0, U_GLA_A // 256)),
                pl.BlockSpec((256, 2 * GLA_QK), lambda b: (0, 0)),
                pl.BlockSpec((1, 2 * GLA_QK), lambda b: (0, 0)),
                pl.BlockSpec((1, GLA_DV), lambda b: (0, 0))]
    args = [u3, u3, wa_p, ba_p, norm_g]
    if has_state:
        in_specs.append(st_spec)
        args.append(st0)
    out_specs = [pl.BlockSpec((None, L, GLA_VW), lambda b: (b, 0, 0))]
    out_shape = [jax.ShapeDtypeStruct((B, L, GLA_VW), BF16)]
    if emit_state:
        out_specs.append(st_spec)
        out_shape.append(jax.ShapeDtypeStruct((B, GLA_STATES, GLA_DV, 2 * GLA_DK), F32))
    npair = L // GLA_PAIR
    res = pl.pallas_call(
        functools.partial(_gla_kernel, L=L, has_state=has_state, emit_state=emit_state),
        grid=(B,),
        in_specs=in_specs,
        out_specs=out_specs,
        out_shape=out_shape,
        scratch_shapes=[pltpu.VMEM((2, L, GLA_QK), BF16),
                        pltpu.VMEM((2, L, GLA_QK), BF16),
                        pltpu.VMEM((2, L, GLA_QK), BF16),
                        pltpu.VMEM((2, L, GLA_QK), F32),
                        pltpu.VMEM((L, GLA_VW), BF16),
                        pltpu.VMEM((npair, GLA_HEADS, GLA_DV, GLA_PAIR), BF16),
                        pltpu.VMEM((2, L, GLA_VW), F32),
                        pltpu.VMEM((GLA_STATES, GLA_DV, 2 * GLA_DK), F32)],
        compiler_params=_cparams(("arbitrary",)),
        name="gla_mixer",
    )(*args)
    return (res[0], res[1]) if emit_state else (res[0], None)


MLA_HW = 256
MLA_TQ = 256
MLA_KCH = 256


def _rope(x, cos, sin):
    lane = lax.broadcasted_iota(jnp.int32, x.shape, 1)
    partner = jnp.where((lane & 16) == 0, pltpu.roll(x, 112, axis=1), pltpu.roll(x, 16, axis=1))
    return x * cos + partner * sin


def _mla_kernel(*refs, L, Lk, rope, emit):
    it = iter(refs)
    u_ref, qn_ref, wq_ref, kvn_ref, wkv_ref = (next(it) for _ in range(5))
    cos_ref, sin_ref, cckv_ref, ckr_ref = (next(it) for _ in range(4)) if rope else (None,) * 4
    y_ref = next(it)
    ckv_out, kr_out = (next(it), next(it)) if emit else (None, None)
    kf_s, v_s = next(it), next(it)

    qi = pl.program_id(1)

    def project(keys, rope_part, r0):
        kv = _dot(keys, wkv_ref[...])
        for h in range(MLA_HEADS):
            kf_s[h, r0:r0 + MLA_KCH, 0:MLA_NOPE] = kv[:, h * 256:h * 256 + MLA_NOPE].astype(BF16)
            kf_s[h, r0:r0 + MLA_KCH, MLA_NOPE:MLA_HW] = rope_part
            v_s[h, r0:r0 + MLA_KCH, :] = kv[:, h * 256 + MLA_NOPE:(h + 1) * 256].astype(BF16)

    @pl.when(qi == 0)
    def _():
        for r0 in range(0, L, MLA_KCH):
            ckvn = _rms(u_ref[r0:r0 + MLA_KCH, MLA_Q_RANK:MLA_Q_RANK + MLA_KV_RANK], kvn_ref[...])
            kr = u_ref[r0:r0 + MLA_KCH, 640:768]
            if emit:
                ckv_out[r0:r0 + MLA_KCH, :] = ckvn
                kr_out[r0:r0 + MLA_KCH, :] = kr[:, 0:MLA_ROPE]
            if rope:
                kr = _rope(kr, cos_ref[r0:r0 + MLA_KCH, :], sin_ref[r0:r0 + MLA_KCH, :])
            project(ckvn.astype(BF16), kr.astype(BF16), r0)
        if rope:
            for r0 in range(0, Lk - L, MLA_KCH):
                project(cckv_ref[r0:r0 + MLA_KCH, :].astype(BF16),
                        ckr_ref[r0:r0 + MLA_KCH, :].astype(BF16), L + r0)

    q0 = pl.multiple_of(qi * MLA_TQ, MLA_TQ)
    cqn = _rms(u_ref[pl.ds(q0, MLA_TQ), 0:MLA_Q_RANK], qn_ref[...]).astype(BF16)
    q = _dot(cqn, wq_ref[...]) * ((MLA_NOPE + MLA_ROPE) ** -0.5)
    if rope:
        cos = cos_ref[pl.ds(q0, MLA_TQ), :]
        sin = sin_ref[pl.ds(q0, MLA_TQ), :]
    for h in range(MLA_HEADS):
        q_nope = q[:, h * MLA_HW:h * MLA_HW + MLA_NOPE]
        q_rope = q[:, h * MLA_HW + MLA_NOPE:(h + 1) * MLA_HW]
        if rope:
            q_rope = _rope(q_rope, cos, sin)
        qh = jnp.concatenate([q_nope, q_rope], axis=1).astype(BF16)
        s = _dot_nt(qh, kf_s[h])
        p = jnp.exp(s - jnp.max(s, axis=-1, keepdims=True))
        denom = jnp.sum(p, axis=-1, keepdims=True)
        o = _dot(p.astype(BF16), v_s[h])
        y_ref[:, h * MLA_V:(h + 1) * MLA_V] = (o / denom).astype(BF16)


def _mla_mixer(u3, q_norm, wq_p, kv_norm, wkv, rope_args, L, emit):
    B = u3.shape[0]
    rope = rope_args is not None
    Lk = L + (rope_args[2].shape[1] if rope else 0)
    const = lambda shape: pl.BlockSpec(shape, lambda b, i: (0,) * len(shape))
    in_specs = [pl.BlockSpec((None, L, 768), lambda b, i: (b, 0, U_MLA // 768)),
                const((1, MLA_Q_RANK)), const((MLA_Q_RANK, MLA_HEADS * MLA_HW)),
                const((1, MLA_KV_RANK)), const((MLA_KV_RANK, MLA_HEADS * 256))]
    args = [u3, q_norm, wq_p, kv_norm, wkv]
    if rope:
        cos, sin, cckv, ckr = rope_args
        in_specs += [const((L, 128)), const((L, 128)),
                     pl.BlockSpec((None, Lk - L, MLA_KV_RANK), lambda b, i: (b, 0, 0)),
                     pl.BlockSpec((None, Lk - L, 128), lambda b, i: (b, 0, 0))]
        args += [cos, sin, cckv, ckr]
    out_specs = [pl.BlockSpec((None, MLA_TQ, MLA_HEADS * MLA_V), lambda b, i: (b, i, 0))]
    out_shape = [jax.ShapeDtypeStruct((B, L, MLA_HEADS * MLA_V), BF16)]
    if emit:
        out_specs += [pl.BlockSpec((None, L, MLA_KV_RANK), lambda b, i: (b, 0, 0)),
                      pl.BlockSpec((None, L, MLA_ROPE), lambda b, i: (b, 0, 0))]
        out_shape += [jax.ShapeDtypeStruct((B, L, MLA_KV_RANK), F32),
                      jax.ShapeDtypeStruct((B, L, MLA_ROPE), F32)]
    res = pl.pallas_call(
        functools.partial(_mla_kernel, L=L, Lk=Lk, rope=rope, emit=emit),
        grid=(B, L // MLA_TQ),
        in_specs=in_specs,
        out_specs=out_specs,
        out_shape=out_shape,
        scratch_shapes=[pltpu.VMEM((MLA_HEADS, Lk, MLA_HW), BF16),
                        pltpu.VMEM((MLA_HEADS, Lk, MLA_V), BF16)],
        compiler_params=_cparams(("arbitrary", "arbitrary")),
        name="mla_mixer",
    )(*args)
    return res if emit else (res[0], None, None)


def _outproj_kernel(x_ref, g_ref, yh_ref, yg_ref, ym_ref, wh_ref, wg_ref, wm_ref, o_ref):
    y = _dot(yh_ref[...], wh_ref[...]) + _dot(yg_ref[...], wg_ref[...]) + _dot(ym_ref[...], wm_ref[...])
    o_ref[...] = x_ref[...] + g_ref[...] * y


def _out_projection(x2d, mod, y_hy, y_gla, y_mla, w_out_b, layer, row_of_tile, tm):
    rows = x2d.shape[0]
    once = pl.Buffered(1)
    return pl.pallas_call(
        _outproj_kernel,
        grid=(rows // tm,),
        in_specs=[pl.BlockSpec((tm, D_MODEL), lambda i: (i, 0)),
                  _mod_spec(2, row_of_tile),
                  pl.BlockSpec((tm, HY_CH), lambda i: (i, 0)),
                  pl.BlockSpec((tm, GLA_VW), lambda i: (i, 0)),
                  pl.BlockSpec((tm, MLA_HEADS * MLA_V), lambda i: (i, 0)),
                  pl.BlockSpec((None, HY_CH, D_MODEL), lambda i: (layer, 0, 0), pipeline_mode=once),
                  pl.BlockSpec((None, GLA_VW, D_MODEL), lambda i: (layer, 1, 0), pipeline_mode=once),
                  pl.BlockSpec((None, MLA_HEADS * MLA_V, D_MODEL), lambda i: (layer, 1, 0), pipeline_mode=once)],
        out_specs=pl.BlockSpec((tm, D_MODEL), lambda i: (i, 0)),
        out_shape=jax.ShapeDtypeStruct((rows, D_MODEL), F32),
        compiler_params=_cparams(("arbitrary",)),
        name="out_projection",
    )(x2d, mod, y_hy, y_gla, y_mla, w_out_b, w_out_b, w_out_b)


FFN_TF = 512
FFN_TR = 512


def _ffn_kernel(x_ref, ln_ref, sh_ref, sc_ref, g_ref, wg_ref, wu_ref, wo_ref, lnf_ref, o_ref, h_ref, *, final):
    f = pl.program_id(1)

    @pl.when(f == 0)
    def _():
        _adaln_to_bf16(x_ref, ln_ref, sc_ref, sh_ref, h_ref)
        o_ref[...] = jnp.zeros_like(o_ref)

    for r in range(0, x_ref.shape[0], FFN_TR):
        rows = slice(r, r + FFN_TR)
        h = h_ref[rows, :]
        gate = _dot(h, wg_ref[...])
        up = _dot(h, wu_ref[...])
        act = (gate * jax.nn.sigmoid(gate) * up).astype(BF16)
        o_ref[rows, :] += _dot(act, wo_ref[...])

    @pl.when(f == pl.num_programs(1) - 1)
    def _():
        def body(i, carry):
            rows = pl.ds(pl.multiple_of(i * ADALN_ROWS, ADALN_ROWS), ADALN_ROWS)
            x = x_ref[rows, :] + g_ref[...] * o_ref[rows, :]
            o_ref[rows, :] = _rms(x, lnf_ref[...]) if final else x
            return carry

        lax.fori_loop(0, x_ref.shape[0] // ADALN_ROWS, body, 0, unroll=4)


def _ffn(x2d, ln, mod, w_in, w_out, layer, ln_final, row_of_tile, tm, final):
    rows = x2d.shape[0]
    nf = D_FF // FFN_TF
    return pl.pallas_call(
        functools.partial(_ffn_kernel, final=final),
        grid=(rows // tm, nf),
        in_specs=[pl.BlockSpec((tm, D_MODEL), lambda i, f: (i, 0), pipeline_mode=pl.Buffered(1)),
                  pl.BlockSpec((1, D_MODEL), lambda i, f: (0, 0)),
                  _mod_spec(3, row_of_tile),
                  _mod_spec(4, row_of_tile),
                  _mod_spec(5, row_of_tile),
                  pl.BlockSpec((None, D_MODEL, FFN_TF), lambda i, f: (layer, 0, f)),
                  pl.BlockSpec((None, D_MODEL, FFN_TF), lambda i, f: (layer, 0, nf + f)),
                  pl.BlockSpec((None, FFN_TF, D_MODEL), lambda i, f: (layer, f, 0)),
                  pl.BlockSpec((1, D_MODEL), lambda i, f: (0, 0))],
        out_specs=pl.BlockSpec((tm, D_MODEL), lambda i, f: (i, 0)),
        out_shape=jax.ShapeDtypeStruct((rows, D_MODEL), F32),
        scratch_shapes=[pltpu.VMEM((tm, D_MODEL), BF16)],
        compiler_params=_cparams(("arbitrary", "arbitrary"), VMEM_LIMIT_FFN_V7X),
        name="ffn",
    )(x2d, ln, mod, mod, mod, w_in, w_in, w_out, ln_final)


def _dft_tables(L):
    k = np.arange(L)
    ang = np.pi * ((k[:, None] * k[None, :]) % (2 * L)) / L
    c = np.cos(ang)
    s = np.sin(ang)
    sp = s.copy()
    sp[0, :] = 1.0 - 2.0 * (k % 2)
    as32 = lambda a: jnp.asarray(a.astype(np.float32))
    return as32(c), as32(s), as32(sp), as32(sp.T.copy())


def _hyena_consts(L):
    c, s, sp, spt = _dft_tables(L)
    c_hi, c_lo = _split(c)
    s_hi, s_lo = _split(s)
    t = jnp.linspace(0.0, 1.0, L, dtype=F32)[:, None]
    w = 2.0 * math.pi * jnp.arange(L, dtype=F32)[:, None] / L
    f = jnp.linspace(1e-4, HY_BANDS - 1, HY_BANDS, dtype=F32)
    zpos = jnp.concatenate([t, jnp.cos(f * w), -jnp.sin(f * w)], axis=-1)
    zpos = jnp.pad(zpos, ((0, 0), (0, 64 - HY_EMB)))
    min_decay = math.log(HY_TARGET) / HY_SLOW_DECAY
    max_decay = math.log(HY_TARGET) / HY_FAST_DECAY
    delta = jnp.abs(jnp.linspace(min_decay, max_decay, HY_CH, dtype=F32))
    decay = jnp.exp(-t * delta)
    filt_consts = (zpos, decay, c_hi, c_lo, s_hi, s_lo)
    main_consts = (c_hi, sp.astype(BF16), spt.astype(BF16))
    return filt_consts, main_consts


def _rope_tables(L):
    t = jnp.arange(L)
    half = MLA_ROPE // 2
    inv = ROPE_THETA ** (-jnp.arange(0, half, 2, dtype=F32) / half)
    ang_r = (t // GRID_W).astype(F32)[:, None] * inv
    ang_c = (t % GRID_W).astype(F32)[:, None] * inv
    cr, sr, cc, sc = jnp.cos(ang_r), jnp.sin(ang_r), jnp.cos(ang_c), jnp.sin(ang_c)
    cos = jnp.concatenate([cr, cr, cc, cc, jnp.ones((L, 128 - MLA_ROPE), F32)], axis=1)
    sin = jnp.concatenate([-sr, sr, -sc, sc, jnp.zeros((L, 128 - MLA_ROPE), F32)], axis=1)
    return cos, sin


def _prep_w_in(w):
    z = lambda n: jnp.zeros((DEPTH, D_MODEL, n), w.dtype)
    a0 = HY_IN + 2 * GLA_QK + 2 * GLA_VW
    mla0 = a0 + 2 * GLA_LOWRANK
    return jnp.concatenate([w[:, :, :a0], w[:, :, mla0:], z(64), w[:, :, a0:mla0], z(256 - 2 * GLA_LOWRANK)],
                           axis=2).astype(BF16)


def _prep_w_uq(w):
    w = w.reshape(MLA_Q_RANK, MLA_HEADS, MLA_NOPE + MLA_ROPE)
    w = jnp.pad(w, ((0, 0), (0, 0), (0, MLA_HW - MLA_NOPE - MLA_ROPE)))
    return w.reshape(MLA_Q_RANK, MLA_HEADS * MLA_HW).astype(BF16)


def _prep_gla_decay(wa_f, ba_f, wa_b, ba_b):
    wa = jnp.zeros((256, 2 * GLA_QK), F32)
    wa = wa.at[0:GLA_LOWRANK, 0:GLA_QK].set(wa_f)
    wa = wa.at[GLA_LOWRANK:2 * GLA_LOWRANK, GLA_QK:].set(wa_b)
    return wa, jnp.concatenate([ba_f, ba_b])[None, :]


def _pack_states(s_f, s_b):
    def one(s):
        st = jnp.swapaxes(s, -1, -2)
        z = jnp.zeros_like(st)
        even = jnp.concatenate([st, z], axis=-1)
        odd = jnp.concatenate([z, st], axis=-1)
        sel = (jnp.arange(GLA_HEADS) % 2 == 0)[None, :, None, None]
        return jnp.where(sel, even, odd)
    return jnp.concatenate([one(s_f), one(s_b)], axis=1).astype(F32)


def _unpack_states(st):
    def one(s):
        sel = (jnp.arange(GLA_HEADS) % 2 == 0)[None, :, None, None]
        s = jnp.where(sel, s[..., :GLA_DK], s[..., GLA_DK:])
        return jnp.swapaxes(s, -1, -2)
    return one(st[:, :GLA_HEADS]), one(st[:, GLA_HEADS:])


def _trunk_layer(x2d, B, L, mod, row_of_tile, lw, consts, ctx, final, ln_final):
    tm = 1024
    tm_out = 512
    u = _in_projection(x2d, lw['ln_mix'], mod, lw['w_in'], lw['layer'], row_of_tile(tm), tm)
    u3 = u.reshape(B, L, U_W)
    filt_consts, main_consts = consts['hyena']
    filt = _hyena_filters(L, filt_consts, lw['hy_w1'], lw['hy_b1'], lw['hy_freq'], lw['hy_w2'],
                          lw['hy_b2'], lw['hy_w3'])
    y_hy = _hyena_mixer(u3, lw['hy_conv_w'], lw['hy_conv_b'], lw['hy_skip'], filt, main_consts, L)
    if ctx is None:
        y_gla, st = _gla_mixer(u3, lw['gla_wa'], lw['gla_ba'], lw['gla_norm'], None, L, True)
        y_mla, ckv, krope = _mla_mixer(u3, lw['mla_q_norm'], lw['mla_w_uq'], lw['mla_kv_norm'],
                                       lw['mla_w_ukv'], None, L, True)
        extras = (ckv, krope) + _unpack_states(st)
    else:
        ctx_ckv, ctx_krope, s0f, s0b = ctx
        y_gla, _ = _gla_mixer(u3, lw['gla_wa'], lw['gla_ba'], lw['gla_norm'], _pack_states(s0f, s0b), L, False)
        cos, sin = consts['rope']
        ckr = jnp.pad(ctx_krope, ((0, 0), (0, 0), (0, 128 - MLA_ROPE)))
        y_mla, _, _ = _mla_mixer(u3, lw['mla_q_norm'], lw['mla_w_uq'], lw['mla_kv_norm'], lw['mla_w_ukv'],
                                 (cos, sin, ctx_ckv, ckr), L, False)
        extras = None
    rows = B * L
    x2d = _out_projection(x2d, mod, y_hy.reshape(rows, -1), y_gla.reshape(rows, -1), y_mla.reshape(rows, -1),
                          lw['w_out'], lw['layer'], row_of_tile(tm_out), tm_out)
    x2d = _ffn(x2d, lw['ln_ffn'], mod, lw['w_ffn_in'], lw['w_ffn_out'], lw['layer'], ln_final, row_of_tile(tm), tm,
               final)
    return x2d, extras


def kernel(x_prompt, x_sample, cache_mla_ckv, cache_mla_krope, state_gla_fwd, state_gla_bwd, c, c_ctx, w_mod, b_mod, ln_mix, w_in, hy_conv_w, hy_conv_b, hy_filt_w1, hy_filt_b1, hy_filt_freq, hy_filt_w2, hy_filt_b2, hy_filt_w3, hy_skip, gla_wa_f, gla_ba_f, gla_wa_b, gla_ba_b, gla_norm, mla_q_norm, mla_w_uq, mla_kv_norm, mla_w_ukv, w_out, ln_ffn, w_ffn_in, w_ffn_out, ln_final):
    Bc, Lc, _ = x_prompt.shape
    Bl, Ll, _ = x_sample.shape
    assert 1 + Bl <= MOD_ROWS

    c_all = jnp.concatenate([c_ctx[None, :], c, jnp.zeros((MOD_ROWS - 1 - Bl, D_MODEL), F32)], axis=0)
    mod_all = _modulation(c_all, w_mod, b_mod).reshape(DEPTH, MOD_ROWS, 6, 1, D_MODEL)

    consts_ctx = {'hyena': _hyena_consts(Lc)}
    consts_lat = {'hyena': _hyena_consts(Ll), 'rope': _rope_tables(Ll)}
    ctx_rows = lambda tm: (lambda i: 0)
    lat_rows = lambda tm: (lambda i: 1 + (i * tm) // Ll)
    lnf = ln_final[None, :]

    x_ctx = x_prompt.reshape(Bc * Lc, D_MODEL)
    x_lat = x_sample.reshape(Bl * Ll, D_MODEL)
    w_in_b = _prep_w_in(w_in)
    w_out_b = w_out.astype(BF16)
    w_ffn_in_b = w_ffn_in.astype(BF16)
    w_ffn_out_b = w_ffn_out.astype(BF16)
    ckv_l, krope_l, sf_l, sb_l = [], [], [], []
    for l in range(DEPTH):
        wa, ba = _prep_gla_decay(gla_wa_f[l], gla_ba_f[l], gla_wa_b[l], gla_ba_b[l])
        lw = {
            'layer': l, 'ln_mix': ln_mix[l][None, :], 'w_in': w_in_b,
            'hy_conv_w': hy_conv_w[l], 'hy_conv_b': hy_conv_b[l][None, :],
            'hy_w1': jnp.pad(hy_filt_w1[l], ((0, 64 - HY_EMB), (0, 0))), 'hy_b1': hy_filt_b1[l][None, :],
            'hy_freq': hy_filt_freq[l], 'hy_w2': hy_filt_w2[l], 'hy_b2': hy_filt_b2[l][None, :],
            'hy_w3': hy_filt_w3[l], 'hy_skip': hy_skip[l],
            'gla_wa': wa, 'gla_ba': ba, 'gla_norm': gla_norm[l][None, :],
            'mla_q_norm': mla_q_norm[l][None, :], 'mla_w_uq': _prep_w_uq(mla_w_uq[l]),
            'mla_kv_norm': mla_kv_norm[l][None, :], 'mla_w_ukv': mla_w_ukv[l].astype(BF16),
            'w_out': w_out_b, 'ln_ffn': ln_ffn[l][None, :],
            'w_ffn_in': w_ffn_in_b, 'w_ffn_out': w_ffn_out_b,
        }
        final = l == DEPTH - 1
        mod = mod_all[l]
        x_ctx, (ckv, krope, s_f, s_b) = _trunk_layer(x_ctx, Bc, Lc, mod, ctx_rows, lw, consts_ctx, None,
                                                     final, lnf)
        ckv_l.append(ckv)
        krope_l.append(krope)
        sf_l.append(s_f)
        sb_l.append(s_b)
        ctx = (cache_mla_ckv[:, l], cache_mla_krope[:, l], state_gla_fwd[:, l], state_gla_bwd[:, l])
        x_lat, _ = _trunk_layer(x_lat, Bl, Ll, mod, lat_rows, lw, consts_lat, ctx, final, lnf)
    return (x_ctx.reshape(Bc, Lc, D_MODEL), x_lat.reshape(Bl, Ll, D_MODEL),
            jnp.stack(ckv_l, axis=1), jnp.stack(krope_l, axis=1),
            jnp.stack(sf_l, axis=1), jnp.stack(sb_l, axis=1))
```

```python
import functools
import math

import numpy as np
import jax
import jax.numpy as jnp
from jax import lax
from jax.experimental import pallas as pl
from jax.experimental.pallas import tpu as pltpu

F32 = jnp.float32
BF16 = jnp.bfloat16

D_MODEL = 2048
DEPTH = 2
GRID_W = 64
EPS = 1e-6
HY_CH = 512
HY_ORDER = 2
HY_BANDS = 16
HY_EMB = 1 + 2 * HY_BANDS
HY_FF = 64
HY_FAST_DECAY = 0.3
HY_SLOW_DECAY = 1.5
HY_TARGET = 1e-2
HY_IN = 3 * HY_CH
GLA_HEADS = 4
GLA_DK = 64
GLA_DV = 128
GLA_LOWRANK = 16
GLA_TAU = 16.0
GLA_CHUNK = 64
GLA_QK = GLA_HEADS * GLA_DK
GLA_VW = GLA_HEADS * GLA_DV
MLA_HEADS = 8
MLA_Q_RANK = 384
MLA_KV_RANK = 256
MLA_NOPE = 128
MLA_ROPE = 64
MLA_V = 128
ROPE_THETA = 10000.0
D_FF = -(-8 * D_MODEL // (3 * 256)) * 256

U_HY = 0
U_GLA = 1536
U_MLA = 3072
U_GLA_A = 3840
U_W = 4096
MOD_ROWS = 16

VMEM_LIMIT_V7X = 48 * 1024 * 1024


def _cparams(sem):
    return pltpu.CompilerParams(dimension_semantics=sem, vmem_limit_bytes=VMEM_LIMIT_V7X)


def _dot(a, b):
    return jnp.dot(a, b, preferred_element_type=F32)


def _dot_nt(a, b):
    return lax.dot_general(a, b, (((1,), (1,)), ((), ())), preferred_element_type=F32)


def _split(x):
    hi = x.astype(BF16)
    lo = (x - hi.astype(F32)).astype(BF16)
    return hi, lo


def _dot3(a, b):
    a_hi, a_lo = _split(a)
    b_hi, b_lo = _split(b)
    return _dot(a_hi, b_hi) + (_dot(a_lo, b_hi) + _dot(a_hi, b_lo))


def _rms(x, g):
    ms = jnp.mean(x * x, axis=-1, keepdims=True)
    return x * lax.rsqrt(ms + EPS) * g


ADALN_ROWS = 16


def _adaln_to_bf16(x_ref, ln_ref, sc_ref, sh_ref, h_ref):
    gain = ln_ref[...] * (1.0 + sc_ref[...])
    shift = sh_ref[...]

    def body(i, carry):
        r = pl.multiple_of(i * ADALN_ROWS, ADALN_ROWS)
        x = x_ref[pl.ds(r, ADALN_ROWS), :]
        ms = jnp.mean(x * x, axis=-1, keepdims=True)
        h_ref[pl.ds(r, ADALN_ROWS), :] = (x * lax.rsqrt(ms + EPS) * gain + shift).astype(BF16)
        return carry

    lax.fori_loop(0, x_ref.shape[0] // ADALN_ROWS, body, 0, unroll=4)


def _mod_kernel(c_ref, w_ref, b_ref, o_ref):
    c = c_ref[...]
    s = (c * jax.nn.sigmoid(c)).astype(BF16)
    o_ref[...] = _dot(s, w_ref[...].astype(BF16)) + b_ref[...]


def _modulation(c_all, w_mod, b_mod):
    tn = 1024
    n6 = 6 * D_MODEL
    return pl.pallas_call(
        _mod_kernel,
        grid=(DEPTH, n6 // tn),
        in_specs=[pl.BlockSpec((MOD_ROWS, D_MODEL), lambda l, j: (0, 0)),
                  pl.BlockSpec((None, D_MODEL, tn), lambda l, j: (l, 0, j)),
                  pl.BlockSpec((None, 1, tn), lambda l, j: (l, 0, j))],
        out_specs=pl.BlockSpec((None, MOD_ROWS, tn), lambda l, j: (l, 0, j)),
        out_shape=jax.ShapeDtypeStruct((DEPTH, MOD_ROWS, n6), F32),
        compiler_params=_cparams(("arbitrary", "arbitrary")),
        name="modulation",
    )(c_all, w_mod, b_mod.reshape(DEPTH, 1, n6))


def _mod_spec(chunk, row_of_tile):
    return pl.BlockSpec((None, None, 1, D_MODEL), lambda i, *_: (row_of_tile(i), chunk, 0, 0))


IN_TN = 1024
IN_MAIN = 3072


def _inproj_kernel(x_ref, ln_ref, sh_ref, sc_ref, w_ref, wt_ref, o_ref, h_ref):
    j = pl.program_id(1)

    @pl.when(j == 0)
    def _():
        _adaln_to_bf16(x_ref, ln_ref, sc_ref, sh_ref, h_ref)

    @pl.when(j < IN_MAIN // IN_TN)
    def _():
        o_ref[...] = _dot(h_ref[...], w_ref[...])

    @pl.when(j >= IN_MAIN // IN_TN)
    def _():
        o_ref[...] = _dot(h_ref[...], wt_ref[...])


def _in_projection(x2d, ln, mod, w_in_b, w_tail, layer, row_of_tile, tm):
    rows = x2d.shape[0]
    n_main = IN_MAIN // IN_TN
    return pl.pallas_call(
        _inproj_kernel,
        grid=(rows // tm, U_W // IN_TN),
        in_specs=[pl.BlockSpec((tm, D_MODEL), lambda i, j: (i, 0)),
                  pl.BlockSpec((1, D_MODEL), lambda i, j: (0, 0)),
                  _mod_spec(0, row_of_tile),
                  _mod_spec(1, row_of_tile),
                  pl.BlockSpec((None, D_MODEL, IN_TN), lambda i, j: (layer, 0, jnp.minimum(j, n_main - 1))),
                  pl.BlockSpec((None, D_MODEL, U_W - IN_MAIN), lambda i, j: (layer, 0, 0))],
        out_specs=pl.BlockSpec((tm, IN_TN), lambda i, j: (i, j)),
        out_shape=jax.ShapeDtypeStruct((rows, U_W), F32),
        scratch_shapes=[pltpu.VMEM((tm, D_MODEL), BF16)],
        compiler_params=_cparams(("arbitrary", "arbitrary")),
        name="in_projection",
    )(x2d, ln, mod, mod, w_in_b, w_tail)


def _filter_kernel(z_ref, w1_ref, b1_ref, fr_ref, w2_ref, b2_ref, w3f_ref, w3b_ref, dec_ref,
                   chi_ref, clo_ref, shi_ref, slo_ref, kr_ref, ki_ref, kn_ref, *, L):
    h = jnp.sin(fr_ref[0:1, :] * (_dot3(z_ref[...], w1_ref[...]) + b1_ref[...]))
    h = jnp.sin(fr_ref[1:2, :] * (_dot3(h, w2_ref[...]) + b2_ref[...]))
    dec = dec_ref[...]
    row = lax.broadcasted_iota(jnp.int32, (L, 1), 0)
    kf = _dot3(h, w3f_ref[...]) * dec
    kb = jnp.where(row == 0, 0.0, _dot3(h, w3b_ref[...]) * dec)
    p_hi, p_lo = _split(kf + kb)
    m_hi, m_lo = _split(kf - kb)
    chi, clo = chi_ref[...], clo_ref[...]
    shi, slo = shi_ref[...], slo_ref[...]
    kc = _dot(chi, p_hi) + (_dot(clo, p_hi) + _dot(chi, p_lo))
    ks = _dot(shi, m_hi) + (_dot(slo, m_hi) + _dot(shi, m_lo))
    sign = jnp.where((row & 1) == 0, 1.0, -1.0)
    kn = jnp.sum((kf + kb) * sign, axis=0, keepdims=True)
    kr_ref[...] = kc * jnp.where(row == 0, 0.5 / L, 1.0 / L)
    ki_ref[...] = ks * (-1.0 / L)
    kn_ref[...] = kn * (0.5 / L)


def _hyena_filters(L, consts, w1p, b1, freq, w2, b2, w3):
    zpos, decay, c_hi, c_lo, s_hi, s_lo = consts
    full = lambda shape: pl.BlockSpec(shape, lambda o: (0,) * len(shape))
    return pl.pallas_call(
        functools.partial(_filter_kernel, L=L),
        grid=(HY_ORDER,),
        in_specs=[full((L, 64)), full((64, HY_FF)), full((1, HY_FF)), full((2, HY_FF)),
                  full((HY_FF, HY_FF)), full((1, HY_FF)),
                  pl.BlockSpec((HY_FF, HY_CH), lambda o: (0, o)),
                  pl.BlockSpec((HY_FF, HY_CH), lambda o: (0, HY_ORDER + o)),
                  full((L, HY_CH)), full((L, L)), full((L, L)), full((L, L)), full((L, L))],
        out_specs=[pl.BlockSpec((None, L, HY_CH), lambda o: (o, 0, 0)),
                   pl.BlockSpec((None, L, HY_CH), lambda o: (o, 0, 0)),
                   pl.BlockSpec((None, 1, HY_CH), lambda o: (o, 0, 0))],
        out_shape=[jax.ShapeDtypeStruct((HY_ORDER, L, HY_CH), F32),
                   jax.ShapeDtypeStruct((HY_ORDER, L, HY_CH), F32),
                   jax.ShapeDtypeStruct((HY_ORDER, 1, HY_CH), F32)],
        compiler_params=_cparams(("arbitrary",)),
        name="hyena_filters",
    )(zpos, w1p, b1, freq, w2, b2, w3, w3, decay, c_hi, c_lo, s_hi, s_lo)


HY_CG = 256


def _hyena_kernel(v_ref, x1_ref, x2_ref, wv_ref, w1_ref, w2_ref, bv_ref, b1_ref, b2_ref, skip_ref,
                  kr_ref, ki_ref, kn_ref, c_ref, sp_ref, spt_ref, o_ref, *, L):
    row = lax.broadcasted_iota(jnp.int32, (L, 1), 0)
    first = row == 0
    last = row == L - 1
    groups = [slice(g * HY_CG, (g + 1) * HY_CG) for g in range(HY_CH // HY_CG)]

    def short_conv(u_ref, w_ref, b_ref, cols):
        u = u_ref[:, cols]
        prev = jnp.where(first, 0.0, pltpu.roll(u, 1, axis=0))
        nxt = jnp.where(last, 0.0, pltpu.roll(u, L - 1, axis=0))
        return prev * w_ref[0:1, cols] + u * w_ref[1:2, cols] + nxt * w_ref[2:3, cols] + b_ref[:, cols]

    gate_refs = ((x1_ref, w1_ref, b1_ref), (x2_ref, w2_ref, b2_ref))
    z = [short_conv(v_ref, wv_ref, bv_ref, cols) for cols in groups]
    for o in range(HY_ORDER):
        zb = [zz.astype(BF16) for zz in z]
        a = [_dot(c_ref[...], x) for x in zb]
        s = [_dot(sp_ref[...], x) for x in zb]
        conv = []
        for g, cols in enumerate(groups):
            kr = kr_ref[o, :, cols]
            ki = ki_ref[o, :, cols]
            qt = a[g] * kr + s[g] * ki
            qb = s[g] * jnp.where(first, kn_ref[o, :, cols], kr) - a[g] * ki
            conv.append(_dot(c_ref[...], qt.astype(BF16)) + _dot(spt_ref[...], qb.astype(BF16)))
        z = [short_conv(*gate_refs[o], cols) * (conv[g] + z[g] * skip_ref[o:o + 1, cols])
             for g, cols in enumerate(groups)]
    for g, cols in enumerate(groups):
        o_ref[:, cols] = z[g].astype(BF16)


def _hyena_mixer(u3, conv_w, conv_b, skip, filt, dft, L):
    B = u3.shape[0]
    kr, ki, kn = filt
    c_b, sp_b, spt_b = dft
    once = pl.Buffered(1)
    ublk = lambda part: pl.BlockSpec((None, L, HY_CH), lambda b: (b, 0, part))
    wblk = lambda part: pl.BlockSpec((3, HY_CH), lambda b: (0, part))
    bblk = lambda part: pl.BlockSpec((1, HY_CH), lambda b: (0, part))
    fblk = lambda rows: pl.BlockSpec((HY_ORDER, rows, HY_CH), lambda b: (0, 0, 0), pipeline_mode=once)
    mat = pl.BlockSpec((L, L), lambda b: (0, 0), pipeline_mode=once)
    return pl.pallas_call(
        functools.partial(_hyena_kernel, L=L),
        grid=(B,),
        in_specs=[ublk(0), ublk(1), ublk(2), wblk(0), wblk(1), wblk(2), bblk(0), bblk(1), bblk(2),
                  pl.BlockSpec((HY_ORDER, HY_CH), lambda b: (0, 0)),
                  fblk(L), fblk(L), fblk(1), mat, mat, mat],
        out_specs=pl.BlockSpec((None, L, HY_CH), lambda b: (b, 0, 0)),
        out_shape=jax.ShapeDtypeStruct((B, L, HY_CH), BF16),
        compiler_params=_cparams(("arbitrary",)),
        name="hyena_mixer",
    )(u3, u3, u3, conv_w, conv_w, conv_w, conv_b, conv_b, conv_b, skip, kr, ki, kn, c_b, sp_b, spt_b)


GLA_PAIR = 2 * GLA_CHUNK
GLA_STATES = 2 * GLA_HEADS


def _gla_kernel(*refs, L, has_state, emit_state):
    it = iter(refs)
    qkvg_ref, a_ref, wa_ref, ba_ref, ng_ref = next(it), next(it), next(it), next(it), next(it)
    st0_refs = (next(it), next(it)) if has_state else None
    y_ref = next(it)
    stout_refs = (next(it), next(it)) if emit_state else None
    qd_s, ki_s, ke_s, ec_s, vb_s, vt_s, o_s, st_s = (next(it) for _ in range(8))

    npair = L // GLA_PAIR
    rc = lax.broadcasted_iota(jnp.int32, (L, 1), 0) & (GLA_CHUNK - 1)
    steps = (1, 2, 4, 8, 16, 32)

    def prefix(x):
        for d in steps:
            x = x + jnp.where(rc >= d, pltpu.roll(x, d, axis=0), 0.0)
        return x

    def suffix(x):
        for d in steps:
            x = x + jnp.where(rc < GLA_CHUNK - d, pltpu.roll(x, L - d, axis=0), 0.0)
        return x

    pre = _dot3(a_ref[...], wa_ref[...]) + ba_ref[...]
    la = (jnp.minimum(pre, 0.0) - jnp.log(1.0 + jnp.exp(-jnp.abs(pre)))) * (1.0 / GLA_TAU)
    q = qkvg_ref[:, 0:GLA_QK] * (GLA_DK ** -0.5)
    k = qkvg_ref[:, GLA_QK:2 * GLA_QK]
    for dirn in range(2):
        la_d = la[:, dirn * GLA_QK:(dirn + 1) * GLA_QK]
        if dirn == 0:
            cum = prefix(la_d)
            rest = suffix(la_d) - la_d
        else:
            cum = suffix(la_d)
            rest = prefix(la_d) - la_d
        ec = jnp.exp(cum)
        ec_s[dirn] = ec
        qd_s[dirn] = (q * ec).astype(BF16)
        ki_s[dirn] = (k * jnp.exp(-cum)).astype(BF16)
        ke_s[dirn] = (k * jnp.exp(rest)).astype(BF16)
    vb_s[...] = qkvg_ref[:, 2 * GLA_QK:2 * GLA_QK + GLA_VW].astype(BF16)

    def transpose_v(n2, carry):
        r0 = pl.multiple_of(n2 * GLA_PAIR, GLA_PAIR)
        for h in range(GLA_HEADS):
            vp = qkvg_ref[pl.ds(r0, GLA_PAIR), 2 * GLA_QK + h * GLA_DV:2 * GLA_QK + (h + 1) * GLA_DV]
            vt_s[n2, h] = vp.T.astype(BF16)
        return carry

    lax.fori_loop(0, npair, transpose_v, 0)

    if has_state:
        zero = jnp.zeros((GLA_DK, GLA_DV), F32)
        for dirn in range(2):
            for h in range(GLA_HEADS):
                s0 = st0_refs[dirn][h]
                both = jnp.concatenate([s0, zero] if h % 2 == 0 else [zero, s0], axis=0)
                st_s[dirn * GLA_HEADS + h] = both.T
    else:
        st_s[...] = jnp.zeros_like(st_s)

    lane = lax.broadcasted_iota(jnp.int32, (1, 2 * GLA_DK), 1)
    head_lanes = (lane < GLA_DK, lane >= GLA_DK)
    ci = lax.broadcasted_iota(jnp.int32, (GLA_CHUNK, GLA_CHUNK), 0)
    si = lax.broadcasted_iota(jnp.int32, (GLA_CHUNK, GLA_CHUNK), 1)
    tri = (si <= ci, si >= ci)

    def pair_step(n, carry):
        for dirn in range(2):
            n2 = n if dirn == 0 else npair - 1 - n
            r0 = pl.multiple_of(n2 * GLA_PAIR, GLA_PAIR)
            for j in ((0, 1) if dirn == 0 else (1, 0)):
                rows = pl.ds(r0 + j * GLA_CHUNK, GLA_CHUNK)
                dec_grp = pl.multiple_of(r0 + j * GLA_CHUNK + (GLA_CHUNK - 8 if dirn == 0 else 0), 8)
                dec_sub = 7 if dirn == 0 else 0
                for p in range(GLA_HEADS // 2):
                    lanes = slice(p * 2 * GLA_DK, (p + 1) * 2 * GLA_DK)
                    qd = qd_s[dirn, rows, lanes]
                    ki = ki_s[dirn, rows, lanes]
                    ke = ke_s[dirn, pl.ds(r0, GLA_PAIR), lanes]
                    dec = ec_s[dirn, pl.ds(dec_grp, 8), lanes][dec_sub:dec_sub + 1, :]
                    for hh in range(2):
                        h = 2 * p + hh
                        qm = jnp.where(head_lanes[hh], qd, jnp.zeros_like(qd))
                        att = jnp.where(tri[dirn], _dot_nt(qm, ki), 0.0)
                        vh = vb_s[rows, h * GLA_DV:(h + 1) * GLA_DV]
                        st = st_s[dirn * GLA_HEADS + h]
                        o = _dot(att.astype(BF16), vh) + _dot_nt(qm, st.astype(BF16))
                        vt = vt_s[n2, h]
                        vt = jnp.where(head_lanes[j], vt, jnp.zeros_like(vt))
                        st_s[dirn * GLA_HEADS + h] = st * dec + _dot(vt, ke)
                        o_s[dirn, rows, h * GLA_DV:(h + 1) * GLA_DV] = o
        return carry

    lax.fori_loop(0, npair, pair_step, 0)

    for h in range(GLA_HEADS):
        cols = slice(h * GLA_DV, (h + 1) * GLA_DV)
        o = _rms(o_s[0, :, cols] + o_s[1, :, cols], ng_ref[...])
        g = qkvg_ref[:, 2 * GLA_QK + GLA_VW + h * GLA_DV:2 * GLA_QK + GLA_VW + (h + 1) * GLA_DV]
        y_ref[:, cols] = (o * (g * jax.nn.sigmoid(g))).astype(BF16)
    if emit_state:
        for dirn in range(2):
            for h in range(GLA_HEADS):
                half = (h % 2) * GLA_DK
                stout_refs[dirn][h] = st_s[dirn * GLA_HEADS + h].T[half:half + GLA_DK, :]


def _gla_mixer(u3, wa_p, ba_p, norm_g, st0, layer, L, emit_state):
    B = u3.shape[0]
    has_state = st0 is not None
    head_state = (GLA_HEADS, GLA_DK, GLA_DV)
    in_specs = [pl.BlockSpec((None, L, 1536), lambda b: (b, 0, U_GLA // 1536)),
                pl.BlockSpec((None, L, 256), lambda b: (b, 0, U_GLA_A // 256)),
                pl.BlockSpec((256, 2 * GLA_QK), lambda b: (0, 0)),
                pl.BlockSpec((1, 2 * GLA_QK), lambda b: (0, 0)),
                pl.BlockSpec((1, GLA_DV), lambda b: (0, 0))]
    args = [u3, u3, wa_p, ba_p, norm_g]
    if has_state:
        in_specs += [pl.BlockSpec((None, None) + head_state, lambda b: (b, layer, 0, 0, 0))] * 2
        args += list(st0)
    out_specs = [pl.BlockSpec((None, L, GLA_VW), lambda b: (b, 0, 0))]
    out_shape = [jax.ShapeDtypeStruct((B, L, GLA_VW), BF16)]
    if emit_state:
        out_specs += [pl.BlockSpec((None,) + head_state, lambda b: (b, 0, 0, 0))] * 2
        out_shape += [jax.ShapeDtypeStruct((B,) + head_state, F32)] * 2
    npair = L // GLA_PAIR
    res = pl.pallas_call(
        functools.partial(_gla_kernel, L=L, has_state=has_state, emit_state=emit_state),
        grid=(B,),
        in_specs=in_specs,
        out_specs=out_specs,
        out_shape=out_shape,
        scratch_shapes=[pltpu.VMEM((2, L, GLA_QK), BF16),
                        pltpu.VMEM((2, L, GLA_QK), BF16),
                        pltpu.VMEM((2, L, GLA_QK), BF16),
                        pltpu.VMEM((2, L, GLA_QK), F32),
                        pltpu.VMEM((L, GLA_VW), BF16),
                        pltpu.VMEM((npair, GLA_HEADS, GLA_DV, GLA_PAIR), BF16),
                        pltpu.VMEM((2, L, GLA_VW), F32),
                        pltpu.VMEM((GLA_STATES, GLA_DV, 2 * GLA_DK), F32)],
        compiler_params=_cparams(("arbitrary",)),
        name="gla_mixer",
    )(*args)
    return (res[0], res[1], res[2]) if emit_state else (res[0], None, None)


MLA_HW = 256
MLA_TQ = 512
MLA_KCH = 256


def _rope(x, cos, sin):
    lane = lax.broadcasted_iota(jnp.int32, x.shape, 1)
    partner = jnp.where((lane & 16) == 0, pltpu.roll(x, 112, axis=1), pltpu.roll(x, 16, axis=1))
    return x * cos + partner * sin


def _mla_kernel(*refs, L, Lk, tq, rope, emit):
    it = iter(refs)
    u_ref, qn_ref, wq_ref, kvn_ref, wkv_ref = (next(it) for _ in range(5))
    cos_ref, sin_ref, cckv_ref, ckr_ref = (next(it) for _ in range(4)) if rope else (None,) * 4
    y_ref = next(it)
    ckv_out, kr_out = (next(it), next(it)) if emit else (None, None)
    kf_s, v_s = next(it), next(it)

    qi = pl.program_id(1)

    def project(keys, rope_part, r0):
        kv = _dot(keys, wkv_ref[...])
        for h in range(MLA_HEADS):
            kf_s[h, r0:r0 + MLA_KCH, 0:MLA_NOPE] = kv[:, h * 256:h * 256 + MLA_NOPE].astype(BF16)
            kf_s[h, r0:r0 + MLA_KCH, MLA_NOPE:MLA_HW] = rope_part
            v_s[h, r0:r0 + MLA_KCH, :] = kv[:, h * 256 + MLA_NOPE:(h + 1) * 256].astype(BF16)

    @pl.when(qi == 0)
    def _():
        for r0 in range(0, L, MLA_KCH):
            ckvn = _rms(u_ref[r0:r0 + MLA_KCH, MLA_Q_RANK:MLA_Q_RANK + MLA_KV_RANK], kvn_ref[...])
            kr = u_ref[r0:r0 + MLA_KCH, 640:768]
            if emit:
                ckv_out[r0:r0 + MLA_KCH, :] = ckvn
                kr_out[r0:r0 + MLA_KCH, :] = kr[:, 0:MLA_ROPE]
            if rope:
                kr = _rope(kr, cos_ref[r0:r0 + MLA_KCH, :], sin_ref[r0:r0 + MLA_KCH, :])
            project(ckvn.astype(BF16), kr.astype(BF16), r0)
        if rope:
            for r0 in range(0, Lk - L, MLA_KCH):
                ckr = ckr_ref[r0:r0 + MLA_KCH, :]
                ckr = jnp.concatenate([ckr, jnp.zeros_like(ckr)], axis=1)
                project(cckv_ref[r0:r0 + MLA_KCH, :].astype(BF16), ckr.astype(BF16), L + r0)

    q0 = pl.multiple_of(qi * tq, tq)
    cqn = _rms(u_ref[pl.ds(q0, tq), 0:MLA_Q_RANK], qn_ref[...]).astype(BF16)
    q = _dot(cqn, wq_ref[...]) * ((MLA_NOPE + MLA_ROPE) ** -0.5)
    if rope:
        cos = cos_ref[pl.ds(q0, tq), :]
        sin = sin_ref[pl.ds(q0, tq), :]
    for h in range(MLA_HEADS):
        q_nope = q[:, h * MLA_HW:h * MLA_HW + MLA_NOPE]
        q_rope = q[:, h * MLA_HW + MLA_NOPE:(h + 1) * MLA_HW]
        if rope:
            q_rope = _rope(q_rope, cos, sin)
        qh = jnp.concatenate([q_nope, q_rope], axis=1).astype(BF16)
        s = _dot_nt(qh, kf_s[h])
        p = jnp.exp(s - jnp.max(s, axis=-1, keepdims=True))
        denom = jnp.sum(p, axis=-1, keepdims=True)
        o = _dot(p.astype(BF16), v_s[h])
        y_ref[:, h * MLA_V:(h + 1) * MLA_V] = (o / denom).astype(BF16)


def _mla_mixer(u3, q_norm, wq_p, kv_norm, wkv, rope_args, layer, L, emit):
    B = u3.shape[0]
    rope = rope_args is not None
    Lk = L + (rope_args[2].shape[2] if rope else 0)
    tq = min(MLA_TQ, L)
    const = lambda shape: pl.BlockSpec(shape, lambda b, i: (0,) * len(shape))
    in_specs = [pl.BlockSpec((None, L, 768), lambda b, i: (b, 0, U_MLA // 768)),
                const((1, MLA_Q_RANK)), const((MLA_Q_RANK, MLA_HEADS * MLA_HW)),
                const((1, MLA_KV_RANK)), const((MLA_KV_RANK, MLA_HEADS * 256))]
    args = [u3, q_norm, wq_p, kv_norm, wkv]
    if rope:
        cos, sin, cckv, ckr = rope_args
        in_specs += [const((L, 128)), const((L, 128)),
                     pl.BlockSpec((None, None, Lk - L, MLA_KV_RANK), lambda b, i: (b, layer, 0, 0)),
                     pl.BlockSpec((None, None, Lk - L, MLA_ROPE), lambda b, i: (b, layer, 0, 0))]
        args += [cos, sin, cckv, ckr]
    out_specs = [pl.BlockSpec((None, tq, MLA_HEADS * MLA_V), lambda b, i: (b, i, 0))]
    out_shape = [jax.ShapeDtypeStruct((B, L, MLA_HEADS * MLA_V), BF16)]
    if emit:
        out_specs += [pl.BlockSpec((None, L, MLA_KV_RANK), lambda b, i: (b, 0, 0)),
                      pl.BlockSpec((None, L, MLA_ROPE), lambda b, i: (b, 0, 0))]
        out_shape += [jax.ShapeDtypeStruct((B, L, MLA_KV_RANK), F32),
                      jax.ShapeDtypeStruct((B, L, MLA_ROPE), F32)]
    res = pl.pallas_call(
        functools.partial(_mla_kernel, L=L, Lk=Lk, tq=tq, rope=rope, emit=emit),
        grid=(B, L // tq),
        in_specs=in_specs,
        out_specs=out_specs,
        out_shape=out_shape,
        scratch_shapes=[pltpu.VMEM((MLA_HEADS, Lk, MLA_HW), BF16),
                        pltpu.VMEM((MLA_HEADS, Lk, MLA_V), BF16)],
        compiler_params=_cparams(("arbitrary", "arbitrary")),
        name="mla_mixer",
    )(*args)
    return res if emit else (res[0], None, None)


def _outproj_kernel(x_ref, g_ref, yh_ref, yg_ref, ym_ref, wh_ref, wg_ref, wm_ref, o_ref):
    y = _dot(yh_ref[...], wh_ref[...]) + _dot(yg_ref[...], wg_ref[...]) + _dot(ym_ref[...], wm_ref[...])
    o_ref[...] = x_ref[...] + g_ref[...] * y


def _out_projection(x2d, mod, y_hy, y_gla, y_mla, w_out_b, layer, row_of_tile, tm):
    rows = x2d.shape[0]
    once = pl.Buffered(1)
    return pl.pallas_call(
        _outproj_kernel,
        grid=(rows // tm,),
        in_specs=[pl.BlockSpec((tm, D_MODEL), lambda i: (i, 0)),
                  _mod_spec(2, row_of_tile),
                  pl.BlockSpec((tm, HY_CH), lambda i: (i, 0)),
                  pl.BlockSpec((tm, GLA_VW), lambda i: (i, 0)),
                  pl.BlockSpec((tm, MLA_HEADS * MLA_V), lambda i: (i, 0)),
                  pl.BlockSpec((None, HY_CH, D_MODEL), lambda i: (layer, 0, 0), pipeline_mode=once),
                  pl.BlockSpec((None, GLA_VW, D_MODEL), lambda i: (layer, 1, 0), pipeline_mode=once),
                  pl.BlockSpec((None, MLA_HEADS * MLA_V, D_MODEL), lambda i: (layer, 1, 0), pipeline_mode=once)],
        out_specs=pl.BlockSpec((tm, D_MODEL), lambda i: (i, 0)),
        out_shape=jax.ShapeDtypeStruct((rows, D_MODEL), F32),
        compiler_params=_cparams(("arbitrary",)),
        name="out_projection",
    )(x2d, mod, y_hy, y_gla, y_mla, w_out_b, w_out_b, w_out_b)


FFN_TF = 512


def _ffn_kernel(x_ref, ln_ref, sh_ref, sc_ref, g_ref, wg_ref, wu_ref, wo_ref, lnf_ref, o_ref,
                h_ref, acc_ref, *, final):
    f = pl.program_id(1)

    @pl.when(f == 0)
    def _():
        _adaln_to_bf16(x_ref, ln_ref, sc_ref, sh_ref, h_ref)
        acc_ref[...] = jnp.zeros_like(acc_ref)

    h = h_ref[...]
    gate = _dot(h, wg_ref[...])
    up = _dot(h, wu_ref[...])
    act = (gate * jax.nn.sigmoid(gate) * up).astype(BF16)
    acc_ref[...] += _dot(act, wo_ref[...])

    @pl.when(f == pl.num_programs(1) - 1)
    def _():
        def body(i, carry):
            rows = pl.ds(pl.multiple_of(i * ADALN_ROWS, ADALN_ROWS), ADALN_ROWS)
            x = x_ref[rows, :] + g_ref[...] * acc_ref[rows, :]
            o_ref[rows, :] = _rms(x, lnf_ref[...]) if final else x
            return carry

        lax.fori_loop(0, x_ref.shape[0] // ADALN_ROWS, body, 0, unroll=4)


def _ffn(x2d, ln, mod, w_in_b, w_out_b, layer, ln_final, row_of_tile, tm, final):
    rows = x2d.shape[0]
    nf = D_FF // FFN_TF
    return pl.pallas_call(
        functools.partial(_ffn_kernel, final=final),
        grid=(rows // tm, nf),
        in_specs=[pl.BlockSpec((tm, D_MODEL), lambda i, f: (i, 0)),
                  pl.BlockSpec((1, D_MODEL), lambda i, f: (0, 0)),
                  _mod_spec(3, row_of_tile),
                  _mod_spec(4, row_of_tile),
                  _mod_spec(5, row_of_tile),
                  pl.BlockSpec((None, D_MODEL, FFN_TF), lambda i, f: (layer, 0, f)),
                  pl.BlockSpec((None, D_MODEL, FFN_TF), lambda i, f: (layer, 0, nf + f)),
                  pl.BlockSpec((None, FFN_TF, D_MODEL), lambda i, f: (layer, f, 0)),
                  pl.BlockSpec((1, D_MODEL), lambda i, f: (0, 0))],
        out_specs=pl.BlockSpec((tm, D_MODEL), lambda i, f: (i, 0)),
        out_shape=jax.ShapeDtypeStruct((rows, D_MODEL), F32),
        scratch_shapes=[pltpu.VMEM((tm, D_MODEL), BF16), pltpu.VMEM((tm, D_MODEL), F32)],
        compiler_params=_cparams(("arbitrary", "arbitrary")),
        name="ffn",
    )(x2d, ln, mod, mod, mod, w_in_b, w_in_b, w_out_b, ln_final)


def _dft_tables(L):
    k = np.arange(L)
    ang = np.pi * ((k[:, None] * k[None, :]) % (2 * L)) / L
    c = np.cos(ang)
    s = np.sin(ang)
    sp = s.copy()
    sp[0, :] = 1.0 - 2.0 * (k % 2)
    as32 = lambda a: jnp.asarray(a.astype(np.float32))
    return as32(c), as32(s), as32(sp), as32(sp.T.copy())


def _hyena_consts(L):
    c, s, sp, spt = _dft_tables(L)
    c_hi, c_lo = _split(c)
    s_hi, s_lo = _split(s)
    t = jnp.linspace(0.0, 1.0, L, dtype=F32)[:, None]
    w = 2.0 * math.pi * jnp.arange(L, dtype=F32)[:, None] / L
    f = jnp.linspace(1e-4, HY_BANDS - 1, HY_BANDS, dtype=F32)
    zpos = jnp.concatenate([t, jnp.cos(f * w), -jnp.sin(f * w)], axis=-1)
    zpos = jnp.pad(zpos, ((0, 0), (0, 64 - HY_EMB)))
    min_decay = math.log(HY_TARGET) / HY_SLOW_DECAY
    max_decay = math.log(HY_TARGET) / HY_FAST_DECAY
    delta = jnp.abs(jnp.linspace(min_decay, max_decay, HY_CH, dtype=F32))
    decay = jnp.exp(-t * delta)
    filt_consts = (zpos, decay, c_hi, c_lo, s_hi, s_lo)
    main_consts = (c_hi, sp.astype(BF16), spt.astype(BF16))
    return filt_consts, main_consts


def _rope_tables(L):
    t = jnp.arange(L)
    half = MLA_ROPE // 2
    inv = ROPE_THETA ** (-jnp.arange(0, half, 2, dtype=F32) / half)
    ang_r = (t // GRID_W).astype(F32)[:, None] * inv
    ang_c = (t % GRID_W).astype(F32)[:, None] * inv
    cr, sr, cc, sc = jnp.cos(ang_r), jnp.sin(ang_r), jnp.cos(ang_c), jnp.sin(ang_c)
    cos = jnp.concatenate([cr, cr, cc, cc, jnp.ones((L, 128 - MLA_ROPE), F32)], axis=1)
    sin = jnp.concatenate([-sr, sr, -sc, sc, jnp.zeros((L, 128 - MLA_ROPE), F32)], axis=1)
    return cos, sin


def _prep_w_in_tail(w):
    z = lambda n: jnp.zeros((DEPTH, D_MODEL, n), w.dtype)
    mla0 = IN_MAIN + 2 * GLA_LOWRANK
    return jnp.concatenate([w[:, :, mla0:], z(64), w[:, :, IN_MAIN:mla0], z(256 - 2 * GLA_LOWRANK)],
                           axis=2).astype(BF16)


def _prep_w_uq(w):
    w = w.reshape(MLA_Q_RANK, MLA_HEADS, MLA_NOPE + MLA_ROPE)
    w = jnp.pad(w, ((0, 0), (0, 0), (0, MLA_HW - MLA_NOPE - MLA_ROPE)))
    return w.reshape(MLA_Q_RANK, MLA_HEADS * MLA_HW).astype(BF16)


def _prep_gla_decay(wa_f, ba_f, wa_b, ba_b):
    wa = jnp.zeros((256, 2 * GLA_QK), F32)
    wa = wa.at[0:GLA_LOWRANK, 0:GLA_QK].set(wa_f)
    wa = wa.at[GLA_LOWRANK:2 * GLA_LOWRANK, GLA_QK:].set(wa_b)
    return wa, jnp.concatenate([ba_f, ba_b])[None, :]


def _trunk_layer(x2d, B, L, mod, row_of_tile, lw, consts, ctx, final, ln_final):
    tm = 1024
    tm_out = 512
    u = _in_projection(x2d, lw['ln_mix'], mod, lw['w_in'], lw['w_in_tail'], lw['layer'], row_of_tile(tm), tm)
    u3 = u.reshape(B, L, U_W)
    filt_consts, main_consts = consts['hyena']
    filt = _hyena_filters(L, filt_consts, lw['hy_w1'], lw['hy_b1'], lw['hy_freq'], lw['hy_w2'],
                          lw['hy_b2'], lw['hy_w3'])
    y_hy = _hyena_mixer(u3, lw['hy_conv_w'], lw['hy_conv_b'], lw['hy_skip'], filt, main_consts, L)
    if ctx is None:
        y_gla, s_f, s_b = _gla_mixer(u3, lw['gla_wa'], lw['gla_ba'], lw['gla_norm'], None, lw['layer'], L, True)
        y_mla, ckv, krope = _mla_mixer(u3, lw['mla_q_norm'], lw['mla_w_uq'], lw['mla_kv_norm'],
                                       lw['mla_w_ukv'], None, lw['layer'], L, True)
        extras = (ckv, krope, s_f, s_b)
    else:
        cache_ckv, cache_krope, s0f, s0b = ctx
        y_gla, _, _ = _gla_mixer(u3, lw['gla_wa'], lw['gla_ba'], lw['gla_norm'], (s0f, s0b), lw['layer'], L, False)
        cos, sin = consts['rope']
        y_mla, _, _ = _mla_mixer(u3, lw['mla_q_norm'], lw['mla_w_uq'], lw['mla_kv_norm'], lw['mla_w_ukv'],
                                 (cos, sin, cache_ckv, cache_krope), lw['layer'], L, False)
        extras = None
    rows = B * L
    x2d = _out_projection(x2d, mod, y_hy.reshape(rows, -1), y_gla.reshape(rows, -1), y_mla.reshape(rows, -1),
                          lw['w_out'], lw['layer'], row_of_tile(tm_out), tm_out)
    x2d = _ffn(x2d, lw['ln_ffn'], mod, lw['w_ffn_in'], lw['w_ffn_out'], lw['layer'], ln_final, row_of_tile(tm_out),
               tm_out, final)
    return x2d, extras


def kernel(x_prompt, x_sample, cache_mla_ckv, cache_mla_krope, state_gla_fwd, state_gla_bwd, c, c_ctx, w_mod, b_mod, ln_mix, w_in, hy_conv_w, hy_conv_b, hy_filt_w1, hy_filt_b1, hy_filt_freq, hy_filt_w2, hy_filt_b2, hy_filt_w3, hy_skip, gla_wa_f, gla_ba_f, gla_wa_b, gla_ba_b, gla_norm, mla_q_norm, mla_w_uq, mla_kv_norm, mla_w_ukv, w_out, ln_ffn, w_ffn_in, w_ffn_out, ln_final):
    Bc, Lc, _ = x_prompt.shape
    Bl, Ll, _ = x_sample.shape
    assert 1 + Bl <= MOD_ROWS

    c_all = jnp.concatenate([c_ctx[None, :], c, jnp.zeros((MOD_ROWS - 1 - Bl, D_MODEL), F32)], axis=0)
    mod_all = _modulation(c_all, w_mod, b_mod).reshape(DEPTH, MOD_ROWS, 6, 1, D_MODEL)

    consts_ctx = {'hyena': _hyena_consts(Lc)}
    consts_lat = {'hyena': _hyena_consts(Ll), 'rope': _rope_tables(Ll)}
    ctx_rows = lambda tm: (lambda i: 0)
    lat_rows = lambda tm: (lambda i: 1 + (i * tm) // Ll)
    lnf = ln_final[None, :]

    x_ctx = x_prompt.reshape(Bc * Lc, D_MODEL)
    x_lat = x_sample.reshape(Bl * Ll, D_MODEL)
    w_in_b = w_in.astype(BF16)
    w_in_tail = _prep_w_in_tail(w_in)
    w_out_b = w_out.astype(BF16)
    w_ffn_in_b = w_ffn_in.astype(BF16)
    w_ffn_out_b = w_ffn_out.astype(BF16)
    ckv_l, krope_l, sf_l, sb_l = [], [], [], []
    for l in range(DEPTH):
        wa, ba = _prep_gla_decay(gla_wa_f[l], gla_ba_f[l], gla_wa_b[l], gla_ba_b[l])
        lw = {
            'layer': l, 'ln_mix': ln_mix[l][None, :], 'w_in': w_in_b, 'w_in_tail': w_in_tail,
            'hy_conv_w': hy_conv_w[l], 'hy_conv_b': hy_conv_b[l][None, :],
            'hy_w1': jnp.pad(hy_filt_w1[l], ((0, 64 - HY_EMB), (0, 0))), 'hy_b1': hy_filt_b1[l][None, :],
            'hy_freq': hy_filt_freq[l], 'hy_w2': hy_filt_w2[l], 'hy_b2': hy_filt_b2[l][None, :],
            'hy_w3': hy_filt_w3[l], 'hy_skip': hy_skip[l],
            'gla_wa': wa, 'gla_ba': ba, 'gla_norm': gla_norm[l][None, :],
            'mla_q_norm': mla_q_norm[l][None, :], 'mla_w_uq': _prep_w_uq(mla_w_uq[l]),
            'mla_kv_norm': mla_kv_norm[l][None, :], 'mla_w_ukv': mla_w_ukv[l].astype(BF16),
            'w_out': w_out_b, 'ln_ffn': ln_ffn[l][None, :],
            'w_ffn_in': w_ffn_in_b, 'w_ffn_out': w_ffn_out_b,
        }
        final = l == DEPTH - 1
        mod = mod_all[l]
        x_ctx, (ckv, krope, s_f, s_b) = _trunk_layer(x_ctx, Bc, Lc, mod, ctx_rows, lw, consts_ctx, None,
                                                     final, lnf)
        ckv_l.append(ckv)
        krope_l.append(krope)
        sf_l.append(s_f)
        sb_l.append(s_b)
        ctx = (cache_mla_ckv, cache_mla_krope, state_gla_fwd, state_gla_bwd)
        x_lat, _ = _trunk_layer(x_lat, Bl, Ll, mod, lat_rows, lw, consts_lat, ctx, final, lnf)
    return (x_ctx.reshape(Bc, Lc, D_MODEL), x_lat.reshape(Bl, Ll, D_MODEL),
            jnp.stack(ckv_l, axis=1), jnp.stack(krope_l, axis=1),
            jnp.stack(sf_l, axis=1), jnp.stack(sb_l, axis=1))
```

```python
import functools
import math

import numpy as np
import jax
import jax.numpy as jnp
from jax import lax
from jax.experimental import pallas as pl
from jax.experimental.pallas import tpu as pltpu

F32 = jnp.float32
BF16 = jnp.bfloat16

D_MODEL = 2048
DEPTH = 2
GRID_W = 64
EPS = 1e-6
HY_CH = 512
HY_ORDER = 2
HY_BANDS = 16
HY_EMB = 1 + 2 * HY_BANDS
HY_FF = 64
HY_FAST_DECAY = 0.3
HY_SLOW_DECAY = 1.5
HY_TARGET = 1e-2
HY_IN = 3 * HY_CH
GLA_HEADS = 4
GLA_DK = 64
GLA_DV = 128
GLA_LOWRANK = 16
GLA_TAU = 16.0
GLA_CHUNK = 64
GLA_QK = GLA_HEADS * GLA_DK
GLA_VW = GLA_HEADS * GLA_DV
MLA_HEADS = 8
MLA_Q_RANK = 384
MLA_KV_RANK = 256
MLA_NOPE = 128
MLA_ROPE = 64
MLA_V = 128
ROPE_THETA = 10000.0
D_FF = -(-8 * D_MODEL // (3 * 256)) * 256

U_HY = 0
U_GLA = 1536
U_MLA = 3072
U_GLA_A = 3840
U_W = 4096
MOD_ROWS = 16

VMEM_LIMIT_V7X = 48 * 1024 * 1024


def _cparams(sem):
    return pltpu.CompilerParams(dimension_semantics=sem, vmem_limit_bytes=VMEM_LIMIT_V7X)


def _dot(a, b):
    return jnp.dot(a, b, preferred_element_type=F32)


def _dot_nt(a, b):
    return lax.dot_general(a, b, (((1,), (1,)), ((), ())), preferred_element_type=F32)


def _split(x):
    hi = x.astype(BF16)
    lo = (x - hi.astype(F32)).astype(BF16)
    return hi, lo


def _dot3(a, b):
    a_hi, a_lo = _split(a)
    b_hi, b_lo = _split(b)
    return _dot(a_hi, b_hi) + (_dot(a_lo, b_hi) + _dot(a_hi, b_lo))


def _rms(x, g):
    ms = jnp.mean(x * x, axis=-1, keepdims=True)
    return x * lax.rsqrt(ms + EPS) * g


ADALN_ROWS = 16


def _adaln_to_bf16(x_ref, ln_ref, sc_ref, sh_ref, h_ref):
    gain = ln_ref[...] * (1.0 + sc_ref[...])
    shift = sh_ref[...]

    def body(i, carry):
        r = pl.multiple_of(i * ADALN_ROWS, ADALN_ROWS)
        x = x_ref[pl.ds(r, ADALN_ROWS), :]
        ms = jnp.mean(x * x, axis=-1, keepdims=True)
        h_ref[pl.ds(r, ADALN_ROWS), :] = (x * lax.rsqrt(ms + EPS) * gain + shift).astype(BF16)
        return carry

    lax.fori_loop(0, x_ref.shape[0] // ADALN_ROWS, body, 0, unroll=4)


def _mod_kernel(c_ref, w_ref, b_ref, o_ref):
    c = c_ref[...]
    s = (c * jax.nn.sigmoid(c)).astype(BF16)
    o_ref[...] = _dot(s, w_ref[...].astype(BF16)) + b_ref[...]


def _modulation(c_all, w_mod, b_mod):
    tn = 1024
    n6 = 6 * D_MODEL
    return pl.pallas_call(
        _mod_kernel,
        grid=(DEPTH, n6 // tn),
        in_specs=[pl.BlockSpec((MOD_ROWS, D_MODEL), lambda l, j: (0, 0)),
                  pl.BlockSpec((None, D_MODEL, tn), lambda l, j: (l, 0, j)),
                  pl.BlockSpec((None, 1, tn), lambda l, j: (l, 0, j))],
        out_specs=pl.BlockSpec((None, MOD_ROWS, tn), lambda l, j: (l, 0, j)),
        out_shape=jax.ShapeDtypeStruct((DEPTH, MOD_ROWS, n6), F32),
        compiler_params=_cparams(("arbitrary", "arbitrary")),
        name="modulation",
    )(c_all, w_mod, b_mod.reshape(DEPTH, 1, n6))


def _mod_spec(chunk, row_of_tile):
    return pl.BlockSpec((None, None, 1, D_MODEL), lambda i, *_: (row_of_tile(i), chunk, 0, 0))


IN_TN = 1024
IN_MAIN = 3072


def _inproj_kernel(x_ref, ln_ref, sh_ref, sc_ref, w_ref, wt_ref, o_ref, h_ref):
    j = pl.program_id(1)

    @pl.when(j == 0)
    def _():
        _adaln_to_bf16(x_ref, ln_ref, sc_ref, sh_ref, h_ref)

    @pl.when(j < IN_MAIN // IN_TN)
    def _():
        o_ref[...] = _dot(h_ref[...], w_ref[...])

    @pl.when(j >= IN_MAIN // IN_TN)
    def _():
        o_ref[...] = _dot(h_ref[...], wt_ref[...])


def _in_projection(x2d, ln, mod, w_in_b, w_tail, layer, row_of_tile, tm):
    rows = x2d.shape[0]
    n_main = IN_MAIN // IN_TN
    return pl.pallas_call(
        _inproj_kernel,
        grid=(rows // tm, U_W // IN_TN),
        in_specs=[pl.BlockSpec((tm, D_MODEL), lambda i, j: (i, 0)),
                  pl.BlockSpec((1, D_MODEL), lambda i, j: (0, 0)),
                  _mod_spec(0, row_of_tile),
                  _mod_spec(1, row_of_tile),
                  pl.BlockSpec((None, D_MODEL, IN_TN), lambda i, j: (layer, 0, jnp.minimum(j, n_main - 1))),
                  pl.BlockSpec((None, D_MODEL, U_W - IN_MAIN), lambda i, j: (layer, 0, 0))],
        out_specs=pl.BlockSpec((tm, IN_TN), lambda i, j: (i, j)),
        out_shape=jax.ShapeDtypeStruct((rows, U_W), F32),
        scratch_shapes=[pltpu.VMEM((tm, D_MODEL), BF16)],
        compiler_params=_cparams(("arbitrary", "arbitrary")),
        name="in_projection",
    )(x2d, ln, mod, mod, w_in_b, w_tail)


def _filter_kernel(z_ref, w1_ref, b1_ref, fr_ref, w2_ref, b2_ref, w3f_ref, w3b_ref, dec_ref,
                   chi_ref, clo_ref, shi_ref, slo_ref, kr_ref, ki_ref, kn_ref, *, L):
    h = jnp.sin(fr_ref[0:1, :] * (_dot3(z_ref[...], w1_ref[...]) + b1_ref[...]))
    h = jnp.sin(fr_ref[1:2, :] * (_dot3(h, w2_ref[...]) + b2_ref[...]))
    dec = dec_ref[...]
    row = lax.broadcasted_iota(jnp.int32, (L, 1), 0)
    kf = _dot3(h, w3f_ref[...]) * dec
    kb = jnp.where(row == 0, 0.0, _dot3(h, w3b_ref[...]) * dec)
    p_hi, p_lo = _split(kf + kb)
    m_hi, m_lo = _split(kf - kb)
    chi, clo = chi_ref[...], clo_ref[...]
    shi, slo = shi_ref[...], slo_ref[...]
    kc = _dot(chi, p_hi) + (_dot(clo, p_hi) + _dot(chi, p_lo))
    ks = _dot(shi, m_hi) + (_dot(slo, m_hi) + _dot(shi, m_lo))
    sign = jnp.where((row & 1) == 0, 1.0, -1.0)
    kn = jnp.sum((kf + kb) * sign, axis=0, keepdims=True)
    kr_ref[...] = kc * jnp.where(row == 0, 0.5 / L, 1.0 / L)
    ki_ref[...] = ks * (-1.0 / L)
    kn_ref[...] = kn * (0.5 / L)


def _hyena_filters(L, consts, w1p, b1, freq, w2, b2, w3):
    zpos, decay, c_hi, c_lo, s_hi, s_lo = consts
    full = lambda shape: pl.BlockSpec(shape, lambda o: (0,) * len(shape))
    return pl.pallas_call(
        functools.partial(_filter_kernel, L=L),
        grid=(HY_ORDER,),
        in_specs=[full((L, 64)), full((64, HY_FF)), full((1, HY_FF)), full((2, HY_FF)),
                  full((HY_FF, HY_FF)), full((1, HY_FF)),
                  pl.BlockSpec((HY_FF, HY_CH), lambda o: (0, o)),
                  pl.BlockSpec((HY_FF, HY_CH), lambda o: (0, HY_ORDER + o)),
                  full((L, HY_CH)), full((L, L)), full((L, L)), full((L, L)), full((L, L))],
        out_specs=[pl.BlockSpec((None, L, HY_CH), lambda o: (o, 0, 0)),
                   pl.BlockSpec((None, L, HY_CH), lambda o: (o, 0, 0)),
                   pl.BlockSpec((None, 1, HY_CH), lambda o: (o, 0, 0))],
        out_shape=[jax.ShapeDtypeStruct((HY_ORDER, L, HY_CH), F32),
                   jax.ShapeDtypeStruct((HY_ORDER, L, HY_CH), F32),
                   jax.ShapeDtypeStruct((HY_ORDER, 1, HY_CH), F32)],
        compiler_params=_cparams(("arbitrary",)),
        name="hyena_filters",
    )(zpos, w1p, b1, freq, w2, b2, w3, w3, decay, c_hi, c_lo, s_hi, s_lo)


HY_CG = 256


def _hyena_kernel(v_ref, x1_ref, x2_ref, wv_ref, w1_ref, w2_ref, bv_ref, b1_ref, b2_ref, skip_ref,
                  kr_ref, ki_ref, kn_ref, c_ref, sp_ref, spt_ref, o_ref, *, L):
    row = lax.broadcasted_iota(jnp.int32, (L, 1), 0)
    first = row == 0
    last = row == L - 1
    groups = [slice(g * HY_CG, (g + 1) * HY_CG) for g in range(HY_CH // HY_CG)]

    def short_conv(u_ref, w_ref, b_ref, cols):
        u = u_ref[:, cols]
        prev = jnp.where(first, 0.0, pltpu.roll(u, 1, axis=0))
        nxt = jnp.where(last, 0.0, pltpu.roll(u, L - 1, axis=0))
        return prev * w_ref[0:1, cols] + u * w_ref[1:2, cols] + nxt * w_ref[2:3, cols] + b_ref[:, cols]

    gate_refs = ((x1_ref, w1_ref, b1_ref), (x2_ref, w2_ref, b2_ref))
    z = [short_conv(v_ref, wv_ref, bv_ref, cols) for cols in groups]
    for o in range(HY_ORDER):
        zb = [zz.astype(BF16) for zz in z]
        a = [_dot(c_ref[...], x) for x in zb]
        s = [_dot(sp_ref[...], x) for x in zb]
        conv = []
        for g, cols in enumerate(groups):
            kr = kr_ref[o, :, cols]
            ki = ki_ref[o, :, cols]
            qt = a[g] * kr + s[g] * ki
            qb = s[g] * jnp.where(first, kn_ref[o, :, cols], kr) - a[g] * ki
            conv.append(_dot(c_ref[...], qt.astype(BF16)) + _dot(spt_ref[...], qb.astype(BF16)))
        z = [short_conv(*gate_refs[o], cols) * (conv[g] + z[g] * skip_ref[o:o + 1, cols])
             for g, cols in enumerate(groups)]
    for g, cols in enumerate(groups):
        o_ref[:, cols] = z[g].astype(BF16)


def _hyena_mixer(u3, conv_w, conv_b, skip, filt, dft, L):
    B = u3.shape[0]
    kr, ki, kn = filt
    c_b, sp_b, spt_b = dft
    once = pl.Buffered(1)
    ublk = lambda part: pl.BlockSpec((None, L, HY_CH), lambda b: (b, 0, part))
    wblk = lambda part: pl.BlockSpec((3, HY_CH), lambda b: (0, part))
    bblk = lambda part: pl.BlockSpec((1, HY_CH), lambda b: (0, part))
    fblk = lambda rows: pl.BlockSpec((HY_ORDER, rows, HY_CH), lambda b: (0, 0, 0), pipeline_mode=once)
    mat = pl.BlockSpec((L, L), lambda b: (0, 0), pipeline_mode=once)
    return pl.pallas_call(
        functools.partial(_hyena_kernel, L=L),
        grid=(B,),
        in_specs=[ublk(0), ublk(1), ublk(2), wblk(0), wblk(1), wblk(2), bblk(0), bblk(1), bblk(2),
                  pl.BlockSpec((HY_ORDER, HY_CH), lambda b: (0, 0)),
                  fblk(L), fblk(L), fblk(1), mat, mat, mat],
        out_specs=pl.BlockSpec((None, L, HY_CH), lambda b: (b, 0, 0)),
        out_shape=jax.ShapeDtypeStruct((B, L, HY_CH), BF16),
        compiler_params=_cparams(("arbitrary",)),
        name="hyena_mixer",
    )(u3, u3, u3, conv_w, conv_w, conv_w, conv_b, conv_b, conv_b, skip, kr, ki, kn, c_b, sp_b, spt_b)


GLA_PAIR = 2 * GLA_CHUNK
GLA_STATES = 2 * GLA_HEADS
GLA_SCAN = 256


def _gla_kernel(*refs, L, has_state, emit_state):
    it = iter(refs)
    qkvg_ref, a_ref, wa_ref, ba_ref, ng_ref, scan_ref = (next(it) for _ in range(6))
    st0_refs = (next(it), next(it)) if has_state else None
    y_ref = next(it)
    stout_refs = (next(it), next(it)) if emit_state else None
    qd_s, ki_s, ke_s, qi_s, ku_s, et_s, vb_s, vt_s, o_s, st_s = (next(it) for _ in range(10))

    npair = L // GLA_PAIR
    pre = _dot3(a_ref[...], wa_ref[...]) + ba_ref[...]
    la = (jnp.minimum(pre, 0.0) - jnp.log(1.0 + jnp.exp(-jnp.abs(pre)))) * (1.0 / GLA_TAU)

    scan = scan_ref[...]
    pfx, tot = [], []
    for r in range(0, L, GLA_SCAN):
        x = la[r:r + GLA_SCAN]
        hi = x.astype(BF16)
        r1 = x - hi.astype(F32)
        mid = r1.astype(BF16)
        lo = (r1 - mid.astype(F32)).astype(BF16)
        res = _dot(scan, hi) + (_dot(scan, mid) + _dot(scan, lo))
        pfx.append(res[:GLA_SCAN])
        tot.append(res[GLA_SCAN:])
    pfx = jnp.concatenate(pfx, axis=0) if len(pfx) > 1 else pfx[0]
    tot = jnp.concatenate(tot, axis=0) if len(tot) > 1 else tot[0]

    upper = (lax.broadcasted_iota(jnp.int32, (L, 1), 0) & GLA_CHUNK) != 0
    q = qkvg_ref[:, 0:GLA_QK] * (GLA_DK ** -0.5)
    k = qkvg_ref[:, GLA_QK:2 * GLA_QK]
    for dirn in range(2):
        cols = slice(dirn * GLA_QK, (dirn + 1) * GLA_QK)
        if dirn == 0:
            cum = pfx[:, cols]
            rest = tot[:, cols] - cum
            first = jnp.logical_not(upper)
        else:
            rest = pfx[:, cols] - la[:, cols]
            cum = tot[:, cols] - rest
            first = upper
        et = jnp.exp(tot[:, cols])
        et_other = jnp.where(upper, pltpu.roll(et, GLA_CHUNK, axis=0), pltpu.roll(et, L - GLA_CHUNK, axis=0))
        qd = q * jnp.exp(cum)
        ke = k * jnp.exp(rest)
        et_s[dirn] = et
        qd_s[dirn] = qd.astype(BF16)
        ki_s[dirn] = (k * jnp.exp(-cum)).astype(BF16)
        ke_s[dirn] = ke.astype(BF16)
        qi_s[dirn] = jnp.where(first, qd, qd * et_other).astype(BF16)
        ku_s[dirn] = jnp.where(first, ke * et_other, ke).astype(BF16)
    vb_s[...] = qkvg_ref[:, 2 * GLA_QK:2 * GLA_QK + GLA_VW].astype(BF16)

    def transpose_v(n2, carry):
        r0 = pl.multiple_of(n2 * GLA_PAIR, GLA_PAIR)
        for h in range(GLA_HEADS):
            vp = qkvg_ref[pl.ds(r0, GLA_PAIR), 2 * GLA_QK + h * GLA_DV:2 * GLA_QK + (h + 1) * GLA_DV]
            vt_s[n2, h * GLA_DV:(h + 1) * GLA_DV, :] = vp.T.astype(BF16)
        return carry

    lax.fori_loop(0, npair, transpose_v, 0)

    if has_state:
        zero = jnp.zeros((GLA_DK, GLA_DV), F32)
        for dirn in range(2):
            for h in range(GLA_HEADS):
                s0 = st0_refs[dirn][h]
                both = jnp.concatenate([s0, zero] if h % 2 == 0 else [zero, s0], axis=0)
                idx = dirn * GLA_HEADS + h
                st_s[idx * GLA_DV:(idx + 1) * GLA_DV, :] = both.T
    else:
        st_s[...] = jnp.zeros_like(st_s)

    lane = lax.broadcasted_iota(jnp.int32, (1, 2 * GLA_DK), 1)
    head_lanes = (lane < GLA_DK, lane >= GLA_DK)
    ri = lax.broadcasted_iota(jnp.int32, (GLA_PAIR, GLA_PAIR), 0)
    ci = lax.broadcasted_iota(jnp.int32, (GLA_PAIR, GLA_PAIR), 1)
    same_chunk = ((ri ^ ci) & GLA_CHUNK) == 0
    keep = (ci <= ri, ci >= ri)

    def pair_step(n, carry):
        for dirn in range(2):
            n2 = n if dirn == 0 else npair - 1 - n
            r0 = pl.multiple_of(n2 * GLA_PAIR, GLA_PAIR)
            rows = pl.ds(r0, GLA_PAIR)
            for p in range(GLA_HEADS // 2):
                lanes = slice(p * 2 * GLA_DK, (p + 1) * 2 * GLA_DK)
                qd = qd_s[dirn, rows, lanes]
                qi = qi_s[dirn, rows, lanes]
                zeros = jnp.zeros_like(qd)
                dec = (et_s[dirn, pl.ds(r0, 8), lanes][0:1, :]
                       * et_s[dirn, pl.ds(r0 + GLA_CHUNK, 8), lanes][0:1, :])
                q2 = jnp.concatenate([jnp.where(head_lanes[0], qd, zeros), jnp.where(head_lanes[1], qd, zeros)],
                                     axis=0)
                att_d = _dot_nt(q2, ki_s[dirn, rows, lanes])
                att_o = _dot_nt(q2, ke_s[dirn, rows, lanes])
                s0 = (dirn * GLA_HEADS + 2 * p) * GLA_DV
                st2 = st_s[s0:s0 + 2 * GLA_DV, :]
                vt2 = vt_s[n2, 2 * p * GLA_DV:(2 * p + 2) * GLA_DV, :]
                st_s[s0:s0 + 2 * GLA_DV, :] = st2 * dec + _dot(vt2, ku_s[dirn, rows, lanes])
                for hh in range(2):
                    h = 2 * p + hh
                    hr = slice(hh * GLA_PAIR, (hh + 1) * GLA_PAIR)
                    att = jnp.where(keep[dirn], jnp.where(same_chunk, att_d[hr], att_o[hr]), 0.0)
                    qm = jnp.where(head_lanes[hh], qi, zeros)
                    o = (_dot(att.astype(BF16), vb_s[rows, h * GLA_DV:(h + 1) * GLA_DV])
                         + _dot_nt(qm, st2[hh * GLA_DV:(hh + 1) * GLA_DV].astype(BF16)))
                    o_s[dirn, rows, h * GLA_DV:(h + 1) * GLA_DV] = o
        return carry

    lax.fori_loop(0, npair, pair_step, 0)

    for h in range(GLA_HEADS):
        cols = slice(h * GLA_DV, (h + 1) * GLA_DV)
        o = _rms(o_s[0, :, cols] + o_s[1, :, cols], ng_ref[...])
        g = qkvg_ref[:, 2 * GLA_QK + GLA_VW + h * GLA_DV:2 * GLA_QK + GLA_VW + (h + 1) * GLA_DV]
        y_ref[:, cols] = (o * (g * jax.nn.sigmoid(g))).astype(BF16)
    if emit_state:
        for dirn in range(2):
            for h in range(GLA_HEADS):
                idx = dirn * GLA_HEADS + h
                half = (h % 2) * GLA_DK
                stout_refs[dirn][h] = st_s[idx * GLA_DV:(idx + 1) * GLA_DV, :].T[half:half + GLA_DK, :]


def _gla_scan_matrix():
    i = np.arange(GLA_SCAN)
    same = (i[:, None] // GLA_CHUNK) == (i[None, :] // GLA_CHUNK)
    tri = same & (i[None, :] <= i[:, None])
    return jnp.asarray(np.concatenate([tri, same], axis=0).astype(np.float32)).astype(BF16)


def _gla_mixer(u3, wa_p, ba_p, norm_g, st0, layer, L, emit_state):
    B = u3.shape[0]
    has_state = st0 is not None
    head_state = (GLA_HEADS, GLA_DK, GLA_DV)
    in_specs = [pl.BlockSpec((None, L, 1536), lambda b: (b, 0, U_GLA // 1536)),
                pl.BlockSpec((None, L, 256), lambda b: (b, 0, U_GLA_A // 256)),
                pl.BlockSpec((256, 2 * GLA_QK), lambda b: (0, 0)),
                pl.BlockSpec((1, 2 * GLA_QK), lambda b: (0, 0)),
                pl.BlockSpec((1, GLA_DV), lambda b: (0, 0)),
                pl.BlockSpec((2 * GLA_SCAN, GLA_SCAN), lambda b: (0, 0))]
    args = [u3, u3, wa_p, ba_p, norm_g, _gla_scan_matrix()]
    if has_state:
        in_specs += [pl.BlockSpec((None, None) + head_state, lambda b: (b, layer, 0, 0, 0))] * 2
        args += list(st0)
    out_specs = [pl.BlockSpec((None, L, GLA_VW), lambda b: (b, 0, 0))]
    out_shape = [jax.ShapeDtypeStruct((B, L, GLA_VW), BF16)]
    if emit_state:
        out_specs += [pl.BlockSpec((None,) + head_state, lambda b: (b, 0, 0, 0))] * 2
        out_shape += [jax.ShapeDtypeStruct((B,) + head_state, F32)] * 2
    npair = L // GLA_PAIR
    qk_bf16 = pltpu.VMEM((2, L, GLA_QK), BF16)
    res = pl.pallas_call(
        functools.partial(_gla_kernel, L=L, has_state=has_state, emit_state=emit_state),
        grid=(B,),
        in_specs=in_specs,
        out_specs=out_specs,
        out_shape=out_shape,
        scratch_shapes=[qk_bf16,
                        qk_bf16,
                        qk_bf16,
                        qk_bf16,
                        qk_bf16,
                        pltpu.VMEM((2, L, GLA_QK), F32),
                        pltpu.VMEM((L, GLA_VW), BF16),
                        pltpu.VMEM((npair, GLA_VW, GLA_PAIR), BF16),
                        pltpu.VMEM((2, L, GLA_VW), F32),
                        pltpu.VMEM((GLA_STATES * GLA_DV, 2 * GLA_DK), F32)],
        compiler_params=_cparams(("arbitrary",)),
        name="gla_mixer",
    )(*args)
    return (res[0], res[1], res[2]) if emit_state else (res[0], None, None)


MLA_HW = 256
MLA_TQ = 512
MLA_KCH = 256


def _rope(x, cos, sin):
    lane = lax.broadcasted_iota(jnp.int32, x.shape, 1)
    partner = jnp.where((lane & 16) == 0, pltpu.roll(x, 112, axis=1), pltpu.roll(x, 16, axis=1))
    return x * cos + partner * sin


def _mla_kernel(*refs, L, Lk, tq, rope, emit):
    it = iter(refs)
    u_ref, qn_ref, wq_ref, kvn_ref, wkv_ref = (next(it) for _ in range(5))
    cos_ref, sin_ref, cckv_ref, ckr_ref = (next(it) for _ in range(4)) if rope else (None,) * 4
    y_ref = next(it)
    ckv_out, kr_out = (next(it), next(it)) if emit else (None, None)
    kf_s, v_s = next(it), next(it)

    qi = pl.program_id(1)

    def project(keys, rope_part, r0):
        kv = _dot(keys, wkv_ref[...])
        for h in range(MLA_HEADS):
            kf_s[h, r0:r0 + MLA_KCH, 0:MLA_NOPE] = kv[:, h * 256:h * 256 + MLA_NOPE].astype(BF16)
            kf_s[h, r0:r0 + MLA_KCH, MLA_NOPE:MLA_HW] = rope_part
            v_s[h, r0:r0 + MLA_KCH, :] = kv[:, h * 256 + MLA_NOPE:(h + 1) * 256].astype(BF16)

    @pl.when(qi == 0)
    def _():
        for r0 in range(0, L, MLA_KCH):
            ckvn = _rms(u_ref[r0:r0 + MLA_KCH, MLA_Q_RANK:MLA_Q_RANK + MLA_KV_RANK], kvn_ref[...])
            kr = u_ref[r0:r0 + MLA_KCH, 640:768]
            if emit:
                ckv_out[r0:r0 + MLA_KCH, :] = ckvn
                kr_out[r0:r0 + MLA_KCH, :] = kr[:, 0:MLA_ROPE]
            if rope:
                kr = _rope(kr, cos_ref[r0:r0 + MLA_KCH, :], sin_ref[r0:r0 + MLA_KCH, :])
            project(ckvn.astype(BF16), kr.astype(BF16), r0)
        if rope:
            for r0 in range(0, Lk - L, MLA_KCH):
                ckr = ckr_ref[r0:r0 + MLA_KCH, :]
                ckr = jnp.concatenate([ckr, jnp.zeros_like(ckr)], axis=1)
                project(cckv_ref[r0:r0 + MLA_KCH, :].astype(BF16), ckr.astype(BF16), L + r0)

    q0 = pl.multiple_of(qi * tq, tq)
    cqn = _rms(u_ref[pl.ds(q0, tq), 0:MLA_Q_RANK], qn_ref[...]).astype(BF16)
    q = _dot(cqn, wq_ref[...]) * ((MLA_NOPE + MLA_ROPE) ** -0.5)
    if rope:
        cos = cos_ref[pl.ds(q0, tq), :]
        sin = sin_ref[pl.ds(q0, tq), :]
    for h in range(MLA_HEADS):
        q_nope = q[:, h * MLA_HW:h * MLA_HW + MLA_NOPE]
        q_rope = q[:, h * MLA_HW + MLA_NOPE:(h + 1) * MLA_HW]
        if rope:
            q_rope = _rope(q_rope, cos, sin)
        qh = jnp.concatenate([q_nope, q_rope], axis=1).astype(BF16)
        s = _dot_nt(qh, kf_s[h])
        p = jnp.exp(s - jnp.max(s, axis=-1, keepdims=True))
        denom = jnp.sum(p, axis=-1, keepdims=True)
        o = _dot(p.astype(BF16), v_s[h])
        y_ref[:, h * MLA_V:(h + 1) * MLA_V] = (o / denom).astype(BF16)


def _mla_mixer(u3, q_norm, wq_p, kv_norm, wkv, rope_args, layer, L, emit):
    B = u3.shape[0]
    rope = rope_args is not None
    Lk = L + (rope_args[2].shape[2] if rope else 0)
    tq = min(MLA_TQ, L)
    const = lambda shape: pl.BlockSpec(shape, lambda b, i: (0,) * len(shape))
    in_specs = [pl.BlockSpec((None, L, 768), lambda b, i: (b, 0, U_MLA // 768)),
                const((1, MLA_Q_RANK)), const((MLA_Q_RANK, MLA_HEADS * MLA_HW)),
                const((1, MLA_KV_RANK)), const((MLA_KV_RANK, MLA_HEADS * 256))]
    args = [u3, q_norm, wq_p, kv_norm, wkv]
    if rope:
        cos, sin, cckv, ckr = rope_args
        in_specs += [const((L, 128)), const((L, 128)),
                     pl.BlockSpec((None, None, Lk - L, MLA_KV_RANK), lambda b, i: (b, layer, 0, 0)),
                     pl.BlockSpec((None, None, Lk - L, MLA_ROPE), lambda b, i: (b, layer, 0, 0))]
        args += [cos, sin, cckv, ckr]
    out_specs = [pl.BlockSpec((None, tq, MLA_HEADS * MLA_V), lambda b, i: (b, i, 0))]
    out_shape = [jax.ShapeDtypeStruct((B, L, MLA_HEADS * MLA_V), BF16)]
    if emit:
        out_specs += [pl.BlockSpec((None, L, MLA_KV_RANK), lambda b, i: (b, 0, 0)),
                      pl.BlockSpec((None, L, MLA_ROPE), lambda b, i: (b, 0, 0))]
        out_shape += [jax.ShapeDtypeStruct((B, L, MLA_KV_RANK), F32),
                      jax.ShapeDtypeStruct((B, L, MLA_ROPE), F32)]
    res = pl.pallas_call(
        functools.partial(_mla_kernel, L=L, Lk=Lk, tq=tq, rope=rope, emit=emit),
        grid=(B, L // tq),
        in_specs=in_specs,
        out_specs=out_specs,
        out_shape=out_shape,
        scratch_shapes=[pltpu.VMEM((MLA_HEADS, Lk, MLA_HW), BF16),
                        pltpu.VMEM((MLA_HEADS, Lk, MLA_V), BF16)],
        compiler_params=_cparams(("arbitrary", "arbitrary")),
        name="mla_mixer",
    )(*args)
    return res if emit else (res[0], None, None)


def _outproj_kernel(x_ref, g_ref, yh_ref, yg_ref, ym_ref, wh_ref, wg_ref, wm_ref, o_ref):
    y = _dot(yh_ref[...], wh_ref[...]) + _dot(yg_ref[...], wg_ref[...]) + _dot(ym_ref[...], wm_ref[...])
    o_ref[...] = x_ref[...] + g_ref[...] * y


def _out_projection(x2d, mod, y_hy, y_gla, y_mla, w_out_b, layer, row_of_tile, tm):
    rows = x2d.shape[0]
    once = pl.Buffered(1)
    return pl.pallas_call(
        _outproj_kernel,
        grid=(rows // tm,),
        in_specs=[pl.BlockSpec((tm, D_MODEL), lambda i: (i, 0)),
                  _mod_spec(2, row_of_tile),
                  pl.BlockSpec((tm, HY_CH), lambda i: (i, 0)),
                  pl.BlockSpec((tm, GLA_VW), lambda i: (i, 0)),
                  pl.BlockSpec((tm, MLA_HEADS * MLA_V), lambda i: (i, 0)),
                  pl.BlockSpec((None, HY_CH, D_MODEL), lambda i: (layer, 0, 0), pipeline_mode=once),
                  pl.BlockSpec((None, GLA_VW, D_MODEL), lambda i: (layer, 1, 0), pipeline_mode=once),
                  pl.BlockSpec((None, MLA_HEADS * MLA_V, D_MODEL), lambda i: (layer, 1, 0), pipeline_mode=once)],
        out_specs=pl.BlockSpec((tm, D_MODEL), lambda i: (i, 0)),
        out_shape=jax.ShapeDtypeStruct((rows, D_MODEL), F32),
        compiler_params=_cparams(("arbitrary",)),
        name="out_projection",
    )(x2d, mod, y_hy, y_gla, y_mla, w_out_b, w_out_b, w_out_b)


FFN_TF = 512


def _ffn_kernel(x_ref, ln_ref, sh_ref, sc_ref, g_ref, wg_ref, wu_ref, wo_ref, lnf_ref, o_ref,
                h_ref, acc_ref, *, final):
    f = pl.program_id(1)

    @pl.when(f == 0)
    def _():
        _adaln_to_bf16(x_ref, ln_ref, sc_ref, sh_ref, h_ref)
        acc_ref[...] = jnp.zeros_like(acc_ref)

    h = h_ref[...]
    gate = _dot(h, wg_ref[...])
    up = _dot(h, wu_ref[...])
    act = (gate * jax.nn.sigmoid(gate) * up).astype(BF16)
    acc_ref[...] += _dot(act, wo_ref[...])

    @pl.when(f == pl.num_programs(1) - 1)
    def _():
        def body(i, carry):
            rows = pl.ds(pl.multiple_of(i * ADALN_ROWS, ADALN_ROWS), ADALN_ROWS)
            x = x_ref[rows, :] + g_ref[...] * acc_ref[rows, :]
            o_ref[rows, :] = _rms(x, lnf_ref[...]) if final else x
            return carry

        lax.fori_loop(0, x_ref.shape[0] // ADALN_ROWS, body, 0, unroll=4)


def _ffn(x2d, ln, mod, w_in_b, w_out_b, layer, ln_final, row_of_tile, tm, final):
    rows = x2d.shape[0]
    nf = D_FF // FFN_TF
    return pl.pallas_call(
        functools.partial(_ffn_kernel, final=final),
        grid=(rows // tm, nf),
        in_specs=[pl.BlockSpec((tm, D_MODEL), lambda i, f: (i, 0)),
                  pl.BlockSpec((1, D_MODEL), lambda i, f: (0, 0)),
                  _mod_spec(3, row_of_tile),
                  _mod_spec(4, row_of_tile),
                  _mod_spec(5, row_of_tile),
                  pl.BlockSpec((None, D_MODEL, FFN_TF), lambda i, f: (layer, 0, f)),
                  pl.BlockSpec((None, D_MODEL, FFN_TF), lambda i, f: (layer, 0, nf + f)),
                  pl.BlockSpec((None, FFN_TF, D_MODEL), lambda i, f: (layer, f, 0)),
                  pl.BlockSpec((1, D_MODEL), lambda i, f: (0, 0))],
        out_specs=pl.BlockSpec((tm, D_MODEL), lambda i, f: (i, 0)),
        out_shape=jax.ShapeDtypeStruct((rows, D_MODEL), F32),
        scratch_shapes=[pltpu.VMEM((tm, D_MODEL), BF16), pltpu.VMEM((tm, D_MODEL), F32)],
        compiler_params=_cparams(("arbitrary", "arbitrary")),
        name="ffn",
    )(x2d, ln, mod, mod, mod, w_in_b, w_in_b, w_out_b, ln_final)


def _dft_tables(L):
    k = np.arange(L)
    ang = np.pi * ((k[:, None] * k[None, :]) % (2 * L)) / L
    c = np.cos(ang)
    s = np.sin(ang)
    sp = s.copy()
    sp[0, :] = 1.0 - 2.0 * (k % 2)
    as32 = lambda a: jnp.asarray(a.astype(np.float32))
    return as32(c), as32(s), as32(sp), as32(sp.T.copy())


def _hyena_consts(L):
    c, s, sp, spt = _dft_tables(L)
    c_hi, c_lo = _split(c)
    s_hi, s_lo = _split(s)
    t = jnp.linspace(0.0, 1.0, L, dtype=F32)[:, None]
    w = 2.0 * math.pi * jnp.arange(L, dtype=F32)[:, None] / L
    f = jnp.linspace(1e-4, HY_BANDS - 1, HY_BANDS, dtype=F32)
    zpos = jnp.concatenate([t, jnp.cos(f * w), -jnp.sin(f * w)], axis=-1)
    zpos = jnp.pad(zpos, ((0, 0), (0, 64 - HY_EMB)))
    min_decay = math.log(HY_TARGET) / HY_SLOW_DECAY
    max_decay = math.log(HY_TARGET) / HY_FAST_DECAY
    delta = jnp.abs(jnp.linspace(min_decay, max_decay, HY_CH, dtype=F32))
    decay = jnp.exp(-t * delta)
    filt_consts = (zpos, decay, c_hi, c_lo, s_hi, s_lo)
    main_consts = (c_hi, sp.astype(BF16), spt.astype(BF16))
    return filt_consts, main_consts


def _rope_tables(L):
    t = jnp.arange(L)
    half = MLA_ROPE // 2
    inv = ROPE_THETA ** (-jnp.arange(0, half, 2, dtype=F32) / half)
    ang_r = (t // GRID_W).astype(F32)[:, None] * inv
    ang_c = (t % GRID_W).astype(F32)[:, None] * inv
    cr, sr, cc, sc = jnp.cos(ang_r), jnp.sin(ang_r), jnp.cos(ang_c), jnp.sin(ang_c)
    cos = jnp.concatenate([cr, cr, cc, cc, jnp.ones((L, 128 - MLA_ROPE), F32)], axis=1)
    sin = jnp.concatenate([-sr, sr, -sc, sc, jnp.zeros((L, 128 - MLA_ROPE), F32)], axis=1)
    return cos, sin


def _prep_w_in_tail(w):
    z = lambda n: jnp.zeros((DEPTH, D_MODEL, n), w.dtype)
    mla0 = IN_MAIN + 2 * GLA_LOWRANK
    return jnp.concatenate([w[:, :, mla0:], z(64), w[:, :, IN_MAIN:mla0], z(256 - 2 * GLA_LOWRANK)],
                           axis=2).astype(BF16)


def _prep_w_uq(w):
    w = w.reshape(MLA_Q_RANK, MLA_HEADS, MLA_NOPE + MLA_ROPE)
    w = jnp.pad(w, ((0, 0), (0, 0), (0, MLA_HW - MLA_NOPE - MLA_ROPE)))
    return w.reshape(MLA_Q_RANK, MLA_HEADS * MLA_HW).astype(BF16)


def _prep_gla_decay(wa_f, ba_f, wa_b, ba_b):
    wa = jnp.zeros((256, 2 * GLA_QK), F32)
    wa = wa.at[0:GLA_LOWRANK, 0:GLA_QK].set(wa_f)
    wa = wa.at[GLA_LOWRANK:2 * GLA_LOWRANK, GLA_QK:].set(wa_b)
    return wa, jnp.concatenate([ba_f, ba_b])[None, :]


def _trunk_layer(x2d, B, L, mod, row_of_tile, lw, consts, ctx, final, ln_final):
    tm = 1024
    tm_out = 512
    u = _in_projection(x2d, lw['ln_mix'], mod, lw['w_in'], lw['w_in_tail'], lw['layer'], row_of_tile(tm), tm)
    u3 = u.reshape(B, L, U_W)
    filt_consts, main_consts = consts['hyena']
    filt = _hyena_filters(L, filt_consts, lw['hy_w1'], lw['hy_b1'], lw['hy_freq'], lw['hy_w2'],
                          lw['hy_b2'], lw['hy_w3'])
    y_hy = _hyena_mixer(u3, lw['hy_conv_w'], lw['hy_conv_b'], lw['hy_skip'], filt, main_consts, L)
    if ctx is None:
        y_gla, s_f, s_b = _gla_mixer(u3, lw['gla_wa'], lw['gla_ba'], lw['gla_norm'], None, lw['layer'], L, True)
        y_mla, ckv, krope = _mla_mixer(u3, lw['mla_q_norm'], lw['mla_w_uq'], lw['mla_kv_norm'],
                                       lw['mla_w_ukv'], None, lw['layer'], L, True)
        extras = (ckv, krope, s_f, s_b)
    else:
        cache_ckv, cache_krope, s0f, s0b = ctx
        y_gla, _, _ = _gla_mixer(u3, lw['gla_wa'], lw['gla_ba'], lw['gla_norm'], (s0f, s0b), lw['layer'], L, False)
        cos, sin = consts['rope']
        y_mla, _, _ = _mla_mixer(u3, lw['mla_q_norm'], lw['mla_w_uq'], lw['mla_kv_norm'], lw['mla_w_ukv'],
                                 (cos, sin, cache_ckv, cache_krope), lw['layer'], L, False)
        extras = None
    rows = B * L
    x2d = _out_projection(x2d, mod, y_hy.reshape(rows, -1), y_gla.reshape(rows, -1), y_mla.reshape(rows, -1),
                          lw['w_out'], lw['layer'], row_of_tile(tm_out), tm_out)
    x2d = _ffn(x2d, lw['ln_ffn'], mod, lw['w_ffn_in'], lw['w_ffn_out'], lw['layer'], ln_final, row_of_tile(tm_out),
               tm_out, final)
    return x2d, extras


def kernel(x_prompt, x_sample, cache_mla_ckv, cache_mla_krope, state_gla_fwd, state_gla_bwd, c, c_ctx, w_mod, b_mod, ln_mix, w_in, hy_conv_w, hy_conv_b, hy_filt_w1, hy_filt_b1, hy_filt_freq, hy_filt_w2, hy_filt_b2, hy_filt_w3, hy_skip, gla_wa_f, gla_ba_f, gla_wa_b, gla_ba_b, gla_norm, mla_q_norm, mla_w_uq, mla_kv_norm, mla_w_ukv, w_out, ln_ffn, w_ffn_in, w_ffn_out, ln_final):
    Bc, Lc, _ = x_prompt.shape
    Bl, Ll, _ = x_sample.shape
    assert 1 + Bl <= MOD_ROWS

    c_all = jnp.concatenate([c_ctx[None, :], c, jnp.zeros((MOD_ROWS - 1 - Bl, D_MODEL), F32)], axis=0)
    mod_all = _modulation(c_all, w_mod, b_mod).reshape(DEPTH, MOD_ROWS, 6, 1, D_MODEL)

    consts_ctx = {'hyena': _hyena_consts(Lc)}
    consts_lat = {'hyena': _hyena_consts(Ll), 'rope': _rope_tables(Ll)}
    ctx_rows = lambda tm: (lambda i: 0)
    lat_rows = lambda tm: (lambda i: 1 + (i * tm) // Ll)
    lnf = ln_final[None, :]

    x_ctx = x_prompt.reshape(Bc * Lc, D_MODEL)
    x_lat = x_sample.reshape(Bl * Ll, D_MODEL)
    w_in_b = w_in.astype(BF16)
    w_in_tail = _prep_w_in_tail(w_in)
    w_out_b = w_out.astype(BF16)
    w_ffn_in_b = w_ffn_in.astype(BF16)
    w_ffn_out_b = w_ffn_out.astype(BF16)
    ckv_l, krope_l, sf_l, sb_l = [], [], [], []
    for l in range(DEPTH):
        wa, ba = _prep_gla_decay(gla_wa_f[l], gla_ba_f[l], gla_wa_b[l], gla_ba_b[l])
        lw = {
            'layer': l, 'ln_mix': ln_mix[l][None, :], 'w_in': w_in_b, 'w_in_tail': w_in_tail,
            'hy_conv_w': hy_conv_w[l], 'hy_conv_b': hy_conv_b[l][None, :],
            'hy_w1': jnp.pad(hy_filt_w1[l], ((0, 64 - HY_EMB), (0, 0))), 'hy_b1': hy_filt_b1[l][None, :],
            'hy_freq': hy_filt_freq[l], 'hy_w2': hy_filt_w2[l], 'hy_b2': hy_filt_b2[l][None, :],
            'hy_w3': hy_filt_w3[l], 'hy_skip': hy_skip[l],
            'gla_wa': wa, 'gla_ba': ba, 'gla_norm': gla_norm[l][None, :],
            'mla_q_norm': mla_q_norm[l][None, :], 'mla_w_uq': _prep_w_uq(mla_w_uq[l]),
            'mla_kv_norm': mla_kv_norm[l][None, :], 'mla_w_ukv': mla_w_ukv[l].astype(BF16),
            'w_out': w_out_b, 'ln_ffn': ln_ffn[l][None, :],
            'w_ffn_in': w_ffn_in_b, 'w_ffn_out': w_ffn_out_b,
        }
        final = l == DEPTH - 1
        mod = mod_all[l]
        x_ctx, (ckv, krope, s_f, s_b) = _trunk_layer(x_ctx, Bc, Lc, mod, ctx_rows, lw, consts_ctx, None,
                                                     final, lnf)
        ckv_l.append(ckv)
        krope_l.append(krope)
        sf_l.append(s_f)
        sb_l.append(s_b)
        ctx = (cache_mla_ckv, cache_mla_krope, state_gla_fwd, state_gla_bwd)
        x_lat, _ = _trunk_layer(x_lat, Bl, Ll, mod, lat_rows, lw, consts_lat, ctx, final, lnf)
    return (x_ctx.reshape(Bc, Lc, D_MODEL), x_lat.reshape(Bl, Ll, D_MODEL),
            jnp.stack(ckv_l, axis=1), jnp.stack(krope_l, axis=1),
            jnp.stack(sf_l, axis=1), jnp.stack(sb_l, axis=1))
```

```python
import functools
import math

import numpy as np
import jax
import jax.numpy as jnp
from jax import lax
from jax.experimental import pallas as pl
from jax.experimental.pallas import tpu as pltpu

F32 = jnp.float32
BF16 = jnp.bfloat16

D_MODEL = 2048
DEPTH = 2
GRID_W = 64
EPS = 1e-6
HY_CH = 512
HY_ORDER = 2
HY_BANDS = 16
HY_EMB = 1 + 2 * HY_BANDS
HY_FF = 64
HY_FAST_DECAY = 0.3
HY_SLOW_DECAY = 1.5
HY_TARGET = 1e-2
HY_IN = 3 * HY_CH
GLA_HEADS = 4
GLA_DK = 64
GLA_DV = 128
GLA_LOWRANK = 16
GLA_TAU = 16.0
GLA_CHUNK = 64
GLA_QK = GLA_HEADS * GLA_DK
GLA_VW = GLA_HEADS * GLA_DV
MLA_HEADS = 8
MLA_Q_RANK = 384
MLA_KV_RANK = 256
MLA_NOPE = 128
MLA_ROPE = 64
MLA_V = 128
ROPE_THETA = 10000.0
D_FF = -(-8 * D_MODEL // (3 * 256)) * 256

U_HY = 0
U_GLA = 1536
U_MLA = 3072
U_GLA_A = 3840
U_W = 4096
MOD_ROWS = 16

VMEM_LIMIT_V7X = 48 * 1024 * 1024


def _cparams(sem):
    return pltpu.CompilerParams(dimension_semantics=sem, vmem_limit_bytes=VMEM_LIMIT_V7X)


def _dot(a, b):
    return jnp.dot(a, b, preferred_element_type=F32)


def _dot_nt(a, b):
    return lax.dot_general(a, b, (((1,), (1,)), ((), ())), preferred_element_type=F32)


def _split(x):
    hi = x.astype(BF16)
    lo = (x - hi.astype(F32)).astype(BF16)
    return hi, lo


def _dot3(a, b):
    a_hi, a_lo = _split(a)
    b_hi, b_lo = _split(b)
    return _dot(a_hi, b_hi) + (_dot(a_lo, b_hi) + _dot(a_hi, b_lo))


def _rms(x, g):
    ms = jnp.mean(x * x, axis=-1, keepdims=True)
    return x * lax.rsqrt(ms + EPS) * g


ADALN_ROWS = 16


def _adaln_group(x_ref, gain, shift, h_ref, r):
    rows = pl.ds(pl.multiple_of(r, ADALN_ROWS), ADALN_ROWS)
    x = x_ref[rows, :]
    ms = jnp.mean(x * x, axis=-1, keepdims=True)
    h_ref[rows, :] = (x * lax.rsqrt(ms + EPS) * gain + shift).astype(BF16)


def _adaln_all(x_ref, gain, shift, h_ref):
    def body(i, carry):
        _adaln_group(x_ref, gain, shift, h_ref, i * ADALN_ROWS)
        return carry

    lax.fori_loop(0, x_ref.shape[0] // ADALN_ROWS, body, 0, unroll=4)


def _adaln_slice(x_ref, gain, shift, h_ref, base, nrows):
    for t in range(nrows // ADALN_ROWS):
        _adaln_group(x_ref, gain, shift, h_ref, base + t * ADALN_ROWS)


def _mod_kernel(c_ref, w_ref, b_ref, o_ref):
    c = c_ref[...]
    s = (c * jax.nn.sigmoid(c)).astype(BF16)
    o_ref[...] = _dot(s, w_ref[...].astype(BF16)) + b_ref[...]


def _modulation(c_all, w_mod, b_mod):
    tn = 1024
    n6 = 6 * D_MODEL
    return pl.pallas_call(
        _mod_kernel,
        grid=(DEPTH, n6 // tn),
        in_specs=[pl.BlockSpec((MOD_ROWS, D_MODEL), lambda l, j: (0, 0)),
                  pl.BlockSpec((None, D_MODEL, tn), lambda l, j: (l, 0, j)),
                  pl.BlockSpec((None, 1, tn), lambda l, j: (l, 0, j))],
        out_specs=pl.BlockSpec((None, MOD_ROWS, tn), lambda l, j: (l, 0, j)),
        out_shape=jax.ShapeDtypeStruct((DEPTH, MOD_ROWS, n6), F32),
        compiler_params=_cparams(("arbitrary", "arbitrary")),
        name="modulation",
    )(c_all, w_mod, b_mod.reshape(DEPTH, 1, n6))


def _mod_spec(chunk, row_of_tile, tile_of_step=lambda i, *_: i):
    return pl.BlockSpec((None, None, 1, D_MODEL), lambda *g: (row_of_tile(tile_of_step(*g)), chunk, 0, 0))


def _lookahead(n_tiles):
    return lambda i, j: jnp.minimum(i + jnp.minimum(j, 1), n_tiles - 1)


IN_TN = 1024
IN_MAIN = 3072
IN_PRE_ROWS = 352
IN_PRE_STRIDE = 336


def _inproj_kernel(x_ref, ln_ref, sh_ref, sc_ref, w_ref, wt_ref, o_ref, ha_ref, hb_ref):
    i = pl.program_id(0)
    j = pl.program_id(1)
    gain = ln_ref[...] * (1.0 + sc_ref[...])
    shift = sh_ref[...]

    @pl.when((i == 0) & (j == 0))
    def _():
        _adaln_all(x_ref, gain, shift, ha_ref)

    base = jnp.clip(j - 1, 0, 2) * IN_PRE_STRIDE

    def step(cur_ref, nxt_ref):
        @pl.when(j < IN_MAIN // IN_TN)
        def _():
            _adaln_slice(x_ref, gain, shift, nxt_ref, base, IN_PRE_ROWS)
            o_ref[...] = _dot(cur_ref[...], w_ref[...])

        @pl.when(j >= IN_MAIN // IN_TN)
        def _():
            _adaln_slice(x_ref, gain, shift, nxt_ref, base, IN_PRE_ROWS)
            o_ref[...] = _dot(cur_ref[...], wt_ref[...])

    @pl.when(i % 2 == 0)
    def _():
        step(ha_ref, hb_ref)

    @pl.when(i % 2 == 1)
    def _():
        step(hb_ref, ha_ref)


def _in_projection(x2d, ln, mod, w_in_b, w_tail, layer, row_of_tile, tm):
    rows = x2d.shape[0]
    n_main = IN_MAIN // IN_TN
    ahead = _lookahead(rows // tm)
    return pl.pallas_call(
        _inproj_kernel,
        grid=(rows // tm, U_W // IN_TN),
        in_specs=[pl.BlockSpec((tm, D_MODEL), lambda i, j: (ahead(i, j), 0)),
                  pl.BlockSpec((1, D_MODEL), lambda i, j: (0, 0)),
                  _mod_spec(0, row_of_tile, ahead),
                  _mod_spec(1, row_of_tile, ahead),
                  pl.BlockSpec((None, D_MODEL, IN_TN), lambda i, j: (layer, 0, jnp.minimum(j, n_main - 1))),
                  pl.BlockSpec((None, D_MODEL, U_W - IN_MAIN), lambda i, j: (layer, 0, 0))],
        out_specs=pl.BlockSpec((tm, IN_TN), lambda i, j: (i, j)),
        out_shape=jax.ShapeDtypeStruct((rows, U_W), F32),
        scratch_shapes=[pltpu.VMEM((tm, D_MODEL), BF16), pltpu.VMEM((tm, D_MODEL), BF16)],
        compiler_params=_cparams(("arbitrary", "arbitrary")),
        name="in_projection",
    )(x2d, ln, mod, mod, w_in_b, w_tail)


def _filter_kernel(z_ref, w1_ref, b1_ref, fr_ref, w2_ref, b2_ref, w3f_ref, w3b_ref, dec_ref,
                   chi_ref, clo_ref, shi_ref, slo_ref, kr_ref, ki_ref, kn_ref, *, L):
    h = jnp.sin(fr_ref[0:1, :] * (_dot3(z_ref[...], w1_ref[...]) + b1_ref[...]))
    h = jnp.sin(fr_ref[1:2, :] * (_dot3(h, w2_ref[...]) + b2_ref[...]))
    dec = dec_ref[...]
    row = lax.broadcasted_iota(jnp.int32, (L, 1), 0)
    kf = _dot3(h, w3f_ref[...]) * dec
    kb = jnp.where(row == 0, 0.0, _dot3(h, w3b_ref[...]) * dec)
    p_hi, p_lo = _split(kf + kb)
    m_hi, m_lo = _split(kf - kb)
    chi, clo = chi_ref[...], clo_ref[...]
    shi, slo = shi_ref[...], slo_ref[...]
    kc = _dot(chi, p_hi) + (_dot(clo, p_hi) + _dot(chi, p_lo))
    ks = _dot(shi, m_hi) + (_dot(slo, m_hi) + _dot(shi, m_lo))
    sign = jnp.where((row & 1) == 0, 1.0, -1.0)
    kn = jnp.sum((kf + kb) * sign, axis=0, keepdims=True)
    kr_ref[...] = kc * jnp.where(row == 0, 0.5 / L, 1.0 / L)
    ki_ref[...] = ks * (-1.0 / L)
    kn_ref[...] = kn * (0.5 / L)


def _hyena_filters(L, consts, w1p, b1, freq, w2, b2, w3):
    zpos, decay, c_hi, c_lo, s_hi, s_lo = consts
    full = lambda shape: pl.BlockSpec(shape, lambda o: (0,) * len(shape))
    return pl.pallas_call(
        functools.partial(_filter_kernel, L=L),
        grid=(HY_ORDER,),
        in_specs=[full((L, 64)), full((64, HY_FF)), full((1, HY_FF)), full((2, HY_FF)),
                  full((HY_FF, HY_FF)), full((1, HY_FF)),
                  pl.BlockSpec((HY_FF, HY_CH), lambda o: (0, o)),
                  pl.BlockSpec((HY_FF, HY_CH), lambda o: (0, HY_ORDER + o)),
                  full((L, HY_CH)), full((L, L)), full((L, L)), full((L, L)), full((L, L))],
        out_specs=[pl.BlockSpec((None, L, HY_CH), lambda o: (o, 0, 0)),
                   pl.BlockSpec((None, L, HY_CH), lambda o: (o, 0, 0)),
                   pl.BlockSpec((None, 1, HY_CH), lambda o: (o, 0, 0))],
        out_shape=[jax.ShapeDtypeStruct((HY_ORDER, L, HY_CH), F32),
                   jax.ShapeDtypeStruct((HY_ORDER, L, HY_CH), F32),
                   jax.ShapeDtypeStruct((HY_ORDER, 1, HY_CH), F32)],
        compiler_params=_cparams(("arbitrary",)),
        name="hyena_filters",
    )(zpos, w1p, b1, freq, w2, b2, w3, w3, decay, c_hi, c_lo, s_hi, s_lo)


HY_CG = 256


def _hyena_kernel(v_ref, x1_ref, x2_ref, wv_ref, w1_ref, w2_ref, bv_ref, b1_ref, b2_ref, skip_ref,
                  kr_ref, ki_ref, kn_ref, c_ref, sp_ref, spt_ref, o_ref, *, L):
    row = lax.broadcasted_iota(jnp.int32, (L, 1), 0)
    first = row == 0
    last = row == L - 1
    groups = [slice(g * HY_CG, (g + 1) * HY_CG) for g in range(HY_CH // HY_CG)]

    def short_conv(u_ref, w_ref, b_ref, cols):
        u = u_ref[:, cols]
        prev = jnp.where(first, 0.0, pltpu.roll(u, 1, axis=0))
        nxt = jnp.where(last, 0.0, pltpu.roll(u, L - 1, axis=0))
        return prev * w_ref[0:1, cols] + u * w_ref[1:2, cols] + nxt * w_ref[2:3, cols] + b_ref[:, cols]

    gate_refs = ((x1_ref, w1_ref, b1_ref), (x2_ref, w2_ref, b2_ref))
    z = [short_conv(v_ref, wv_ref, bv_ref, cols) for cols in groups]
    for o in range(HY_ORDER):
        zb = [zz.astype(BF16) for zz in z]
        a = [_dot(c_ref[...], x) for x in zb]
        s = [_dot(sp_ref[...], x) for x in zb]
        conv = []
        for g, cols in enumerate(groups):
            kr = kr_ref[o, :, cols]
            ki = ki_ref[o, :, cols]
            qt = a[g] * kr + s[g] * ki
            qb = s[g] * jnp.where(first, kn_ref[o, :, cols], kr) - a[g] * ki
            conv.append(_dot(c_ref[...], qt.astype(BF16)) + _dot(spt_ref[...], qb.astype(BF16)))
        z = [short_conv(*gate_refs[o], cols) * (conv[g] + z[g] * skip_ref[o:o + 1, cols])
             for g, cols in enumerate(groups)]
    for g, cols in enumerate(groups):
        o_ref[:, cols] = z[g].astype(BF16)


def _hyena_mixer(u3, conv_w, conv_b, skip, filt, dft, L):
    B = u3.shape[0]
    kr, ki, kn = filt
    c_b, sp_b, spt_b = dft
    once = pl.Buffered(1)
    ublk = lambda part: pl.BlockSpec((None, L, HY_CH), lambda b: (b, 0, part))
    wblk = lambda part: pl.BlockSpec((3, HY_CH), lambda b: (0, part))
    bblk = lambda part: pl.BlockSpec((1, HY_CH), lambda b: (0, part))
    fblk = lambda rows: pl.BlockSpec((HY_ORDER, rows, HY_CH), lambda b: (0, 0, 0), pipeline_mode=once)
    mat = pl.BlockSpec((L, L), lambda b: (0, 0), pipeline_mode=once)
    return pl.pallas_call(
        functools.partial(_hyena_kernel, L=L),
        grid=(B,),
        in_specs=[ublk(0), ublk(1), ublk(2), wblk(0), wblk(1), wblk(2), bblk(0), bblk(1), bblk(2),
                  pl.BlockSpec((HY_ORDER, HY_CH), lambda b: (0, 0)),
                  fblk(L), fblk(L), fblk(1), mat, mat, mat],
        out_specs=pl.BlockSpec((None, L, HY_CH), lambda b: (b, 0, 0)),
        out_shape=jax.ShapeDtypeStruct((B, L, HY_CH), BF16),
        compiler_params=_cparams(("arbitrary",)),
        name="hyena_mixer",
    )(u3, u3, u3, conv_w, conv_w, conv_w, conv_b, conv_b, conv_b, skip, kr, ki, kn, c_b, sp_b, spt_b)


GLA_PAIR = 2 * GLA_CHUNK
GLA_STATES = 2 * GLA_HEADS
GLA_SCAN = 256


def _gla_kernel(*refs, L, has_state, emit_state):
    it = iter(refs)
    qkvg_ref, a_ref, wa_ref, ba_ref, ng_ref, scan_ref = (next(it) for _ in range(6))
    st0_refs = (next(it), next(it)) if has_state else None
    y_ref = next(it)
    stout_refs = (next(it), next(it)) if emit_state else None
    qd_s, ki_s, ke_s, qi_s, ku_s, et_s, vb_s, vt_s, o_s, st_s = (next(it) for _ in range(10))

    npair = L // GLA_PAIR
    pre = _dot3(a_ref[...], wa_ref[...]) + ba_ref[...]
    la = (jnp.minimum(pre, 0.0) - jnp.log(1.0 + jnp.exp(-jnp.abs(pre)))) * (1.0 / GLA_TAU)

    scan = scan_ref[...]
    pfx, tot = [], []
    for r in range(0, L, GLA_SCAN):
        x = la[r:r + GLA_SCAN]
        hi = x.astype(BF16)
        r1 = x - hi.astype(F32)
        mid = r1.astype(BF16)
        lo = (r1 - mid.astype(F32)).astype(BF16)
        res = _dot(scan, hi) + (_dot(scan, mid) + _dot(scan, lo))
        pfx.append(res[:GLA_SCAN])
        tot.append(res[GLA_SCAN:])
    pfx = jnp.concatenate(pfx, axis=0) if len(pfx) > 1 else pfx[0]
    tot = jnp.concatenate(tot, axis=0) if len(tot) > 1 else tot[0]

    upper = (lax.broadcasted_iota(jnp.int32, (L, 1), 0) & GLA_CHUNK) != 0
    q = qkvg_ref[:, 0:GLA_QK] * (GLA_DK ** -0.5)
    k = qkvg_ref[:, GLA_QK:2 * GLA_QK]
    for dirn in range(2):
        cols = slice(dirn * GLA_QK, (dirn + 1) * GLA_QK)
        if dirn == 0:
            cum = pfx[:, cols]
            rest = tot[:, cols] - cum
            first = jnp.logical_not(upper)
        else:
            rest = pfx[:, cols] - la[:, cols]
            cum = tot[:, cols] - rest
            first = upper
        et = jnp.exp(tot[:, cols])
        et_other = jnp.where(upper, pltpu.roll(et, GLA_CHUNK, axis=0), pltpu.roll(et, L - GLA_CHUNK, axis=0))
        qd = q * jnp.exp(cum)
        ke = k * jnp.exp(rest)
        et_s[dirn] = et
        qd_s[dirn] = qd.astype(BF16)
        ki_s[dirn] = (k * jnp.exp(-cum)).astype(BF16)
        ke_s[dirn] = ke.astype(BF16)
        qi_s[dirn] = jnp.where(first, qd, qd * et_other).astype(BF16)
        ku_s[dirn] = jnp.where(first, ke * et_other, ke).astype(BF16)
    vb_s[...] = qkvg_ref[:, 2 * GLA_QK:2 * GLA_QK + GLA_VW].astype(BF16)

    def transpose_v(n2, carry):
        r0 = pl.multiple_of(n2 * GLA_PAIR, GLA_PAIR)
        for h in range(GLA_HEADS):
            vp = qkvg_ref[pl.ds(r0, GLA_PAIR), 2 * GLA_QK + h * GLA_DV:2 * GLA_QK + (h + 1) * GLA_DV]
            vt_s[n2, h * GLA_DV:(h + 1) * GLA_DV, :] = vp.T.astype(BF16)
        return carry

    lax.fori_loop(0, npair, transpose_v, 0)

    if has_state:
        zero = jnp.zeros((GLA_DK, GLA_DV), F32)
        for dirn in range(2):
            for h in range(GLA_HEADS):
                s0 = st0_refs[dirn][h]
                both = jnp.concatenate([s0, zero] if h % 2 == 0 else [zero, s0], axis=0)
                idx = dirn * GLA_HEADS + h
                st_s[idx * GLA_DV:(idx + 1) * GLA_DV, :] = both.T
    else:
        st_s[...] = jnp.zeros_like(st_s)

    lane = lax.broadcasted_iota(jnp.int32, (1, 2 * GLA_DK), 1)
    head_lanes = (lane < GLA_DK, lane >= GLA_DK)
    ri = lax.broadcasted_iota(jnp.int32, (GLA_PAIR, GLA_PAIR), 0)
    ci = lax.broadcasted_iota(jnp.int32, (GLA_PAIR, GLA_PAIR), 1)
    same_chunk = ((ri ^ ci) & GLA_CHUNK) == 0
    keep = (ci <= ri, ci >= ri)

    def pair_step(n, carry):
        for dirn in range(2):
            n2 = n if dirn == 0 else npair - 1 - n
            r0 = pl.multiple_of(n2 * GLA_PAIR, GLA_PAIR)
            rows = pl.ds(r0, GLA_PAIR)
            for p in range(GLA_HEADS // 2):
                lanes = slice(p * 2 * GLA_DK, (p + 1) * 2 * GLA_DK)
                qd = qd_s[dirn, rows, lanes]
                qi = qi_s[dirn, rows, lanes]
                zeros = jnp.zeros_like(qd)
                dec = (et_s[dirn, pl.ds(r0, 8), lanes][0:1, :]
                       * et_s[dirn, pl.ds(r0 + GLA_CHUNK, 8), lanes][0:1, :])
                q2 = jnp.concatenate([jnp.where(head_lanes[0], qd, zeros), jnp.where(head_lanes[1], qd, zeros)],
                                     axis=0)
                att_d = _dot_nt(q2, ki_s[dirn, rows, lanes])
                att_o = _dot_nt(q2, ke_s[dirn, rows, lanes])
                s0 = (dirn * GLA_HEADS + 2 * p) * GLA_DV
                st2 = st_s[s0:s0 + 2 * GLA_DV, :]
                vt2 = vt_s[n2, 2 * p * GLA_DV:(2 * p + 2) * GLA_DV, :]
                st_s[s0:s0 + 2 * GLA_DV, :] = st2 * dec + _dot(vt2, ku_s[dirn, rows, lanes])
                for hh in range(2):
                    h = 2 * p + hh
                    hr = slice(hh * GLA_PAIR, (hh + 1) * GLA_PAIR)
                    att = jnp.where(keep[dirn], jnp.where(same_chunk, att_d[hr], att_o[hr]), 0.0)
                    qm = jnp.where(head_lanes[hh], qi, zeros)
                    o = (_dot(att.astype(BF16), vb_s[rows, h * GLA_DV:(h + 1) * GLA_DV])
                         + _dot_nt(qm, st2[hh * GLA_DV:(hh + 1) * GLA_DV].astype(BF16)))
                    o_s[dirn, rows, h * GLA_DV:(h + 1) * GLA_DV] = o
        return carry

    lax.fori_loop(0, npair, pair_step, 0)

    for h in range(GLA_HEADS):
        cols = slice(h * GLA_DV, (h + 1) * GLA_DV)
        o = _rms(o_s[0, :, cols] + o_s[1, :, cols], ng_ref[...])
        g = qkvg_ref[:, 2 * GLA_QK + GLA_VW + h * GLA_DV:2 * GLA_QK + GLA_VW + (h + 1) * GLA_DV]
        y_ref[:, cols] = (o * (g * jax.nn.sigmoid(g))).astype(BF16)
    if emit_state:
        for dirn in range(2):
            for h in range(GLA_HEADS):
                idx = dirn * GLA_HEADS + h
                half = (h % 2) * GLA_DK
                stout_refs[dirn][h] = st_s[idx * GLA_DV:(idx + 1) * GLA_DV, :].T[half:half + GLA_DK, :]


def _gla_scan_matrix():
    i = np.arange(GLA_SCAN)
    same = (i[:, None] // GLA_CHUNK) == (i[None, :] // GLA_CHUNK)
    tri = same & (i[None, :] <= i[:, None])
    return jnp.asarray(np.concatenate([tri, same], axis=0).astype(np.float32)).astype(BF16)


def _gla_mixer(u3, wa_p, ba_p, norm_g, st0, layer, L, emit_state):
    B = u3.shape[0]
    has_state = st0 is not None
    head_state = (GLA_HEADS, GLA_DK, GLA_DV)
    in_specs = [pl.BlockSpec((None, L, 1536), lambda b: (b, 0, U_GLA // 1536)),
                pl.BlockSpec((None, L, 256), lambda b: (b, 0, U_GLA_A // 256)),
                pl.BlockSpec((256, 2 * GLA_QK), lambda b: (0, 0)),
                pl.BlockSpec((1, 2 * GLA_QK), lambda b: (0, 0)),
                pl.BlockSpec((1, GLA_DV), lambda b: (0, 0)),
                pl.BlockSpec((2 * GLA_SCAN, GLA_SCAN), lambda b: (0, 0))]
    args = [u3, u3, wa_p, ba_p, norm_g, _gla_scan_matrix()]
    if has_state:
        in_specs += [pl.BlockSpec((None, None) + head_state, lambda b: (b, layer, 0, 0, 0))] * 2
        args += list(st0)
    out_specs = [pl.BlockSpec((None, L, GLA_VW), lambda b: (b, 0, 0))]
    out_shape = [jax.ShapeDtypeStruct((B, L, GLA_VW), BF16)]
    if emit_state:
        out_specs += [pl.BlockSpec((None,) + head_state, lambda b: (b, 0, 0, 0))] * 2
        out_shape += [jax.ShapeDtypeStruct((B,) + head_state, F32)] * 2
    npair = L // GLA_PAIR
    qk_bf16 = pltpu.VMEM((2, L, GLA_QK), BF16)
    res = pl.pallas_call(
        functools.partial(_gla_kernel, L=L, has_state=has_state, emit_state=emit_state),
        grid=(B,),
        in_specs=in_specs,
        out_specs=out_specs,
        out_shape=out_shape,
        scratch_shapes=[qk_bf16,
                        qk_bf16,
                        qk_bf16,
                        qk_bf16,
                        qk_bf16,
                        pltpu.VMEM((2, L, GLA_QK), F32),
                        pltpu.VMEM((L, GLA_VW), BF16),
                        pltpu.VMEM((npair, GLA_VW, GLA_PAIR), BF16),
                        pltpu.VMEM((2, L, GLA_VW), F32),
                        pltpu.VMEM((GLA_STATES * GLA_DV, 2 * GLA_DK), F32)],
        compiler_params=_cparams(("arbitrary",)),
        name="gla_mixer",
    )(*args)
    return (res[0], res[1], res[2]) if emit_state else (res[0], None, None)


MLA_HW = 256
MLA_TQ = 512
MLA_KCH = 256


def _rope(x, cos, sin):
    lane = lax.broadcasted_iota(jnp.int32, x.shape, 1)
    partner = jnp.where((lane & 16) == 0, pltpu.roll(x, 112, axis=1), pltpu.roll(x, 16, axis=1))
    return x * cos + partner * sin


def _mla_kernel(*refs, L, Lk, tq, rope, emit):
    it = iter(refs)
    u_ref, qn_ref, wq_ref, kvn_ref, wkv_ref = (next(it) for _ in range(5))
    cos_ref, sin_ref, cckv_ref, ckr_ref = (next(it) for _ in range(4)) if rope else (None,) * 4
    y_ref = next(it)
    ckv_out, kr_out = (next(it), next(it)) if emit else (None, None)
    kf_s, v_s = next(it), next(it)

    qi = pl.program_id(1)

    def project(keys, rope_part, r0):
        kv = _dot(keys, wkv_ref[...])
        for h in range(MLA_HEADS):
            kf_s[h, r0:r0 + MLA_KCH, 0:MLA_NOPE] = kv[:, h * 256:h * 256 + MLA_NOPE].astype(BF16)
            kf_s[h, r0:r0 + MLA_KCH, MLA_NOPE:MLA_HW] = rope_part
            v_s[h, r0:r0 + MLA_KCH, :] = kv[:, h * 256 + MLA_NOPE:(h + 1) * 256].astype(BF16)

    @pl.when(qi == 0)
    def _():
        for r0 in range(0, L, MLA_KCH):
            ckvn = _rms(u_ref[r0:r0 + MLA_KCH, MLA_Q_RANK:MLA_Q_RANK + MLA_KV_RANK], kvn_ref[...])
            kr = u_ref[r0:r0 + MLA_KCH, 640:768]
            if emit:
                ckv_out[r0:r0 + MLA_KCH, :] = ckvn
                kr_out[r0:r0 + MLA_KCH, :] = kr[:, 0:MLA_ROPE]
            if rope:
                kr = _rope(kr, cos_ref[r0:r0 + MLA_KCH, :], sin_ref[r0:r0 + MLA_KCH, :])
            project(ckvn.astype(BF16), kr.astype(BF16), r0)
        if rope:
            for r0 in range(0, Lk - L, MLA_KCH):
                ckr = ckr_ref[r0:r0 + MLA_KCH, :]
                ckr = jnp.concatenate([ckr, jnp.zeros_like(ckr)], axis=1)
                project(cckv_ref[r0:r0 + MLA_KCH, :].astype(BF16), ckr.astype(BF16), L + r0)

    q0 = pl.multiple_of(qi * tq, tq)
    cqn = _rms(u_ref[pl.ds(q0, tq), 0:MLA_Q_RANK], qn_ref[...]).astype(BF16)
    q = _dot(cqn, wq_ref[...]) * ((MLA_NOPE + MLA_ROPE) ** -0.5)
    if rope:
        cos = cos_ref[pl.ds(q0, tq), :]
        sin = sin_ref[pl.ds(q0, tq), :]
    for h in range(MLA_HEADS):
        q_nope = q[:, h * MLA_HW:h * MLA_HW + MLA_NOPE]
        q_rope = q[:, h * MLA_HW + MLA_NOPE:(h + 1) * MLA_HW]
        if rope:
            q_rope = _rope(q_rope, cos, sin)
        qh = jnp.concatenate([q_nope, q_rope], axis=1).astype(BF16)
        s = _dot_nt(qh, kf_s[h])
        p = jnp.exp(s - jnp.max(s, axis=-1, keepdims=True))
        denom = jnp.sum(p, axis=-1, keepdims=True)
        o = _dot(p.astype(BF16), v_s[h])
        y_ref[:, h * MLA_V:(h + 1) * MLA_V] = (o / denom).astype(BF16)


def _mla_mixer(u3, q_norm, wq_p, kv_norm, wkv, rope_args, layer, L, emit):
    B = u3.shape[0]
    rope = rope_args is not None
    Lk = L + (rope_args[2].shape[2] if rope else 0)
    tq = min(MLA_TQ, L)
    const = lambda shape: pl.BlockSpec(shape, lambda b, i: (0,) * len(shape))
    in_specs = [pl.BlockSpec((None, L, 768), lambda b, i: (b, 0, U_MLA // 768)),
                const((1, MLA_Q_RANK)), const((MLA_Q_RANK, MLA_HEADS * MLA_HW)),
                const((1, MLA_KV_RANK)), const((MLA_KV_RANK, MLA_HEADS * 256))]
    args = [u3, q_norm, wq_p, kv_norm, wkv]
    if rope:
        cos, sin, cckv, ckr = rope_args
        in_specs += [const((L, 128)), const((L, 128)),
                     pl.BlockSpec((None, None, Lk - L, MLA_KV_RANK), lambda b, i: (b, layer, 0, 0)),
                     pl.BlockSpec((None, None, Lk - L, MLA_ROPE), lambda b, i: (b, layer, 0, 0))]
        args += [cos, sin, cckv, ckr]
    out_specs = [pl.BlockSpec((None, tq, MLA_HEADS * MLA_V), lambda b, i: (b, i, 0))]
    out_shape = [jax.ShapeDtypeStruct((B, L, MLA_HEADS * MLA_V), BF16)]
    if emit:
        out_specs += [pl.BlockSpec((None, L, MLA_KV_RANK), lambda b, i: (b, 0, 0)),
                      pl.BlockSpec((None, L, MLA_ROPE), lambda b, i: (b, 0, 0))]
        out_shape += [jax.ShapeDtypeStruct((B, L, MLA_KV_RANK), F32),
                      jax.ShapeDtypeStruct((B, L, MLA_ROPE), F32)]
    res = pl.pallas_call(
        functools.partial(_mla_kernel, L=L, Lk=Lk, tq=tq, rope=rope, emit=emit),
        grid=(B, L // tq),
        in_specs=in_specs,
        out_specs=out_specs,
        out_shape=out_shape,
        scratch_shapes=[pltpu.VMEM((MLA_HEADS, Lk, MLA_HW), BF16),
                        pltpu.VMEM((MLA_HEADS, Lk, MLA_V), BF16)],
        compiler_params=_cparams(("arbitrary", "arbitrary")),
        name="mla_mixer",
    )(*args)
    return res if emit else (res[0], None, None)


def _outproj_kernel(x_ref, g_ref, yh_ref, yg_ref, ym_ref, wh_ref, wg_ref, wm_ref, o_ref):
    y = _dot(yh_ref[...], wh_ref[...]) + _dot(yg_ref[...], wg_ref[...]) + _dot(ym_ref[...], wm_ref[...])
    o_ref[...] = x_ref[...] + g_ref[...] * y


def _out_projection(x2d, mod, y_hy, y_gla, y_mla, w_out_b, layer, row_of_tile, tm):
    rows = x2d.shape[0]
    once = pl.Buffered(1)
    return pl.pallas_call(
        _outproj_kernel,
        grid=(rows // tm,),
        in_specs=[pl.BlockSpec((tm, D_MODEL), lambda i: (i, 0)),
                  _mod_spec(2, row_of_tile),
                  pl.BlockSpec((tm, HY_CH), lambda i: (i, 0)),
                  pl.BlockSpec((tm, GLA_VW), lambda i: (i, 0)),
                  pl.BlockSpec((tm, MLA_HEADS * MLA_V), lambda i: (i, 0)),
                  pl.BlockSpec((None, HY_CH, D_MODEL), lambda i: (layer, 0, 0), pipeline_mode=once),
                  pl.BlockSpec((None, GLA_VW, D_MODEL), lambda i: (layer, 1, 0), pipeline_mode=once),
                  pl.BlockSpec((None, MLA_HEADS * MLA_V, D_MODEL), lambda i: (layer, 1, 0), pipeline_mode=once)],
        out_specs=pl.BlockSpec((tm, D_MODEL), lambda i: (i, 0)),
        out_shape=jax.ShapeDtypeStruct((rows, D_MODEL), F32),
        compiler_params=_cparams(("arbitrary",)),
        name="out_projection",
    )(x2d, mod, y_hy, y_gla, y_mla, w_out_b, w_out_b, w_out_b)


FFN_TF = 512
FFN_PRE_ROWS = 64


def _ffn_kernel(x_ref, xn_ref, ln_ref, sh_ref, sc_ref, g_ref, wg_ref, wu_ref, wo_ref, lnf_ref, o_ref,
                ha_ref, hb_ref, acc_ref, *, final):
    i = pl.program_id(0)
    f = pl.program_id(1)
    tm = x_ref.shape[0]
    gain = ln_ref[...] * (1.0 + sc_ref[...])
    shift = sh_ref[...]

    @pl.when((i == 0) & (f == 0))
    def _():
        _adaln_all(xn_ref, gain, shift, ha_ref)

    @pl.when(f == 0)
    def _():
        acc_ref[...] = jnp.zeros_like(acc_ref)

    base = jnp.minimum(jnp.maximum(f - 1, 0) * FFN_PRE_ROWS, tm - FFN_PRE_ROWS)

    def step(cur_ref, nxt_ref):
        _adaln_slice(xn_ref, gain, shift, nxt_ref, base, FFN_PRE_ROWS)
        h = cur_ref[...]
        gate = _dot(h, wg_ref[...])
        up = _dot(h, wu_ref[...])
        act = (gate * jax.nn.sigmoid(gate) * up).astype(BF16)
        acc_ref[...] += _dot(act, wo_ref[...])

    @pl.when(i % 2 == 0)
    def _():
        step(ha_ref, hb_ref)

    @pl.when(i % 2 == 1)
    def _():
        step(hb_ref, ha_ref)

    @pl.when(f == pl.num_programs(1) - 1)
    def _():
        def body(t, carry):
            rows = pl.ds(pl.multiple_of(t * ADALN_ROWS, ADALN_ROWS), ADALN_ROWS)
            x = x_ref[rows, :] + g_ref[...] * acc_ref[rows, :]
            o_ref[rows, :] = _rms(x, lnf_ref[...]) if final else x
            return carry

        lax.fori_loop(0, tm // ADALN_ROWS, body, 0, unroll=4)


def _ffn(x2d, ln, mod, w_in_b, w_out_b, layer, ln_final, row_of_tile, tm, final):
    rows = x2d.shape[0]
    nf = D_FF // FFN_TF
    assert (nf - 1) * FFN_PRE_ROWS >= tm
    ahead = _lookahead(rows // tm)
    return pl.pallas_call(
        functools.partial(_ffn_kernel, final=final),
        grid=(rows // tm, nf),
        in_specs=[pl.BlockSpec((tm, D_MODEL), lambda i, f: (i, 0)),
                  pl.BlockSpec((tm, D_MODEL), lambda i, f: (ahead(i, f), 0)),
                  pl.BlockSpec((1, D_MODEL), lambda i, f: (0, 0)),
                  _mod_spec(3, row_of_tile, ahead),
                  _mod_spec(4, row_of_tile, ahead),
                  _mod_spec(5, row_of_tile),
                  pl.BlockSpec((None, D_MODEL, FFN_TF), lambda i, f: (layer, 0, f)),
                  pl.BlockSpec((None, D_MODEL, FFN_TF), lambda i, f: (layer, 0, nf + f)),
                  pl.BlockSpec((None, FFN_TF, D_MODEL), lambda i, f: (layer, f, 0)),
                  pl.BlockSpec((1, D_MODEL), lambda i, f: (0, 0))],
        out_specs=pl.BlockSpec((tm, D_MODEL), lambda i, f: (i, 0)),
        out_shape=jax.ShapeDtypeStruct((rows, D_MODEL), F32),
        scratch_shapes=[pltpu.VMEM((tm, D_MODEL), BF16), pltpu.VMEM((tm, D_MODEL), BF16),
                        pltpu.VMEM((tm, D_MODEL), F32)],
        compiler_params=_cparams(("arbitrary", "arbitrary")),
        name="ffn",
    )(x2d, x2d, ln, mod, mod, mod, w_in_b, w_in_b, w_out_b, ln_final)


def _dft_tables(L):
    k = np.arange(L)
    ang = np.pi * ((k[:, None] * k[None, :]) % (2 * L)) / L
    c = np.cos(ang)
    s = np.sin(ang)
    sp = s.copy()
    sp[0, :] = 1.0 - 2.0 * (k % 2)
    as32 = lambda a: jnp.asarray(a.astype(np.float32))
    return as32(c), as32(s), as32(sp), as32(sp.T.copy())


def _hyena_consts(L):
    c, s, sp, spt = _dft_tables(L)
    c_hi, c_lo = _split(c)
    s_hi, s_lo = _split(s)
    t = jnp.linspace(0.0, 1.0, L, dtype=F32)[:, None]
    w = 2.0 * math.pi * jnp.arange(L, dtype=F32)[:, None] / L
    f = jnp.linspace(1e-4, HY_BANDS - 1, HY_BANDS, dtype=F32)
    zpos = jnp.concatenate([t, jnp.cos(f * w), -jnp.sin(f * w)], axis=-1)
    zpos = jnp.pad(zpos, ((0, 0), (0, 64 - HY_EMB)))
    min_decay = math.log(HY_TARGET) / HY_SLOW_DECAY
    max_decay = math.log(HY_TARGET) / HY_FAST_DECAY
    delta = jnp.abs(jnp.linspace(min_decay, max_decay, HY_CH, dtype=F32))
    decay = jnp.exp(-t * delta)
    filt_consts = (zpos, decay, c_hi, c_lo, s_hi, s_lo)
    main_consts = (c_hi, sp.astype(BF16), spt.astype(BF16))
    return filt_consts, main_consts


def _rope_tables(L):
    t = jnp.arange(L)
    half = MLA_ROPE // 2
    inv = ROPE_THETA ** (-jnp.arange(0, half, 2, dtype=F32) / half)
    ang_r = (t // GRID_W).astype(F32)[:, None] * inv
    ang_c = (t % GRID_W).astype(F32)[:, None] * inv
    cr, sr, cc, sc = jnp.cos(ang_r), jnp.sin(ang_r), jnp.cos(ang_c), jnp.sin(ang_c)
    cos = jnp.concatenate([cr, cr, cc, cc, jnp.ones((L, 128 - MLA_ROPE), F32)], axis=1)
    sin = jnp.concatenate([-sr, sr, -sc, sc, jnp.zeros((L, 128 - MLA_ROPE), F32)], axis=1)
    return cos, sin


def _prep_w_in_tail(w):
    z = lambda n: jnp.zeros((DEPTH, D_MODEL, n), w.dtype)
    mla0 = IN_MAIN + 2 * GLA_LOWRANK
    return jnp.concatenate([w[:, :, mla0:], z(64), w[:, :, IN_MAIN:mla0], z(256 - 2 * GLA_LOWRANK)],
                           axis=2).astype(BF16)


def _prep_w_uq(w):
    w = w.reshape(MLA_Q_RANK, MLA_HEADS, MLA_NOPE + MLA_ROPE)
    w = jnp.pad(w, ((0, 0), (0, 0), (0, MLA_HW - MLA_NOPE - MLA_ROPE)))
    return w.reshape(MLA_Q_RANK, MLA_HEADS * MLA_HW).astype(BF16)


def _prep_gla_decay(wa_f, ba_f, wa_b, ba_b):
    wa = jnp.zeros((256, 2 * GLA_QK), F32)
    wa = wa.at[0:GLA_LOWRANK, 0:GLA_QK].set(wa_f)
    wa = wa.at[GLA_LOWRANK:2 * GLA_LOWRANK, GLA_QK:].set(wa_b)
    return wa, jnp.concatenate([ba_f, ba_b])[None, :]


def _trunk_layer(x2d, B, L, mod, row_of_tile, lw, consts, ctx, final, ln_final):
    tm = 1024
    tm_out = 512
    u = _in_projection(x2d, lw['ln_mix'], mod, lw['w_in'], lw['w_in_tail'], lw['layer'], row_of_tile(tm), tm)
    u3 = u.reshape(B, L, U_W)
    filt_consts, main_consts = consts['hyena']
    filt = _hyena_filters(L, filt_consts, lw['hy_w1'], lw['hy_b1'], lw['hy_freq'], lw['hy_w2'],
                          lw['hy_b2'], lw['hy_w3'])
    y_hy = _hyena_mixer(u3, lw['hy_conv_w'], lw['hy_conv_b'], lw['hy_skip'], filt, main_consts, L)
    if ctx is None:
        y_gla, s_f, s_b = _gla_mixer(u3, lw['gla_wa'], lw['gla_ba'], lw['gla_norm'], None, lw['layer'], L, True)
        y_mla, ckv, krope = _mla_mixer(u3, lw['mla_q_norm'], lw['mla_w_uq'], lw['mla_kv_norm'],
                                       lw['mla_w_ukv'], None, lw['layer'], L, True)
        extras = (ckv, krope, s_f, s_b)
    else:
        cache_ckv, cache_krope, s0f, s0b = ctx
        y_gla, _, _ = _gla_mixer(u3, lw['gla_wa'], lw['gla_ba'], lw['gla_norm'], (s0f, s0b), lw['layer'], L, False)
        cos, sin = consts['rope']
        y_mla, _, _ = _mla_mixer(u3, lw['mla_q_norm'], lw['mla_w_uq'], lw['mla_kv_norm'], lw['mla_w_ukv'],
                                 (cos, sin, cache_ckv, cache_krope), lw['layer'], L, False)
        extras = None
    rows = B * L
    x2d = _out_projection(x2d, mod, y_hy.reshape(rows, -1), y_gla.reshape(rows, -1), y_mla.reshape(rows, -1),
                          lw['w_out'], lw['layer'], row_of_tile(tm_out), tm_out)
    x2d = _ffn(x2d, lw['ln_ffn'], mod, lw['w_ffn_in'], lw['w_ffn_out'], lw['layer'], ln_final, row_of_tile(tm_out),
               tm_out, final)
    return x2d, extras


def kernel(x_prompt, x_sample, cache_mla_ckv, cache_mla_krope, state_gla_fwd, state_gla_bwd, c, c_ctx, w_mod, b_mod, ln_mix, w_in, hy_conv_w, hy_conv_b, hy_filt_w1, hy_filt_b1, hy_filt_freq, hy_filt_w2, hy_filt_b2, hy_filt_w3, hy_skip, gla_wa_f, gla_ba_f, gla_wa_b, gla_ba_b, gla_norm, mla_q_norm, mla_w_uq, mla_kv_norm, mla_w_ukv, w_out, ln_ffn, w_ffn_in, w_ffn_out, ln_final):
    Bc, Lc, _ = x_prompt.shape
    Bl, Ll, _ = x_sample.shape
    assert 1 + Bl <= MOD_ROWS

    c_all = jnp.concatenate([c_ctx[None, :], c, jnp.zeros((MOD_ROWS - 1 - Bl, D_MODEL), F32)], axis=0)
    mod_all = _modulation(c_all, w_mod, b_mod).reshape(DEPTH, MOD_ROWS, 6, 1, D_MODEL)

    consts_ctx = {'hyena': _hyena_consts(Lc)}
    consts_lat = {'hyena': _hyena_consts(Ll), 'rope': _rope_tables(Ll)}
    ctx_rows = lambda tm: (lambda i: 0)
    lat_rows = lambda tm: (lambda i: 1 + (i * tm) // Ll)
    lnf = ln_final[None, :]

    x_ctx = x_prompt.reshape(Bc * Lc, D_MODEL)
    x_lat = x_sample.reshape(Bl * Ll, D_MODEL)
    w_in_b = w_in.astype(BF16)
    w_in_tail = _prep_w_in_tail(w_in)
    w_out_b = w_out.astype(BF16)
    w_ffn_in_b = w_ffn_in.astype(BF16)
    w_ffn_out_b = w_ffn_out.astype(BF16)
    ckv_l, krope_l, sf_l, sb_l = [], [], [], []
    for l in range(DEPTH):
        wa, ba = _prep_gla_decay(gla_wa_f[l], gla_ba_f[l], gla_wa_b[l], gla_ba_b[l])
        lw = {
            'layer': l, 'ln_mix': ln_mix[l][None, :], 'w_in': w_in_b, 'w_in_tail': w_in_tail,
            'hy_conv_w': hy_conv_w[l], 'hy_conv_b': hy_conv_b[l][None, :],
            'hy_w1': jnp.pad(hy_filt_w1[l], ((0, 64 - HY_EMB), (0, 0))), 'hy_b1': hy_filt_b1[l][None, :],
            'hy_freq': hy_filt_freq[l], 'hy_w2': hy_filt_w2[l], 'hy_b2': hy_filt_b2[l][None, :],
            'hy_w3': hy_filt_w3[l], 'hy_skip': hy_skip[l],
            'gla_wa': wa, 'gla_ba': ba, 'gla_norm': gla_norm[l][None, :],
            'mla_q_norm': mla_q_norm[l][None, :], 'mla_w_uq': _prep_w_uq(mla_w_uq[l]),
            'mla_kv_norm': mla_kv_norm[l][None, :], 'mla_w_ukv': mla_w_ukv[l].astype(BF16),
            'w_out': w_out_b, 'ln_ffn': ln_ffn[l][None, :],
            'w_ffn_in': w_ffn_in_b, 'w_ffn_out': w_ffn_out_b,
        }
        final = l == DEPTH - 1
        mod = mod_all[l]
        x_ctx, (ckv, krope, s_f, s_b) = _trunk_layer(x_ctx, Bc, Lc, mod, ctx_rows, lw, consts_ctx, None,
                                                     final, lnf)
        ckv_l.append(ckv)
        krope_l.append(krope)
        sf_l.append(s_f)
        sb_l.append(s_b)
        ctx = (cache_mla_ckv, cache_mla_krope, state_gla_fwd, state_gla_bwd)
        x_lat, _ = _trunk_layer(x_lat, Bl, Ll, mod, lat_rows, lw, consts_lat, ctx, final, lnf)
    return (x_ctx.reshape(Bc, Lc, D_MODEL), x_lat.reshape(Bl, Ll, D_MODEL),
            jnp.stack(ckv_l, axis=1), jnp.stack(krope_l, axis=1),
            jnp.stack(sf_l, axis=1), jnp.stack(sb_l, axis=1))
```

```python
import functools
import math

import numpy as np
import jax
import jax.numpy as jnp
from jax import lax
from jax.experimental import pallas as pl
from jax.experimental.pallas import tpu as pltpu

F32 = jnp.float32
BF16 = jnp.bfloat16

D_MODEL = 2048
DEPTH = 2
GRID_W = 64
EPS = 1e-6
HY_CH = 512
HY_ORDER = 2
HY_BANDS = 16
HY_EMB = 1 + 2 * HY_BANDS
HY_FF = 64
HY_FAST_DECAY = 0.3
HY_SLOW_DECAY = 1.5
HY_TARGET = 1e-2
HY_IN = 3 * HY_CH
GLA_HEADS = 4
GLA_DK = 64
GLA_DV = 128
GLA_LOWRANK = 16
GLA_TAU = 16.0
GLA_CHUNK = 64
GLA_QK = GLA_HEADS * GLA_DK
GLA_VW = GLA_HEADS * GLA_DV
MLA_HEADS = 8
MLA_Q_RANK = 384
MLA_KV_RANK = 256
MLA_NOPE = 128
MLA_ROPE = 64
MLA_V = 128
ROPE_THETA = 10000.0
D_FF = -(-8 * D_MODEL // (3 * 256)) * 256

U_HY = 0
U_GLA = 1536
U_MLA = 3072
U_GLA_A = 3840
U_W = 4096
MOD_ROWS = 16

VMEM_LIMIT_V7X = 48 * 1024 * 1024


def _cparams(sem):
    return pltpu.CompilerParams(dimension_semantics=sem, vmem_limit_bytes=VMEM_LIMIT_V7X)


def _dot(a, b):
    return jnp.dot(a, b, preferred_element_type=F32)


def _dot_nt(a, b):
    return lax.dot_general(a, b, (((1,), (1,)), ((), ())), preferred_element_type=F32)


def _split(x):
    hi = x.astype(BF16)
    lo = (x - hi.astype(F32)).astype(BF16)
    return hi, lo


def _dot3(a, b):
    a_hi, a_lo = _split(a)
    b_hi, b_lo = _split(b)
    return _dot(a_hi, b_hi) + (_dot(a_lo, b_hi) + _dot(a_hi, b_lo))


def _rms(x, g):
    ms = jnp.mean(x * x, axis=-1, keepdims=True)
    return x * lax.rsqrt(ms + EPS) * g


ADALN_ROWS = 16


def _adaln_group(x_ref, gain, shift, h_ref, r):
    rows = pl.ds(pl.multiple_of(r, ADALN_ROWS), ADALN_ROWS)
    x = x_ref[rows, :]
    ms = jnp.mean(x * x, axis=-1, keepdims=True)
    h_ref[rows, :] = (x * lax.rsqrt(ms + EPS) * gain + shift).astype(BF16)


def _adaln_all(x_ref, gain, shift, h_ref):
    def body(i, carry):
        _adaln_group(x_ref, gain, shift, h_ref, i * ADALN_ROWS)
        return carry

    lax.fori_loop(0, x_ref.shape[0] // ADALN_ROWS, body, 0, unroll=4)


def _adaln_slice(x_ref, gain, shift, h_ref, base, nrows):
    for t in range(nrows // ADALN_ROWS):
        _adaln_group(x_ref, gain, shift, h_ref, base + t * ADALN_ROWS)


def _mod_kernel(c_ref, w_ref, b_ref, o_ref):
    c = c_ref[...]
    s = (c * jax.nn.sigmoid(c)).astype(BF16)
    o_ref[...] = _dot(s, w_ref[...].astype(BF16)) + b_ref[...]


def _modulation(c_all, w_mod, b_mod):
    tn = 1024
    n6 = 6 * D_MODEL
    return pl.pallas_call(
        _mod_kernel,
        grid=(DEPTH, n6 // tn),
        in_specs=[pl.BlockSpec((MOD_ROWS, D_MODEL), lambda l, j: (0, 0)),
                  pl.BlockSpec((None, D_MODEL, tn), lambda l, j: (l, 0, j)),
                  pl.BlockSpec((None, 1, tn), lambda l, j: (l, 0, j))],
        out_specs=pl.BlockSpec((None, MOD_ROWS, tn), lambda l, j: (l, 0, j)),
        out_shape=jax.ShapeDtypeStruct((DEPTH, MOD_ROWS, n6), F32),
        compiler_params=_cparams(("arbitrary", "arbitrary")),
        name="modulation",
    )(c_all, w_mod, b_mod.reshape(DEPTH, 1, n6))


def _mod_spec(chunk, row_of_tile, tile_of_step=lambda i, *_: i):
    return pl.BlockSpec((None, None, 1, D_MODEL), lambda *g: (row_of_tile(tile_of_step(*g)), chunk, 0, 0))


def _lookahead(n_tiles):
    return lambda i, j: jnp.minimum(i + jnp.minimum(j, 1), n_tiles - 1)


IN_TN = 1024
IN_MAIN = 3072
IN_PRE_ROWS = 352
IN_PRE_STRIDE = 336


def _inproj_kernel(x_ref, ln_ref, sh_ref, sc_ref, w_ref, wt_ref, o_ref, ha_ref, hb_ref):
    i = pl.program_id(0)
    j = pl.program_id(1)
    gain = ln_ref[...] * (1.0 + sc_ref[...])
    shift = sh_ref[...]

    @pl.when((i == 0) & (j == 0))
    def _():
        _adaln_all(x_ref, gain, shift, ha_ref)

    base = jnp.clip(j - 1, 0, 2) * IN_PRE_STRIDE

    def step(cur_ref, nxt_ref):
        @pl.when(j < IN_MAIN // IN_TN)
        def _():
            _adaln_slice(x_ref, gain, shift, nxt_ref, base, IN_PRE_ROWS)
            o_ref[...] = _dot(cur_ref[...], w_ref[...])

        @pl.when(j >= IN_MAIN // IN_TN)
        def _():
            _adaln_slice(x_ref, gain, shift, nxt_ref, base, IN_PRE_ROWS)
            o_ref[...] = _dot(cur_ref[...], wt_ref[...])

    @pl.when(i % 2 == 0)
    def _():
        step(ha_ref, hb_ref)

    @pl.when(i % 2 == 1)
    def _():
        step(hb_ref, ha_ref)


def _in_projection(x2d, ln, mod, w_in_b, w_tail, layer, row_of_tile, tm):
    rows = x2d.shape[0]
    n_main = IN_MAIN // IN_TN
    ahead = _lookahead(rows // tm)
    return pl.pallas_call(
        _inproj_kernel,
        grid=(rows // tm, U_W // IN_TN),
        in_specs=[pl.BlockSpec((tm, D_MODEL), lambda i, j: (ahead(i, j), 0)),
                  pl.BlockSpec((1, D_MODEL), lambda i, j: (0, 0)),
                  _mod_spec(0, row_of_tile, ahead),
                  _mod_spec(1, row_of_tile, ahead),
                  pl.BlockSpec((None, None, D_MODEL, IN_TN), lambda i, j: (layer, jnp.minimum(j, n_main - 1), 0, 0)),
                  pl.BlockSpec((None, D_MODEL, U_W - IN_MAIN), lambda i, j: (layer, 0, 0))],
        out_specs=pl.BlockSpec((tm, IN_TN), lambda i, j: (i, j)),
        out_shape=jax.ShapeDtypeStruct((rows, U_W), F32),
        scratch_shapes=[pltpu.VMEM((tm, D_MODEL), BF16), pltpu.VMEM((tm, D_MODEL), BF16)],
        compiler_params=_cparams(("arbitrary", "arbitrary")),
        name="in_projection",
    )(x2d, ln, mod, mod, w_in_b, w_tail)


def _filter_kernel(z_ref, w1_ref, b1_ref, fr_ref, w2_ref, b2_ref, w3f_ref, w3b_ref, dec_ref,
                   chi_ref, clo_ref, shi_ref, slo_ref, kr_ref, ki_ref, kn_ref, *, L):
    h = jnp.sin(fr_ref[0:1, :] * (_dot3(z_ref[...], w1_ref[...]) + b1_ref[...]))
    h = jnp.sin(fr_ref[1:2, :] * (_dot3(h, w2_ref[...]) + b2_ref[...]))
    dec = dec_ref[...]
    row = lax.broadcasted_iota(jnp.int32, (L, 1), 0)
    kf = _dot3(h, w3f_ref[...]) * dec
    kb = jnp.where(row == 0, 0.0, _dot3(h, w3b_ref[...]) * dec)
    p_hi, p_lo = _split(kf + kb)
    m_hi, m_lo = _split(kf - kb)
    chi, clo = chi_ref[...], clo_ref[...]
    shi, slo = shi_ref[...], slo_ref[...]
    kc = _dot(chi, p_hi) + (_dot(clo, p_hi) + _dot(chi, p_lo))
    ks = _dot(shi, m_hi) + (_dot(slo, m_hi) + _dot(shi, m_lo))
    sign = jnp.where((row & 1) == 0, 1.0, -1.0)
    kn = jnp.sum((kf + kb) * sign, axis=0, keepdims=True)
    kr_ref[...] = kc * jnp.where(row == 0, 0.5 / L, 1.0 / L)
    ki_ref[...] = ks * (-1.0 / L)
    kn_ref[...] = kn * (0.5 / L)


def _hyena_filters(L, consts, w1p, b1, freq, w2, b2, w3):
    zpos, decay, c_hi, c_lo, s_hi, s_lo = consts
    full = lambda shape: pl.BlockSpec(shape, lambda o: (0,) * len(shape))
    return pl.pallas_call(
        functools.partial(_filter_kernel, L=L),
        grid=(HY_ORDER,),
        in_specs=[full((L, 64)), full((64, HY_FF)), full((1, HY_FF)), full((2, HY_FF)),
                  full((HY_FF, HY_FF)), full((1, HY_FF)),
                  pl.BlockSpec((HY_FF, HY_CH), lambda o: (0, o)),
                  pl.BlockSpec((HY_FF, HY_CH), lambda o: (0, HY_ORDER + o)),
                  full((L, HY_CH)), full((L, L)), full((L, L)), full((L, L)), full((L, L))],
        out_specs=[pl.BlockSpec((None, L, HY_CH), lambda o: (o, 0, 0)),
                   pl.BlockSpec((None, L, HY_CH), lambda o: (o, 0, 0)),
                   pl.BlockSpec((None, 1, HY_CH), lambda o: (o, 0, 0))],
        out_shape=[jax.ShapeDtypeStruct((HY_ORDER, L, HY_CH), F32),
                   jax.ShapeDtypeStruct((HY_ORDER, L, HY_CH), F32),
                   jax.ShapeDtypeStruct((HY_ORDER, 1, HY_CH), F32)],
        compiler_params=_cparams(("arbitrary",)),
        name="hyena_filters",
    )(zpos, w1p, b1, freq, w2, b2, w3, w3, decay, c_hi, c_lo, s_hi, s_lo)


HY_CG = 256


def _hyena_kernel(v_ref, x1_ref, x2_ref, wv_ref, w1_ref, w2_ref, bv_ref, b1_ref, b2_ref, skip_ref,
                  kr_ref, ki_ref, kn_ref, c_ref, sp_ref, spt_ref, o_ref, *, L):
    row = lax.broadcasted_iota(jnp.int32, (L, 1), 0)
    first = row == 0
    last = row == L - 1
    groups = [slice(g * HY_CG, (g + 1) * HY_CG) for g in range(HY_CH // HY_CG)]

    def short_conv(u_ref, w_ref, b_ref, cols):
        u = u_ref[:, cols]
        prev = jnp.where(first, 0.0, pltpu.roll(u, 1, axis=0))
        nxt = jnp.where(last, 0.0, pltpu.roll(u, L - 1, axis=0))
        return prev * w_ref[0:1, cols] + u * w_ref[1:2, cols] + nxt * w_ref[2:3, cols] + b_ref[:, cols]

    gate_refs = ((x1_ref, w1_ref, b1_ref), (x2_ref, w2_ref, b2_ref))
    z = [short_conv(v_ref, wv_ref, bv_ref, cols) for cols in groups]
    for o in range(HY_ORDER):
        zb = [zz.astype(BF16) for zz in z]
        a = [_dot(c_ref[...], x) for x in zb]
        s = [_dot(sp_ref[...], x) for x in zb]
        conv = []
        for g, cols in enumerate(groups):
            kr = kr_ref[o, :, cols]
            ki = ki_ref[o, :, cols]
            qt = a[g] * kr + s[g] * ki
            qb = s[g] * jnp.where(first, kn_ref[o, :, cols], kr) - a[g] * ki
            conv.append(_dot(c_ref[...], qt.astype(BF16)) + _dot(spt_ref[...], qb.astype(BF16)))
        z = [short_conv(*gate_refs[o], cols) * (conv[g] + z[g] * skip_ref[o:o + 1, cols])
             for g, cols in enumerate(groups)]
    for g, cols in enumerate(groups):
        o_ref[:, cols] = z[g].astype(BF16)


def _hyena_mixer(u3, conv_w, conv_b, skip, filt, dft, L):
    B = u3.shape[0]
    kr, ki, kn = filt
    c_b, sp_b, spt_b = dft
    once = pl.Buffered(1)
    ublk = lambda part: pl.BlockSpec((None, L, HY_CH), lambda b: (b, 0, part))
    wblk = lambda part: pl.BlockSpec((3, HY_CH), lambda b: (0, part))
    bblk = lambda part: pl.BlockSpec((1, HY_CH), lambda b: (0, part))
    fblk = lambda rows: pl.BlockSpec((HY_ORDER, rows, HY_CH), lambda b: (0, 0, 0), pipeline_mode=once)
    mat = pl.BlockSpec((L, L), lambda b: (0, 0), pipeline_mode=once)
    return pl.pallas_call(
        functools.partial(_hyena_kernel, L=L),
        grid=(B,),
        in_specs=[ublk(0), ublk(1), ublk(2), wblk(0), wblk(1), wblk(2), bblk(0), bblk(1), bblk(2),
                  pl.BlockSpec((HY_ORDER, HY_CH), lambda b: (0, 0)),
                  fblk(L), fblk(L), fblk(1), mat, mat, mat],
        out_specs=pl.BlockSpec((None, L, HY_CH), lambda b: (b, 0, 0)),
        out_shape=jax.ShapeDtypeStruct((B, L, HY_CH), BF16),
        compiler_params=_cparams(("arbitrary",)),
        name="hyena_mixer",
    )(u3, u3, u3, conv_w, conv_w, conv_w, conv_b, conv_b, conv_b, skip, kr, ki, kn, c_b, sp_b, spt_b)


GLA_PAIR = 2 * GLA_CHUNK
GLA_STATES = 2 * GLA_HEADS
GLA_SCAN = 256


def _gla_kernel(*refs, L, has_state, emit_state):
    it = iter(refs)
    qkvg_ref, a_ref, wa_ref, ba_ref, ng_ref, scan_ref = (next(it) for _ in range(6))
    st0_refs = (next(it), next(it)) if has_state else None
    y_ref = next(it)
    stout_refs = (next(it), next(it)) if emit_state else None
    qd_s, ki_s, ke_s, qi_s, ku_s, et_s, vb_s, vt_s, o_s, st_s = (next(it) for _ in range(10))

    npair = L // GLA_PAIR
    pre = _dot3(a_ref[...], wa_ref[...]) + ba_ref[...]
    la = (jnp.minimum(pre, 0.0) - jnp.log(1.0 + jnp.exp(-jnp.abs(pre)))) * (1.0 / GLA_TAU)

    scan = scan_ref[...]
    pfx, tot = [], []
    for r in range(0, L, GLA_SCAN):
        x = la[r:r + GLA_SCAN]
        hi = x.astype(BF16)
        r1 = x - hi.astype(F32)
        mid = r1.astype(BF16)
        lo = (r1 - mid.astype(F32)).astype(BF16)
        res = _dot(scan, hi) + (_dot(scan, mid) + _dot(scan, lo))
        pfx.append(res[:GLA_SCAN])
        tot.append(res[GLA_SCAN:])
    pfx = jnp.concatenate(pfx, axis=0) if len(pfx) > 1 else pfx[0]
    tot = jnp.concatenate(tot, axis=0) if len(tot) > 1 else tot[0]

    upper = (lax.broadcasted_iota(jnp.int32, (L, 1), 0) & GLA_CHUNK) != 0
    q = qkvg_ref[:, 0:GLA_QK] * (GLA_DK ** -0.5)
    k = qkvg_ref[:, GLA_QK:2 * GLA_QK]
    for dirn in range(2):
        cols = slice(dirn * GLA_QK, (dirn + 1) * GLA_QK)
        if dirn == 0:
            cum = pfx[:, cols]
            rest = tot[:, cols] - cum
            first = jnp.logical_not(upper)
        else:
            rest = pfx[:, cols] - la[:, cols]
            cum = tot[:, cols] - rest
            first = upper
        et = jnp.exp(tot[:, cols])
        et_other = jnp.where(upper, pltpu.roll(et, GLA_CHUNK, axis=0), pltpu.roll(et, L - GLA_CHUNK, axis=0))
        qd = q * jnp.exp(cum)
        ke = k * jnp.exp(rest)
        et_s[dirn] = et
        qd_s[dirn] = qd.astype(BF16)
        ki_s[dirn] = (k * jnp.exp(-cum)).astype(BF16)
        ke_s[dirn] = ke.astype(BF16)
        qi_s[dirn] = jnp.where(first, qd, qd * et_other).astype(BF16)
        ku_s[dirn] = jnp.where(first, ke * et_other, ke).astype(BF16)
    vb_s[...] = qkvg_ref[:, 2 * GLA_QK:2 * GLA_QK + GLA_VW].astype(BF16)

    def transpose_v(n2, carry):
        r0 = pl.multiple_of(n2 * GLA_PAIR, GLA_PAIR)
        for h in range(GLA_HEADS):
            vp = qkvg_ref[pl.ds(r0, GLA_PAIR), 2 * GLA_QK + h * GLA_DV:2 * GLA_QK + (h + 1) * GLA_DV]
            vt_s[n2, h * GLA_DV:(h + 1) * GLA_DV, :] = vp.T.astype(BF16)
        return carry

    lax.fori_loop(0, npair, transpose_v, 0)

    if has_state:
        zero = jnp.zeros((GLA_DK, GLA_DV), F32)
        for dirn in range(2):
            for h in range(GLA_HEADS):
                s0 = st0_refs[dirn][h]
                both = jnp.concatenate([s0, zero] if h % 2 == 0 else [zero, s0], axis=0)
                idx = dirn * GLA_HEADS + h
                st_s[idx * GLA_DV:(idx + 1) * GLA_DV, :] = both.T
    else:
        st_s[...] = jnp.zeros_like(st_s)

    lane = lax.broadcasted_iota(jnp.int32, (1, 2 * GLA_DK), 1)
    head_lanes = (lane < GLA_DK, lane >= GLA_DK)
    ri = lax.broadcasted_iota(jnp.int32, (GLA_PAIR, GLA_PAIR), 0)
    ci = lax.broadcasted_iota(jnp.int32, (GLA_PAIR, GLA_PAIR), 1)
    same_chunk = ((ri ^ ci) & GLA_CHUNK) == 0
    keep = (ci <= ri, ci >= ri)

    def pair_step(n, carry):
        for dirn in range(2):
            n2 = n if dirn == 0 else npair - 1 - n
            r0 = pl.multiple_of(n2 * GLA_PAIR, GLA_PAIR)
            rows = pl.ds(r0, GLA_PAIR)
            for p in range(GLA_HEADS // 2):
                lanes = slice(p * 2 * GLA_DK, (p + 1) * 2 * GLA_DK)
                qd = qd_s[dirn, rows, lanes]
                qi = qi_s[dirn, rows, lanes]
                zeros = jnp.zeros_like(qd)
                dec = (et_s[dirn, pl.ds(r0, 8), lanes][0:1, :]
                       * et_s[dirn, pl.ds(r0 + GLA_CHUNK, 8), lanes][0:1, :])
                q2 = jnp.concatenate([jnp.where(head_lanes[0], qd, zeros), jnp.where(head_lanes[1], qd, zeros)],
                                     axis=0)
                att_d = _dot_nt(q2, ki_s[dirn, rows, lanes])
                att_o = _dot_nt(q2, ke_s[dirn, rows, lanes])
                s0 = (dirn * GLA_HEADS + 2 * p) * GLA_DV
                st2 = st_s[s0:s0 + 2 * GLA_DV, :]
                vt2 = vt_s[n2, 2 * p * GLA_DV:(2 * p + 2) * GLA_DV, :]
                st_s[s0:s0 + 2 * GLA_DV, :] = st2 * dec + _dot(vt2, ku_s[dirn, rows, lanes])
                for hh in range(2):
                    h = 2 * p + hh
                    hr = slice(hh * GLA_PAIR, (hh + 1) * GLA_PAIR)
                    att = jnp.where(keep[dirn], jnp.where(same_chunk, att_d[hr], att_o[hr]), 0.0)
                    qm = jnp.where(head_lanes[hh], qi, zeros)
                    o = (_dot(att.astype(BF16), vb_s[rows, h * GLA_DV:(h + 1) * GLA_DV])
                         + _dot_nt(qm, st2[hh * GLA_DV:(hh + 1) * GLA_DV].astype(BF16)))
                    o_s[dirn, rows, h * GLA_DV:(h + 1) * GLA_DV] = o
        return carry

    lax.fori_loop(0, npair, pair_step, 0)

    for h in range(GLA_HEADS):
        cols = slice(h * GLA_DV, (h + 1) * GLA_DV)
        o = _rms(o_s[0, :, cols] + o_s[1, :, cols], ng_ref[...])
        g = qkvg_ref[:, 2 * GLA_QK + GLA_VW + h * GLA_DV:2 * GLA_QK + GLA_VW + (h + 1) * GLA_DV]
        y_ref[:, cols] = (o * (g * jax.nn.sigmoid(g))).astype(BF16)
    if emit_state:
        for dirn in range(2):
            for h in range(GLA_HEADS):
                idx = dirn * GLA_HEADS + h
                half = (h % 2) * GLA_DK
                stout_refs[dirn][h] = st_s[idx * GLA_DV:(idx + 1) * GLA_DV, :].T[half:half + GLA_DK, :]


def _gla_scan_matrix():
    i = np.arange(GLA_SCAN)
    same = (i[:, None] // GLA_CHUNK) == (i[None, :] // GLA_CHUNK)
    tri = same & (i[None, :] <= i[:, None])
    return jnp.asarray(np.concatenate([tri, same], axis=0).astype(np.float32)).astype(BF16)


def _gla_mixer(u3, wa_p, ba_p, norm_g, st0, layer, L, emit_state):
    B = u3.shape[0]
    has_state = st0 is not None
    head_state = (GLA_HEADS, GLA_DK, GLA_DV)
    in_specs = [pl.BlockSpec((None, L, 1536), lambda b: (b, 0, U_GLA // 1536)),
                pl.BlockSpec((None, L, 256), lambda b: (b, 0, U_GLA_A // 256)),
                pl.BlockSpec((256, 2 * GLA_QK), lambda b: (0, 0)),
                pl.BlockSpec((1, 2 * GLA_QK), lambda b: (0, 0)),
                pl.BlockSpec((1, GLA_DV), lambda b: (0, 0)),
                pl.BlockSpec((2 * GLA_SCAN, GLA_SCAN), lambda b: (0, 0))]
    args = [u3, u3, wa_p, ba_p, norm_g, _gla_scan_matrix()]
    if has_state:
        in_specs += [pl.BlockSpec((None, None) + head_state, lambda b: (b, layer, 0, 0, 0))] * 2
        args += list(st0)
    out_specs = [pl.BlockSpec((None, L, GLA_VW), lambda b: (b, 0, 0))]
    out_shape = [jax.ShapeDtypeStruct((B, L, GLA_VW), BF16)]
    if emit_state:
        out_specs += [pl.BlockSpec((None,) + head_state, lambda b: (b, 0, 0, 0))] * 2
        out_shape += [jax.ShapeDtypeStruct((B,) + head_state, F32)] * 2
    npair = L // GLA_PAIR
    qk_bf16 = pltpu.VMEM((2, L, GLA_QK), BF16)
    res = pl.pallas_call(
        functools.partial(_gla_kernel, L=L, has_state=has_state, emit_state=emit_state),
        grid=(B,),
        in_specs=in_specs,
        out_specs=out_specs,
        out_shape=out_shape,
        scratch_shapes=[qk_bf16,
                        qk_bf16,
                        qk_bf16,
                        qk_bf16,
                        qk_bf16,
                        pltpu.VMEM((2, L, GLA_QK), F32),
                        pltpu.VMEM((L, GLA_VW), BF16),
                        pltpu.VMEM((npair, GLA_VW, GLA_PAIR), BF16),
                        pltpu.VMEM((2, L, GLA_VW), F32),
                        pltpu.VMEM((GLA_STATES * GLA_DV, 2 * GLA_DK), F32)],
        compiler_params=_cparams(("arbitrary",)),
        name="gla_mixer",
    )(*args)
    return (res[0], res[1], res[2]) if emit_state else (res[0], None, None)


MLA_HW = 256
MLA_TQ = 512
MLA_KCH = 256


def _rope(x, cos, sin):
    lane = lax.broadcasted_iota(jnp.int32, x.shape, 1)
    partner = jnp.where((lane & 16) == 0, pltpu.roll(x, 112, axis=1), pltpu.roll(x, 16, axis=1))
    return x * cos + partner * sin


def _mla_kernel(*refs, L, Lk, tq, rope, emit):
    it = iter(refs)
    u_ref, qn_ref, wq_ref, kvn_ref, wkv_ref = (next(it) for _ in range(5))
    cos_ref, sin_ref, cckv_ref, ckr_ref = (next(it) for _ in range(4)) if rope else (None,) * 4
    y_ref = next(it)
    ckv_out, kr_out = (next(it), next(it)) if emit else (None, None)
    kf_s, v_s = next(it), next(it)

    qi = pl.program_id(1)

    def project(keys, rope_part, r0):
        kv = _dot(keys, wkv_ref[...])
        for h in range(MLA_HEADS):
            kf_s[h, r0:r0 + MLA_KCH, 0:MLA_NOPE] = kv[:, h * 256:h * 256 + MLA_NOPE].astype(BF16)
            kf_s[h, r0:r0 + MLA_KCH, MLA_NOPE:MLA_HW] = rope_part
            v_s[h, r0:r0 + MLA_KCH, :] = kv[:, h * 256 + MLA_NOPE:(h + 1) * 256].astype(BF16)

    @pl.when(qi == 0)
    def _():
        for r0 in range(0, L, MLA_KCH):
            ckvn = _rms(u_ref[r0:r0 + MLA_KCH, MLA_Q_RANK:MLA_Q_RANK + MLA_KV_RANK], kvn_ref[...])
            kr = u_ref[r0:r0 + MLA_KCH, 640:768]
            if emit:
                ckv_out[r0:r0 + MLA_KCH, :] = ckvn
                kr_out[r0:r0 + MLA_KCH, :] = kr[:, 0:MLA_ROPE]
            if rope:
                kr = _rope(kr, cos_ref[r0:r0 + MLA_KCH, :], sin_ref[r0:r0 + MLA_KCH, :])
            project(ckvn.astype(BF16), kr.astype(BF16), r0)
        if rope:
            for r0 in range(0, Lk - L, MLA_KCH):
                ckr = ckr_ref[r0:r0 + MLA_KCH, :]
                ckr = jnp.concatenate([ckr, jnp.zeros_like(ckr)], axis=1)
                project(cckv_ref[r0:r0 + MLA_KCH, :].astype(BF16), ckr.astype(BF16), L + r0)

    q0 = pl.multiple_of(qi * tq, tq)
    cqn = _rms(u_ref[pl.ds(q0, tq), 0:MLA_Q_RANK], qn_ref[...]).astype(BF16)
    q = _dot(cqn, wq_ref[...]) * ((MLA_NOPE + MLA_ROPE) ** -0.5)
    if rope:
        cos = cos_ref[pl.ds(q0, tq), :]
        sin = sin_ref[pl.ds(q0, tq), :]
    for h in range(MLA_HEADS):
        q_nope = q[:, h * MLA_HW:h * MLA_HW + MLA_NOPE]
        q_rope = q[:, h * MLA_HW + MLA_NOPE:(h + 1) * MLA_HW]
        if rope:
            q_rope = _rope(q_rope, cos, sin)
        qh = jnp.concatenate([q_nope, q_rope], axis=1).astype(BF16)
        s = _dot_nt(qh, kf_s[h])
        p = jnp.exp(s - jnp.max(s, axis=-1, keepdims=True))
        denom = jnp.sum(p, axis=-1, keepdims=True)
        o = _dot(p.astype(BF16), v_s[h])
        y_ref[:, h * MLA_V:(h + 1) * MLA_V] = (o / denom).astype(BF16)


def _mla_mixer(u3, q_norm, wq_p, kv_norm, wkv, rope_args, layer, L, emit):
    B = u3.shape[0]
    rope = rope_args is not None
    Lk = L + (rope_args[2].shape[2] if rope else 0)
    tq = min(MLA_TQ, L)
    const = lambda shape: pl.BlockSpec(shape, lambda b, i: (0,) * len(shape))
    in_specs = [pl.BlockSpec((None, L, 768), lambda b, i: (b, 0, U_MLA // 768)),
                const((1, MLA_Q_RANK)), const((MLA_Q_RANK, MLA_HEADS * MLA_HW)),
                const((1, MLA_KV_RANK)), const((MLA_KV_RANK, MLA_HEADS * 256))]
    args = [u3, q_norm, wq_p, kv_norm, wkv]
    if rope:
        cos, sin, cckv, ckr = rope_args
        in_specs += [const((L, 128)), const((L, 128)),
                     pl.BlockSpec((None, None, Lk - L, MLA_KV_RANK), lambda b, i: (b, layer, 0, 0)),
                     pl.BlockSpec((None, None, Lk - L, MLA_ROPE), lambda b, i: (b, layer, 0, 0))]
        args += [cos, sin, cckv, ckr]
    out_specs = [pl.BlockSpec((None, tq, MLA_HEADS * MLA_V), lambda b, i: (b, i, 0))]
    out_shape = [jax.ShapeDtypeStruct((B, L, MLA_HEADS * MLA_V), BF16)]
    if emit:
        out_specs += [pl.BlockSpec((None, L, MLA_KV_RANK), lambda b, i: (b, 0, 0)),
                      pl.BlockSpec((None, L, MLA_ROPE), lambda b, i: (b, 0, 0))]
        out_shape += [jax.ShapeDtypeStruct((B, L, MLA_KV_RANK), F32),
                      jax.ShapeDtypeStruct((B, L, MLA_ROPE), F32)]
    res = pl.pallas_call(
        functools.partial(_mla_kernel, L=L, Lk=Lk, tq=tq, rope=rope, emit=emit),
        grid=(B, L // tq),
        in_specs=in_specs,
        out_specs=out_specs,
        out_shape=out_shape,
        scratch_shapes=[pltpu.VMEM((MLA_HEADS, Lk, MLA_HW), BF16),
                        pltpu.VMEM((MLA_HEADS, Lk, MLA_V), BF16)],
        compiler_params=_cparams(("arbitrary", "arbitrary")),
        name="mla_mixer",
    )(*args)
    return res if emit else (res[0], None, None)


def _outproj_kernel(x_ref, g_ref, yh_ref, yg_ref, ym_ref, wh_ref, wg_ref, wm_ref, o_ref):
    y = _dot(yh_ref[...], wh_ref[...]) + _dot(yg_ref[...], wg_ref[...]) + _dot(ym_ref[...], wm_ref[...])
    o_ref[...] = x_ref[...] + g_ref[...] * y


def _out_projection(x2d, mod, y_hy, y_gla, y_mla, w_out_b, layer, row_of_tile, tm):
    rows = x2d.shape[0]
    once = pl.Buffered(1)
    return pl.pallas_call(
        _outproj_kernel,
        grid=(rows // tm,),
        in_specs=[pl.BlockSpec((tm, D_MODEL), lambda i: (i, 0)),
                  _mod_spec(2, row_of_tile),
                  pl.BlockSpec((tm, HY_CH), lambda i: (i, 0)),
                  pl.BlockSpec((tm, GLA_VW), lambda i: (i, 0)),
                  pl.BlockSpec((tm, MLA_HEADS * MLA_V), lambda i: (i, 0)),
                  pl.BlockSpec((None, HY_CH, D_MODEL), lambda i: (layer, 0, 0), pipeline_mode=once),
                  pl.BlockSpec((None, GLA_VW, D_MODEL), lambda i: (layer, 1, 0), pipeline_mode=once),
                  pl.BlockSpec((None, MLA_HEADS * MLA_V, D_MODEL), lambda i: (layer, 1, 0), pipeline_mode=once)],
        out_specs=pl.BlockSpec((tm, D_MODEL), lambda i: (i, 0)),
        out_shape=jax.ShapeDtypeStruct((rows, D_MODEL), F32),
        compiler_params=_cparams(("arbitrary",)),
        name="out_projection",
    )(x2d, mod, y_hy, y_gla, y_mla, w_out_b, w_out_b, w_out_b)


FFN_TF = 512


def _ffn_kernel(x_ref, ln_ref, sh_ref, sc_ref, g_ref, wg_ref, wu_ref, wo_ref, lnf_ref, o_ref,
                h_ref, acc_ref, *, final):
    f = pl.program_id(1)

    @pl.when(f == 0)
    def _():
        _adaln_all(x_ref, ln_ref[...] * (1.0 + sc_ref[...]), sh_ref[...], h_ref)
        acc_ref[...] = jnp.zeros_like(acc_ref)

    h = h_ref[...]
    gate = _dot(h, wg_ref[...])
    up = _dot(h, wu_ref[...])
    act = (gate * jax.nn.sigmoid(gate) * up).astype(BF16)
    acc_ref[...] += _dot(act, wo_ref[...])

    @pl.when(f == pl.num_programs(1) - 1)
    def _():
        def body(i, carry):
            rows = pl.ds(pl.multiple_of(i * ADALN_ROWS, ADALN_ROWS), ADALN_ROWS)
            x = x_ref[rows, :] + g_ref[...] * acc_ref[rows, :]
            o_ref[rows, :] = _rms(x, lnf_ref[...]) if final else x
            return carry

        lax.fori_loop(0, x_ref.shape[0] // ADALN_ROWS, body, 0, unroll=4)


def _ffn(x2d, ln, mod, w_in_b, w_out_b, layer, ln_final, row_of_tile, tm, final):
    rows = x2d.shape[0]
    nf = D_FF // FFN_TF
    return pl.pallas_call(
        functools.partial(_ffn_kernel, final=final),
        grid=(rows // tm, nf),
        in_specs=[pl.BlockSpec((tm, D_MODEL), lambda i, f: (i, 0)),
                  pl.BlockSpec((1, D_MODEL), lambda i, f: (0, 0)),
                  _mod_spec(3, row_of_tile),
                  _mod_spec(4, row_of_tile),
                  _mod_spec(5, row_of_tile),
                  pl.BlockSpec((None, None, D_MODEL, FFN_TF), lambda i, f: (layer, f, 0, 0)),
                  pl.BlockSpec((None, None, D_MODEL, FFN_TF), lambda i, f: (layer, nf + f, 0, 0)),
                  pl.BlockSpec((None, FFN_TF, D_MODEL), lambda i, f: (layer, f, 0)),
                  pl.BlockSpec((1, D_MODEL), lambda i, f: (0, 0))],
        out_specs=pl.BlockSpec((tm, D_MODEL), lambda i, f: (i, 0)),
        out_shape=jax.ShapeDtypeStruct((rows, D_MODEL), F32),
        scratch_shapes=[pltpu.VMEM((tm, D_MODEL), BF16), pltpu.VMEM((tm, D_MODEL), F32)],
        compiler_params=_cparams(("arbitrary", "arbitrary")),
        name="ffn",
    )(x2d, ln, mod, mod, mod, w_in_b, w_in_b, w_out_b, ln_final)


def _dft_tables(L):
    k = np.arange(L)
    ang = np.pi * ((k[:, None] * k[None, :]) % (2 * L)) / L
    c = np.cos(ang)
    s = np.sin(ang)
    sp = s.copy()
    sp[0, :] = 1.0 - 2.0 * (k % 2)
    as32 = lambda a: jnp.asarray(a.astype(np.float32))
    return as32(c), as32(s), as32(sp), as32(sp.T.copy())


def _hyena_consts(L):
    c, s, sp, spt = _dft_tables(L)
    c_hi, c_lo = _split(c)
    s_hi, s_lo = _split(s)
    t = jnp.linspace(0.0, 1.0, L, dtype=F32)[:, None]
    w = 2.0 * math.pi * jnp.arange(L, dtype=F32)[:, None] / L
    f = jnp.linspace(1e-4, HY_BANDS - 1, HY_BANDS, dtype=F32)
    zpos = jnp.concatenate([t, jnp.cos(f * w), -jnp.sin(f * w)], axis=-1)
    zpos = jnp.pad(zpos, ((0, 0), (0, 64 - HY_EMB)))
    min_decay = math.log(HY_TARGET) / HY_SLOW_DECAY
    max_decay = math.log(HY_TARGET) / HY_FAST_DECAY
    delta = jnp.abs(jnp.linspace(min_decay, max_decay, HY_CH, dtype=F32))
    decay = jnp.exp(-t * delta)
    filt_consts = (zpos, decay, c_hi, c_lo, s_hi, s_lo)
    main_consts = (c_hi, sp.astype(BF16), spt.astype(BF16))
    return filt_consts, main_consts


def _rope_tables(L):
    t = jnp.arange(L)
    half = MLA_ROPE // 2
    inv = ROPE_THETA ** (-jnp.arange(0, half, 2, dtype=F32) / half)
    ang_r = (t // GRID_W).astype(F32)[:, None] * inv
    ang_c = (t % GRID_W).astype(F32)[:, None] * inv
    cr, sr, cc, sc = jnp.cos(ang_r), jnp.sin(ang_r), jnp.cos(ang_c), jnp.sin(ang_c)
    cos = jnp.concatenate([cr, cr, cc, cc, jnp.ones((L, 128 - MLA_ROPE), F32)], axis=1)
    sin = jnp.concatenate([-sr, sr, -sc, sc, jnp.zeros((L, 128 - MLA_ROPE), F32)], axis=1)
    return cos, sin


def _prep_w_in_tail(w):
    z = lambda n: jnp.zeros((DEPTH, D_MODEL, n), w.dtype)
    mla0 = IN_MAIN + 2 * GLA_LOWRANK
    return jnp.concatenate([w[:, :, mla0:], z(64), w[:, :, IN_MAIN:mla0], z(256 - 2 * GLA_LOWRANK)],
                           axis=2).astype(BF16)


def _column_blocks(w, tn):
    depth, k, n = w.shape
    return w.reshape(depth, k, n // tn, tn).transpose(0, 2, 1, 3).astype(BF16)


def _prep_w_uq(w):
    w = w.reshape(MLA_Q_RANK, MLA_HEADS, MLA_NOPE + MLA_ROPE)
    w = jnp.pad(w, ((0, 0), (0, 0), (0, MLA_HW - MLA_NOPE - MLA_ROPE)))
    return w.reshape(MLA_Q_RANK, MLA_HEADS * MLA_HW).astype(BF16)


def _prep_gla_decay(wa_f, ba_f, wa_b, ba_b):
    wa = jnp.zeros((256, 2 * GLA_QK), F32)
    wa = wa.at[0:GLA_LOWRANK, 0:GLA_QK].set(wa_f)
    wa = wa.at[GLA_LOWRANK:2 * GLA_LOWRANK, GLA_QK:].set(wa_b)
    return wa, jnp.concatenate([ba_f, ba_b])[None, :]


def _trunk_layer(x2d, B, L, mod, row_of_tile, lw, consts, ctx, final, ln_final):
    tm = 1024
    tm_out = 512
    u = _in_projection(x2d, lw['ln_mix'], mod, lw['w_in'], lw['w_in_tail'], lw['layer'], row_of_tile(tm), tm)
    u3 = u.reshape(B, L, U_W)
    filt_consts, main_consts = consts['hyena']
    filt = _hyena_filters(L, filt_consts, lw['hy_w1'], lw['hy_b1'], lw['hy_freq'], lw['hy_w2'],
                          lw['hy_b2'], lw['hy_w3'])
    y_hy = _hyena_mixer(u3, lw['hy_conv_w'], lw['hy_conv_b'], lw['hy_skip'], filt, main_consts, L)
    if ctx is None:
        y_gla, s_f, s_b = _gla_mixer(u3, lw['gla_wa'], lw['gla_ba'], lw['gla_norm'], None, lw['layer'], L, True)
        y_mla, ckv, krope = _mla_mixer(u3, lw['mla_q_norm'], lw['mla_w_uq'], lw['mla_kv_norm'],
                                       lw['mla_w_ukv'], None, lw['layer'], L, True)
        extras = (ckv, krope, s_f, s_b)
    else:
        cache_ckv, cache_krope, s0f, s0b = ctx
        y_gla, _, _ = _gla_mixer(u3, lw['gla_wa'], lw['gla_ba'], lw['gla_norm'], (s0f, s0b), lw['layer'], L, False)
        cos, sin = consts['rope']
        y_mla, _, _ = _mla_mixer(u3, lw['mla_q_norm'], lw['mla_w_uq'], lw['mla_kv_norm'], lw['mla_w_ukv'],
                                 (cos, sin, cache_ckv, cache_krope), lw['layer'], L, False)
        extras = None
    rows = B * L
    x2d = _out_projection(x2d, mod, y_hy.reshape(rows, -1), y_gla.reshape(rows, -1), y_mla.reshape(rows, -1),
                          lw['w_out'], lw['layer'], row_of_tile(tm_out), tm_out)
    x2d = _ffn(x2d, lw['ln_ffn'], mod, lw['w_ffn_in'], lw['w_ffn_out'], lw['layer'], ln_final, row_of_tile(tm_out),
               tm_out, final)
    return x2d, extras


def kernel(x_prompt, x_sample, cache_mla_ckv, cache_mla_krope, state_gla_fwd, state_gla_bwd, c, c_ctx, w_mod, b_mod, ln_mix, w_in, hy_conv_w, hy_conv_b, hy_filt_w1, hy_filt_b1, hy_filt_freq, hy_filt_w2, hy_filt_b2, hy_filt_w3, hy_skip, gla_wa_f, gla_ba_f, gla_wa_b, gla_ba_b, gla_norm, mla_q_norm, mla_w_uq, mla_kv_norm, mla_w_ukv, w_out, ln_ffn, w_ffn_in, w_ffn_out, ln_final):
    Bc, Lc, _ = x_prompt.shape
    Bl, Ll, _ = x_sample.shape
    assert 1 + Bl <= MOD_ROWS

    c_all = jnp.concatenate([c_ctx[None, :], c, jnp.zeros((MOD_ROWS - 1 - Bl, D_MODEL), F32)], axis=0)
    mod_all = _modulation(c_all, w_mod, b_mod).reshape(DEPTH, MOD_ROWS, 6, 1, D_MODEL)

    consts_ctx = {'hyena': _hyena_consts(Lc)}
    consts_lat = {'hyena': _hyena_consts(Ll), 'rope': _rope_tables(Ll)}
    ctx_rows = lambda tm: (lambda i: 0)
    lat_rows = lambda tm: (lambda i: 1 + (i * tm) // Ll)
    lnf = ln_final[None, :]

    x_ctx = x_prompt.reshape(Bc * Lc, D_MODEL)
    x_lat = x_sample.reshape(Bl * Ll, D_MODEL)
    w_in_b = _column_blocks(w_in[:, :, :IN_MAIN], IN_TN)
    w_in_tail = _prep_w_in_tail(w_in)
    w_out_b = w_out.astype(BF16)
    w_ffn_in_b = _column_blocks(w_ffn_in, FFN_TF)
    w_ffn_out_b = w_ffn_out.astype(BF16)
    ckv_l, krope_l, sf_l, sb_l = [], [], [], []
    for l in range(DEPTH):
        wa, ba = _prep_gla_decay(gla_wa_f[l], gla_ba_f[l], gla_wa_b[l], gla_ba_b[l])
        lw = {
            'layer': l, 'ln_mix': ln_mix[l][None, :], 'w_in': w_in_b, 'w_in_tail': w_in_tail,
            'hy_conv_w': hy_conv_w[l], 'hy_conv_b': hy_conv_b[l][None, :],
            'hy_w1': jnp.pad(hy_filt_w1[l], ((0, 64 - HY_EMB), (0, 0))), 'hy_b1': hy_filt_b1[l][None, :],
            'hy_freq': hy_filt_freq[l], 'hy_w2': hy_filt_w2[l], 'hy_b2': hy_filt_b2[l][None, :],
            'hy_w3': hy_filt_w3[l], 'hy_skip': hy_skip[l],
            'gla_wa': wa, 'gla_ba': ba, 'gla_norm': gla_norm[l][None, :],
            'mla_q_norm': mla_q_norm[l][None, :], 'mla_w_uq': _prep_w_uq(mla_w_uq[l]),
            'mla_kv_norm': mla_kv_norm[l][None, :], 'mla_w_ukv': mla_w_ukv[l].astype(BF16),
            'w_out': w_out_b, 'ln_ffn': ln_ffn[l][None, :],
            'w_ffn_in': w_ffn_in_b, 'w_ffn_out': w_ffn_out_b,
        }
        final = l == DEPTH - 1
        mod = mod_all[l]
        x_ctx, (ckv, krope, s_f, s_b) = _trunk_layer(x_ctx, Bc, Lc, mod, ctx_rows, lw, consts_ctx, None,
                                                     final, lnf)
        ckv_l.append(ckv)
        krope_l.append(krope)
        sf_l.append(s_f)
        sb_l.append(s_b)
        ctx = (cache_mla_ckv, cache_mla_krope, state_gla_fwd, state_gla_bwd)
        x_lat, _ = _trunk_layer(x_lat, Bl, Ll, mod, lat_rows, lw, consts_lat, ctx, final, lnf)
    return (x_ctx.reshape(Bc, Lc, D_MODEL), x_lat.reshape(Bl, Ll, D_MODEL),
            jnp.stack(ckv_l, axis=1), jnp.stack(krope_l, axis=1),
            jnp.stack(sf_l, axis=1), jnp.stack(sb_l, axis=1))
```

```python
import functools
import math

import numpy as np
import jax
import jax.numpy as jnp
from jax import lax
from jax.experimental import pallas as pl
from jax.experimental.pallas import tpu as pltpu

F32 = jnp.float32
BF16 = jnp.bfloat16

D_MODEL = 2048
DEPTH = 2
GRID_W = 64
EPS = 1e-6
HY_CH = 512
HY_ORDER = 2
HY_BANDS = 16
HY_EMB = 1 + 2 * HY_BANDS
HY_FF = 64
HY_FAST_DECAY = 0.3
HY_SLOW_DECAY = 1.5
HY_TARGET = 1e-2
HY_IN = 3 * HY_CH
GLA_HEADS = 4
GLA_DK = 64
GLA_DV = 128
GLA_LOWRANK = 16
GLA_TAU = 16.0
GLA_CHUNK = 64
GLA_QK = GLA_HEADS * GLA_DK
GLA_VW = GLA_HEADS * GLA_DV
MLA_HEADS = 8
MLA_Q_RANK = 384
MLA_KV_RANK = 256
MLA_NOPE = 128
MLA_ROPE = 64
MLA_V = 128
ROPE_THETA = 10000.0
D_FF = -(-8 * D_MODEL // (3 * 256)) * 256

U_HY = 0
U_GLA = 1536
U_MLA = 3072
U_GLA_A = 3840
U_W = 4096
MOD_ROWS = 16

VMEM_LIMIT_V7X = 48 * 1024 * 1024


def _cparams(sem):
    return pltpu.CompilerParams(dimension_semantics=sem, vmem_limit_bytes=VMEM_LIMIT_V7X)


def _dot(a, b):
    return jnp.dot(a, b, preferred_element_type=F32)


def _dot_nt(a, b):
    return lax.dot_general(a, b, (((1,), (1,)), ((), ())), preferred_element_type=F32)


def _split(x):
    hi = x.astype(BF16)
    lo = (x - hi.astype(F32)).astype(BF16)
    return hi, lo


def _dot3(a, b):
    a_hi, a_lo = _split(a)
    b_hi, b_lo = _split(b)
    return _dot(a_hi, b_hi) + (_dot(a_lo, b_hi) + _dot(a_hi, b_lo))


def _rms(x, g):
    ms = jnp.mean(x * x, axis=-1, keepdims=True)
    return x * lax.rsqrt(ms + EPS) * g


ADALN_ROWS = 16


def _adaln_group(x_ref, gain, shift, h_ref, r):
    rows = pl.ds(pl.multiple_of(r, ADALN_ROWS), ADALN_ROWS)
    x = x_ref[rows, :]
    ms = jnp.mean(x * x, axis=-1, keepdims=True)
    h_ref[rows, :] = (x * lax.rsqrt(ms + EPS) * gain + shift).astype(BF16)


def _adaln_all(x_ref, gain, shift, h_ref):
    def body(i, carry):
        _adaln_group(x_ref, gain, shift, h_ref, i * ADALN_ROWS)
        return carry

    lax.fori_loop(0, x_ref.shape[0] // ADALN_ROWS, body, 0, unroll=4)


def _adaln_slice(x_ref, gain, shift, h_ref, base, nrows):
    for t in range(nrows // ADALN_ROWS):
        _adaln_group(x_ref, gain, shift, h_ref, base + t * ADALN_ROWS)


def _mod_kernel(c_ref, w_ref, b_ref, o_ref):
    c = c_ref[...]
    s = (c * jax.nn.sigmoid(c)).astype(BF16)
    o_ref[...] = _dot(s, w_ref[...].astype(BF16)) + b_ref[...]


def _modulation(c_all, w_mod, b_mod):
    tn = 1024
    n6 = 6 * D_MODEL
    return pl.pallas_call(
        _mod_kernel,
        grid=(DEPTH, n6 // tn),
        in_specs=[pl.BlockSpec((MOD_ROWS, D_MODEL), lambda l, j: (0, 0)),
                  pl.BlockSpec((None, D_MODEL, tn), lambda l, j: (l, 0, j)),
                  pl.BlockSpec((None, 1, tn), lambda l, j: (l, 0, j))],
        out_specs=pl.BlockSpec((None, MOD_ROWS, tn), lambda l, j: (l, 0, j)),
        out_shape=jax.ShapeDtypeStruct((DEPTH, MOD_ROWS, n6), F32),
        compiler_params=_cparams(("arbitrary", "arbitrary")),
        name="modulation",
    )(c_all, w_mod, b_mod.reshape(DEPTH, 1, n6))


def _mod_spec(chunk, row_of_tile, tile_of_step=lambda i, *_: i):
    return pl.BlockSpec((None, None, 1, D_MODEL), lambda *g: (row_of_tile(tile_of_step(*g)), chunk, 0, 0))


def _lookahead(n_tiles):
    return lambda i, j: jnp.minimum(i + jnp.minimum(j, 1), n_tiles - 1)


IN_TN = 1024
IN_MAIN = 3072
IN_PRE_ROWS = 352
IN_PRE_STRIDE = 336


def _inproj_kernel(x_ref, ln_ref, sh_ref, sc_ref, w_ref, wt_ref, o_ref, ha_ref, hb_ref):
    i = pl.program_id(0)
    j = pl.program_id(1)
    gain = ln_ref[...] * (1.0 + sc_ref[...])
    shift = sh_ref[...]

    @pl.when((i == 0) & (j == 0))
    def _():
        _adaln_all(x_ref, gain, shift, ha_ref)

    base = jnp.clip(j - 1, 0, 2) * IN_PRE_STRIDE

    def step(cur_ref, nxt_ref):
        @pl.when(j < IN_MAIN // IN_TN)
        def _():
            _adaln_slice(x_ref, gain, shift, nxt_ref, base, IN_PRE_ROWS)
            o_ref[...] = _dot(cur_ref[...], w_ref[...])

        @pl.when(j >= IN_MAIN // IN_TN)
        def _():
            _adaln_slice(x_ref, gain, shift, nxt_ref, base, IN_PRE_ROWS)
            o_ref[...] = _dot(cur_ref[...], wt_ref[...])

    @pl.when(i % 2 == 0)
    def _():
        step(ha_ref, hb_ref)

    @pl.when(i % 2 == 1)
    def _():
        step(hb_ref, ha_ref)


def _in_projection(x2d, ln, mod, w_in_b, w_tail, layer, row_of_tile, tm):
    rows = x2d.shape[0]
    n_main = IN_MAIN // IN_TN
    ahead = _lookahead(rows // tm)
    return pl.pallas_call(
        _inproj_kernel,
        grid=(rows // tm, U_W // IN_TN),
        in_specs=[pl.BlockSpec((tm, D_MODEL), lambda i, j: (ahead(i, j), 0)),
                  pl.BlockSpec((1, D_MODEL), lambda i, j: (0, 0)),
                  _mod_spec(0, row_of_tile, ahead),
                  _mod_spec(1, row_of_tile, ahead),
                  pl.BlockSpec((None, D_MODEL, IN_TN), lambda i, j: (layer, 0, jnp.minimum(j, n_main - 1))),
                  pl.BlockSpec((None, D_MODEL, U_W - IN_MAIN), lambda i, j: (layer, 0, 0))],
        out_specs=pl.BlockSpec((tm, IN_TN), lambda i, j: (i, j)),
        out_shape=jax.ShapeDtypeStruct((rows, U_W), F32),
        scratch_shapes=[pltpu.VMEM((tm, D_MODEL), BF16), pltpu.VMEM((tm, D_MODEL), BF16)],
        compiler_params=_cparams(("arbitrary", "arbitrary")),
        name="in_projection",
    )(x2d, ln, mod, mod, w_in_b, w_tail)


def _filter_kernel(z_ref, w1_ref, b1_ref, fr_ref, w2_ref, b2_ref, w3f_ref, w3b_ref, dec_ref,
                   chi_ref, clo_ref, shi_ref, slo_ref, kr_ref, ki_ref, kn_ref, *, L):
    h = jnp.sin(fr_ref[0:1, :] * (_dot3(z_ref[...], w1_ref[...]) + b1_ref[...]))
    h = jnp.sin(fr_ref[1:2, :] * (_dot3(h, w2_ref[...]) + b2_ref[...]))
    dec = dec_ref[...]
    row = lax.broadcasted_iota(jnp.int32, (L, 1), 0)
    kf = _dot3(h, w3f_ref[...]) * dec
    kb = jnp.where(row == 0, 0.0, _dot3(h, w3b_ref[...]) * dec)
    p_hi, p_lo = _split(kf + kb)
    m_hi, m_lo = _split(kf - kb)
    chi, clo = chi_ref[...], clo_ref[...]
    shi, slo = shi_ref[...], slo_ref[...]
    kc = _dot(chi, p_hi) + (_dot(clo, p_hi) + _dot(chi, p_lo))
    ks = _dot(shi, m_hi) + (_dot(slo, m_hi) + _dot(shi, m_lo))
    sign = jnp.where((row & 1) == 0, 1.0, -1.0)
    kn = jnp.sum((kf + kb) * sign, axis=0, keepdims=True)
    kr_ref[...] = kc * jnp.where(row == 0, 0.5 / L, 1.0 / L)
    ki_ref[...] = ks * (-1.0 / L)
    kn_ref[...] = kn * (0.5 / L)


def _hyena_filters(L, consts, w1p, b1, freq, w2, b2, w3):
    zpos, decay, c_hi, c_lo, s_hi, s_lo = consts
    full = lambda shape: pl.BlockSpec(shape, lambda o: (0,) * len(shape))
    return pl.pallas_call(
        functools.partial(_filter_kernel, L=L),
        grid=(HY_ORDER,),
        in_specs=[full((L, 64)), full((64, HY_FF)), full((1, HY_FF)), full((2, HY_FF)),
                  full((HY_FF, HY_FF)), full((1, HY_FF)),
                  pl.BlockSpec((HY_FF, HY_CH), lambda o: (0, o)),
                  pl.BlockSpec((HY_FF, HY_CH), lambda o: (0, HY_ORDER + o)),
                  full((L, HY_CH)), full((L, L)), full((L, L)), full((L, L)), full((L, L))],
        out_specs=[pl.BlockSpec((None, L, HY_CH), lambda o: (o, 0, 0)),
                   pl.BlockSpec((None, L, HY_CH), lambda o: (o, 0, 0)),
                   pl.BlockSpec((None, 1, HY_CH), lambda o: (o, 0, 0))],
        out_shape=[jax.ShapeDtypeStruct((HY_ORDER, L, HY_CH), F32),
                   jax.ShapeDtypeStruct((HY_ORDER, L, HY_CH), F32),
                   jax.ShapeDtypeStruct((HY_ORDER, 1, HY_CH), F32)],
        compiler_params=_cparams(("arbitrary",)),
        name="hyena_filters",
    )(zpos, w1p, b1, freq, w2, b2, w3, w3, decay, c_hi, c_lo, s_hi, s_lo)


HY_CG = 256
HY_STEP_ROWS = 1024


def _hyena_kernel(v_ref, x1_ref, x2_ref, wv_ref, w1_ref, w2_ref, bv_ref, b1_ref, b2_ref, skip_ref,
                  kr_ref, ki_ref, kn_ref, c_ref, sp_ref, spt_ref, o_ref, *, L):
    row = lax.broadcasted_iota(jnp.int32, (L, 1), 0)
    first = row == 0
    last = row == L - 1
    chains = [(b, slice(g * HY_CG, (g + 1) * HY_CG)) for b in range(v_ref.shape[0]) for g in range(HY_CH // HY_CG)]

    def short_conv(u_ref, w_ref, b_ref, b, cols):
        u = u_ref[b, :, cols]
        prev = jnp.where(first, 0.0, pltpu.roll(u, 1, axis=0))
        nxt = jnp.where(last, 0.0, pltpu.roll(u, L - 1, axis=0))
        return prev * w_ref[0:1, cols] + u * w_ref[1:2, cols] + nxt * w_ref[2:3, cols] + b_ref[:, cols]

    gate_refs = ((x1_ref, w1_ref, b1_ref), (x2_ref, w2_ref, b2_ref))
    z = [short_conv(v_ref, wv_ref, bv_ref, b, cols) for b, cols in chains]
    for o in range(HY_ORDER):
        zb = [zz.astype(BF16) for zz in z]
        a = [_dot(c_ref[...], x) for x in zb]
        s = [_dot(sp_ref[...], x) for x in zb]
        conv = []
        for n, (b, cols) in enumerate(chains):
            kr = kr_ref[o, :, cols]
            ki = ki_ref[o, :, cols]
            qt = a[n] * kr + s[n] * ki
            qb = s[n] * jnp.where(first, kn_ref[o, :, cols], kr) - a[n] * ki
            conv.append(_dot(c_ref[...], qt.astype(BF16)) + _dot(spt_ref[...], qb.astype(BF16)))
        z = [short_conv(*gate_refs[o], b, cols) * (conv[n] + z[n] * skip_ref[o:o + 1, cols])
             for n, (b, cols) in enumerate(chains)]
    for n, (b, cols) in enumerate(chains):
        o_ref[b, :, cols] = z[n].astype(BF16)


def _hyena_mixer(u3, conv_w, conv_b, skip, filt, dft, L):
    B = u3.shape[0]
    nb = max(1, HY_STEP_ROWS // L)
    kr, ki, kn = filt
    c_b, sp_b, spt_b = dft
    once = pl.Buffered(1)
    ublk = lambda part: pl.BlockSpec((nb, L, HY_CH), lambda b: (b, 0, part))
    wblk = lambda part: pl.BlockSpec((3, HY_CH), lambda b: (0, part))
    bblk = lambda part: pl.BlockSpec((1, HY_CH), lambda b: (0, part))
    fblk = lambda rows: pl.BlockSpec((HY_ORDER, rows, HY_CH), lambda b: (0, 0, 0), pipeline_mode=once)
    mat = pl.BlockSpec((L, L), lambda b: (0, 0), pipeline_mode=once)
    return pl.pallas_call(
        functools.partial(_hyena_kernel, L=L),
        grid=(B // nb,),
        in_specs=[ublk(0), ublk(1), ublk(2), wblk(0), wblk(1), wblk(2), bblk(0), bblk(1), bblk(2),
                  pl.BlockSpec((HY_ORDER, HY_CH), lambda b: (0, 0)),
                  fblk(L), fblk(L), fblk(1), mat, mat, mat],
        out_specs=pl.BlockSpec((nb, L, HY_CH), lambda b: (b, 0, 0)),
        out_shape=jax.ShapeDtypeStruct((B, L, HY_CH), BF16),
        compiler_params=_cparams(("arbitrary",)),
        name="hyena_mixer",
    )(u3, u3, u3, conv_w, conv_w, conv_w, conv_b, conv_b, conv_b, skip, kr, ki, kn, c_b, sp_b, spt_b)


GLA_PAIR = 2 * GLA_CHUNK
GLA_STATES = 2 * GLA_HEADS
GLA_SCAN = 256


def _gla_kernel(*refs, L, has_state, emit_state):
    it = iter(refs)
    qkvg_ref, a_ref, wa_ref, ba_ref, ng_ref, scan_ref = (next(it) for _ in range(6))
    st0_refs = (next(it), next(it)) if has_state else None
    y_ref = next(it)
    stout_refs = (next(it), next(it)) if emit_state else None
    qd_s, ki_s, ke_s, qi_s, ku_s, et_s, vb_s, vt_s, o_s, st_s = (next(it) for _ in range(10))

    npair = L // GLA_PAIR
    pre = _dot3(a_ref[...], wa_ref[...]) + ba_ref[...]
    la = (jnp.minimum(pre, 0.0) - jnp.log(1.0 + jnp.exp(-jnp.abs(pre)))) * (1.0 / GLA_TAU)

    scan = scan_ref[...]
    pfx, tot = [], []
    for r in range(0, L, GLA_SCAN):
        x = la[r:r + GLA_SCAN]
        hi = x.astype(BF16)
        r1 = x - hi.astype(F32)
        mid = r1.astype(BF16)
        lo = (r1 - mid.astype(F32)).astype(BF16)
        res = _dot(scan, hi) + (_dot(scan, mid) + _dot(scan, lo))
        pfx.append(res[:GLA_SCAN])
        tot.append(res[GLA_SCAN:])
    pfx = jnp.concatenate(pfx, axis=0) if len(pfx) > 1 else pfx[0]
    tot = jnp.concatenate(tot, axis=0) if len(tot) > 1 else tot[0]

    upper = (lax.broadcasted_iota(jnp.int32, (L, 1), 0) & GLA_CHUNK) != 0
    q = qkvg_ref[:, 0:GLA_QK] * (GLA_DK ** -0.5)
    k = qkvg_ref[:, GLA_QK:2 * GLA_QK]
    for dirn in range(2):
        cols = slice(dirn * GLA_QK, (dirn + 1) * GLA_QK)
        if dirn == 0:
            cum = pfx[:, cols]
            rest = tot[:, cols] - cum
            first = jnp.logical_not(upper)
        else:
            rest = pfx[:, cols] - la[:, cols]
            cum = tot[:, cols] - rest
            first = upper
        et = jnp.exp(tot[:, cols])
        et_other = jnp.where(upper, pltpu.roll(et, GLA_CHUNK, axis=0), pltpu.roll(et, L - GLA_CHUNK, axis=0))
        qd = q * jnp.exp(cum)
        ke = k * jnp.exp(rest)
        et_s[dirn] = et
        qd_s[dirn] = qd.astype(BF16)
        ki_s[dirn] = (k * jnp.exp(-cum)).astype(BF16)
        ke_s[dirn] = ke.astype(BF16)
        qi_s[dirn] = jnp.where(first, qd, qd * et_other).astype(BF16)
        ku_s[dirn] = jnp.where(first, ke * et_other, ke).astype(BF16)
    vb_s[...] = qkvg_ref[:, 2 * GLA_QK:2 * GLA_QK + GLA_VW].astype(BF16)

    def transpose_v(n2, carry):
        r0 = pl.multiple_of(n2 * GLA_PAIR, GLA_PAIR)
        for h in range(GLA_HEADS):
            vp = qkvg_ref[pl.ds(r0, GLA_PAIR), 2 * GLA_QK + h * GLA_DV:2 * GLA_QK + (h + 1) * GLA_DV]
            vt_s[n2, h * GLA_DV:(h + 1) * GLA_DV, :] = vp.T.astype(BF16)
        return carry

    lax.fori_loop(0, npair, transpose_v, 0)

    if has_state:
        zero = jnp.zeros((GLA_DK, GLA_DV), F32)
        for dirn in range(2):
            for h in range(GLA_HEADS):
                s0 = st0_refs[dirn][h]
                both = jnp.concatenate([s0, zero] if h % 2 == 0 else [zero, s0], axis=0)
                idx = dirn * GLA_HEADS + h
                st_s[idx * GLA_DV:(idx + 1) * GLA_DV, :] = both.T
    else:
        st_s[...] = jnp.zeros_like(st_s)

    lane = lax.broadcasted_iota(jnp.int32, (1, 2 * GLA_DK), 1)
    head_lanes = (lane < GLA_DK, lane >= GLA_DK)
    ri = lax.broadcasted_iota(jnp.int32, (GLA_PAIR, GLA_PAIR), 0)
    ci = lax.broadcasted_iota(jnp.int32, (GLA_PAIR, GLA_PAIR), 1)
    same_chunk = ((ri ^ ci) & GLA_CHUNK) == 0
    keep = (ci <= ri, ci >= ri)

    def pair_step(n, carry):
        for dirn in range(2):
            n2 = n if dirn == 0 else npair - 1 - n
            r0 = pl.multiple_of(n2 * GLA_PAIR, GLA_PAIR)
            rows = pl.ds(r0, GLA_PAIR)
            for p in range(GLA_HEADS // 2):
                lanes = slice(p * 2 * GLA_DK, (p + 1) * 2 * GLA_DK)
                qd = qd_s[dirn, rows, lanes]
                qi = qi_s[dirn, rows, lanes]
                zeros = jnp.zeros_like(qd)
                dec = (et_s[dirn, pl.ds(r0, 8), lanes][0:1, :]
                       * et_s[dirn, pl.ds(r0 + GLA_CHUNK, 8), lanes][0:1, :])
                q2 = jnp.concatenate([jnp.where(head_lanes[0], qd, zeros), jnp.where(head_lanes[1], qd, zeros)],
                                     axis=0)
                att_d = _dot_nt(q2, ki_s[dirn, rows, lanes])
                att_o = _dot_nt(q2, ke_s[dirn, rows, lanes])
                s0 = (dirn * GLA_HEADS + 2 * p) * GLA_DV
                st2 = st_s[s0:s0 + 2 * GLA_DV, :]
                vt2 = vt_s[n2, 2 * p * GLA_DV:(2 * p + 2) * GLA_DV, :]
                st_s[s0:s0 + 2 * GLA_DV, :] = st2 * dec + _dot(vt2, ku_s[dirn, rows, lanes])
                for hh in range(2):
                    h = 2 * p + hh
                    hr = slice(hh * GLA_PAIR, (hh + 1) * GLA_PAIR)
                    att = jnp.where(keep[dirn], jnp.where(same_chunk, att_d[hr], att_o[hr]), 0.0)
                    qm = jnp.where(head_lanes[hh], qi, zeros)
                    o = (_dot(att.astype(BF16), vb_s[rows, h * GLA_DV:(h + 1) * GLA_DV])
                         + _dot_nt(qm, st2[hh * GLA_DV:(hh + 1) * GLA_DV].astype(BF16)))
                    o_s[dirn, rows, h * GLA_DV:(h + 1) * GLA_DV] = o
        return carry

    lax.fori_loop(0, npair, pair_step, 0)

    for h in range(GLA_HEADS):
        cols = slice(h * GLA_DV, (h + 1) * GLA_DV)
        o = _rms(o_s[0, :, cols] + o_s[1, :, cols], ng_ref[...])
        g = qkvg_ref[:, 2 * GLA_QK + GLA_VW + h * GLA_DV:2 * GLA_QK + GLA_VW + (h + 1) * GLA_DV]
        y_ref[:, cols] = (o * (g * jax.nn.sigmoid(g))).astype(BF16)
    if emit_state:
        for dirn in range(2):
            for h in range(GLA_HEADS):
                idx = dirn * GLA_HEADS + h
                half = (h % 2) * GLA_DK
                stout_refs[dirn][h] = st_s[idx * GLA_DV:(idx + 1) * GLA_DV, :].T[half:half + GLA_DK, :]


def _gla_scan_matrix():
    i = np.arange(GLA_SCAN)
    same = (i[:, None] // GLA_CHUNK) == (i[None, :] // GLA_CHUNK)
    tri = same & (i[None, :] <= i[:, None])
    return jnp.asarray(np.concatenate([tri, same], axis=0).astype(np.float32)).astype(BF16)


def _gla_mixer(u3, wa_p, ba_p, norm_g, st0, layer, L, emit_state):
    B = u3.shape[0]
    has_state = st0 is not None
    head_state = (GLA_HEADS, GLA_DK, GLA_DV)
    in_specs = [pl.BlockSpec((None, L, 1536), lambda b: (b, 0, U_GLA // 1536)),
                pl.BlockSpec((None, L, 256), lambda b: (b, 0, U_GLA_A // 256)),
                pl.BlockSpec((256, 2 * GLA_QK), lambda b: (0, 0)),
                pl.BlockSpec((1, 2 * GLA_QK), lambda b: (0, 0)),
                pl.BlockSpec((1, GLA_DV), lambda b: (0, 0)),
                pl.BlockSpec((2 * GLA_SCAN, GLA_SCAN), lambda b: (0, 0))]
    args = [u3, u3, wa_p, ba_p, norm_g, _gla_scan_matrix()]
    if has_state:
        in_specs += [pl.BlockSpec((None, None) + head_state, lambda b: (b, layer, 0, 0, 0))] * 2
        args += list(st0)
    out_specs = [pl.BlockSpec((None, L, GLA_VW), lambda b: (b, 0, 0))]
    out_shape = [jax.ShapeDtypeStruct((B, L, GLA_VW), BF16)]
    if emit_state:
        out_specs += [pl.BlockSpec((None,) + head_state, lambda b: (b, 0, 0, 0))] * 2
        out_shape += [jax.ShapeDtypeStruct((B,) + head_state, F32)] * 2
    npair = L // GLA_PAIR
    qk_bf16 = pltpu.VMEM((2, L, GLA_QK), BF16)
    res = pl.pallas_call(
        functools.partial(_gla_kernel, L=L, has_state=has_state, emit_state=emit_state),
        grid=(B,),
        in_specs=in_specs,
        out_specs=out_specs,
        out_shape=out_shape,
        scratch_shapes=[qk_bf16,
                        qk_bf16,
                        qk_bf16,
                        qk_bf16,
                        qk_bf16,
                        pltpu.VMEM((2, L, GLA_QK), F32),
                        pltpu.VMEM((L, GLA_VW), BF16),
                        pltpu.VMEM((npair, GLA_VW, GLA_PAIR), BF16),
                        pltpu.VMEM((2, L, GLA_VW), F32),
                        pltpu.VMEM((GLA_STATES * GLA_DV, 2 * GLA_DK), F32)],
        compiler_params=_cparams(("arbitrary",)),
        name="gla_mixer",
    )(*args)
    return (res[0], res[1], res[2]) if emit_state else (res[0], None, None)


MLA_HW = 256
MLA_TQ = 512
MLA_KCH = 256


def _rope(x, cos, sin):
    lane = lax.broadcasted_iota(jnp.int32, x.shape, 1)
    partner = jnp.where((lane & 16) == 0, pltpu.roll(x, 112, axis=1), pltpu.roll(x, 16, axis=1))
    return x * cos + partner * sin


def _mla_kernel(*refs, L, Lk, tq, rope, emit):
    it = iter(refs)
    u_ref, qn_ref, wq_ref, kvn_ref, wkv_ref = (next(it) for _ in range(5))
    cos_ref, sin_ref, cckv_ref, ckr_ref = (next(it) for _ in range(4)) if rope else (None,) * 4
    y_ref = next(it)
    ckv_out, kr_out = (next(it), next(it)) if emit else (None, None)
    kf_s, v_s = next(it), next(it)

    qi = pl.program_id(1)

    def project(keys, rope_part, r0):
        kv = _dot(keys, wkv_ref[...])
        for h in range(MLA_HEADS):
            kf_s[h, r0:r0 + MLA_KCH, 0:MLA_NOPE] = kv[:, h * 256:h * 256 + MLA_NOPE].astype(BF16)
            kf_s[h, r0:r0 + MLA_KCH, MLA_NOPE:MLA_HW] = rope_part
            v_s[h, r0:r0 + MLA_KCH, :] = kv[:, h * 256 + MLA_NOPE:(h + 1) * 256].astype(BF16)

    @pl.when(qi == 0)
    def _():
        for r0 in range(0, L, MLA_KCH):
            ckvn = _rms(u_ref[r0:r0 + MLA_KCH, MLA_Q_RANK:MLA_Q_RANK + MLA_KV_RANK], kvn_ref[...])
            kr = u_ref[r0:r0 + MLA_KCH, 640:768]
            if emit:
                ckv_out[r0:r0 + MLA_KCH, :] = ckvn
                kr_out[r0:r0 + MLA_KCH, :] = kr[:, 0:MLA_ROPE]
            if rope:
                kr = _rope(kr, cos_ref[r0:r0 + MLA_KCH, :], sin_ref[r0:r0 + MLA_KCH, :])
            project(ckvn.astype(BF16), kr.astype(BF16), r0)
        if rope:
            for r0 in range(0, Lk - L, MLA_KCH):
                ckr = ckr_ref[r0:r0 + MLA_KCH, :]
                ckr = jnp.concatenate([ckr, jnp.zeros_like(ckr)], axis=1)
                project(cckv_ref[r0:r0 + MLA_KCH, :].astype(BF16), ckr.astype(BF16), L + r0)

    q0 = pl.multiple_of(qi * tq, tq)
    cqn = _rms(u_ref[pl.ds(q0, tq), 0:MLA_Q_RANK], qn_ref[...]).astype(BF16)
    q = _dot(cqn, wq_ref[...]) * ((MLA_NOPE + MLA_ROPE) ** -0.5)
    if rope:
        cos = cos_ref[pl.ds(q0, tq), :]
        sin = sin_ref[pl.ds(q0, tq), :]
    for h in range(MLA_HEADS):
        q_nope = q[:, h * MLA_HW:h * MLA_HW + MLA_NOPE]
        q_rope = q[:, h * MLA_HW + MLA_NOPE:(h + 1) * MLA_HW]
        if rope:
            q_rope = _rope(q_rope, cos, sin)
        qh = jnp.concatenate([q_nope, q_rope], axis=1).astype(BF16)
        s = _dot_nt(qh, kf_s[h])
        p = jnp.exp(s - jnp.max(s, axis=-1, keepdims=True))
        denom = jnp.sum(p, axis=-1, keepdims=True)
        o = _dot(p.astype(BF16), v_s[h])
        y_ref[:, h * MLA_V:(h + 1) * MLA_V] = (o / denom).astype(BF16)


def _mla_mixer(u3, q_norm, wq_p, kv_norm, wkv, rope_args, layer, L, emit):
    B = u3.shape[0]
    rope = rope_args is not None
    Lk = L + (rope_args[2].shape[2] if rope else 0)
    tq = min(MLA_TQ, L)
    const = lambda shape: pl.BlockSpec(shape, lambda b, i: (0,) * len(shape))
    in_specs = [pl.BlockSpec((None, L, 768), lambda b, i: (b, 0, U_MLA // 768)),
                const((1, MLA_Q_RANK)), const((MLA_Q_RANK, MLA_HEADS * MLA_HW)),
                const((1, MLA_KV_RANK)), const((MLA_KV_RANK, MLA_HEADS * 256))]
    args = [u3, q_norm, wq_p, kv_norm, wkv]
    if rope:
        cos, sin, cckv, ckr = rope_args
        in_specs += [const((L, 128)), const((L, 128)),
                     pl.BlockSpec((None, None, Lk - L, MLA_KV_RANK), lambda b, i: (b, layer, 0, 0)),
                     pl.BlockSpec((None, None, Lk - L, MLA_ROPE), lambda b, i: (b, layer, 0, 0))]
        args += [cos, sin, cckv, ckr]
    out_specs = [pl.BlockSpec((None, tq, MLA_HEADS * MLA_V), lambda b, i: (b, i, 0))]
    out_shape = [jax.ShapeDtypeStruct((B, L, MLA_HEADS * MLA_V), BF16)]
    if emit:
        out_specs += [pl.BlockSpec((None, L, MLA_KV_RANK), lambda b, i: (b, 0, 0)),
                      pl.BlockSpec((None, L, MLA_ROPE), lambda b, i: (b, 0, 0))]
        out_shape += [jax.ShapeDtypeStruct((B, L, MLA_KV_RANK), F32),
                      jax.ShapeDtypeStruct((B, L, MLA_ROPE), F32)]
    res = pl.pallas_call(
        functools.partial(_mla_kernel, L=L, Lk=Lk, tq=tq, rope=rope, emit=emit),
        grid=(B, L // tq),
        in_specs=in_specs,
        out_specs=out_specs,
        out_shape=out_shape,
        scratch_shapes=[pltpu.VMEM((MLA_HEADS, Lk, MLA_HW), BF16),
                        pltpu.VMEM((MLA_HEADS, Lk, MLA_V), BF16)],
        compiler_params=_cparams(("arbitrary", "arbitrary")),
        name="mla_mixer",
    )(*args)
    return res if emit else (res[0], None, None)


def _outproj_kernel(x_ref, g_ref, yh_ref, yg_ref, ym_ref, wh_ref, wg_ref, wm_ref, o_ref):
    y = _dot(yh_ref[...], wh_ref[...]) + _dot(yg_ref[...], wg_ref[...]) + _dot(ym_ref[...], wm_ref[...])
    o_ref[...] = x_ref[...] + g_ref[...] * y


def _out_projection(x2d, mod, y_hy, y_gla, y_mla, w_out_b, layer, row_of_tile, tm):
    rows = x2d.shape[0]
    once = pl.Buffered(1)
    return pl.pallas_call(
        _outproj_kernel,
        grid=(rows // tm,),
        in_specs=[pl.BlockSpec((tm, D_MODEL), lambda i: (i, 0)),
                  _mod_spec(2, row_of_tile),
                  pl.BlockSpec((tm, HY_CH), lambda i: (i, 0)),
                  pl.BlockSpec((tm, GLA_VW), lambda i: (i, 0)),
                  pl.BlockSpec((tm, MLA_HEADS * MLA_V), lambda i: (i, 0)),
                  pl.BlockSpec((None, HY_CH, D_MODEL), lambda i: (layer, 0, 0), pipeline_mode=once),
                  pl.BlockSpec((None, GLA_VW, D_MODEL), lambda i: (layer, 1, 0), pipeline_mode=once),
                  pl.BlockSpec((None, MLA_HEADS * MLA_V, D_MODEL), lambda i: (layer, 1, 0), pipeline_mode=once)],
        out_specs=pl.BlockSpec((tm, D_MODEL), lambda i: (i, 0)),
        out_shape=jax.ShapeDtypeStruct((rows, D_MODEL), F32),
        compiler_params=_cparams(("arbitrary",)),
        name="out_projection",
    )(x2d, mod, y_hy, y_gla, y_mla, w_out_b, w_out_b, w_out_b)


FFN_TF = 512


def _ffn_kernel(x_ref, ln_ref, sh_ref, sc_ref, g_ref, wg_ref, wu_ref, wo_ref, lnf_ref, o_ref,
                h_ref, acc_ref, *, final):
    f = pl.program_id(1)

    @pl.when(f == 0)
    def _():
        _adaln_all(x_ref, ln_ref[...] * (1.0 + sc_ref[...]), sh_ref[...], h_ref)
        acc_ref[...] = jnp.zeros_like(acc_ref)

    h = h_ref[...]
    gate = _dot(h, wg_ref[...])
    up = _dot(h, wu_ref[...])
    act = (gate * jax.nn.sigmoid(gate) * up).astype(BF16)
    acc_ref[...] += _dot(act, wo_ref[...])

    @pl.when(f == pl.num_programs(1) - 1)
    def _():
        def body(i, carry):
            rows = pl.ds(pl.multiple_of(i * ADALN_ROWS, ADALN_ROWS), ADALN_ROWS)
            x = x_ref[rows, :] + g_ref[...] * acc_ref[rows, :]
            o_ref[rows, :] = _rms(x, lnf_ref[...]) if final else x
            return carry

        lax.fori_loop(0, x_ref.shape[0] // ADALN_ROWS, body, 0, unroll=4)


def _ffn(x2d, ln, mod, w_in_b, w_out_b, layer, ln_final, row_of_tile, tm, final):
    rows = x2d.shape[0]
    nf = D_FF // FFN_TF
    return pl.pallas_call(
        functools.partial(_ffn_kernel, final=final),
        grid=(rows // tm, nf),
        in_specs=[pl.BlockSpec((tm, D_MODEL), lambda i, f: (i, 0)),
                  pl.BlockSpec((1, D_MODEL), lambda i, f: (0, 0)),
                  _mod_spec(3, row_of_tile),
                  _mod_spec(4, row_of_tile),
                  _mod_spec(5, row_of_tile),
                  pl.BlockSpec((None, D_MODEL, FFN_TF), lambda i, f: (layer, 0, f)),
                  pl.BlockSpec((None, D_MODEL, FFN_TF), lambda i, f: (layer, 0, nf + f)),
                  pl.BlockSpec((None, FFN_TF, D_MODEL), lambda i, f: (layer, f, 0)),
                  pl.BlockSpec((1, D_MODEL), lambda i, f: (0, 0))],
        out_specs=pl.BlockSpec((tm, D_MODEL), lambda i, f: (i, 0)),
        out_shape=jax.ShapeDtypeStruct((rows, D_MODEL), F32),
        scratch_shapes=[pltpu.VMEM((tm, D_MODEL), BF16), pltpu.VMEM((tm, D_MODEL), F32)],
        compiler_params=_cparams(("arbitrary", "arbitrary")),
        name="ffn",
    )(x2d, ln, mod, mod, mod, w_in_b, w_in_b, w_out_b, ln_final)


def _dft_tables(L):
    k = np.arange(L)
    ang = np.pi * ((k[:, None] * k[None, :]) % (2 * L)) / L
    c = np.cos(ang)
    s = np.sin(ang)
    sp = s.copy()
    sp[0, :] = 1.0 - 2.0 * (k % 2)
    as32 = lambda a: jnp.asarray(a.astype(np.float32))
    return as32(c), as32(s), as32(sp), as32(sp.T.copy())


def _hyena_consts(L):
    c, s, sp, spt = _dft_tables(L)
    c_hi, c_lo = _split(c)
    s_hi, s_lo = _split(s)
    t = jnp.linspace(0.0, 1.0, L, dtype=F32)[:, None]
    w = 2.0 * math.pi * jnp.arange(L, dtype=F32)[:, None] / L
    f = jnp.linspace(1e-4, HY_BANDS - 1, HY_BANDS, dtype=F32)
    zpos = jnp.concatenate([t, jnp.cos(f * w), -jnp.sin(f * w)], axis=-1)
    zpos = jnp.pad(zpos, ((0, 0), (0, 64 - HY_EMB)))
    min_decay = math.log(HY_TARGET) / HY_SLOW_DECAY
    max_decay = math.log(HY_TARGET) / HY_FAST_DECAY
    delta = jnp.abs(jnp.linspace(min_decay, max_decay, HY_CH, dtype=F32))
    decay = jnp.exp(-t * delta)
    filt_consts = (zpos, decay, c_hi, c_lo, s_hi, s_lo)
    main_consts = (c_hi, sp.astype(BF16), spt.astype(BF16))
    return filt_consts, main_consts


def _rope_tables(L):
    t = jnp.arange(L)
    half = MLA_ROPE // 2
    inv = ROPE_THETA ** (-jnp.arange(0, half, 2, dtype=F32) / half)
    ang_r = (t // GRID_W).astype(F32)[:, None] * inv
    ang_c = (t % GRID_W).astype(F32)[:, None] * inv
    cr, sr, cc, sc = jnp.cos(ang_r), jnp.sin(ang_r), jnp.cos(ang_c), jnp.sin(ang_c)
    cos = jnp.concatenate([cr, cr, cc, cc, jnp.ones((L, 128 - MLA_ROPE), F32)], axis=1)
    sin = jnp.concatenate([-sr, sr, -sc, sc, jnp.zeros((L, 128 - MLA_ROPE), F32)], axis=1)
    return cos, sin


def _prep_w_in_tail(w):
    z = lambda n: jnp.zeros((DEPTH, D_MODEL, n), w.dtype)
    mla0 = IN_MAIN + 2 * GLA_LOWRANK
    return jnp.concatenate([w[:, :, mla0:], z(64), w[:, :, IN_MAIN:mla0], z(256 - 2 * GLA_LOWRANK)],
                           axis=2).astype(BF16)


def _prep_w_uq(w):
    w = w.reshape(MLA_Q_RANK, MLA_HEADS, MLA_NOPE + MLA_ROPE)
    w = jnp.pad(w, ((0, 0), (0, 0), (0, MLA_HW - MLA_NOPE - MLA_ROPE)))
    return w.reshape(MLA_Q_RANK, MLA_HEADS * MLA_HW).astype(BF16)


def _prep_gla_decay(wa_f, ba_f, wa_b, ba_b):
    wa = jnp.zeros((256, 2 * GLA_QK), F32)
    wa = wa.at[0:GLA_LOWRANK, 0:GLA_QK].set(wa_f)
    wa = wa.at[GLA_LOWRANK:2 * GLA_LOWRANK, GLA_QK:].set(wa_b)
    return wa, jnp.concatenate([ba_f, ba_b])[None, :]


def _trunk_layer(x2d, B, L, mod, row_of_tile, lw, consts, ctx, final, ln_final):
    tm = 1024
    tm_out = 512
    u = _in_projection(x2d, lw['ln_mix'], mod, lw['w_in'], lw['w_in_tail'], lw['layer'], row_of_tile(tm), tm)
    u3 = u.reshape(B, L, U_W)
    filt_consts, main_consts = consts['hyena']
    filt = _hyena_filters(L, filt_consts, lw['hy_w1'], lw['hy_b1'], lw['hy_freq'], lw['hy_w2'],
                          lw['hy_b2'], lw['hy_w3'])
    y_hy = _hyena_mixer(u3, lw['hy_conv_w'], lw['hy_conv_b'], lw['hy_skip'], filt, main_consts, L)
    if ctx is None:
        y_gla, s_f, s_b = _gla_mixer(u3, lw['gla_wa'], lw['gla_ba'], lw['gla_norm'], None, lw['layer'], L, True)
        y_mla, ckv, krope = _mla_mixer(u3, lw['mla_q_norm'], lw['mla_w_uq'], lw['mla_kv_norm'],
                                       lw['mla_w_ukv'], None, lw['layer'], L, True)
        extras = (ckv, krope, s_f, s_b)
    else:
        cache_ckv, cache_krope, s0f, s0b = ctx
        y_gla, _, _ = _gla_mixer(u3, lw['gla_wa'], lw['gla_ba'], lw['gla_norm'], (s0f, s0b), lw['layer'], L, False)
        cos, sin = consts['rope']
        y_mla, _, _ = _mla_mixer(u3, lw['mla_q_norm'], lw['mla_w_uq'], lw['mla_kv_norm'], lw['mla_w_ukv'],
                                 (cos, sin, cache_ckv, cache_krope), lw['layer'], L, False)
        extras = None
    rows = B * L
    x2d = _out_projection(x2d, mod, y_hy.reshape(rows, -1), y_gla.reshape(rows, -1), y_mla.reshape(rows, -1),
                          lw['w_out'], lw['layer'], row_of_tile(tm_out), tm_out)
    x2d = _ffn(x2d, lw['ln_ffn'], mod, lw['w_ffn_in'], lw['w_ffn_out'], lw['layer'], ln_final, row_of_tile(tm_out),
               tm_out, final)
    return x2d, extras


def kernel(x_prompt, x_sample, cache_mla_ckv, cache_mla_krope, state_gla_fwd, state_gla_bwd, c, c_ctx, w_mod, b_mod, ln_mix, w_in, hy_conv_w, hy_conv_b, hy_filt_w1, hy_filt_b1, hy_filt_freq, hy_filt_w2, hy_filt_b2, hy_filt_w3, hy_skip, gla_wa_f, gla_ba_f, gla_wa_b, gla_ba_b, gla_norm, mla_q_norm, mla_w_uq, mla_kv_norm, mla_w_ukv, w_out, ln_ffn, w_ffn_in, w_ffn_out, ln_final):
    Bc, Lc, _ = x_prompt.shape
    Bl, Ll, _ = x_sample.shape
    assert 1 + Bl <= MOD_ROWS

    c_all = jnp.concatenate([c_ctx[None, :], c, jnp.zeros((MOD_ROWS - 1 - Bl, D_MODEL), F32)], axis=0)
    mod_all = _modulation(c_all, w_mod, b_mod).reshape(DEPTH, MOD_ROWS, 6, 1, D_MODEL)

    consts_ctx = {'hyena': _hyena_consts(Lc)}
    consts_lat = {'hyena': _hyena_consts(Ll), 'rope': _rope_tables(Ll)}
    ctx_rows = lambda tm: (lambda i: 0)
    lat_rows = lambda tm: (lambda i: 1 + (i * tm) // Ll)
    lnf = ln_final[None, :]

    x_ctx = x_prompt.reshape(Bc * Lc, D_MODEL)
    x_lat = x_sample.reshape(Bl * Ll, D_MODEL)
    w_in_b = w_in.astype(BF16)
    w_in_tail = _prep_w_in_tail(w_in)
    w_out_b = w_out.astype(BF16)
    w_ffn_in_b = w_ffn_in.astype(BF16)
    w_ffn_out_b = w_ffn_out.astype(BF16)
    ckv_l, krope_l, sf_l, sb_l = [], [], [], []
    for l in range(DEPTH):
        wa, ba = _prep_gla_decay(gla_wa_f[l], gla_ba_f[l], gla_wa_b[l], gla_ba_b[l])
        lw = {
            'layer': l, 'ln_mix': ln_mix[l][None, :], 'w_in': w_in_b, 'w_in_tail': w_in_tail,
            'hy_conv_w': hy_conv_w[l], 'hy_conv_b': hy_conv_b[l][None, :],
            'hy_w1': jnp.pad(hy_filt_w1[l], ((0, 64 - HY_EMB), (0, 0))), 'hy_b1': hy_filt_b1[l][None, :],
            'hy_freq': hy_filt_freq[l], 'hy_w2': hy_filt_w2[l], 'hy_b2': hy_filt_b2[l][None, :],
            'hy_w3': hy_filt_w3[l], 'hy_skip': hy_skip[l],
            'gla_wa': wa, 'gla_ba': ba, 'gla_norm': gla_norm[l][None, :],
            'mla_q_norm': mla_q_norm[l][None, :], 'mla_w_uq': _prep_w_uq(mla_w_uq[l]),
            'mla_kv_norm': mla_kv_norm[l][None, :], 'mla_w_ukv': mla_w_ukv[l].astype(BF16),
            'w_out': w_out_b, 'ln_ffn': ln_ffn[l][None, :],
            'w_ffn_in': w_ffn_in_b, 'w_ffn_out': w_ffn_out_b,
        }
        final = l == DEPTH - 1
        mod = mod_all[l]
        x_ctx, (ckv, krope, s_f, s_b) = _trunk_layer(x_ctx, Bc, Lc, mod, ctx_rows, lw, consts_ctx, None,
                                                     final, lnf)
        ckv_l.append(ckv)
        krope_l.append(krope)
        sf_l.append(s_f)
        sb_l.append(s_b)
        ctx = (cache_mla_ckv, cache_mla_krope, state_gla_fwd, state_gla_bwd)
        x_lat, _ = _trunk_layer(x_lat, Bl, Ll, mod, lat_rows, lw, consts_lat, ctx, final, lnf)
    return (x_ctx.reshape(Bc, Lc, D_MODEL), x_lat.reshape(Bl, Ll, D_MODEL),
            jnp.stack(ckv_l, axis=1), jnp.stack(krope_l, axis=1),
            jnp.stack(sf_l, axis=1), jnp.stack(sb_l, axis=1))
```

```python
import functools
import math

import numpy as np
import jax
import jax.numpy as jnp
from jax import lax
from jax.experimental import pallas as pl
from jax.experimental.pallas import tpu as pltpu

F32 = jnp.float32
BF16 = jnp.bfloat16

D_MODEL = 2048
DEPTH = 2
GRID_W = 64
EPS = 1e-6
HY_CH = 512
HY_ORDER = 2
HY_BANDS = 16
HY_EMB = 1 + 2 * HY_BANDS
HY_FF = 64
HY_FAST_DECAY = 0.3
HY_SLOW_DECAY = 1.5
HY_TARGET = 1e-2
HY_IN = 3 * HY_CH
GLA_HEADS = 4
GLA_DK = 64
GLA_DV = 128
GLA_LOWRANK = 16
GLA_TAU = 16.0
GLA_CHUNK = 64
GLA_QK = GLA_HEADS * GLA_DK
GLA_VW = GLA_HEADS * GLA_DV
MLA_HEADS = 8
MLA_Q_RANK = 384
MLA_KV_RANK = 256
MLA_NOPE = 128
MLA_ROPE = 64
MLA_V = 128
ROPE_THETA = 10000.0
D_FF = -(-8 * D_MODEL // (3 * 256)) * 256

U_HY = 0
U_GLA = 1536
U_MLA = 3072
U_GLA_A = 3840
U_W = 4096
MOD_ROWS = 16

VMEM_LIMIT_V7X = 48 * 1024 * 1024


def _cparams(sem):
    return pltpu.CompilerParams(dimension_semantics=sem, vmem_limit_bytes=VMEM_LIMIT_V7X)


def _dot(a, b):
    return jnp.dot(a, b, preferred_element_type=F32)


def _dot_nt(a, b):
    return lax.dot_general(a, b, (((1,), (1,)), ((), ())), preferred_element_type=F32)


def _split(x):
    hi = x.astype(BF16)
    lo = (x - hi.astype(F32)).astype(BF16)
    return hi, lo


def _dot3(a, b):
    a_hi, a_lo = _split(a)
    b_hi, b_lo = _split(b)
    return _dot(a_hi, b_hi) + (_dot(a_lo, b_hi) + _dot(a_hi, b_lo))


def _rms(x, g):
    ms = jnp.mean(x * x, axis=-1, keepdims=True)
    return x * lax.rsqrt(ms + EPS) * g


ADALN_ROWS = 16


def _adaln_group(x_ref, gain, shift, h_ref, r):
    rows = pl.ds(pl.multiple_of(r, ADALN_ROWS), ADALN_ROWS)
    x = x_ref[rows, :]
    ms = jnp.mean(x * x, axis=-1, keepdims=True)
    h_ref[rows, :] = (x * lax.rsqrt(ms + EPS) * gain + shift).astype(BF16)


def _adaln_all(x_ref, gain, shift, h_ref):
    def body(i, carry):
        _adaln_group(x_ref, gain, shift, h_ref, i * ADALN_ROWS)
        return carry

    lax.fori_loop(0, x_ref.shape[0] // ADALN_ROWS, body, 0, unroll=4)


def _adaln_slice(x_ref, gain, shift, h_ref, base, nrows):
    for t in range(nrows // ADALN_ROWS):
        _adaln_group(x_ref, gain, shift, h_ref, base + t * ADALN_ROWS)


def _mod_kernel(c_ref, w_ref, b_ref, o_ref):
    c = c_ref[...]
    s = (c * jax.nn.sigmoid(c)).astype(BF16)
    o_ref[...] = _dot(s, w_ref[...].astype(BF16)) + b_ref[...]


def _modulation(c_all, w_mod, b_mod):
    tn = 1024
    n6 = 6 * D_MODEL
    return pl.pallas_call(
        _mod_kernel,
        grid=(DEPTH, n6 // tn),
        in_specs=[pl.BlockSpec((MOD_ROWS, D_MODEL), lambda l, j: (0, 0)),
                  pl.BlockSpec((None, D_MODEL, tn), lambda l, j: (l, 0, j)),
                  pl.BlockSpec((None, 1, tn), lambda l, j: (l, 0, j))],
        out_specs=pl.BlockSpec((None, MOD_ROWS, tn), lambda l, j: (l, 0, j)),
        out_shape=jax.ShapeDtypeStruct((DEPTH, MOD_ROWS, n6), F32),
        compiler_params=_cparams(("arbitrary", "arbitrary")),
        name="modulation",
    )(c_all, w_mod, b_mod.reshape(DEPTH, 1, n6))


def _mod_spec(chunk, row_of_tile, tile_of_step=lambda i, *_: i):
    return pl.BlockSpec((None, None, 1, D_MODEL), lambda *g: (row_of_tile(tile_of_step(*g)), chunk, 0, 0))


def _lookahead(n_tiles):
    return lambda i, j: jnp.minimum(i + jnp.minimum(j, 1), n_tiles - 1)


IN_TN = 1024
IN_MAIN = 3072
IN_PRE_ROWS = 352
IN_PRE_STRIDE = 336


def _inproj_kernel(x_ref, ln_ref, sh_ref, sc_ref, w_ref, wt_ref, o_ref, ha_ref, hb_ref):
    i = pl.program_id(0)
    j = pl.program_id(1)
    gain = ln_ref[...] * (1.0 + sc_ref[...])
    shift = sh_ref[...]

    @pl.when((i == 0) & (j == 0))
    def _():
        _adaln_all(x_ref, gain, shift, ha_ref)

    base = jnp.clip(j - 1, 0, 2) * IN_PRE_STRIDE

    def step(cur_ref, nxt_ref):
        @pl.when(j < IN_MAIN // IN_TN)
        def _():
            _adaln_slice(x_ref, gain, shift, nxt_ref, base, IN_PRE_ROWS)
            o_ref[...] = _dot(cur_ref[...], w_ref[...])

        @pl.when(j >= IN_MAIN // IN_TN)
        def _():
            _adaln_slice(x_ref, gain, shift, nxt_ref, base, IN_PRE_ROWS)
            o_ref[...] = _dot(cur_ref[...], wt_ref[...])

    @pl.when(i % 2 == 0)
    def _():
        step(ha_ref, hb_ref)

    @pl.when(i % 2 == 1)
    def _():
        step(hb_ref, ha_ref)


def _in_projection(x2d, ln, mod, w_in_b, w_tail, layer, row_of_tile, tm):
    rows = x2d.shape[0]
    n_main = IN_MAIN // IN_TN
    ahead = _lookahead(rows // tm)
    return pl.pallas_call(
        _inproj_kernel,
        grid=(rows // tm, U_W // IN_TN),
        in_specs=[pl.BlockSpec((tm, D_MODEL), lambda i, j: (ahead(i, j), 0)),
                  pl.BlockSpec((1, D_MODEL), lambda i, j: (0, 0)),
                  _mod_spec(0, row_of_tile, ahead),
                  _mod_spec(1, row_of_tile, ahead),
                  pl.BlockSpec((None, D_MODEL, IN_TN), lambda i, j: (layer, 0, jnp.minimum(j, n_main - 1))),
                  pl.BlockSpec((None, D_MODEL, U_W - IN_MAIN), lambda i, j: (layer, 0, 0))],
        out_specs=pl.BlockSpec((tm, IN_TN), lambda i, j: (i, j)),
        out_shape=jax.ShapeDtypeStruct((rows, U_W), F32),
        scratch_shapes=[pltpu.VMEM((tm, D_MODEL), BF16), pltpu.VMEM((tm, D_MODEL), BF16)],
        compiler_params=_cparams(("arbitrary", "arbitrary")),
        name="in_projection",
    )(x2d, ln, mod, mod, w_in_b, w_tail)


def _filter_kernel(z_ref, w1_ref, b1_ref, fr_ref, w2_ref, b2_ref, w3f_ref, w3b_ref, dec_ref,
                   chi_ref, clo_ref, shi_ref, slo_ref, kr_ref, ki_ref, kn_ref, *, L):
    h = jnp.sin(fr_ref[0:1, :] * (_dot3(z_ref[...], w1_ref[...]) + b1_ref[...]))
    h = jnp.sin(fr_ref[1:2, :] * (_dot3(h, w2_ref[...]) + b2_ref[...]))
    dec = dec_ref[...]
    row = lax.broadcasted_iota(jnp.int32, (L, 1), 0)
    kf = _dot3(h, w3f_ref[...]) * dec
    kb = jnp.where(row == 0, 0.0, _dot3(h, w3b_ref[...]) * dec)
    p_hi, p_lo = _split(kf + kb)
    m_hi, m_lo = _split(kf - kb)
    chi, clo = chi_ref[...], clo_ref[...]
    shi, slo = shi_ref[...], slo_ref[...]
    kc = _dot(chi, p_hi) + (_dot(clo, p_hi) + _dot(chi, p_lo))
    ks = _dot(shi, m_hi) + (_dot(slo, m_hi) + _dot(shi, m_lo))
    sign = jnp.where((row & 1) == 0, 1.0, -1.0)
    kn = jnp.sum((kf + kb) * sign, axis=0, keepdims=True)
    kr_ref[...] = kc * jnp.where(row == 0, 0.5 / L, 1.0 / L)
    ki_ref[...] = ks * (-1.0 / L)
    kn_ref[...] = kn * (0.5 / L)


def _hyena_filters(L, consts, w1p, b1, freq, w2, b2, w3):
    zpos, decay, c_hi, c_lo, s_hi, s_lo = consts
    full = lambda shape: pl.BlockSpec(shape, lambda o: (0,) * len(shape))
    return pl.pallas_call(
        functools.partial(_filter_kernel, L=L),
        grid=(HY_ORDER,),
        in_specs=[full((L, 64)), full((64, HY_FF)), full((1, HY_FF)), full((2, HY_FF)),
                  full((HY_FF, HY_FF)), full((1, HY_FF)),
                  pl.BlockSpec((HY_FF, HY_CH), lambda o: (0, o)),
                  pl.BlockSpec((HY_FF, HY_CH), lambda o: (0, HY_ORDER + o)),
                  full((L, HY_CH)), full((L, L)), full((L, L)), full((L, L)), full((L, L))],
        out_specs=[pl.BlockSpec((None, L, HY_CH), lambda o: (o, 0, 0)),
                   pl.BlockSpec((None, L, HY_CH), lambda o: (o, 0, 0)),
                   pl.BlockSpec((None, 1, HY_CH), lambda o: (o, 0, 0))],
        out_shape=[jax.ShapeDtypeStruct((HY_ORDER, L, HY_CH), F32),
                   jax.ShapeDtypeStruct((HY_ORDER, L, HY_CH), F32),
                   jax.ShapeDtypeStruct((HY_ORDER, 1, HY_CH), F32)],
        compiler_params=_cparams(("arbitrary",)),
        name="hyena_filters",
    )(zpos, w1p, b1, freq, w2, b2, w3, w3, decay, c_hi, c_lo, s_hi, s_lo)


HY_CG = 256
HY_STEP_ROWS = 1024


def _hyena_kernel(v_ref, x1_ref, x2_ref, wv_ref, w1_ref, w2_ref, bv_ref, b1_ref, b2_ref, skip_ref,
                  kr_ref, ki_ref, kn_ref, c_ref, sp_ref, spt_ref, o_ref, *, L):
    row = lax.broadcasted_iota(jnp.int32, (L, 1), 0)
    first = row == 0
    last = row == L - 1
    chains = [(b, slice(g * HY_CG, (g + 1) * HY_CG)) for b in range(v_ref.shape[0]) for g in range(HY_CH // HY_CG)]

    def short_conv(u_ref, w_ref, b_ref, b, cols):
        u = u_ref[b, :, cols]
        prev = jnp.where(first, 0.0, pltpu.roll(u, 1, axis=0))
        nxt = jnp.where(last, 0.0, pltpu.roll(u, L - 1, axis=0))
        return prev * w_ref[0:1, cols] + u * w_ref[1:2, cols] + nxt * w_ref[2:3, cols] + b_ref[:, cols]

    gate_refs = ((x1_ref, w1_ref, b1_ref), (x2_ref, w2_ref, b2_ref))
    z = [short_conv(v_ref, wv_ref, bv_ref, b, cols) for b, cols in chains]
    for o in range(HY_ORDER):
        zb = [zz.astype(BF16) for zz in z]
        a = [_dot(c_ref[...], x) for x in zb]
        s = [_dot(sp_ref[...], x) for x in zb]
        conv = []
        for n, (b, cols) in enumerate(chains):
            kr = kr_ref[o, :, cols]
            ki = ki_ref[o, :, cols]
            qt = a[n] * kr + s[n] * ki
            qb = s[n] * jnp.where(first, kn_ref[o, :, cols], kr) - a[n] * ki
            conv.append(_dot(c_ref[...], qt.astype(BF16)) + _dot(spt_ref[...], qb.astype(BF16)))
        z = [short_conv(*gate_refs[o], b, cols) * (conv[n] + z[n] * skip_ref[o:o + 1, cols])
             for n, (b, cols) in enumerate(chains)]
    for n, (b, cols) in enumerate(chains):
        o_ref[b, :, cols] = z[n].astype(BF16)


def _hyena_mixer(u3, conv_w, conv_b, skip, filt, dft, L):
    B = u3.shape[0]
    nb = max(1, HY_STEP_ROWS // L)
    kr, ki, kn = filt
    c_b, sp_b, spt_b = dft
    once = pl.Buffered(1)
    ublk = lambda part: pl.BlockSpec((nb, L, HY_CH), lambda b: (b, 0, part))
    wblk = lambda part: pl.BlockSpec((3, HY_CH), lambda b: (0, part))
    bblk = lambda part: pl.BlockSpec((1, HY_CH), lambda b: (0, part))
    fblk = lambda rows: pl.BlockSpec((HY_ORDER, rows, HY_CH), lambda b: (0, 0, 0), pipeline_mode=once)
    mat = pl.BlockSpec((L, L), lambda b: (0, 0), pipeline_mode=once)
    return pl.pallas_call(
        functools.partial(_hyena_kernel, L=L),
        grid=(B // nb,),
        in_specs=[ublk(0), ublk(1), ublk(2), wblk(0), wblk(1), wblk(2), bblk(0), bblk(1), bblk(2),
                  pl.BlockSpec((HY_ORDER, HY_CH), lambda b: (0, 0)),
                  fblk(L), fblk(L), fblk(1), mat, mat, mat],
        out_specs=pl.BlockSpec((nb, L, HY_CH), lambda b: (b, 0, 0)),
        out_shape=jax.ShapeDtypeStruct((B, L, HY_CH), BF16),
        compiler_params=_cparams(("arbitrary",)),
        name="hyena_mixer",
    )(u3, u3, u3, conv_w, conv_w, conv_w, conv_b, conv_b, conv_b, skip, kr, ki, kn, c_b, sp_b, spt_b)


GLA_PAIR = 2 * GLA_CHUNK
GLA_STATES = 2 * GLA_HEADS
GLA_SCAN = 256


def _gla_kernel(*refs, L, has_state, emit_state, n_alias):
    it = iter(refs)
    qkvg_ref, a_ref, wa_ref, ba_ref, ng_ref, scan_ref = (next(it) for _ in range(6))
    st0_refs = (next(it), next(it)) if has_state else None
    for _ in range(n_alias):
        next(it)
    y_ref = next(it)
    stout_refs = (next(it), next(it)) if emit_state else None
    qd_s, ki_s, ke_s, qi_s, ku_s, et_s, vb_s, vt_s, o_s, st_s = (next(it) for _ in range(10))

    npair = L // GLA_PAIR
    pre = _dot3(a_ref[...], wa_ref[...]) + ba_ref[...]
    la = (jnp.minimum(pre, 0.0) - jnp.log(1.0 + jnp.exp(-jnp.abs(pre)))) * (1.0 / GLA_TAU)

    scan = scan_ref[...]
    pfx, tot = [], []
    for r in range(0, L, GLA_SCAN):
        x = la[r:r + GLA_SCAN]
        hi = x.astype(BF16)
        r1 = x - hi.astype(F32)
        mid = r1.astype(BF16)
        lo = (r1 - mid.astype(F32)).astype(BF16)
        res = _dot(scan, hi) + (_dot(scan, mid) + _dot(scan, lo))
        pfx.append(res[:GLA_SCAN])
        tot.append(res[GLA_SCAN:])
    pfx = jnp.concatenate(pfx, axis=0) if len(pfx) > 1 else pfx[0]
    tot = jnp.concatenate(tot, axis=0) if len(tot) > 1 else tot[0]

    upper = (lax.broadcasted_iota(jnp.int32, (L, 1), 0) & GLA_CHUNK) != 0
    q = qkvg_ref[:, 0:GLA_QK] * (GLA_DK ** -0.5)
    k = qkvg_ref[:, GLA_QK:2 * GLA_QK]
    for dirn in range(2):
        cols = slice(dirn * GLA_QK, (dirn + 1) * GLA_QK)
        if dirn == 0:
            cum = pfx[:, cols]
            rest = tot[:, cols] - cum
            first = jnp.logical_not(upper)
        else:
            rest = pfx[:, cols] - la[:, cols]
            cum = tot[:, cols] - rest
            first = upper
        et = jnp.exp(tot[:, cols])
        et_other = jnp.where(upper, pltpu.roll(et, GLA_CHUNK, axis=0), pltpu.roll(et, L - GLA_CHUNK, axis=0))
        qd = q * jnp.exp(cum)
        ke = k * jnp.exp(rest)
        et_s[dirn] = et
        qd_s[dirn] = qd.astype(BF16)
        ki_s[dirn] = (k * jnp.exp(-cum)).astype(BF16)
        ke_s[dirn] = ke.astype(BF16)
        qi_s[dirn] = jnp.where(first, qd, qd * et_other).astype(BF16)
        ku_s[dirn] = jnp.where(first, ke * et_other, ke).astype(BF16)
    vb_s[...] = qkvg_ref[:, 2 * GLA_QK:2 * GLA_QK + GLA_VW].astype(BF16)

    def transpose_v(n2, carry):
        r0 = pl.multiple_of(n2 * GLA_PAIR, GLA_PAIR)
        for h in range(GLA_HEADS):
            vp = qkvg_ref[pl.ds(r0, GLA_PAIR), 2 * GLA_QK + h * GLA_DV:2 * GLA_QK + (h + 1) * GLA_DV]
            vt_s[n2, h * GLA_DV:(h + 1) * GLA_DV, :] = vp.T.astype(BF16)
        return carry

    lax.fori_loop(0, npair, transpose_v, 0)

    if has_state:
        zero = jnp.zeros((GLA_DK, GLA_DV), F32)
        for dirn in range(2):
            for h in range(GLA_HEADS):
                s0 = st0_refs[dirn][h]
                both = jnp.concatenate([s0, zero] if h % 2 == 0 else [zero, s0], axis=0)
                idx = dirn * GLA_HEADS + h
                st_s[idx * GLA_DV:(idx + 1) * GLA_DV, :] = both.T
    else:
        st_s[...] = jnp.zeros_like(st_s)

    lane = lax.broadcasted_iota(jnp.int32, (1, 2 * GLA_DK), 1)
    head_lanes = (lane < GLA_DK, lane >= GLA_DK)
    ri = lax.broadcasted_iota(jnp.int32, (GLA_PAIR, GLA_PAIR), 0)
    ci = lax.broadcasted_iota(jnp.int32, (GLA_PAIR, GLA_PAIR), 1)
    same_chunk = ((ri ^ ci) & GLA_CHUNK) == 0
    keep = (ci <= ri, ci >= ri)

    def pair_step(n, carry):
        for dirn in range(2):
            n2 = n if dirn == 0 else npair - 1 - n
            r0 = pl.multiple_of(n2 * GLA_PAIR, GLA_PAIR)
            rows = pl.ds(r0, GLA_PAIR)
            for p in range(GLA_HEADS // 2):
                lanes = slice(p * 2 * GLA_DK, (p + 1) * 2 * GLA_DK)
                qd = qd_s[dirn, rows, lanes]
                qi = qi_s[dirn, rows, lanes]
                zeros = jnp.zeros_like(qd)
                dec = (et_s[dirn, pl.ds(r0, 8), lanes][0:1, :]
                       * et_s[dirn, pl.ds(r0 + GLA_CHUNK, 8), lanes][0:1, :])
                q2 = jnp.concatenate([jnp.where(head_lanes[0], qd, zeros), jnp.where(head_lanes[1], qd, zeros)],
                                     axis=0)
                att_d = _dot_nt(q2, ki_s[dirn, rows, lanes])
                att_o = _dot_nt(q2, ke_s[dirn, rows, lanes])
                s0 = (dirn * GLA_HEADS + 2 * p) * GLA_DV
                st2 = st_s[s0:s0 + 2 * GLA_DV, :]
                vt2 = vt_s[n2, 2 * p * GLA_DV:(2 * p + 2) * GLA_DV, :]
                st_s[s0:s0 + 2 * GLA_DV, :] = st2 * dec + _dot(vt2, ku_s[dirn, rows, lanes])
                for hh in range(2):
                    h = 2 * p + hh
                    hr = slice(hh * GLA_PAIR, (hh + 1) * GLA_PAIR)
                    att = jnp.where(keep[dirn], jnp.where(same_chunk, att_d[hr], att_o[hr]), 0.0)
                    qm = jnp.where(head_lanes[hh], qi, zeros)
                    o = (_dot(att.astype(BF16), vb_s[rows, h * GLA_DV:(h + 1) * GLA_DV])
                         + _dot_nt(qm, st2[hh * GLA_DV:(hh + 1) * GLA_DV].astype(BF16)))
                    o_s[dirn, rows, h * GLA_DV:(h + 1) * GLA_DV] = o
        return carry

    lax.fori_loop(0, npair, pair_step, 0, unroll=2)

    for h in range(GLA_HEADS):
        cols = slice(h * GLA_DV, (h + 1) * GLA_DV)
        o = _rms(o_s[0, :, cols] + o_s[1, :, cols], ng_ref[...])
        g = qkvg_ref[:, 2 * GLA_QK + GLA_VW + h * GLA_DV:2 * GLA_QK + GLA_VW + (h + 1) * GLA_DV]
        y_ref[:, cols] = (o * (g * jax.nn.sigmoid(g))).astype(BF16)
    if emit_state:
        for dirn in range(2):
            for h in range(GLA_HEADS):
                idx = dirn * GLA_HEADS + h
                half = (h % 2) * GLA_DK
                stout_refs[dirn][h] = st_s[idx * GLA_DV:(idx + 1) * GLA_DV, :].T[half:half + GLA_DK, :]


def _gla_scan_matrix():
    i = np.arange(GLA_SCAN)
    same = (i[:, None] // GLA_CHUNK) == (i[None, :] // GLA_CHUNK)
    tri = same & (i[None, :] <= i[:, None])
    return jnp.asarray(np.concatenate([tri, same], axis=0).astype(np.float32)).astype(BF16)


def _gla_mixer(u3, wa_p, ba_p, norm_g, st0, layer, L, emit_state):
    B = u3.shape[0]
    has_state = st0 is not None
    head_state = (GLA_HEADS, GLA_DK, GLA_DV)
    in_specs = [pl.BlockSpec((None, L, 1536), lambda b: (b, 0, U_GLA // 1536)),
                pl.BlockSpec((None, L, 256), lambda b: (b, 0, U_GLA_A // 256)),
                pl.BlockSpec((256, 2 * GLA_QK), lambda b: (0, 0)),
                pl.BlockSpec((1, 2 * GLA_QK), lambda b: (0, 0)),
                pl.BlockSpec((1, GLA_DV), lambda b: (0, 0)),
                pl.BlockSpec((2 * GLA_SCAN, GLA_SCAN), lambda b: (0, 0))]
    args = [u3, u3, wa_p, ba_p, norm_g, _gla_scan_matrix()]
    if has_state:
        in_specs += [pl.BlockSpec((None, None) + head_state, lambda b: (b, layer, 0, 0, 0))] * 2
        args += list(st0)
    out_specs = [pl.BlockSpec((None, L, GLA_VW), lambda b: (b, 0, 0))]
    out_shape = [jax.ShapeDtypeStruct((B, L, GLA_VW), BF16)]
    aliases = {}
    if emit_state is not None:
        out_specs += [pl.BlockSpec((None, None) + head_state, lambda b: (b, layer, 0, 0, 0))] * 2
        out_shape += [jax.ShapeDtypeStruct((B, DEPTH) + head_state, F32)] * 2
        for k, prev in enumerate(emit_state):
            if prev is not None:
                aliases[len(args)] = 1 + k
                in_specs.append(pl.BlockSpec(memory_space=pl.ANY))
                args.append(prev)
    npair = L // GLA_PAIR
    qk_bf16 = pltpu.VMEM((2, L, GLA_QK), BF16)
    res = pl.pallas_call(
        functools.partial(_gla_kernel, L=L, has_state=has_state, emit_state=emit_state is not None,
                          n_alias=len(aliases)),
        grid=(B,),
        in_specs=in_specs,
        out_specs=out_specs,
        out_shape=out_shape,
        input_output_aliases=aliases,
        scratch_shapes=[qk_bf16,
                        qk_bf16,
                        qk_bf16,
                        qk_bf16,
                        qk_bf16,
                        pltpu.VMEM((2, L, GLA_QK), F32),
                        pltpu.VMEM((L, GLA_VW), BF16),
                        pltpu.VMEM((npair, GLA_VW, GLA_PAIR), BF16),
                        pltpu.VMEM((2, L, GLA_VW), F32),
                        pltpu.VMEM((GLA_STATES * GLA_DV, 2 * GLA_DK), F32)],
        compiler_params=_cparams(("arbitrary",)),
        name="gla_mixer",
    )(*args)
    return (res[0], res[1], res[2]) if emit_state is not None else (res[0], None, None)


MLA_HW = 256
MLA_TQ = 512
MLA_KCH = 256


def _rope(x, cos, sin):
    lane = lax.broadcasted_iota(jnp.int32, x.shape, 1)
    partner = jnp.where((lane & 16) == 0, pltpu.roll(x, 112, axis=1), pltpu.roll(x, 16, axis=1))
    return x * cos + partner * sin


def _mla_kernel(*refs, L, Lk, tq, rope, emit, n_alias):
    it = iter(refs)
    u_ref, qn_ref, wq_ref, kvn_ref, wkv_ref = (next(it) for _ in range(5))
    cos_ref, sin_ref, cckv_ref, ckr_ref = (next(it) for _ in range(4)) if rope else (None,) * 4
    for _ in range(n_alias):
        next(it)
    y_ref = next(it)
    ckv_out, kr_out = (next(it), next(it)) if emit else (None, None)
    kf_s, v_s = next(it), next(it)

    qi = pl.program_id(1)

    def project(keys, rope_part, r0):
        kv = _dot(keys, wkv_ref[...])
        for h in range(MLA_HEADS):
            kf_s[h, r0:r0 + MLA_KCH, 0:MLA_NOPE] = kv[:, h * 256:h * 256 + MLA_NOPE].astype(BF16)
            kf_s[h, r0:r0 + MLA_KCH, MLA_NOPE:MLA_HW] = rope_part
            v_s[h, r0:r0 + MLA_KCH, :] = kv[:, h * 256 + MLA_NOPE:(h + 1) * 256].astype(BF16)

    @pl.when(qi == 0)
    def _():
        for r0 in range(0, L, MLA_KCH):
            ckvn = _rms(u_ref[r0:r0 + MLA_KCH, MLA_Q_RANK:MLA_Q_RANK + MLA_KV_RANK], kvn_ref[...])
            kr = u_ref[r0:r0 + MLA_KCH, 640:768]
            if emit:
                ckv_out[r0:r0 + MLA_KCH, :] = ckvn
                kr_out[r0:r0 + MLA_KCH, :] = kr[:, 0:MLA_ROPE]
            if rope:
                kr = _rope(kr, cos_ref[r0:r0 + MLA_KCH, :], sin_ref[r0:r0 + MLA_KCH, :])
            project(ckvn.astype(BF16), kr.astype(BF16), r0)
        if rope:
            for r0 in range(0, Lk - L, MLA_KCH):
                ckr = ckr_ref[r0:r0 + MLA_KCH, :]
                ckr = jnp.concatenate([ckr, jnp.zeros_like(ckr)], axis=1)
                project(cckv_ref[r0:r0 + MLA_KCH, :].astype(BF16), ckr.astype(BF16), L + r0)

    q0 = pl.multiple_of(qi * tq, tq)
    cqn = _rms(u_ref[pl.ds(q0, tq), 0:MLA_Q_RANK], qn_ref[...]).astype(BF16)
    q = _dot(cqn, wq_ref[...]) * ((MLA_NOPE + MLA_ROPE) ** -0.5)
    if rope:
        cos = cos_ref[pl.ds(q0, tq), :]
        sin = sin_ref[pl.ds(q0, tq), :]
    for h in range(MLA_HEADS):
        q_nope = q[:, h * MLA_HW:h * MLA_HW + MLA_NOPE]
        q_rope = q[:, h * MLA_HW + MLA_NOPE:(h + 1) * MLA_HW]
        if rope:
            q_rope = _rope(q_rope, cos, sin)
        qh = jnp.concatenate([q_nope, q_rope], axis=1).astype(BF16)
        s = _dot_nt(qh, kf_s[h])
        p = jnp.exp(s - jnp.max(s, axis=-1, keepdims=True))
        denom = jnp.sum(p, axis=-1, keepdims=True)
        o = _dot(p.astype(BF16), v_s[h])
        y_ref[:, h * MLA_V:(h + 1) * MLA_V] = (o / denom).astype(BF16)


def _mla_mixer(u3, q_norm, wq_p, kv_norm, wkv, rope_args, layer, L, emit):
    B = u3.shape[0]
    rope = rope_args is not None
    Lk = L + (rope_args[2].shape[2] if rope else 0)
    tq = min(MLA_TQ, L)
    const = lambda shape: pl.BlockSpec(shape, lambda b, i: (0,) * len(shape))
    in_specs = [pl.BlockSpec((None, L, 768), lambda b, i: (b, 0, U_MLA // 768)),
                const((1, MLA_Q_RANK)), const((MLA_Q_RANK, MLA_HEADS * MLA_HW)),
                const((1, MLA_KV_RANK)), const((MLA_KV_RANK, MLA_HEADS * 256))]
    args = [u3, q_norm, wq_p, kv_norm, wkv]
    if rope:
        cos, sin, cckv, ckr = rope_args
        in_specs += [const((L, 128)), const((L, 128)),
                     pl.BlockSpec((None, None, Lk - L, MLA_KV_RANK), lambda b, i: (b, layer, 0, 0)),
                     pl.BlockSpec((None, None, Lk - L, MLA_ROPE), lambda b, i: (b, layer, 0, 0))]
        args += [cos, sin, cckv, ckr]
    out_specs = [pl.BlockSpec((None, tq, MLA_HEADS * MLA_V), lambda b, i: (b, i, 0))]
    out_shape = [jax.ShapeDtypeStruct((B, L, MLA_HEADS * MLA_V), BF16)]
    aliases = {}
    if emit is not None:
        out_specs += [pl.BlockSpec((None, None, L, MLA_KV_RANK), lambda b, i: (b, layer, 0, 0)),
                      pl.BlockSpec((None, None, L, MLA_ROPE), lambda b, i: (b, layer, 0, 0))]
        out_shape += [jax.ShapeDtypeStruct((B, DEPTH, L, MLA_KV_RANK), F32),
                      jax.ShapeDtypeStruct((B, DEPTH, L, MLA_ROPE), F32)]
        for k, prev in enumerate(emit):
            if prev is not None:
                aliases[len(args)] = 1 + k
                in_specs.append(pl.BlockSpec(memory_space=pl.ANY))
                args.append(prev)
    res = pl.pallas_call(
        functools.partial(_mla_kernel, L=L, Lk=Lk, tq=tq, rope=rope, emit=emit is not None, n_alias=len(aliases)),
        grid=(B, L // tq),
        in_specs=in_specs,
        out_specs=out_specs,
        out_shape=out_shape,
        input_output_aliases=aliases,
        scratch_shapes=[pltpu.VMEM((MLA_HEADS, Lk, MLA_HW), BF16),
                        pltpu.VMEM((MLA_HEADS, Lk, MLA_V), BF16)],
        compiler_params=_cparams(("arbitrary", "arbitrary")),
        name="mla_mixer",
    )(*args)
    return res if emit is not None else (res[0], None, None)


def _outproj_kernel(x_ref, g_ref, yh_ref, yg_ref, ym_ref, wh_ref, wg_ref, wm_ref, o_ref):
    y = _dot(yh_ref[...], wh_ref[...]) + _dot(yg_ref[...], wg_ref[...]) + _dot(ym_ref[...], wm_ref[...])
    o_ref[...] = x_ref[...] + g_ref[...] * y


def _out_projection(x2d, mod, y_hy, y_gla, y_mla, w_out_b, layer, row_of_tile, tm):
    rows = x2d.shape[0]
    once = pl.Buffered(1)
    return pl.pallas_call(
        _outproj_kernel,
        grid=(rows // tm,),
        in_specs=[pl.BlockSpec((tm, D_MODEL), lambda i: (i, 0)),
                  _mod_spec(2, row_of_tile),
                  pl.BlockSpec((tm, HY_CH), lambda i: (i, 0)),
                  pl.BlockSpec((tm, GLA_VW), lambda i: (i, 0)),
                  pl.BlockSpec((tm, MLA_HEADS * MLA_V), lambda i: (i, 0)),
                  pl.BlockSpec((None, HY_CH, D_MODEL), lambda i: (layer, 0, 0), pipeline_mode=once),
                  pl.BlockSpec((None, GLA_VW, D_MODEL), lambda i: (layer, 1, 0), pipeline_mode=once),
                  pl.BlockSpec((None, MLA_HEADS * MLA_V, D_MODEL), lambda i: (layer, 1, 0), pipeline_mode=once)],
        out_specs=pl.BlockSpec((tm, D_MODEL), lambda i: (i, 0)),
        out_shape=jax.ShapeDtypeStruct((rows, D_MODEL), F32),
        compiler_params=_cparams(("arbitrary",)),
        name="out_projection",
    )(x2d, mod, y_hy, y_gla, y_mla, w_out_b, w_out_b, w_out_b)


FFN_TF = 512


def _ffn_kernel(x_ref, ln_ref, sh_ref, sc_ref, g_ref, wg_ref, wu_ref, wo_ref, lnf_ref, o_ref,
                h_ref, acc_ref, *, final):
    f = pl.program_id(1)

    @pl.when(f == 0)
    def _():
        _adaln_all(x_ref, ln_ref[...] * (1.0 + sc_ref[...]), sh_ref[...], h_ref)
        acc_ref[...] = jnp.zeros_like(acc_ref)

    h = h_ref[...]
    gate = _dot(h, wg_ref[...])
    up = _dot(h, wu_ref[...])
    act = (gate * jax.nn.sigmoid(gate) * up).astype(BF16)
    acc_ref[...] += _dot(act, wo_ref[...])

    @pl.when(f == pl.num_programs(1) - 1)
    def _():
        def body(i, carry):
            rows = pl.ds(pl.multiple_of(i * ADALN_ROWS, ADALN_ROWS), ADALN_ROWS)
            x = x_ref[rows, :] + g_ref[...] * acc_ref[rows, :]
            o_ref[rows, :] = _rms(x, lnf_ref[...]) if final else x
            return carry

        lax.fori_loop(0, x_ref.shape[0] // ADALN_ROWS, body, 0, unroll=4)


def _ffn(x2d, ln, mod, w_in_b, w_out_b, layer, ln_final, row_of_tile, tm, final):
    rows = x2d.shape[0]
    nf = D_FF // FFN_TF
    return pl.pallas_call(
        functools.partial(_ffn_kernel, final=final),
        grid=(rows // tm, nf),
        in_specs=[pl.BlockSpec((tm, D_MODEL), lambda i, f: (i, 0)),
                  pl.BlockSpec((1, D_MODEL), lambda i, f: (0, 0)),
                  _mod_spec(3, row_of_tile),
                  _mod_spec(4, row_of_tile),
                  _mod_spec(5, row_of_tile),
                  pl.BlockSpec((None, D_MODEL, FFN_TF), lambda i, f: (layer, 0, f)),
                  pl.BlockSpec((None, D_MODEL, FFN_TF), lambda i, f: (layer, 0, nf + f)),
                  pl.BlockSpec((None, FFN_TF, D_MODEL), lambda i, f: (layer, f, 0)),
                  pl.BlockSpec((1, D_MODEL), lambda i, f: (0, 0))],
        out_specs=pl.BlockSpec((tm, D_MODEL), lambda i, f: (i, 0)),
        out_shape=jax.ShapeDtypeStruct((rows, D_MODEL), F32),
        scratch_shapes=[pltpu.VMEM((tm, D_MODEL), BF16), pltpu.VMEM((tm, D_MODEL), F32)],
        compiler_params=_cparams(("arbitrary", "arbitrary")),
        name="ffn",
    )(x2d, ln, mod, mod, mod, w_in_b, w_in_b, w_out_b, ln_final)


def _dft_tables(L):
    k = np.arange(L)
    ang = np.pi * ((k[:, None] * k[None, :]) % (2 * L)) / L
    c = np.cos(ang)
    s = np.sin(ang)
    sp = s.copy()
    sp[0, :] = 1.0 - 2.0 * (k % 2)
    as32 = lambda a: jnp.asarray(a.astype(np.float32))
    return as32(c), as32(s), as32(sp), as32(sp.T.copy())


def _hyena_consts(L):
    c, s, sp, spt = _dft_tables(L)
    c_hi, c_lo = _split(c)
    s_hi, s_lo = _split(s)
    f32 = np.float32
    t = np.linspace(0.0, 1.0, L, dtype=f32)[:, None]
    w = (f32(2.0 * math.pi) * np.arange(L, dtype=f32)[:, None] / f32(L)).astype(f32)
    f = np.linspace(1e-4, HY_BANDS - 1, HY_BANDS, dtype=f32)
    zpos = np.concatenate([t, np.cos(f * w), -np.sin(f * w)], axis=-1).astype(f32)
    zpos = np.pad(zpos, ((0, 0), (0, 64 - HY_EMB)))
    min_decay = math.log(HY_TARGET) / HY_SLOW_DECAY
    max_decay = math.log(HY_TARGET) / HY_FAST_DECAY
    delta = np.abs(np.linspace(min_decay, max_decay, HY_CH, dtype=f32))
    decay = np.exp(-t * delta).astype(f32)
    filt_consts = (jnp.asarray(zpos), jnp.asarray(decay), c_hi, c_lo, s_hi, s_lo)
    main_consts = (c_hi, sp.astype(BF16), spt.astype(BF16))
    return filt_consts, main_consts


def _rope_tables(L):
    t = jnp.arange(L)
    half = MLA_ROPE // 2
    inv = ROPE_THETA ** (-jnp.arange(0, half, 2, dtype=F32) / half)
    ang_r = (t // GRID_W).astype(F32)[:, None] * inv
    ang_c = (t % GRID_W).astype(F32)[:, None] * inv
    cr, sr, cc, sc = jnp.cos(ang_r), jnp.sin(ang_r), jnp.cos(ang_c), jnp.sin(ang_c)
    cos = jnp.concatenate([cr, cr, cc, cc, jnp.ones((L, 128 - MLA_ROPE), F32)], axis=1)
    sin = jnp.concatenate([-sr, sr, -sc, sc, jnp.zeros((L, 128 - MLA_ROPE), F32)], axis=1)
    return cos, sin


def _prep_w_in_tail(w):
    z = lambda n: jnp.zeros((DEPTH, D_MODEL, n), w.dtype)
    mla0 = IN_MAIN + 2 * GLA_LOWRANK
    return jnp.concatenate([w[:, :, mla0:], z(64), w[:, :, IN_MAIN:mla0], z(256 - 2 * GLA_LOWRANK)],
                           axis=2).astype(BF16)


def _prep_w_uq(w):
    w = w.reshape(MLA_Q_RANK, MLA_HEADS, MLA_NOPE + MLA_ROPE)
    w = jnp.pad(w, ((0, 0), (0, 0), (0, MLA_HW - MLA_NOPE - MLA_ROPE)))
    return w.reshape(MLA_Q_RANK, MLA_HEADS * MLA_HW).astype(BF16)


def _prep_gla_decay(wa_f, ba_f, wa_b, ba_b):
    wa = jnp.zeros((256, 2 * GLA_QK), F32)
    wa = wa.at[0:GLA_LOWRANK, 0:GLA_QK].set(wa_f)
    wa = wa.at[GLA_LOWRANK:2 * GLA_LOWRANK, GLA_QK:].set(wa_b)
    return wa, jnp.concatenate([ba_f, ba_b])[None, :]


def _trunk_layer(x2d, B, L, mod, row_of_tile, lw, consts, ctx, caches, final, ln_final):
    tm = 1024
    tm_out = 512
    u = _in_projection(x2d, lw['ln_mix'], mod, lw['w_in'], lw['w_in_tail'], lw['layer'], row_of_tile(tm), tm)
    u3 = u.reshape(B, L, U_W)
    filt_consts, main_consts = consts['hyena']
    filt = _hyena_filters(L, filt_consts, lw['hy_w1'], lw['hy_b1'], lw['hy_freq'], lw['hy_w2'],
                          lw['hy_b2'], lw['hy_w3'])
    y_hy = _hyena_mixer(u3, lw['hy_conv_w'], lw['hy_conv_b'], lw['hy_skip'], filt, main_consts, L)
    if ctx is None:
        y_gla, s_f, s_b = _gla_mixer(u3, lw['gla_wa'], lw['gla_ba'], lw['gla_norm'], None, lw['layer'], L, caches[2:])
        y_mla, ckv, krope = _mla_mixer(u3, lw['mla_q_norm'], lw['mla_w_uq'], lw['mla_kv_norm'],
                                       lw['mla_w_ukv'], None, lw['layer'], L, caches[:2])
        extras = (ckv, krope, s_f, s_b)
    else:
        cache_ckv, cache_krope, s0f, s0b = ctx
        y_gla, _, _ = _gla_mixer(u3, lw['gla_wa'], lw['gla_ba'], lw['gla_norm'], (s0f, s0b), lw['layer'], L, None)
        cos, sin = consts['rope']
        y_mla, _, _ = _mla_mixer(u3, lw['mla_q_norm'], lw['mla_w_uq'], lw['mla_kv_norm'], lw['mla_w_ukv'],
                                 (cos, sin, cache_ckv, cache_krope), lw['layer'], L, None)
        extras = None
    rows = B * L
    x2d = _out_projection(x2d, mod, y_hy.reshape(rows, -1), y_gla.reshape(rows, -1), y_mla.reshape(rows, -1),
                          lw['w_out'], lw['layer'], row_of_tile(tm_out), tm_out)
    x2d = _ffn(x2d, lw['ln_ffn'], mod, lw['w_ffn_in'], lw['w_ffn_out'], lw['layer'], ln_final, row_of_tile(tm_out),
               tm_out, final)
    return x2d, extras


def kernel(x_prompt, x_sample, cache_mla_ckv, cache_mla_krope, state_gla_fwd, state_gla_bwd, c, c_ctx, w_mod, b_mod, ln_mix, w_in, hy_conv_w, hy_conv_b, hy_filt_w1, hy_filt_b1, hy_filt_freq, hy_filt_w2, hy_filt_b2, hy_filt_w3, hy_skip, gla_wa_f, gla_ba_f, gla_wa_b, gla_ba_b, gla_norm, mla_q_norm, mla_w_uq, mla_kv_norm, mla_w_ukv, w_out, ln_ffn, w_ffn_in, w_ffn_out, ln_final):
    Bc, Lc, _ = x_prompt.shape
    Bl, Ll, _ = x_sample.shape
    assert 1 + Bl <= MOD_ROWS

    c_all = jnp.concatenate([c_ctx[None, :], c, jnp.zeros((MOD_ROWS - 1 - Bl, D_MODEL), F32)], axis=0)
    mod_all = _modulation(c_all, w_mod, b_mod).reshape(DEPTH, MOD_ROWS, 6, 1, D_MODEL)

    consts_ctx = {'hyena': _hyena_consts(Lc)}
    consts_lat = {'hyena': _hyena_consts(Ll), 'rope': _rope_tables(Ll)}
    ctx_rows = lambda tm: (lambda i: 0)
    lat_rows = lambda tm: (lambda i: 1 + (i * tm) // Ll)
    lnf = ln_final[None, :]

    x_ctx = x_prompt.reshape(Bc * Lc, D_MODEL)
    x_lat = x_sample.reshape(Bl * Ll, D_MODEL)
    w_in_b = w_in.astype(BF16)
    w_in_tail = _prep_w_in_tail(w_in)
    w_out_b = w_out.astype(BF16)
    w_ffn_in_b = w_ffn_in.astype(BF16)
    w_ffn_out_b = w_ffn_out.astype(BF16)
    caches = (None, None, None, None)
    for l in range(DEPTH):
        wa, ba = _prep_gla_decay(gla_wa_f[l], gla_ba_f[l], gla_wa_b[l], gla_ba_b[l])
        lw = {
            'layer': l, 'ln_mix': ln_mix[l][None, :], 'w_in': w_in_b, 'w_in_tail': w_in_tail,
            'hy_conv_w': hy_conv_w[l], 'hy_conv_b': hy_conv_b[l][None, :],
            'hy_w1': jnp.pad(hy_filt_w1[l], ((0, 64 - HY_EMB), (0, 0))), 'hy_b1': hy_filt_b1[l][None, :],
            'hy_freq': hy_filt_freq[l], 'hy_w2': hy_filt_w2[l], 'hy_b2': hy_filt_b2[l][None, :],
            'hy_w3': hy_filt_w3[l], 'hy_skip': hy_skip[l],
            'gla_wa': wa, 'gla_ba': ba, 'gla_norm': gla_norm[l][None, :],
            'mla_q_norm': mla_q_norm[l][None, :], 'mla_w_uq': _prep_w_uq(mla_w_uq[l]),
            'mla_kv_norm': mla_kv_norm[l][None, :], 'mla_w_ukv': mla_w_ukv[l].astype(BF16),
            'w_out': w_out_b, 'ln_ffn': ln_ffn[l][None, :],
            'w_ffn_in': w_ffn_in_b, 'w_ffn_out': w_ffn_out_b,
        }
        final = l == DEPTH - 1
        mod = mod_all[l]
        x_ctx, caches = _trunk_layer(x_ctx, Bc, Lc, mod, ctx_rows, lw, consts_ctx, None, caches, final, lnf)
        ctx = (cache_mla_ckv, cache_mla_krope, state_gla_fwd, state_gla_bwd)
        x_lat, _ = _trunk_layer(x_lat, Bl, Ll, mod, lat_rows, lw, consts_lat, ctx, None, final, lnf)
    return (x_ctx.reshape(Bc, Lc, D_MODEL), x_lat.reshape(Bl, Ll, D_MODEL)) + tuple(caches)
```

```python
import functools
import math

import numpy as np
import jax
import jax.numpy as jnp
from jax import lax
from jax.experimental import pallas as pl
from jax.experimental.pallas import tpu as pltpu

F32 = jnp.float32
BF16 = jnp.bfloat16

D_MODEL = 2048
DEPTH = 2
GRID_W = 64
EPS = 1e-6
HY_CH = 512
HY_ORDER = 2
HY_BANDS = 16
HY_EMB = 1 + 2 * HY_BANDS
HY_FF = 64
HY_FAST_DECAY = 0.3
HY_SLOW_DECAY = 1.5
HY_TARGET = 1e-2
HY_IN = 3 * HY_CH
GLA_HEADS = 4
GLA_DK = 64
GLA_DV = 128
GLA_LOWRANK = 16
GLA_TAU = 16.0
GLA_CHUNK = 64
GLA_QK = GLA_HEADS * GLA_DK
GLA_VW = GLA_HEADS * GLA_DV
MLA_HEADS = 8
MLA_Q_RANK = 384
MLA_KV_RANK = 256
MLA_NOPE = 128
MLA_ROPE = 64
MLA_V = 128
ROPE_THETA = 10000.0
D_FF = -(-8 * D_MODEL // (3 * 256)) * 256

U_HY = 0
U_GLA = 1536
U_MLA = 3072
U_GLA_A = 3712
U_GLA_A_LANE = 64
U_W = 3840
MOD_ROWS = 16

VMEM_LIMIT_V7X = 48 * 1024 * 1024


def _cparams(sem):
    return pltpu.CompilerParams(dimension_semantics=sem, vmem_limit_bytes=VMEM_LIMIT_V7X)


def _dot(a, b):
    return jnp.dot(a, b, preferred_element_type=F32)


def _dot_nt(a, b):
    return lax.dot_general(a, b, (((1,), (1,)), ((), ())), preferred_element_type=F32)


def _split(x):
    hi = x.astype(BF16)
    lo = (x - hi.astype(F32)).astype(BF16)
    return hi, lo


def _dot3(a, b):
    a_hi, a_lo = _split(a)
    b_hi, b_lo = _split(b)
    return _dot(a_hi, b_hi) + (_dot(a_lo, b_hi) + _dot(a_hi, b_lo))


def _rms(x, g):
    ms = jnp.mean(x * x, axis=-1, keepdims=True)
    return x * lax.rsqrt(ms + EPS) * g


ADALN_ROWS = 16


def _adaln_group(x_ref, gain, shift, h_ref, r):
    rows = pl.ds(pl.multiple_of(r, ADALN_ROWS), ADALN_ROWS)
    x = x_ref[rows, :]
    ms = jnp.mean(x * x, axis=-1, keepdims=True)
    h_ref[rows, :] = (x * lax.rsqrt(ms + EPS) * gain + shift).astype(BF16)


def _adaln_all(x_ref, gain, shift, h_ref):
    def body(i, carry):
        _adaln_group(x_ref, gain, shift, h_ref, i * ADALN_ROWS)
        return carry

    lax.fori_loop(0, x_ref.shape[0] // ADALN_ROWS, body, 0, unroll=4)


def _adaln_slice(x_ref, gain, shift, h_ref, base, nrows):
    for t in range(nrows // ADALN_ROWS):
        _adaln_group(x_ref, gain, shift, h_ref, base + t * ADALN_ROWS)


def _mod_kernel(c_ref, w_ref, b_ref, o_ref):
    c = c_ref[...]
    s = (c * jax.nn.sigmoid(c)).astype(BF16)
    o_ref[...] = _dot(s, w_ref[...].astype(BF16)) + b_ref[...]


def _modulation(c_all, w_mod, b_mod):
    tn = 1024
    n6 = 6 * D_MODEL
    return pl.pallas_call(
        _mod_kernel,
        grid=(DEPTH, n6 // tn),
        in_specs=[pl.BlockSpec((MOD_ROWS, D_MODEL), lambda l, j: (0, 0)),
                  pl.BlockSpec((None, D_MODEL, tn), lambda l, j: (l, 0, j)),
                  pl.BlockSpec((None, 1, tn), lambda l, j: (l, 0, j))],
        out_specs=pl.BlockSpec((None, MOD_ROWS, tn), lambda l, j: (l, 0, j)),
        out_shape=jax.ShapeDtypeStruct((DEPTH, MOD_ROWS, n6), F32),
        compiler_params=_cparams(("arbitrary", "arbitrary")),
        name="modulation",
    )(c_all, w_mod, b_mod.reshape(DEPTH, 1, n6))


def _mod_spec(chunk, row_of_tile, tile_of_step=lambda i, *_: i):
    return pl.BlockSpec((None, None, 1, D_MODEL), lambda *g: (row_of_tile(tile_of_step(*g)), chunk, 0, 0))


def _lookahead(n_tiles):
    return lambda i, j: jnp.minimum(i + jnp.minimum(j, 1), n_tiles - 1)


IN_TN = 768
IN_MAIN = 3072
IN_PRE_ROWS = 256


def _inproj_kernel(x_ref, ln_ref, sh_ref, sc_ref, w_ref, wt_ref, o_ref, ha_ref, hb_ref):
    i = pl.program_id(0)
    j = pl.program_id(1)
    gain = ln_ref[...] * (1.0 + sc_ref[...])
    shift = sh_ref[...]

    @pl.when((i == 0) & (j == 0))
    def _():
        _adaln_all(x_ref, gain, shift, ha_ref)

    base = jnp.clip(j - 1, 0, 3) * IN_PRE_ROWS

    def step(cur_ref, nxt_ref):
        @pl.when(j < IN_MAIN // IN_TN)
        def _():
            _adaln_slice(x_ref, gain, shift, nxt_ref, base, IN_PRE_ROWS)
            o_ref[...] = _dot(cur_ref[...], w_ref[...])

        @pl.when(j >= IN_MAIN // IN_TN)
        def _():
            _adaln_slice(x_ref, gain, shift, nxt_ref, base, IN_PRE_ROWS)
            o_ref[...] = _dot(cur_ref[...], wt_ref[...])

    @pl.when(i % 2 == 0)
    def _():
        step(ha_ref, hb_ref)

    @pl.when(i % 2 == 1)
    def _():
        step(hb_ref, ha_ref)


def _in_projection(x2d, ln, mod, w_in_b, w_tail, layer, row_of_tile, tm):
    rows = x2d.shape[0]
    n_main = IN_MAIN // IN_TN
    assert (U_W // IN_TN - 1) * IN_PRE_ROWS >= tm and U_W - IN_MAIN == IN_TN
    ahead = _lookahead(rows // tm)
    return pl.pallas_call(
        _inproj_kernel,
        grid=(rows // tm, U_W // IN_TN),
        in_specs=[pl.BlockSpec((tm, D_MODEL), lambda i, j: (ahead(i, j), 0)),
                  pl.BlockSpec((1, D_MODEL), lambda i, j: (0, 0)),
                  _mod_spec(0, row_of_tile, ahead),
                  _mod_spec(1, row_of_tile, ahead),
                  pl.BlockSpec((None, D_MODEL, IN_TN), lambda i, j: (layer, 0, jnp.minimum(j, n_main - 1))),
                  pl.BlockSpec((None, D_MODEL, U_W - IN_MAIN), lambda i, j: (layer, 0, 0))],
        out_specs=pl.BlockSpec((tm, IN_TN), lambda i, j: (i, j)),
        out_shape=jax.ShapeDtypeStruct((rows, U_W), F32),
        scratch_shapes=[pltpu.VMEM((tm, D_MODEL), BF16), pltpu.VMEM((tm, D_MODEL), BF16)],
        compiler_params=_cparams(("arbitrary", "arbitrary")),
        name="in_projection",
    )(x2d, ln, mod, mod, w_in_b, w_tail)


def _filter_kernel(z_ref, w1_ref, b1_ref, fr_ref, w2_ref, b2_ref, w3f_ref, w3b_ref, dec_ref,
                   chi_ref, clo_ref, shi_ref, slo_ref, kr_ref, ki_ref, kn_ref, *, L):
    h = jnp.sin(fr_ref[0:1, :] * (_dot3(z_ref[...], w1_ref[...]) + b1_ref[...]))
    h = jnp.sin(fr_ref[1:2, :] * (_dot3(h, w2_ref[...]) + b2_ref[...]))
    dec = dec_ref[...]
    row = lax.broadcasted_iota(jnp.int32, (L, 1), 0)
    kf = _dot3(h, w3f_ref[...]) * dec
    kb = jnp.where(row == 0, 0.0, _dot3(h, w3b_ref[...]) * dec)
    p_hi, p_lo = _split(kf + kb)
    m_hi, m_lo = _split(kf - kb)
    chi, clo = chi_ref[...], clo_ref[...]
    shi, slo = shi_ref[...], slo_ref[...]
    kc = _dot(chi, p_hi) + (_dot(clo, p_hi) + _dot(chi, p_lo))
    ks = _dot(shi, m_hi) + (_dot(slo, m_hi) + _dot(shi, m_lo))
    sign = jnp.where((row & 1) == 0, 1.0, -1.0)
    kn = jnp.sum((kf + kb) * sign, axis=0, keepdims=True)
    kr_ref[...] = kc * jnp.where(row == 0, 0.5 / L, 1.0 / L)
    ki_ref[...] = ks * (-1.0 / L)
    kn_ref[...] = kn * (0.5 / L)


def _hyena_filters(L, consts, w1p, b1, freq, w2, b2, w3):
    zpos, decay, c_hi, c_lo, s_hi, s_lo = consts
    full = lambda shape: pl.BlockSpec(shape, lambda o: (0,) * len(shape))
    return pl.pallas_call(
        functools.partial(_filter_kernel, L=L),
        grid=(HY_ORDER,),
        in_specs=[full((L, 64)), full((64, HY_FF)), full((1, HY_FF)), full((2, HY_FF)),
                  full((HY_FF, HY_FF)), full((1, HY_FF)),
                  pl.BlockSpec((HY_FF, HY_CH), lambda o: (0, o)),
                  pl.BlockSpec((HY_FF, HY_CH), lambda o: (0, HY_ORDER + o)),
                  full((L, HY_CH)), full((L, L)), full((L, L)), full((L, L)), full((L, L))],
        out_specs=[pl.BlockSpec((None, L, HY_CH), lambda o: (o, 0, 0)),
                   pl.BlockSpec((None, L, HY_CH), lambda o: (o, 0, 0)),
                   pl.BlockSpec((None, 1, HY_CH), lambda o: (o, 0, 0))],
        out_shape=[jax.ShapeDtypeStruct((HY_ORDER, L, HY_CH), F32),
                   jax.ShapeDtypeStruct((HY_ORDER, L, HY_CH), F32),
                   jax.ShapeDtypeStruct((HY_ORDER, 1, HY_CH), F32)],
        compiler_params=_cparams(("arbitrary",)),
        name="hyena_filters",
    )(zpos, w1p, b1, freq, w2, b2, w3, w3, decay, c_hi, c_lo, s_hi, s_lo)


HY_CG = 256
HY_STEP_ROWS = 1024


def _hyena_kernel(v_ref, x1_ref, x2_ref, wv_ref, w1_ref, w2_ref, bv_ref, b1_ref, b2_ref, skip_ref,
                  kr_ref, ki_ref, kn_ref, c_ref, sp_ref, spt_ref, o_ref, *, L):
    row = lax.broadcasted_iota(jnp.int32, (L, 1), 0)
    first = row == 0
    last = row == L - 1
    chains = [(b, slice(g * HY_CG, (g + 1) * HY_CG)) for b in range(v_ref.shape[0]) for g in range(HY_CH // HY_CG)]

    def short_conv(u_ref, w_ref, b_ref, b, cols):
        u = u_ref[b, :, cols]
        prev = jnp.where(first, 0.0, pltpu.roll(u, 1, axis=0))
        nxt = jnp.where(last, 0.0, pltpu.roll(u, L - 1, axis=0))
        return prev * w_ref[0:1, cols] + u * w_ref[1:2, cols] + nxt * w_ref[2:3, cols] + b_ref[:, cols]

    gate_refs = ((x1_ref, w1_ref, b1_ref), (x2_ref, w2_ref, b2_ref))
    z = [short_conv(v_ref, wv_ref, bv_ref, b, cols) for b, cols in chains]
    for o in range(HY_ORDER):
        zb = [zz.astype(BF16) for zz in z]
        a = [_dot(c_ref[...], x) for x in zb]
        s = [_dot(sp_ref[...], x) for x in zb]
        conv = []
        for n, (b, cols) in enumerate(chains):
            kr = kr_ref[o, :, cols]
            ki = ki_ref[o, :, cols]
            qt = a[n] * kr + s[n] * ki
            qb = s[n] * jnp.where(first, kn_ref[o, :, cols], kr) - a[n] * ki
            conv.append(_dot(c_ref[...], qt.astype(BF16)) + _dot(spt_ref[...], qb.astype(BF16)))
        z = [short_conv(*gate_refs[o], b, cols) * (conv[n] + z[n] * skip_ref[o:o + 1, cols])
             for n, (b, cols) in enumerate(chains)]
    for n, (b, cols) in enumerate(chains):
        o_ref[b, :, cols] = z[n].astype(BF16)


def _hyena_mixer(u3, conv_w, conv_b, skip, filt, dft, L):
    B = u3.shape[0]
    nb = max(1, HY_STEP_ROWS // L)
    kr, ki, kn = filt
    c_b, sp_b, spt_b = dft
    once = pl.Buffered(1)
    ublk = lambda part: pl.BlockSpec((nb, L, HY_CH), lambda b: (b, 0, part))
    wblk = lambda part: pl.BlockSpec((3, HY_CH), lambda b: (0, part))
    bblk = lambda part: pl.BlockSpec((1, HY_CH), lambda b: (0, part))
    fblk = lambda rows: pl.BlockSpec((HY_ORDER, rows, HY_CH), lambda b: (0, 0, 0), pipeline_mode=once)
    mat = pl.BlockSpec((L, L), lambda b: (0, 0), pipeline_mode=once)
    return pl.pallas_call(
        functools.partial(_hyena_kernel, L=L),
        grid=(B // nb,),
        in_specs=[ublk(0), ublk(1), ublk(2), wblk(0), wblk(1), wblk(2), bblk(0), bblk(1), bblk(2),
                  pl.BlockSpec((HY_ORDER, HY_CH), lambda b: (0, 0)),
                  fblk(L), fblk(L), fblk(1), mat, mat, mat],
        out_specs=pl.BlockSpec((nb, L, HY_CH), lambda b: (b, 0, 0)),
        out_shape=jax.ShapeDtypeStruct((B, L, HY_CH), BF16),
        compiler_params=_cparams(("arbitrary",)),
        name="hyena_mixer",
    )(u3, u3, u3, conv_w, conv_w, conv_w, conv_b, conv_b, conv_b, skip, kr, ki, kn, c_b, sp_b, spt_b)


GLA_PAIR = 2 * GLA_CHUNK
GLA_STATES = 2 * GLA_HEADS
GLA_SCAN = 256


def _gla_kernel(*refs, L, has_state, emit_state, n_alias):
    it = iter(refs)
    qkvg_ref, a_ref, wa_ref, ba_ref, ng_ref, scan_ref = (next(it) for _ in range(6))
    st0_refs = (next(it), next(it)) if has_state else None
    for _ in range(n_alias):
        next(it)
    y_ref = next(it)
    stout_refs = (next(it), next(it)) if emit_state else None
    qd_s, ki_s, ke_s, qi_s, ku_s, et_s, vb_s, vt_s, o_s, st_s = (next(it) for _ in range(10))

    npair = L // GLA_PAIR
    pre = _dot3(a_ref[...], wa_ref[...]) + ba_ref[...]
    la = (jnp.minimum(pre, 0.0) - jnp.log(1.0 + jnp.exp(-jnp.abs(pre)))) * (1.0 / GLA_TAU)

    scan = scan_ref[...]
    pfx, tot = [], []
    for r in range(0, L, GLA_SCAN):
        x = la[r:r + GLA_SCAN]
        hi = x.astype(BF16)
        r1 = x - hi.astype(F32)
        mid = r1.astype(BF16)
        lo = (r1 - mid.astype(F32)).astype(BF16)
        res = _dot(scan, hi) + (_dot(scan, mid) + _dot(scan, lo))
        pfx.append(res[:GLA_SCAN])
        tot.append(res[GLA_SCAN:])
    pfx = jnp.concatenate(pfx, axis=0) if len(pfx) > 1 else pfx[0]
    tot = jnp.concatenate(tot, axis=0) if len(tot) > 1 else tot[0]

    upper = (lax.broadcasted_iota(jnp.int32, (L, 1), 0) & GLA_CHUNK) != 0
    q = qkvg_ref[:, 0:GLA_QK] * (GLA_DK ** -0.5)
    k = qkvg_ref[:, GLA_QK:2 * GLA_QK]
    for dirn in range(2):
        cols = slice(dirn * GLA_QK, (dirn + 1) * GLA_QK)
        if dirn == 0:
            cum = pfx[:, cols]
            rest = tot[:, cols] - cum
            first = jnp.logical_not(upper)
        else:
            rest = pfx[:, cols] - la[:, cols]
            cum = tot[:, cols] - rest
            first = upper
        et = jnp.exp(tot[:, cols])
        et_other = jnp.where(upper, pltpu.roll(et, GLA_CHUNK, axis=0), pltpu.roll(et, L - GLA_CHUNK, axis=0))
        qd = q * jnp.exp(cum)
        ke = k * jnp.exp(rest)
        et_s[dirn] = et
        qd_s[dirn] = qd.astype(BF16)
        ki_s[dirn] = (k * jnp.exp(-cum)).astype(BF16)
        ke_s[dirn] = ke.astype(BF16)
        qi_s[dirn] = jnp.where(first, qd, qd * et_other).astype(BF16)
        ku_s[dirn] = jnp.where(first, ke * et_other, ke).astype(BF16)
    vb_s[...] = qkvg_ref[:, 2 * GLA_QK:2 * GLA_QK + GLA_VW].astype(BF16)

    def transpose_v(n2, carry):
        r0 = pl.multiple_of(n2 * GLA_PAIR, GLA_PAIR)
        for h in range(GLA_HEADS):
            vp = qkvg_ref[pl.ds(r0, GLA_PAIR), 2 * GLA_QK + h * GLA_DV:2 * GLA_QK + (h + 1) * GLA_DV]
            vt_s[n2, h * GLA_DV:(h + 1) * GLA_DV, :] = vp.T.astype(BF16)
        return carry

    lax.fori_loop(0, npair, transpose_v, 0)

    if has_state:
        zero = jnp.zeros((GLA_DK, GLA_DV), F32)
        for dirn in range(2):
            for h in range(GLA_HEADS):
                s0 = st0_refs[dirn][h]
                both = jnp.concatenate([s0, zero] if h % 2 == 0 else [zero, s0], axis=0)
                idx = dirn * GLA_HEADS + h
                st_s[idx * GLA_DV:(idx + 1) * GLA_DV, :] = both.T
    else:
        st_s[...] = jnp.zeros_like(st_s)

    lane = lax.broadcasted_iota(jnp.int32, (1, 2 * GLA_DK), 1)
    head_lanes = (lane < GLA_DK, lane >= GLA_DK)
    ri = lax.broadcasted_iota(jnp.int32, (GLA_PAIR, GLA_PAIR), 0)
    ci = lax.broadcasted_iota(jnp.int32, (GLA_PAIR, GLA_PAIR), 1)
    same_chunk = ((ri ^ ci) & GLA_CHUNK) == 0
    keep = (ci <= ri, ci >= ri)

    def pair_step(n, carry):
        for dirn in range(2):
            n2 = n if dirn == 0 else npair - 1 - n
            r0 = pl.multiple_of(n2 * GLA_PAIR, GLA_PAIR)
            rows = pl.ds(r0, GLA_PAIR)
            for p in range(GLA_HEADS // 2):
                lanes = slice(p * 2 * GLA_DK, (p + 1) * 2 * GLA_DK)
                qd = qd_s[dirn, rows, lanes]
                qi = qi_s[dirn, rows, lanes]
                zeros = jnp.zeros_like(qd)
                dec = (et_s[dirn, pl.ds(r0, 8), lanes][0:1, :]
                       * et_s[dirn, pl.ds(r0 + GLA_CHUNK, 8), lanes][0:1, :])
                q2 = jnp.concatenate([jnp.where(head_lanes[0], qd, zeros), jnp.where(head_lanes[1], qd, zeros)],
                                     axis=0)
                att_d = _dot_nt(q2, ki_s[dirn, rows, lanes])
                att_o = _dot_nt(q2, ke_s[dirn, rows, lanes])
                s0 = (dirn * GLA_HEADS + 2 * p) * GLA_DV
                st2 = st_s[s0:s0 + 2 * GLA_DV, :]
                vt2 = vt_s[n2, 2 * p * GLA_DV:(2 * p + 2) * GLA_DV, :]
                st_s[s0:s0 + 2 * GLA_DV, :] = st2 * dec + _dot(vt2, ku_s[dirn, rows, lanes])
                for hh in range(2):
                    h = 2 * p + hh
                    hr = slice(hh * GLA_PAIR, (hh + 1) * GLA_PAIR)
                    att = jnp.where(keep[dirn], jnp.where(same_chunk, att_d[hr], att_o[hr]), 0.0)
                    qm = jnp.where(head_lanes[hh], qi, zeros)
                    o = (_dot(att.astype(BF16), vb_s[rows, h * GLA_DV:(h + 1) * GLA_DV])
                         + _dot_nt(qm, st2[hh * GLA_DV:(hh + 1) * GLA_DV].astype(BF16)))
                    o_s[dirn, rows, h * GLA_DV:(h + 1) * GLA_DV] = o
        return carry

    lax.fori_loop(0, npair, pair_step, 0, unroll=2)

    for h in range(GLA_HEADS):
        cols = slice(h * GLA_DV, (h + 1) * GLA_DV)
        o = _rms(o_s[0, :, cols] + o_s[1, :, cols], ng_ref[...])
        g = qkvg_ref[:, 2 * GLA_QK + GLA_VW + h * GLA_DV:2 * GLA_QK + GLA_VW + (h + 1) * GLA_DV]
        y_ref[:, cols] = (o * (g * jax.nn.sigmoid(g))).astype(BF16)
    if emit_state:
        for dirn in range(2):
            for h in range(GLA_HEADS):
                idx = dirn * GLA_HEADS + h
                half = (h % 2) * GLA_DK
                stout_refs[dirn][h] = st_s[idx * GLA_DV:(idx + 1) * GLA_DV, :].T[half:half + GLA_DK, :]


def _gla_scan_matrix():
    i = np.arange(GLA_SCAN)
    same = (i[:, None] // GLA_CHUNK) == (i[None, :] // GLA_CHUNK)
    tri = same & (i[None, :] <= i[:, None])
    return jnp.asarray(np.concatenate([tri, same], axis=0).astype(np.float32)).astype(BF16)


def _gla_mixer(u3, wa_p, ba_p, norm_g, st0, layer, L, emit_state):
    B = u3.shape[0]
    has_state = st0 is not None
    head_state = (GLA_HEADS, GLA_DK, GLA_DV)
    in_specs = [pl.BlockSpec((None, L, 1536), lambda b: (b, 0, U_GLA // 1536)),
                pl.BlockSpec((None, L, 128), lambda b: (b, 0, U_GLA_A // 128)),
                pl.BlockSpec((128, 2 * GLA_QK), lambda b: (0, 0)),
                pl.BlockSpec((1, 2 * GLA_QK), lambda b: (0, 0)),
                pl.BlockSpec((1, GLA_DV), lambda b: (0, 0)),
                pl.BlockSpec((2 * GLA_SCAN, GLA_SCAN), lambda b: (0, 0))]
    args = [u3, u3, wa_p, ba_p, norm_g, _gla_scan_matrix()]
    if has_state:
        in_specs += [pl.BlockSpec((None, None) + head_state, lambda b: (b, layer, 0, 0, 0))] * 2
        args += list(st0)
    out_specs = [pl.BlockSpec((None, L, GLA_VW), lambda b: (b, 0, 0))]
    out_shape = [jax.ShapeDtypeStruct((B, L, GLA_VW), BF16)]
    aliases = {}
    if emit_state is not None:
        out_specs += [pl.BlockSpec((None, None) + head_state, lambda b: (b, layer, 0, 0, 0))] * 2
        out_shape += [jax.ShapeDtypeStruct((B, DEPTH) + head_state, F32)] * 2
        for k, prev in enumerate(emit_state):
            if prev is not None:
                aliases[len(args)] = 1 + k
                in_specs.append(pl.BlockSpec(memory_space=pl.ANY))
                args.append(prev)
    npair = L // GLA_PAIR
    qk_bf16 = pltpu.VMEM((2, L, GLA_QK), BF16)
    res = pl.pallas_call(
        functools.partial(_gla_kernel, L=L, has_state=has_state, emit_state=emit_state is not None,
                          n_alias=len(aliases)),
        grid=(B,),
        in_specs=in_specs,
        out_specs=out_specs,
        out_shape=out_shape,
        input_output_aliases=aliases,
        scratch_shapes=[qk_bf16,
                        qk_bf16,
                        qk_bf16,
                        qk_bf16,
                        qk_bf16,
                        pltpu.VMEM((2, L, GLA_QK), F32),
                        pltpu.VMEM((L, GLA_VW), BF16),
                        pltpu.VMEM((npair, GLA_VW, GLA_PAIR), BF16),
                        pltpu.VMEM((2, L, GLA_VW), F32),
                        pltpu.VMEM((GLA_STATES * GLA_DV, 2 * GLA_DK), F32)],
        compiler_params=_cparams(("arbitrary",)),
        name="gla_mixer",
    )(*args)
    return (res[0], res[1], res[2]) if emit_state is not None else (res[0], None, None)


MLA_HW = 256
MLA_TQ = 512
MLA_KCH = 256


def _rope(x, cos, sin):
    lane = lax.broadcasted_iota(jnp.int32, x.shape, 1)
    partner = jnp.where((lane & 16) == 0, pltpu.roll(x, 112, axis=1), pltpu.roll(x, 16, axis=1))
    return x * cos + partner * sin


def _mla_kernel(*refs, L, Lk, tq, rope, emit, n_alias):
    it = iter(refs)
    u_ref, qn_ref, wq_ref, kvn_ref, wkv_ref = (next(it) for _ in range(5))
    cos_ref, sin_ref, cckv_ref, ckr_ref = (next(it) for _ in range(4)) if rope else (None,) * 4
    for _ in range(n_alias):
        next(it)
    y_ref = next(it)
    ckv_out, kr_out = (next(it), next(it)) if emit else (None, None)
    kf_s, v_s = next(it), next(it)

    qi = pl.program_id(1)

    def project(keys, rope_part, r0):
        kv = _dot(keys, wkv_ref[...])
        for h in range(MLA_HEADS):
            kf_s[h, r0:r0 + MLA_KCH, 0:MLA_NOPE] = kv[:, h * 256:h * 256 + MLA_NOPE].astype(BF16)
            kf_s[h, r0:r0 + MLA_KCH, MLA_NOPE:MLA_HW] = rope_part
            v_s[h, r0:r0 + MLA_KCH, :] = kv[:, h * 256 + MLA_NOPE:(h + 1) * 256].astype(BF16)

    @pl.when(qi == 0)
    def _():
        for r0 in range(0, L, MLA_KCH):
            ckvn = _rms(u_ref[r0:r0 + MLA_KCH, MLA_Q_RANK:MLA_Q_RANK + MLA_KV_RANK], kvn_ref[...])
            kr = u_ref[r0:r0 + MLA_KCH, 640:768]
            kr = jnp.where(lax.broadcasted_iota(jnp.int32, kr.shape, 1) < MLA_ROPE, kr, 0.0)
            if emit:
                ckv_out[r0:r0 + MLA_KCH, :] = ckvn
                kr_out[r0:r0 + MLA_KCH, :] = kr[:, 0:MLA_ROPE]
            if rope:
                kr = _rope(kr, cos_ref[r0:r0 + MLA_KCH, :], sin_ref[r0:r0 + MLA_KCH, :])
            project(ckvn.astype(BF16), kr.astype(BF16), r0)
        if rope:
            for r0 in range(0, Lk - L, MLA_KCH):
                ckr = ckr_ref[r0:r0 + MLA_KCH, :]
                ckr = jnp.concatenate([ckr, jnp.zeros_like(ckr)], axis=1)
                project(cckv_ref[r0:r0 + MLA_KCH, :].astype(BF16), ckr.astype(BF16), L + r0)

    q0 = pl.multiple_of(qi * tq, tq)
    cqn = _rms(u_ref[pl.ds(q0, tq), 0:MLA_Q_RANK], qn_ref[...]).astype(BF16)
    q = _dot(cqn, wq_ref[...]) * ((MLA_NOPE + MLA_ROPE) ** -0.5)
    if rope:
        cos = cos_ref[pl.ds(q0, tq), :]
        sin = sin_ref[pl.ds(q0, tq), :]
    for h in range(MLA_HEADS):
        q_nope = q[:, h * MLA_HW:h * MLA_HW + MLA_NOPE]
        q_rope = q[:, h * MLA_HW + MLA_NOPE:(h + 1) * MLA_HW]
        if rope:
            q_rope = _rope(q_rope, cos, sin)
        qh = jnp.concatenate([q_nope, q_rope], axis=1).astype(BF16)
        s = _dot_nt(qh, kf_s[h])
        p = jnp.exp(s - jnp.max(s, axis=-1, keepdims=True))
        denom = jnp.sum(p, axis=-1, keepdims=True)
        o = _dot(p.astype(BF16), v_s[h])
        y_ref[:, h * MLA_V:(h + 1) * MLA_V] = (o / denom).astype(BF16)


def _mla_mixer(u3, q_norm, wq_p, kv_norm, wkv, rope_args, layer, L, emit):
    B = u3.shape[0]
    rope = rope_args is not None
    Lk = L + (rope_args[2].shape[2] if rope else 0)
    tq = min(MLA_TQ, L)
    const = lambda shape: pl.BlockSpec(shape, lambda b, i: (0,) * len(shape))
    in_specs = [pl.BlockSpec((None, L, 768), lambda b, i: (b, 0, U_MLA // 768)),
                const((1, MLA_Q_RANK)), const((MLA_Q_RANK, MLA_HEADS * MLA_HW)),
                const((1, MLA_KV_RANK)), const((MLA_KV_RANK, MLA_HEADS * 256))]
    args = [u3, q_norm, wq_p, kv_norm, wkv]
    if rope:
        cos, sin, cckv, ckr = rope_args
        in_specs += [const((L, 128)), const((L, 128)),
                     pl.BlockSpec((None, None, Lk - L, MLA_KV_RANK), lambda b, i: (b, layer, 0, 0)),
                     pl.BlockSpec((None, None, Lk - L, MLA_ROPE), lambda b, i: (b, layer, 0, 0))]
        args += [cos, sin, cckv, ckr]
    out_specs = [pl.BlockSpec((None, tq, MLA_HEADS * MLA_V), lambda b, i: (b, i, 0))]
    out_shape = [jax.ShapeDtypeStruct((B, L, MLA_HEADS * MLA_V), BF16)]
    aliases = {}
    if emit is not None:
        out_specs += [pl.BlockSpec((None, None, L, MLA_KV_RANK), lambda b, i: (b, layer, 0, 0)),
                      pl.BlockSpec((None, None, L, MLA_ROPE), lambda b, i: (b, layer, 0, 0))]
        out_shape += [jax.ShapeDtypeStruct((B, DEPTH, L, MLA_KV_RANK), F32),
                      jax.ShapeDtypeStruct((B, DEPTH, L, MLA_ROPE), F32)]
        for k, prev in enumerate(emit):
            if prev is not None:
                aliases[len(args)] = 1 + k
                in_specs.append(pl.BlockSpec(memory_space=pl.ANY))
                args.append(prev)
    res = pl.pallas_call(
        functools.partial(_mla_kernel, L=L, Lk=Lk, tq=tq, rope=rope, emit=emit is not None, n_alias=len(aliases)),
        grid=(B, L // tq),
        in_specs=in_specs,
        out_specs=out_specs,
        out_shape=out_shape,
        input_output_aliases=aliases,
        scratch_shapes=[pltpu.VMEM((MLA_HEADS, Lk, MLA_HW), BF16),
                        pltpu.VMEM((MLA_HEADS, Lk, MLA_V), BF16)],
        compiler_params=_cparams(("arbitrary", "arbitrary")),
        name="mla_mixer",
    )(*args)
    return res if emit is not None else (res[0], None, None)


def _outproj_kernel(x_ref, g_ref, yh_ref, yg_ref, ym_ref, wh_ref, wg_ref, wm_ref, o_ref):
    y = _dot(yh_ref[...], wh_ref[...]) + _dot(yg_ref[...], wg_ref[...]) + _dot(ym_ref[...], wm_ref[...])
    o_ref[...] = x_ref[...] + g_ref[...] * y


def _out_projection(x2d, mod, y_hy, y_gla, y_mla, w_out_b, layer, row_of_tile, tm):
    rows = x2d.shape[0]
    once = pl.Buffered(1)
    return pl.pallas_call(
        _outproj_kernel,
        grid=(rows // tm,),
        in_specs=[pl.BlockSpec((tm, D_MODEL), lambda i: (i, 0)),
                  _mod_spec(2, row_of_tile),
                  pl.BlockSpec((tm, HY_CH), lambda i: (i, 0)),
                  pl.BlockSpec((tm, GLA_VW), lambda i: (i, 0)),
                  pl.BlockSpec((tm, MLA_HEADS * MLA_V), lambda i: (i, 0)),
                  pl.BlockSpec((None, HY_CH, D_MODEL), lambda i: (layer, 0, 0), pipeline_mode=once),
                  pl.BlockSpec((None, GLA_VW, D_MODEL), lambda i: (layer, 1, 0), pipeline_mode=once),
                  pl.BlockSpec((None, MLA_HEADS * MLA_V, D_MODEL), lambda i: (layer, 1, 0), pipeline_mode=once)],
        out_specs=pl.BlockSpec((tm, D_MODEL), lambda i: (i, 0)),
        out_shape=jax.ShapeDtypeStruct((rows, D_MODEL), F32),
        compiler_params=_cparams(("arbitrary",)),
        name="out_projection",
    )(x2d, mod, y_hy, y_gla, y_mla, w_out_b, w_out_b, w_out_b)


FFN_TF = 512


def _ffn_kernel(x_ref, ln_ref, sh_ref, sc_ref, g_ref, wg_ref, wu_ref, wo_ref, lnf_ref, o_ref,
                h_ref, acc_ref, *, final):
    f = pl.program_id(1)

    @pl.when(f == 0)
    def _():
        _adaln_all(x_ref, ln_ref[...] * (1.0 + sc_ref[...]), sh_ref[...], h_ref)
        acc_ref[...] = jnp.zeros_like(acc_ref)

    h = h_ref[...]
    gate = _dot(h, wg_ref[...])
    up = _dot(h, wu_ref[...])
    act = (gate * jax.nn.sigmoid(gate) * up).astype(BF16)
    acc_ref[...] += _dot(act, wo_ref[...])

    @pl.when(f == pl.num_programs(1) - 1)
    def _():
        def body(i, carry):
            rows = pl.ds(pl.multiple_of(i * ADALN_ROWS, ADALN_ROWS), ADALN_ROWS)
            x = x_ref[rows, :] + g_ref[...] * acc_ref[rows, :]
            o_ref[rows, :] = _rms(x, lnf_ref[...]) if final else x
            return carry

        lax.fori_loop(0, x_ref.shape[0] // ADALN_ROWS, body, 0, unroll=4)


def _ffn(x2d, ln, mod, w_in_b, w_out_b, layer, ln_final, row_of_tile, tm, final):
    rows = x2d.shape[0]
    nf = D_FF // FFN_TF
    return pl.pallas_call(
        functools.partial(_ffn_kernel, final=final),
        grid=(rows // tm, nf),
        in_specs=[pl.BlockSpec((tm, D_MODEL), lambda i, f: (i, 0)),
                  pl.BlockSpec((1, D_MODEL), lambda i, f: (0, 0)),
                  _mod_spec(3, row_of_tile),
                  _mod_spec(4, row_of_tile),
                  _mod_spec(5, row_of_tile),
                  pl.BlockSpec((None, D_MODEL, FFN_TF), lambda i, f: (layer, 0, f)),
                  pl.BlockSpec((None, D_MODEL, FFN_TF), lambda i, f: (layer, 0, nf + f)),
                  pl.BlockSpec((None, FFN_TF, D_MODEL), lambda i, f: (layer, f, 0)),
                  pl.BlockSpec((1, D_MODEL), lambda i, f: (0, 0))],
        out_specs=pl.BlockSpec((tm, D_MODEL), lambda i, f: (i, 0)),
        out_shape=jax.ShapeDtypeStruct((rows, D_MODEL), F32),
        scratch_shapes=[pltpu.VMEM((tm, D_MODEL), BF16), pltpu.VMEM((tm, D_MODEL), F32)],
        compiler_params=_cparams(("arbitrary", "arbitrary")),
        name="ffn",
    )(x2d, ln, mod, mod, mod, w_in_b, w_in_b, w_out_b, ln_final)


def _dft_tables(L):
    k = np.arange(L)
    ang = np.pi * ((k[:, None] * k[None, :]) % (2 * L)) / L
    c = np.cos(ang)
    s = np.sin(ang)
    sp = s.copy()
    sp[0, :] = 1.0 - 2.0 * (k % 2)
    as32 = lambda a: jnp.asarray(a.astype(np.float32))
    return as32(c), as32(s), as32(sp), as32(sp.T.copy())


def _hyena_consts(L):
    c, s, sp, spt = _dft_tables(L)
    c_hi, c_lo = _split(c)
    s_hi, s_lo = _split(s)
    f32 = np.float32
    t = np.linspace(0.0, 1.0, L, dtype=f32)[:, None]
    w = (f32(2.0 * math.pi) * np.arange(L, dtype=f32)[:, None] / f32(L)).astype(f32)
    f = np.linspace(1e-4, HY_BANDS - 1, HY_BANDS, dtype=f32)
    zpos = np.concatenate([t, np.cos(f * w), -np.sin(f * w)], axis=-1).astype(f32)
    zpos = np.pad(zpos, ((0, 0), (0, 64 - HY_EMB)))
    min_decay = math.log(HY_TARGET) / HY_SLOW_DECAY
    max_decay = math.log(HY_TARGET) / HY_FAST_DECAY
    delta = np.abs(np.linspace(min_decay, max_decay, HY_CH, dtype=f32))
    decay = np.exp(-t * delta).astype(f32)
    filt_consts = (jnp.asarray(zpos), jnp.asarray(decay), c_hi, c_lo, s_hi, s_lo)
    main_consts = (c_hi, sp.astype(BF16), spt.astype(BF16))
    return filt_consts, main_consts


def _rope_tables(L):
    t = jnp.arange(L)
    half = MLA_ROPE // 2
    inv = ROPE_THETA ** (-jnp.arange(0, half, 2, dtype=F32) / half)
    ang_r = (t // GRID_W).astype(F32)[:, None] * inv
    ang_c = (t % GRID_W).astype(F32)[:, None] * inv
    cr, sr, cc, sc = jnp.cos(ang_r), jnp.sin(ang_r), jnp.cos(ang_c), jnp.sin(ang_c)
    cos = jnp.concatenate([cr, cr, cc, cc, jnp.ones((L, 128 - MLA_ROPE), F32)], axis=1)
    sin = jnp.concatenate([-sr, sr, -sc, sc, jnp.zeros((L, 128 - MLA_ROPE), F32)], axis=1)
    return cos, sin


def _prep_w_in_tail(w):
    mla0 = IN_MAIN + 2 * GLA_LOWRANK
    pad = jnp.zeros((DEPTH, D_MODEL, IN_TN - (w.shape[2] - IN_MAIN)), w.dtype)
    return jnp.concatenate([w[:, :, mla0:], w[:, :, IN_MAIN:mla0], pad], axis=2).astype(BF16)


def _prep_w_uq(w):
    w = w.reshape(MLA_Q_RANK, MLA_HEADS, MLA_NOPE + MLA_ROPE)
    w = jnp.pad(w, ((0, 0), (0, 0), (0, MLA_HW - MLA_NOPE - MLA_ROPE)))
    return w.reshape(MLA_Q_RANK, MLA_HEADS * MLA_HW).astype(BF16)


def _prep_gla_decay(wa_f, ba_f, wa_b, ba_b):
    wa = jnp.zeros((128, 2 * GLA_QK), F32)
    wa = wa.at[U_GLA_A_LANE:U_GLA_A_LANE + GLA_LOWRANK, 0:GLA_QK].set(wa_f)
    wa = wa.at[U_GLA_A_LANE + GLA_LOWRANK:U_GLA_A_LANE + 2 * GLA_LOWRANK, GLA_QK:].set(wa_b)
    return wa, jnp.concatenate([ba_f, ba_b])[None, :]


def _trunk_layer(x2d, B, L, mod, row_of_tile, lw, consts, ctx, caches, final, ln_final):
    tm = 1024
    tm_out = 512
    u = _in_projection(x2d, lw['ln_mix'], mod, lw['w_in'], lw['w_in_tail'], lw['layer'], row_of_tile(tm), tm)
    u3 = u.reshape(B, L, U_W)
    filt_consts, main_consts = consts['hyena']
    filt = _hyena_filters(L, filt_consts, lw['hy_w1'], lw['hy_b1'], lw['hy_freq'], lw['hy_w2'],
                          lw['hy_b2'], lw['hy_w3'])
    y_hy = _hyena_mixer(u3, lw['hy_conv_w'], lw['hy_conv_b'], lw['hy_skip'], filt, main_consts, L)
    if ctx is None:
        y_gla, s_f, s_b = _gla_mixer(u3, lw['gla_wa'], lw['gla_ba'], lw['gla_norm'], None, lw['layer'], L, caches[2:])
        y_mla, ckv, krope = _mla_mixer(u3, lw['mla_q_norm'], lw['mla_w_uq'], lw['mla_kv_norm'],
                                       lw['mla_w_ukv'], None, lw['layer'], L, caches[:2])
        extras = (ckv, krope, s_f, s_b)
    else:
        cache_ckv, cache_krope, s0f, s0b = ctx
        y_gla, _, _ = _gla_mixer(u3, lw['gla_wa'], lw['gla_ba'], lw['gla_norm'], (s0f, s0b), lw['layer'], L, None)
        cos, sin = consts['rope']
        y_mla, _, _ = _mla_mixer(u3, lw['mla_q_norm'], lw['mla_w_uq'], lw['mla_kv_norm'], lw['mla_w_ukv'],
                                 (cos, sin, cache_ckv, cache_krope), lw['layer'], L, None)
        extras = None
    rows = B * L
    x2d = _out_projection(x2d, mod, y_hy.reshape(rows, -1), y_gla.reshape(rows, -1), y_mla.reshape(rows, -1),
                          lw['w_out'], lw['layer'], row_of_tile(tm_out), tm_out)
    x2d = _ffn(x2d, lw['ln_ffn'], mod, lw['w_ffn_in'], lw['w_ffn_out'], lw['layer'], ln_final, row_of_tile(tm_out),
               tm_out, final)
    return x2d, extras


def kernel(x_prompt, x_sample, cache_mla_ckv, cache_mla_krope, state_gla_fwd, state_gla_bwd, c, c_ctx, w_mod, b_mod, ln_mix, w_in, hy_conv_w, hy_conv_b, hy_filt_w1, hy_filt_b1, hy_filt_freq, hy_filt_w2, hy_filt_b2, hy_filt_w3, hy_skip, gla_wa_f, gla_ba_f, gla_wa_b, gla_ba_b, gla_norm, mla_q_norm, mla_w_uq, mla_kv_norm, mla_w_ukv, w_out, ln_ffn, w_ffn_in, w_ffn_out, ln_final):
    Bc, Lc, _ = x_prompt.shape
    Bl, Ll, _ = x_sample.shape
    assert 1 + Bl <= MOD_ROWS

    c_all = jnp.concatenate([c_ctx[None, :], c, jnp.zeros((MOD_ROWS - 1 - Bl, D_MODEL), F32)], axis=0)
    mod_all = _modulation(c_all, w_mod, b_mod).reshape(DEPTH, MOD_ROWS, 6, 1, D_MODEL)

    consts_ctx = {'hyena': _hyena_consts(Lc)}
    consts_lat = {'hyena': _hyena_consts(Ll), 'rope': _rope_tables(Ll)}
    ctx_rows = lambda tm: (lambda i: 0)
    lat_rows = lambda tm: (lambda i: 1 + (i * tm) // Ll)
    lnf = ln_final[None, :]

    x_ctx = x_prompt.reshape(Bc * Lc, D_MODEL)
    x_lat = x_sample.reshape(Bl * Ll, D_MODEL)
    w_in_b = w_in.astype(BF16)
    w_in_tail = _prep_w_in_tail(w_in)
    w_out_b = w_out.astype(BF16)
    w_ffn_in_b = w_ffn_in.astype(BF16)
    w_ffn_out_b = w_ffn_out.astype(BF16)
    caches = (None, None, None, None)
    for l in range(DEPTH):
        wa, ba = _prep_gla_decay(gla_wa_f[l], gla_ba_f[l], gla_wa_b[l], gla_ba_b[l])
        lw = {
            'layer': l, 'ln_mix': ln_mix[l][None, :], 'w_in': w_in_b, 'w_in_tail': w_in_tail,
            'hy_conv_w': hy_conv_w[l], 'hy_conv_b': hy_conv_b[l][None, :],
            'hy_w1': jnp.pad(hy_filt_w1[l], ((0, 64 - HY_EMB), (0, 0))), 'hy_b1': hy_filt_b1[l][None, :],
            'hy_freq': hy_filt_freq[l], 'hy_w2': hy_filt_w2[l], 'hy_b2': hy_filt_b2[l][None, :],
            'hy_w3': hy_filt_w3[l], 'hy_skip': hy_skip[l],
            'gla_wa': wa, 'gla_ba': ba, 'gla_norm': gla_norm[l][None, :],
            'mla_q_norm': mla_q_norm[l][None, :], 'mla_w_uq': _prep_w_uq(mla_w_uq[l]),
            'mla_kv_norm': mla_kv_norm[l][None, :], 'mla_w_ukv': mla_w_ukv[l].astype(BF16),
            'w_out': w_out_b, 'ln_ffn': ln_ffn[l][None, :],
            'w_ffn_in': w_ffn_in_b, 'w_ffn_out': w_ffn_out_b,
        }
        final = l == DEPTH - 1
        mod = mod_all[l]
        x_ctx, caches = _trunk_layer(x_ctx, Bc, Lc, mod, ctx_rows, lw, consts_ctx, None, caches, final, lnf)
        ctx = (cache_mla_ckv, cache_mla_krope, state_gla_fwd, state_gla_bwd)
        x_lat, _ = _trunk_layer(x_lat, Bl, Ll, mod, lat_rows, lw, consts_lat, ctx, None, final, lnf)
    return (x_ctx.reshape(Bc, Lc, D_MODEL), x_lat.reshape(Bl, Ll, D_MODEL)) + tuple(caches)
```

```python
import functools
import math

import numpy as np
import jax
import jax.numpy as jnp
from jax import lax
from jax.experimental import pallas as pl
from jax.experimental.pallas import tpu as pltpu

F32 = jnp.float32
BF16 = jnp.bfloat16

D_MODEL = 2048
DEPTH = 2
GRID_W = 64
EPS = 1e-6
HY_CH = 512
HY_ORDER = 2
HY_BANDS = 16
HY_EMB = 1 + 2 * HY_BANDS
HY_FF = 64
HY_FAST_DECAY = 0.3
HY_SLOW_DECAY = 1.5
HY_TARGET = 1e-2
HY_IN = 3 * HY_CH
GLA_HEADS = 4
GLA_DK = 64
GLA_DV = 128
GLA_LOWRANK = 16
GLA_TAU = 16.0
GLA_CHUNK = 64
GLA_QK = GLA_HEADS * GLA_DK
GLA_VW = GLA_HEADS * GLA_DV
MLA_HEADS = 8
MLA_Q_RANK = 384
MLA_KV_RANK = 256
MLA_NOPE = 128
MLA_ROPE = 64
MLA_V = 128
ROPE_THETA = 10000.0
D_FF = -(-8 * D_MODEL // (3 * 256)) * 256

U_HY = 0
U_GLA = 1536
U_MLA = 3072
U_GLA_A = 3712
U_GLA_A_LANE = 64
U_W = 3840
MOD_ROWS = 16

VMEM_LIMIT_V7X = 48 * 1024 * 1024


def _cparams(sem):
    return pltpu.CompilerParams(dimension_semantics=sem, vmem_limit_bytes=VMEM_LIMIT_V7X)


def _dot(a, b):
    return jnp.dot(a, b, preferred_element_type=F32)


def _dot_nt(a, b):
    return lax.dot_general(a, b, (((1,), (1,)), ((), ())), preferred_element_type=F32)


def _split(x):
    hi = x.astype(BF16)
    lo = (x - hi.astype(F32)).astype(BF16)
    return hi, lo


def _dot3(a, b):
    a_hi, a_lo = _split(a)
    b_hi, b_lo = _split(b)
    return _dot(a_hi, b_hi) + (_dot(a_lo, b_hi) + _dot(a_hi, b_lo))


def _rms(x, g):
    ms = jnp.mean(x * x, axis=-1, keepdims=True)
    return x * lax.rsqrt(ms + EPS) * g


ADALN_ROWS = 16


def _adaln_group(x_ref, gain, shift, h_ref, r):
    rows = pl.ds(pl.multiple_of(r, ADALN_ROWS), ADALN_ROWS)
    x = x_ref[rows, :]
    ms = jnp.mean(x * x, axis=-1, keepdims=True)
    h_ref[rows, :] = (x * lax.rsqrt(ms + EPS) * gain + shift).astype(BF16)


def _adaln_all(x_ref, gain, shift, h_ref):
    def body(i, carry):
        _adaln_group(x_ref, gain, shift, h_ref, i * ADALN_ROWS)
        return carry

    lax.fori_loop(0, x_ref.shape[0] // ADALN_ROWS, body, 0, unroll=4)


def _mod_kernel(c_ref, w_ref, b_ref, o_ref):
    c = c_ref[...]
    s = (c * jax.nn.sigmoid(c)).astype(BF16)
    o_ref[...] = _dot(s, w_ref[...].astype(BF16)) + b_ref[...]


def _modulation(c_all, w_mod, b_mod):
    tn = 1024
    n6 = 6 * D_MODEL
    return pl.pallas_call(
        _mod_kernel,
        grid=(DEPTH, n6 // tn),
        in_specs=[pl.BlockSpec((MOD_ROWS, D_MODEL), lambda l, j: (0, 0)),
                  pl.BlockSpec((None, D_MODEL, tn), lambda l, j: (l, 0, j)),
                  pl.BlockSpec((None, 1, tn), lambda l, j: (l, 0, j))],
        out_specs=pl.BlockSpec((None, MOD_ROWS, tn), lambda l, j: (l, 0, j)),
        out_shape=jax.ShapeDtypeStruct((DEPTH, MOD_ROWS, n6), F32),
        compiler_params=_cparams(("arbitrary", "arbitrary")),
        name="modulation",
    )(c_all, w_mod, b_mod.reshape(DEPTH, 1, n6))


def _mod_spec(chunk, row_of_tile, tile_of_step=lambda i, *_: i):
    return pl.BlockSpec((None, None, 1, D_MODEL), lambda *g: (row_of_tile(tile_of_step(*g)), chunk, 0, 0))


IN_TN = 768
IN_MAIN = 3072
IN_PRE_ROWS = 256


def _inproj_kernel(x0_ref, sh0_ref, sc0_ref, xn_ref, shn_ref, scn_ref, ln_ref, w_ref, wt_ref, o_ref, ha_ref, hb_ref):
    i = pl.program_id(0)
    j = pl.program_id(1)

    @pl.when((i == 0) & (j == 0))
    def _():
        _adaln_all(x0_ref, ln_ref[...] * (1.0 + sc0_ref[...]), sh0_ref[...], ha_ref)

    gain = ln_ref[...] * (1.0 + scn_ref[...])
    shift = shn_ref[...]
    base = jnp.clip(j - 1, 0, 3) * IN_PRE_ROWS

    def step(cur_ref, nxt_ref):
        def prepare_next():
            for t in range(IN_PRE_ROWS // ADALN_ROWS):
                rows = pl.ds(t * ADALN_ROWS, ADALN_ROWS)
                x = xn_ref[rows, :]
                ms = jnp.mean(x * x, axis=-1, keepdims=True)
                out_rows = pl.ds(pl.multiple_of(base + t * ADALN_ROWS, ADALN_ROWS), ADALN_ROWS)
                nxt_ref[out_rows, :] = (x * lax.rsqrt(ms + EPS) * gain + shift).astype(BF16)

        @pl.when(j < IN_MAIN // IN_TN)
        def _():
            prepare_next()
            o_ref[...] = _dot(cur_ref[...], w_ref[...])

        @pl.when(j >= IN_MAIN // IN_TN)
        def _():
            prepare_next()
            o_ref[...] = _dot(cur_ref[...], wt_ref[...])

    @pl.when(i % 2 == 0)
    def _():
        step(ha_ref, hb_ref)

    @pl.when(i % 2 == 1)
    def _():
        step(hb_ref, ha_ref)


def _in_projection(x2d, ln, mod, w_in_b, w_tail, layer, row_of_tile, tm):
    rows = x2d.shape[0]
    n_main = IN_MAIN // IN_TN
    n_tiles = rows // tm
    per_tile = tm // IN_PRE_ROWS
    assert U_W // IN_TN - 1 == per_tile and U_W - IN_MAIN == IN_TN
    nxt = lambda i, *_: jnp.minimum(i + 1, n_tiles - 1)
    once = pl.Buffered(1)
    return pl.pallas_call(
        _inproj_kernel,
        grid=(n_tiles, U_W // IN_TN),
        in_specs=[pl.BlockSpec((tm, D_MODEL), lambda i, j: (0, 0), pipeline_mode=once),
                  _mod_spec(0, row_of_tile, lambda *_: 0),
                  _mod_spec(1, row_of_tile, lambda *_: 0),
                  pl.BlockSpec((IN_PRE_ROWS, D_MODEL),
                               lambda i, j: (nxt(i) * per_tile + jnp.clip(j - 1, 0, per_tile - 1), 0)),
                  _mod_spec(0, row_of_tile, nxt),
                  _mod_spec(1, row_of_tile, nxt),
                  pl.BlockSpec((1, D_MODEL), lambda i, j: (0, 0)),
                  pl.BlockSpec((None, D_MODEL, IN_TN), lambda i, j: (layer, 0, jnp.minimum(j, n_main - 1))),
                  pl.BlockSpec((None, D_MODEL, U_W - IN_MAIN), lambda i, j: (layer, 0, 0))],
        out_specs=pl.BlockSpec((tm, IN_TN), lambda i, j: (i, j)),
        out_shape=jax.ShapeDtypeStruct((rows, U_W), F32),
        scratch_shapes=[pltpu.VMEM((tm, D_MODEL), BF16), pltpu.VMEM((tm, D_MODEL), BF16)],
        compiler_params=_cparams(("arbitrary", "arbitrary")),
        name="in_projection",
    )(x2d, mod, mod, x2d, mod, mod, ln, w_in_b, w_tail)


def _filter_kernel(z_ref, w1_ref, b1_ref, fr_ref, w2_ref, b2_ref, w3f_ref, w3b_ref, dec_ref,
                   chi_ref, clo_ref, shi_ref, slo_ref, kr_ref, ki_ref, kn_ref, *, L):
    h = jnp.sin(fr_ref[0:1, :] * (_dot3(z_ref[...], w1_ref[...]) + b1_ref[...]))
    h = jnp.sin(fr_ref[1:2, :] * (_dot3(h, w2_ref[...]) + b2_ref[...]))
    dec = dec_ref[...]
    row = lax.broadcasted_iota(jnp.int32, (L, 1), 0)
    kf = _dot3(h, w3f_ref[...]) * dec
    kb = jnp.where(row == 0, 0.0, _dot3(h, w3b_ref[...]) * dec)
    p_hi, p_lo = _split(kf + kb)
    m_hi, m_lo = _split(kf - kb)
    chi, clo = chi_ref[...], clo_ref[...]
    shi, slo = shi_ref[...], slo_ref[...]
    kc = _dot(chi, p_hi) + (_dot(clo, p_hi) + _dot(chi, p_lo))
    ks = _dot(shi, m_hi) + (_dot(slo, m_hi) + _dot(shi, m_lo))
    sign = jnp.where((row & 1) == 0, 1.0, -1.0)
    kn = jnp.sum((kf + kb) * sign, axis=0, keepdims=True)
    kr_ref[...] = kc * jnp.where(row == 0, 0.5 / L, 1.0 / L)
    ki_ref[...] = ks * (-1.0 / L)
    kn_ref[...] = kn * (0.5 / L)


def _hyena_filters(L, consts, w1p, b1, freq, w2, b2, w3):
    zpos, decay, c_hi, c_lo, s_hi, s_lo = consts
    full = lambda shape: pl.BlockSpec(shape, lambda o: (0,) * len(shape))
    return pl.pallas_call(
        functools.partial(_filter_kernel, L=L),
        grid=(HY_ORDER,),
        in_specs=[full((L, 64)), full((64, HY_FF)), full((1, HY_FF)), full((2, HY_FF)),
                  full((HY_FF, HY_FF)), full((1, HY_FF)),
                  pl.BlockSpec((HY_FF, HY_CH), lambda o: (0, o)),
                  pl.BlockSpec((HY_FF, HY_CH), lambda o: (0, HY_ORDER + o)),
                  full((L, HY_CH)), full((L, L)), full((L, L)), full((L, L)), full((L, L))],
        out_specs=[pl.BlockSpec((None, L, HY_CH), lambda o: (o, 0, 0)),
                   pl.BlockSpec((None, L, HY_CH), lambda o: (o, 0, 0)),
                   pl.BlockSpec((None, 1, HY_CH), lambda o: (o, 0, 0))],
        out_shape=[jax.ShapeDtypeStruct((HY_ORDER, L, HY_CH), F32),
                   jax.ShapeDtypeStruct((HY_ORDER, L, HY_CH), F32),
                   jax.ShapeDtypeStruct((HY_ORDER, 1, HY_CH), F32)],
        compiler_params=_cparams(("arbitrary",)),
        name="hyena_filters",
    )(zpos, w1p, b1, freq, w2, b2, w3, w3, decay, c_hi, c_lo, s_hi, s_lo)


HY_CG = 256
HY_STEP_ROWS = 1024


def _hyena_kernel(v_ref, x1_ref, x2_ref, wv_ref, w1_ref, w2_ref, bv_ref, b1_ref, b2_ref, skip_ref,
                  kr_ref, ki_ref, kn_ref, c_ref, sp_ref, spt_ref, o_ref, *, L):
    row = lax.broadcasted_iota(jnp.int32, (L, 1), 0)
    first = row == 0
    last = row == L - 1
    chains = [(b, slice(g * HY_CG, (g + 1) * HY_CG)) for b in range(v_ref.shape[0]) for g in range(HY_CH // HY_CG)]

    def short_conv(u_ref, w_ref, b_ref, b, cols):
        u = u_ref[b, :, cols]
        prev = jnp.where(first, 0.0, pltpu.roll(u, 1, axis=0))
        nxt = jnp.where(last, 0.0, pltpu.roll(u, L - 1, axis=0))
        return prev * w_ref[0:1, cols] + u * w_ref[1:2, cols] + nxt * w_ref[2:3, cols] + b_ref[:, cols]

    gate_refs = ((x1_ref, w1_ref, b1_ref), (x2_ref, w2_ref, b2_ref))
    z = [short_conv(v_ref, wv_ref, bv_ref, b, cols) for b, cols in chains]
    for o in range(HY_ORDER):
        zb = [zz.astype(BF16) for zz in z]
        a = [_dot(c_ref[...], x) for x in zb]
        s = [_dot(sp_ref[...], x) for x in zb]
        conv = []
        for n, (b, cols) in enumerate(chains):
            kr = kr_ref[o, :, cols]
            ki = ki_ref[o, :, cols]
            qt = a[n] * kr + s[n] * ki
            qb = s[n] * jnp.where(first, kn_ref[o, :, cols], kr) - a[n] * ki
            conv.append(_dot(c_ref[...], qt.astype(BF16)) + _dot(spt_ref[...], qb.astype(BF16)))
        z = [short_conv(*gate_refs[o], b, cols) * (conv[n] + z[n] * skip_ref[o:o + 1, cols])
             for n, (b, cols) in enumerate(chains)]
    for n, (b, cols) in enumerate(chains):
        o_ref[b, :, cols] = z[n].astype(BF16)


def _hyena_mixer(u3, conv_w, conv_b, skip, filt, dft, L):
    B = u3.shape[0]
    nb = max(1, HY_STEP_ROWS // L)
    kr, ki, kn = filt
    c_b, sp_b, spt_b = dft
    once = pl.Buffered(1)
    ublk = lambda part: pl.BlockSpec((nb, L, HY_CH), lambda b: (b, 0, part))
    wblk = lambda part: pl.BlockSpec((3, HY_CH), lambda b: (0, part))
    bblk = lambda part: pl.BlockSpec((1, HY_CH), lambda b: (0, part))
    fblk = lambda rows: pl.BlockSpec((HY_ORDER, rows, HY_CH), lambda b: (0, 0, 0), pipeline_mode=once)
    mat = pl.BlockSpec((L, L), lambda b: (0, 0), pipeline_mode=once)
    return pl.pallas_call(
        functools.partial(_hyena_kernel, L=L),
        grid=(B // nb,),
        in_specs=[ublk(0), ublk(1), ublk(2), wblk(0), wblk(1), wblk(2), bblk(0), bblk(1), bblk(2),
                  pl.BlockSpec((HY_ORDER, HY_CH), lambda b: (0, 0)),
                  fblk(L), fblk(L), fblk(1), mat, mat, mat],
        out_specs=pl.BlockSpec((nb, L, HY_CH), lambda b: (b, 0, 0)),
        out_shape=jax.ShapeDtypeStruct((B, L, HY_CH), BF16),
        compiler_params=_cparams(("arbitrary",)),
        name="hyena_mixer",
    )(u3, u3, u3, conv_w, conv_w, conv_w, conv_b, conv_b, conv_b, skip, kr, ki, kn, c_b, sp_b, spt_b)


GLA_PAIR = 2 * GLA_CHUNK
GLA_STATES = 2 * GLA_HEADS
GLA_SCAN = 256


def _gla_kernel(*refs, L, has_state, emit_state, n_alias):
    it = iter(refs)
    qkvg_ref, a_ref, wa_ref, ba_ref, ng_ref, scan_ref = (next(it) for _ in range(6))
    st0_refs = (next(it), next(it)) if has_state else None
    for _ in range(n_alias):
        next(it)
    y_ref = next(it)
    stout_refs = (next(it), next(it)) if emit_state else None
    qd_s, ki_s, ke_s, qi_s, ku_s, et_s, vb_s, vt_s, o_s, st_s = (next(it) for _ in range(10))

    npair = L // GLA_PAIR
    pre = _dot3(a_ref[...], wa_ref[...]) + ba_ref[...]
    la = (jnp.minimum(pre, 0.0) - jnp.log(1.0 + jnp.exp(-jnp.abs(pre)))) * (1.0 / GLA_TAU)

    scan = scan_ref[...]
    pfx, tot = [], []
    for r in range(0, L, GLA_SCAN):
        x = la[r:r + GLA_SCAN]
        hi = x.astype(BF16)
        r1 = x - hi.astype(F32)
        mid = r1.astype(BF16)
        lo = (r1 - mid.astype(F32)).astype(BF16)
        res = _dot(scan, hi) + (_dot(scan, mid) + _dot(scan, lo))
        pfx.append(res[:GLA_SCAN])
        tot.append(res[GLA_SCAN:])
    pfx = jnp.concatenate(pfx, axis=0) if len(pfx) > 1 else pfx[0]
    tot = jnp.concatenate(tot, axis=0) if len(tot) > 1 else tot[0]

    upper = (lax.broadcasted_iota(jnp.int32, (L, 1), 0) & GLA_CHUNK) != 0
    q = qkvg_ref[:, 0:GLA_QK] * (GLA_DK ** -0.5)
    k = qkvg_ref[:, GLA_QK:2 * GLA_QK]
    for dirn in range(2):
        cols = slice(dirn * GLA_QK, (dirn + 1) * GLA_QK)
        if dirn == 0:
            cum = pfx[:, cols]
            rest = tot[:, cols] - cum
            first = jnp.logical_not(upper)
        else:
            rest = pfx[:, cols] - la[:, cols]
            cum = tot[:, cols] - rest
            first = upper
        et = jnp.exp(tot[:, cols])
        et_other = jnp.where(upper, pltpu.roll(et, GLA_CHUNK, axis=0), pltpu.roll(et, L - GLA_CHUNK, axis=0))
        qd = q * jnp.exp(cum)
        ke = k * jnp.exp(rest)
        et_s[dirn] = et
        qd_s[dirn] = qd.astype(BF16)
        ki_s[dirn] = (k * jnp.exp(-cum)).astype(BF16)
        ke_s[dirn] = ke.astype(BF16)
        qi_s[dirn] = jnp.where(first, qd, qd * et_other).astype(BF16)
        ku_s[dirn] = jnp.where(first, ke * et_other, ke).astype(BF16)
    vb_s[...] = qkvg_ref[:, 2 * GLA_QK:2 * GLA_QK + GLA_VW].astype(BF16)

    def transpose_v(n2, carry):
        r0 = pl.multiple_of(n2 * GLA_PAIR, GLA_PAIR)
        for h in range(GLA_HEADS):
            vp = qkvg_ref[pl.ds(r0, GLA_PAIR), 2 * GLA_QK + h * GLA_DV:2 * GLA_QK + (h + 1) * GLA_DV]
            vt_s[n2, h * GLA_DV:(h + 1) * GLA_DV, :] = vp.T.astype(BF16)
        return carry

    lax.fori_loop(0, npair, transpose_v, 0)

    if has_state:
        zero = jnp.zeros((GLA_DK, GLA_DV), F32)
        for dirn in range(2):
            for h in range(GLA_HEADS):
                s0 = st0_refs[dirn][h]
                both = jnp.concatenate([s0, zero] if h % 2 == 0 else [zero, s0], axis=0)
                idx = dirn * GLA_HEADS + h
                st_s[idx * GLA_DV:(idx + 1) * GLA_DV, :] = both.T
    else:
        st_s[...] = jnp.zeros_like(st_s)

    lane = lax.broadcasted_iota(jnp.int32, (1, 2 * GLA_DK), 1)
    head_lanes = (lane < GLA_DK, lane >= GLA_DK)
    ri = lax.broadcasted_iota(jnp.int32, (GLA_PAIR, GLA_PAIR), 0)
    ci = lax.broadcasted_iota(jnp.int32, (GLA_PAIR, GLA_PAIR), 1)
    same_chunk = ((ri ^ ci) & GLA_CHUNK) == 0
    keep = (ci <= ri, ci >= ri)

    def pair_step(n, carry):
        for dirn in range(2):
            n2 = n if dirn == 0 else npair - 1 - n
            r0 = pl.multiple_of(n2 * GLA_PAIR, GLA_PAIR)
            rows = pl.ds(r0, GLA_PAIR)
            for p in range(GLA_HEADS // 2):
                lanes = slice(p * 2 * GLA_DK, (p + 1) * 2 * GLA_DK)
                qd = qd_s[dirn, rows, lanes]
                qi = qi_s[dirn, rows, lanes]
                zeros = jnp.zeros_like(qd)
                dec = (et_s[dirn, pl.ds(r0, 8), lanes][0:1, :]
                       * et_s[dirn, pl.ds(r0 + GLA_CHUNK, 8), lanes][0:1, :])
                q2 = jnp.concatenate([jnp.where(head_lanes[0], qd, zeros), jnp.where(head_lanes[1], qd, zeros)],
                                     axis=0)
                att_d = _dot_nt(q2, ki_s[dirn, rows, lanes])
                att_o = _dot_nt(q2, ke_s[dirn, rows, lanes])
                s0 = (dirn * GLA_HEADS + 2 * p) * GLA_DV
                st2 = st_s[s0:s0 + 2 * GLA_DV, :]
                vt2 = vt_s[n2, 2 * p * GLA_DV:(2 * p + 2) * GLA_DV, :]
                st_s[s0:s0 + 2 * GLA_DV, :] = st2 * dec + _dot(vt2, ku_s[dirn, rows, lanes])
                for hh in range(2):
                    h = 2 * p + hh
                    hr = slice(hh * GLA_PAIR, (hh + 1) * GLA_PAIR)
                    att = jnp.where(keep[dirn], jnp.where(same_chunk, att_d[hr], att_o[hr]), 0.0)
                    qm = jnp.where(head_lanes[hh], qi, zeros)
                    o = (_dot(att.astype(BF16), vb_s[rows, h * GLA_DV:(h + 1) * GLA_DV])
                         + _dot_nt(qm, st2[hh * GLA_DV:(hh + 1) * GLA_DV].astype(BF16)))
                    o_s[dirn, rows, h * GLA_DV:(h + 1) * GLA_DV] = o
        return carry

    lax.fori_loop(0, npair, pair_step, 0, unroll=2)

    for h in range(GLA_HEADS):
        cols = slice(h * GLA_DV, (h + 1) * GLA_DV)
        o = _rms(o_s[0, :, cols] + o_s[1, :, cols], ng_ref[...])
        g = qkvg_ref[:, 2 * GLA_QK + GLA_VW + h * GLA_DV:2 * GLA_QK + GLA_VW + (h + 1) * GLA_DV]
        y_ref[:, cols] = (o * (g * jax.nn.sigmoid(g))).astype(BF16)
    if emit_state:
        for dirn in range(2):
            for h in range(GLA_HEADS):
                idx = dirn * GLA_HEADS + h
                half = (h % 2) * GLA_DK
                stout_refs[dirn][h] = st_s[idx * GLA_DV:(idx + 1) * GLA_DV, :].T[half:half + GLA_DK, :]


def _gla_scan_matrix():
    i = np.arange(GLA_SCAN)
    same = (i[:, None] // GLA_CHUNK) == (i[None, :] // GLA_CHUNK)
    tri = same & (i[None, :] <= i[:, None])
    return jnp.asarray(np.concatenate([tri, same], axis=0).astype(np.float32)).astype(BF16)


def _gla_mixer(u3, wa_p, ba_p, norm_g, st0, layer, L, emit_state):
    B = u3.shape[0]
    has_state = st0 is not None
    head_state = (GLA_HEADS, GLA_DK, GLA_DV)
    in_specs = [pl.BlockSpec((None, L, 1536), lambda b: (b, 0, U_GLA // 1536)),
                pl.BlockSpec((None, L, 128), lambda b: (b, 0, U_GLA_A // 128)),
                pl.BlockSpec((128, 2 * GLA_QK), lambda b: (0, 0)),
                pl.BlockSpec((1, 2 * GLA_QK), lambda b: (0, 0)),
                pl.BlockSpec((1, GLA_DV), lambda b: (0, 0)),
                pl.BlockSpec((2 * GLA_SCAN, GLA_SCAN), lambda b: (0, 0))]
    args = [u3, u3, wa_p, ba_p, norm_g, _gla_scan_matrix()]
    if has_state:
        in_specs += [pl.BlockSpec((None, None) + head_state, lambda b: (b, layer, 0, 0, 0))] * 2
        args += list(st0)
    out_specs = [pl.BlockSpec((None, L, GLA_VW), lambda b: (b, 0, 0))]
    out_shape = [jax.ShapeDtypeStruct((B, L, GLA_VW), BF16)]
    aliases = {}
    if emit_state is not None:
        out_specs += [pl.BlockSpec((None, None) + head_state, lambda b: (b, layer, 0, 0, 0))] * 2
        out_shape += [jax.ShapeDtypeStruct((B, DEPTH) + head_state, F32)] * 2
        for k, prev in enumerate(emit_state):
            if prev is not None:
                aliases[len(args)] = 1 + k
                in_specs.append(pl.BlockSpec(memory_space=pl.ANY))
                args.append(prev)
    npair = L // GLA_PAIR
    qk_bf16 = pltpu.VMEM((2, L, GLA_QK), BF16)
    res = pl.pallas_call(
        functools.partial(_gla_kernel, L=L, has_state=has_state, emit_state=emit_state is not None,
                          n_alias=len(aliases)),
        grid=(B,),
        in_specs=in_specs,
        out_specs=out_specs,
        out_shape=out_shape,
        input_output_aliases=aliases,
        scratch_shapes=[qk_bf16,
                        qk_bf16,
                        qk_bf16,
                        qk_bf16,
                        qk_bf16,
                        pltpu.VMEM((2, L, GLA_QK), F32),
                        pltpu.VMEM((L, GLA_VW), BF16),
                        pltpu.VMEM((npair, GLA_VW, GLA_PAIR), BF16),
                        pltpu.VMEM((2, L, GLA_VW), F32),
                        pltpu.VMEM((GLA_STATES * GLA_DV, 2 * GLA_DK), F32)],
        compiler_params=_cparams(("arbitrary",)),
        name="gla_mixer",
    )(*args)
    return (res[0], res[1], res[2]) if emit_state is not None else (res[0], None, None)


MLA_HW = 256
MLA_TQ = 512
MLA_KCH = 256


def _rope(x, cos, sin):
    lane = lax.broadcasted_iota(jnp.int32, x.shape, 1)
    partner = jnp.where((lane & 16) == 0, pltpu.roll(x, 112, axis=1), pltpu.roll(x, 16, axis=1))
    return x * cos + partner * sin


def _mla_kernel(*refs, L, Lk, tq, rope, emit, n_alias):
    it = iter(refs)
    u_ref, qn_ref, wq_ref, kvn_ref, wkv_ref = (next(it) for _ in range(5))
    cos_ref, sin_ref, cckv_ref, ckr_ref = (next(it) for _ in range(4)) if rope else (None,) * 4
    for _ in range(n_alias):
        next(it)
    y_ref = next(it)
    ckv_out, kr_out = (next(it), next(it)) if emit else (None, None)
    kf_s, v_s = next(it), next(it)

    qi = pl.program_id(1)

    def project(keys, rope_part, r0):
        kv = _dot(keys, wkv_ref[...])
        for h in range(MLA_HEADS):
            kf_s[h, r0:r0 + MLA_KCH, 0:MLA_NOPE] = kv[:, h * 256:h * 256 + MLA_NOPE].astype(BF16)
            kf_s[h, r0:r0 + MLA_KCH, MLA_NOPE:MLA_HW] = rope_part
            v_s[h, r0:r0 + MLA_KCH, :] = kv[:, h * 256 + MLA_NOPE:(h + 1) * 256].astype(BF16)

    @pl.when(qi == 0)
    def _():
        for r0 in range(0, L, MLA_KCH):
            ckvn = _rms(u_ref[r0:r0 + MLA_KCH, MLA_Q_RANK:MLA_Q_RANK + MLA_KV_RANK], kvn_ref[...])
            kr = u_ref[r0:r0 + MLA_KCH, 640:768]
            kr = jnp.where(lax.broadcasted_iota(jnp.int32, kr.shape, 1) < MLA_ROPE, kr, 0.0)
            if emit:
                ckv_out[r0:r0 + MLA_KCH, :] = ckvn
                kr_out[r0:r0 + MLA_KCH, :] = kr[:, 0:MLA_ROPE]
            if rope:
                kr = _rope(kr, cos_ref[r0:r0 + MLA_KCH, :], sin_ref[r0:r0 + MLA_KCH, :])
            project(ckvn.astype(BF16), kr.astype(BF16), r0)
        if rope:
            for r0 in range(0, Lk - L, MLA_KCH):
                ckr = ckr_ref[r0:r0 + MLA_KCH, :]
                ckr = jnp.concatenate([ckr, jnp.zeros_like(ckr)], axis=1)
                project(cckv_ref[r0:r0 + MLA_KCH, :].astype(BF16), ckr.astype(BF16), L + r0)

    q0 = pl.multiple_of(qi * tq, tq)
    cqn = _rms(u_ref[pl.ds(q0, tq), 0:MLA_Q_RANK], qn_ref[...]).astype(BF16)
    q = _dot(cqn, wq_ref[...]) * ((MLA_NOPE + MLA_ROPE) ** -0.5)
    if rope:
        cos = cos_ref[pl.ds(q0, tq), :]
        sin = sin_ref[pl.ds(q0, tq), :]
    for h in range(MLA_HEADS):
        q_nope = q[:, h * MLA_HW:h * MLA_HW + MLA_NOPE]
        q_rope = q[:, h * MLA_HW + MLA_NOPE:(h + 1) * MLA_HW]
        if rope:
            q_rope = _rope(q_rope, cos, sin)
        qh = jnp.concatenate([q_nope, q_rope], axis=1).astype(BF16)
        s = _dot_nt(qh, kf_s[h])
        p = jnp.exp(s - jnp.max(s, axis=-1, keepdims=True))
        denom = jnp.sum(p, axis=-1, keepdims=True)
        o = _dot(p.astype(BF16), v_s[h])
        y_ref[:, h * MLA_V:(h + 1) * MLA_V] = (o / denom).astype(BF16)


def _mla_mixer(u3, q_norm, wq_p, kv_norm, wkv, rope_args, layer, L, emit):
    B = u3.shape[0]
    rope = rope_args is not None
    Lk = L + (rope_args[2].shape[2] if rope else 0)
    tq = min(MLA_TQ, L)
    const = lambda shape: pl.BlockSpec(shape, lambda b, i: (0,) * len(shape))
    in_specs = [pl.BlockSpec((None, L, 768), lambda b, i: (b, 0, U_MLA // 768)),
                const((1, MLA_Q_RANK)), const((MLA_Q_RANK, MLA_HEADS * MLA_HW)),
                const((1, MLA_KV_RANK)), const((MLA_KV_RANK, MLA_HEADS * 256))]
    args = [u3, q_norm, wq_p, kv_norm, wkv]
    if rope:
        cos, sin, cckv, ckr = rope_args
        in_specs += [const((L, 128)), const((L, 128)),
                     pl.BlockSpec((None, None, Lk - L, MLA_KV_RANK), lambda b, i: (b, layer, 0, 0)),
                     pl.BlockSpec((None, None, Lk - L, MLA_ROPE), lambda b, i: (b, layer, 0, 0))]
        args += [cos, sin, cckv, ckr]
    out_specs = [pl.BlockSpec((None, tq, MLA_HEADS * MLA_V), lambda b, i: (b, i, 0))]
    out_shape = [jax.ShapeDtypeStruct((B, L, MLA_HEADS * MLA_V), BF16)]
    aliases = {}
    if emit is not None:
        out_specs += [pl.BlockSpec((None, None, L, MLA_KV_RANK), lambda b, i: (b, layer, 0, 0)),
                      pl.BlockSpec((None, None, L, MLA_ROPE), lambda b, i: (b, layer, 0, 0))]
        out_shape += [jax.ShapeDtypeStruct((B, DEPTH, L, MLA_KV_RANK), F32),
                      jax.ShapeDtypeStruct((B, DEPTH, L, MLA_ROPE), F32)]
        for k, prev in enumerate(emit):
            if prev is not None:
                aliases[len(args)] = 1 + k
                in_specs.append(pl.BlockSpec(memory_space=pl.ANY))
                args.append(prev)
    res = pl.pallas_call(
        functools.partial(_mla_kernel, L=L, Lk=Lk, tq=tq, rope=rope, emit=emit is not None, n_alias=len(aliases)),
        grid=(B, L // tq),
        in_specs=in_specs,
        out_specs=out_specs,
        out_shape=out_shape,
        input_output_aliases=aliases,
        scratch_shapes=[pltpu.VMEM((MLA_HEADS, Lk, MLA_HW), BF16),
                        pltpu.VMEM((MLA_HEADS, Lk, MLA_V), BF16)],
        compiler_params=_cparams(("arbitrary", "arbitrary")),
        name="mla_mixer",
    )(*args)
    return res if emit is not None else (res[0], None, None)


def _outproj_kernel(x_ref, g_ref, yh_ref, yg_ref, ym_ref, wh_ref, wg_ref, wm_ref, o_ref):
    y = _dot(yh_ref[...], wh_ref[...]) + _dot(yg_ref[...], wg_ref[...]) + _dot(ym_ref[...], wm_ref[...])
    o_ref[...] = x_ref[...] + g_ref[...] * y


def _out_projection(x2d, mod, y_hy, y_gla, y_mla, w_out_b, layer, row_of_tile, tm):
    rows = x2d.shape[0]
    once = pl.Buffered(1)
    return pl.pallas_call(
        _outproj_kernel,
        grid=(rows // tm,),
        in_specs=[pl.BlockSpec((tm, D_MODEL), lambda i: (i, 0)),
                  _mod_spec(2, row_of_tile),
                  pl.BlockSpec((tm, HY_CH), lambda i: (i, 0)),
                  pl.BlockSpec((tm, GLA_VW), lambda i: (i, 0)),
                  pl.BlockSpec((tm, MLA_HEADS * MLA_V), lambda i: (i, 0)),
                  pl.BlockSpec((None, HY_CH, D_MODEL), lambda i: (layer, 0, 0), pipeline_mode=once),
                  pl.BlockSpec((None, GLA_VW, D_MODEL), lambda i: (layer, 1, 0), pipeline_mode=once),
                  pl.BlockSpec((None, MLA_HEADS * MLA_V, D_MODEL), lambda i: (layer, 1, 0), pipeline_mode=once)],
        out_specs=pl.BlockSpec((tm, D_MODEL), lambda i: (i, 0)),
        out_shape=jax.ShapeDtypeStruct((rows, D_MODEL), F32),
        compiler_params=_cparams(("arbitrary",)),
        name="out_projection",
    )(x2d, mod, y_hy, y_gla, y_mla, w_out_b, w_out_b, w_out_b)


FFN_TF = 512


def _ffn_kernel(x_ref, ln_ref, sh_ref, sc_ref, g_ref, wg_ref, wu_ref, wo_ref, lnf_ref, o_ref,
                h_ref, acc_ref, *, final):
    f = pl.program_id(1)

    @pl.when(f == 0)
    def _():
        _adaln_all(x_ref, ln_ref[...] * (1.0 + sc_ref[...]), sh_ref[...], h_ref)
        acc_ref[...] = jnp.zeros_like(acc_ref)

    h = h_ref[...]
    gate = _dot(h, wg_ref[...])
    up = _dot(h, wu_ref[...])
    act = (gate * jax.nn.sigmoid(gate) * up).astype(BF16)
    acc_ref[...] += _dot(act, wo_ref[...])

    @pl.when(f == pl.num_programs(1) - 1)
    def _():
        def body(i, carry):
            rows = pl.ds(pl.multiple_of(i * ADALN_ROWS, ADALN_ROWS), ADALN_ROWS)
            x = x_ref[rows, :] + g_ref[...] * acc_ref[rows, :]
            o_ref[rows, :] = _rms(x, lnf_ref[...]) if final else x
            return carry

        lax.fori_loop(0, x_ref.shape[0] // ADALN_ROWS, body, 0, unroll=4)


def _ffn(x2d, ln, mod, w_in_b, w_out_b, layer, ln_final, row_of_tile, tm, final):
    rows = x2d.shape[0]
    nf = D_FF // FFN_TF
    return pl.pallas_call(
        functools.partial(_ffn_kernel, final=final),
        grid=(rows // tm, nf),
        in_specs=[pl.BlockSpec((tm, D_MODEL), lambda i, f: (i, 0)),
                  pl.BlockSpec((1, D_MODEL), lambda i, f: (0, 0)),
                  _mod_spec(3, row_of_tile),
                  _mod_spec(4, row_of_tile),
                  _mod_spec(5, row_of_tile),
                  pl.BlockSpec((None, D_MODEL, FFN_TF), lambda i, f: (layer, 0, f)),
                  pl.BlockSpec((None, D_MODEL, FFN_TF), lambda i, f: (layer, 0, nf + f)),
                  pl.BlockSpec((None, FFN_TF, D_MODEL), lambda i, f: (layer, f, 0)),
                  pl.BlockSpec((1, D_MODEL), lambda i, f: (0, 0))],
        out_specs=pl.BlockSpec((tm, D_MODEL), lambda i, f: (i, 0)),
        out_shape=jax.ShapeDtypeStruct((rows, D_MODEL), F32),
        scratch_shapes=[pltpu.VMEM((tm, D_MODEL), BF16), pltpu.VMEM((tm, D_MODEL), F32)],
        compiler_params=_cparams(("arbitrary", "arbitrary")),
        name="ffn",
    )(x2d, ln, mod, mod, mod, w_in_b, w_in_b, w_out_b, ln_final)


def _dft_tables(L):
    k = np.arange(L)
    ang = np.pi * ((k[:, None] * k[None, :]) % (2 * L)) / L
    c = np.cos(ang)
    s = np.sin(ang)
    sp = s.copy()
    sp[0, :] = 1.0 - 2.0 * (k % 2)
    as32 = lambda a: jnp.asarray(a.astype(np.float32))
    return as32(c), as32(s), as32(sp), as32(sp.T.copy())


def _hyena_consts(L):
    c, s, sp, spt = _dft_tables(L)
    c_hi, c_lo = _split(c)
    s_hi, s_lo = _split(s)
    f32 = np.float32
    t = np.linspace(0.0, 1.0, L, dtype=f32)[:, None]
    w = (f32(2.0 * math.pi) * np.arange(L, dtype=f32)[:, None] / f32(L)).astype(f32)
    f = np.linspace(1e-4, HY_BANDS - 1, HY_BANDS, dtype=f32)
    zpos = np.concatenate([t, np.cos(f * w), -np.sin(f * w)], axis=-1).astype(f32)
    zpos = np.pad(zpos, ((0, 0), (0, 64 - HY_EMB)))
    min_decay = math.log(HY_TARGET) / HY_SLOW_DECAY
    max_decay = math.log(HY_TARGET) / HY_FAST_DECAY
    delta = np.abs(np.linspace(min_decay, max_decay, HY_CH, dtype=f32))
    decay = np.exp(-t * delta).astype(f32)
    filt_consts = (jnp.asarray(zpos), jnp.asarray(decay), c_hi, c_lo, s_hi, s_lo)
    main_consts = (c_hi, sp.astype(BF16), spt.astype(BF16))
    return filt_consts, main_consts


def _rope_tables(L):
    t = jnp.arange(L)
    half = MLA_ROPE // 2
    inv = ROPE_THETA ** (-jnp.arange(0, half, 2, dtype=F32) / half)
    ang_r = (t // GRID_W).astype(F32)[:, None] * inv
    ang_c = (t % GRID_W).astype(F32)[:, None] * inv
    cr, sr, cc, sc = jnp.cos(ang_r), jnp.sin(ang_r), jnp.cos(ang_c), jnp.sin(ang_c)
    cos = jnp.concatenate([cr, cr, cc, cc, jnp.ones((L, 128 - MLA_ROPE), F32)], axis=1)
    sin = jnp.concatenate([-sr, sr, -sc, sc, jnp.zeros((L, 128 - MLA_ROPE), F32)], axis=1)
    return cos, sin


def _prep_w_in_tail(w):
    mla0 = IN_MAIN + 2 * GLA_LOWRANK
    pad = jnp.zeros((DEPTH, D_MODEL, IN_TN - (w.shape[2] - IN_MAIN)), w.dtype)
    return jnp.concatenate([w[:, :, mla0:], w[:, :, IN_MAIN:mla0], pad], axis=2).astype(BF16)


def _prep_w_uq(w):
    w = w.reshape(MLA_Q_RANK, MLA_HEADS, MLA_NOPE + MLA_ROPE)
    w = jnp.pad(w, ((0, 0), (0, 0), (0, MLA_HW - MLA_NOPE - MLA_ROPE)))
    return w.reshape(MLA_Q_RANK, MLA_HEADS * MLA_HW).astype(BF16)


def _prep_gla_decay(wa_f, ba_f, wa_b, ba_b):
    wa = jnp.zeros((128, 2 * GLA_QK), F32)
    wa = wa.at[U_GLA_A_LANE:U_GLA_A_LANE + GLA_LOWRANK, 0:GLA_QK].set(wa_f)
    wa = wa.at[U_GLA_A_LANE + GLA_LOWRANK:U_GLA_A_LANE + 2 * GLA_LOWRANK, GLA_QK:].set(wa_b)
    return wa, jnp.concatenate([ba_f, ba_b])[None, :]


def _trunk_layer(x2d, B, L, mod, row_of_tile, lw, consts, ctx, caches, final, ln_final):
    tm = 1024
    tm_out = 512
    u = _in_projection(x2d, lw['ln_mix'], mod, lw['w_in'], lw['w_in_tail'], lw['layer'], row_of_tile(tm), tm)
    u3 = u.reshape(B, L, U_W)
    filt_consts, main_consts = consts['hyena']
    filt = _hyena_filters(L, filt_consts, lw['hy_w1'], lw['hy_b1'], lw['hy_freq'], lw['hy_w2'],
                          lw['hy_b2'], lw['hy_w3'])
    y_hy = _hyena_mixer(u3, lw['hy_conv_w'], lw['hy_conv_b'], lw['hy_skip'], filt, main_consts, L)
    if ctx is None:
        y_gla, s_f, s_b = _gla_mixer(u3, lw['gla_wa'], lw['gla_ba'], lw['gla_norm'], None, lw['layer'], L, caches[2:])
        y_mla, ckv, krope = _mla_mixer(u3, lw['mla_q_norm'], lw['mla_w_uq'], lw['mla_kv_norm'],
                                       lw['mla_w_ukv'], None, lw['layer'], L, caches[:2])
        extras = (ckv, krope, s_f, s_b)
    else:
        cache_ckv, cache_krope, s0f, s0b = ctx
        y_gla, _, _ = _gla_mixer(u3, lw['gla_wa'], lw['gla_ba'], lw['gla_norm'], (s0f, s0b), lw['layer'], L, None)
        cos, sin = consts['rope']
        y_mla, _, _ = _mla_mixer(u3, lw['mla_q_norm'], lw['mla_w_uq'], lw['mla_kv_norm'], lw['mla_w_ukv'],
                                 (cos, sin, cache_ckv, cache_krope), lw['layer'], L, None)
        extras = None
    rows = B * L
    x2d = _out_projection(x2d, mod, y_hy.reshape(rows, -1), y_gla.reshape(rows, -1), y_mla.reshape(rows, -1),
                          lw['w_out'], lw['layer'], row_of_tile(tm_out), tm_out)
    x2d = _ffn(x2d, lw['ln_ffn'], mod, lw['w_ffn_in'], lw['w_ffn_out'], lw['layer'], ln_final, row_of_tile(tm_out),
               tm_out, final)
    return x2d, extras


def kernel(x_prompt, x_sample, cache_mla_ckv, cache_mla_krope, state_gla_fwd, state_gla_bwd, c, c_ctx, w_mod, b_mod, ln_mix, w_in, hy_conv_w, hy_conv_b, hy_filt_w1, hy_filt_b1, hy_filt_freq, hy_filt_w2, hy_filt_b2, hy_filt_w3, hy_skip, gla_wa_f, gla_ba_f, gla_wa_b, gla_ba_b, gla_norm, mla_q_norm, mla_w_uq, mla_kv_norm, mla_w_ukv, w_out, ln_ffn, w_ffn_in, w_ffn_out, ln_final):
    Bc, Lc, _ = x_prompt.shape
    Bl, Ll, _ = x_sample.shape
    assert 1 + Bl <= MOD_ROWS

    c_all = jnp.concatenate([c_ctx[None, :], c, jnp.zeros((MOD_ROWS - 1 - Bl, D_MODEL), F32)], axis=0)
    mod_all = _modulation(c_all, w_mod, b_mod).reshape(DEPTH, MOD_ROWS, 6, 1, D_MODEL)

    consts_ctx = {'hyena': _hyena_consts(Lc)}
    consts_lat = {'hyena': _hyena_consts(Ll), 'rope': _rope_tables(Ll)}
    ctx_rows = lambda tm: (lambda i: 0)
    lat_rows = lambda tm: (lambda i: 1 + (i * tm) // Ll)
    lnf = ln_final[None, :]

    x_ctx = x_prompt.reshape(Bc * Lc, D_MODEL)
    x_lat = x_sample.reshape(Bl * Ll, D_MODEL)
    w_in_b = w_in.astype(BF16)
    w_in_tail = _prep_w_in_tail(w_in)
    w_out_b = w_out.astype(BF16)
    w_ffn_in_b = w_ffn_in.astype(BF16)
    w_ffn_out_b = w_ffn_out.astype(BF16)
    caches = (None, None, None, None)
    for l in range(DEPTH):
        wa, ba = _prep_gla_decay(gla_wa_f[l], gla_ba_f[l], gla_wa_b[l], gla_ba_b[l])
        lw = {
            'layer': l, 'ln_mix': ln_mix[l][None, :], 'w_in': w_in_b, 'w_in_tail': w_in_tail,
            'hy_conv_w': hy_conv_w[l], 'hy_conv_b': hy_conv_b[l][None, :],
            'hy_w1': jnp.pad(hy_filt_w1[l], ((0, 64 - HY_EMB), (0, 0))), 'hy_b1': hy_filt_b1[l][None, :],
            'hy_freq': hy_filt_freq[l], 'hy_w2': hy_filt_w2[l], 'hy_b2': hy_filt_b2[l][None, :],
            'hy_w3': hy_filt_w3[l], 'hy_skip': hy_skip[l],
            'gla_wa': wa, 'gla_ba': ba, 'gla_norm': gla_norm[l][None, :],
            'mla_q_norm': mla_q_norm[l][None, :], 'mla_w_uq': _prep_w_uq(mla_w_uq[l]),
            'mla_kv_norm': mla_kv_norm[l][None, :], 'mla_w_ukv': mla_w_ukv[l].astype(BF16),
            'w_out': w_out_b, 'ln_ffn': ln_ffn[l][None, :],
            'w_ffn_in': w_ffn_in_b, 'w_ffn_out': w_ffn_out_b,
        }
        final = l == DEPTH - 1
        mod = mod_all[l]
        x_ctx, caches = _trunk_layer(x_ctx, Bc, Lc, mod, ctx_rows, lw, consts_ctx, None, caches, final, lnf)
        ctx = (cache_mla_ckv, cache_mla_krope, state_gla_fwd, state_gla_bwd)
        x_lat, _ = _trunk_layer(x_lat, Bl, Ll, mod, lat_rows, lw, consts_lat, ctx, None, final, lnf)
    return (x_ctx.reshape(Bc, Lc, D_MODEL), x_lat.reshape(Bl, Ll, D_MODEL)) + tuple(caches)
```

```python
import functools
import math

import numpy as np
import jax
import jax.numpy as jnp
from jax import lax
from jax.experimental import pallas as pl
from jax.experimental.pallas import tpu as pltpu

F32 = jnp.float32
BF16 = jnp.bfloat16

D_MODEL = 2048
DEPTH = 2
GRID_W = 64
EPS = 1e-6
HY_CH = 512
HY_ORDER = 2
HY_BANDS = 16
HY_EMB = 1 + 2 * HY_BANDS
HY_FF = 64
HY_FAST_DECAY = 0.3
HY_SLOW_DECAY = 1.5
HY_TARGET = 1e-2
HY_IN = 3 * HY_CH
GLA_HEADS = 4
GLA_DK = 64
GLA_DV = 128
GLA_LOWRANK = 16
GLA_TAU = 16.0
GLA_CHUNK = 64
GLA_QK = GLA_HEADS * GLA_DK
GLA_VW = GLA_HEADS * GLA_DV
MLA_HEADS = 8
MLA_Q_RANK = 384
MLA_KV_RANK = 256
MLA_NOPE = 128
MLA_ROPE = 64
MLA_V = 128
ROPE_THETA = 10000.0
D_FF = -(-8 * D_MODEL // (3 * 256)) * 256

U_HY = 0
U_GLA = 1536
U_MLA = 3072
U_GLA_A = 3712
U_GLA_A_LANE = 64
U_W = 3840
MOD_ROWS = 16

VMEM_LIMIT_V7X = 48 * 1024 * 1024


def _cparams(sem):
    return pltpu.CompilerParams(dimension_semantics=sem, vmem_limit_bytes=VMEM_LIMIT_V7X)


def _dot(a, b):
    return jnp.dot(a, b, preferred_element_type=F32)


def _dot_nt(a, b):
    return lax.dot_general(a, b, (((1,), (1,)), ((), ())), preferred_element_type=F32)


def _split(x):
    hi = x.astype(BF16)
    lo = (x - hi.astype(F32)).astype(BF16)
    return hi, lo


def _dot3(a, b):
    a_hi, a_lo = _split(a)
    b_hi, b_lo = _split(b)
    return _dot(a_hi, b_hi) + (_dot(a_lo, b_hi) + _dot(a_hi, b_lo))


def _rms(x, g):
    ms = jnp.mean(x * x, axis=-1, keepdims=True)
    return x * lax.rsqrt(ms + EPS) * g


ADALN_ROWS = 16


def _adaln_group(x_ref, gain, shift, h_ref, r):
    rows = pl.ds(pl.multiple_of(r, ADALN_ROWS), ADALN_ROWS)
    x = x_ref[rows, :]
    ms = jnp.mean(x * x, axis=-1, keepdims=True)
    h_ref[rows, :] = (x * lax.rsqrt(ms + EPS) * gain + shift).astype(BF16)


def _adaln_all(x_ref, gain, shift, h_ref):
    def body(i, carry):
        _adaln_group(x_ref, gain, shift, h_ref, i * ADALN_ROWS)
        return carry

    lax.fori_loop(0, x_ref.shape[0] // ADALN_ROWS, body, 0, unroll=4)


def _mod_kernel(c_ref, w_ref, b_ref, o_ref):
    c = c_ref[...]
    s = (c * jax.nn.sigmoid(c)).astype(BF16)
    o_ref[...] = _dot(s, w_ref[...].astype(BF16)) + b_ref[...]


def _modulation(c_all, w_mod, b_mod):
    tn = 1024
    n6 = 6 * D_MODEL
    return pl.pallas_call(
        _mod_kernel,
        grid=(DEPTH, n6 // tn),
        in_specs=[pl.BlockSpec((MOD_ROWS, D_MODEL), lambda l, j: (0, 0)),
                  pl.BlockSpec((None, D_MODEL, tn), lambda l, j: (l, 0, j)),
                  pl.BlockSpec((None, 1, tn), lambda l, j: (l, 0, j))],
        out_specs=pl.BlockSpec((None, MOD_ROWS, tn), lambda l, j: (l, 0, j)),
        out_shape=jax.ShapeDtypeStruct((DEPTH, MOD_ROWS, n6), F32),
        compiler_params=_cparams(("arbitrary", "arbitrary")),
        name="modulation",
    )(c_all, w_mod, b_mod.reshape(DEPTH, 1, n6))


def _mod_spec(chunk, row_of_tile, tile_of_step=lambda i, *_: i):
    return pl.BlockSpec((None, None, 1, D_MODEL), lambda *g: (row_of_tile(tile_of_step(*g)), chunk, 0, 0))


IN_TN = 768
IN_MAIN = 3072
IN_PRE_ROWS = 256


def _inproj_kernel(x0_ref, sh0_ref, sc0_ref, xn_ref, shn_ref, scn_ref, ln_ref, w_ref, wt_ref, o_ref, ha_ref, hb_ref):
    i = pl.program_id(0)
    j = pl.program_id(1)

    @pl.when((i == 0) & (j == 0))
    def _():
        _adaln_all(x0_ref, ln_ref[...] * (1.0 + sc0_ref[...]), sh0_ref[...], ha_ref)

    gain = ln_ref[...] * (1.0 + scn_ref[...])
    shift = shn_ref[...]
    base = jnp.clip(j - 1, 0, 3) * IN_PRE_ROWS

    def step(cur_ref, nxt_ref):
        def prepare_next():
            for t in range(IN_PRE_ROWS // ADALN_ROWS):
                rows = pl.ds(t * ADALN_ROWS, ADALN_ROWS)
                x = xn_ref[rows, :]
                ms = jnp.mean(x * x, axis=-1, keepdims=True)
                out_rows = pl.ds(pl.multiple_of(base + t * ADALN_ROWS, ADALN_ROWS), ADALN_ROWS)
                nxt_ref[out_rows, :] = (x * lax.rsqrt(ms + EPS) * gain + shift).astype(BF16)

        @pl.when(j < IN_MAIN // IN_TN)
        def _():
            prepare_next()
            o_ref[...] = _dot(cur_ref[...], w_ref[...])

        @pl.when(j >= IN_MAIN // IN_TN)
        def _():
            prepare_next()
            o_ref[...] = _dot(cur_ref[...], wt_ref[...])

    @pl.when(i % 2 == 0)
    def _():
        step(ha_ref, hb_ref)

    @pl.when(i % 2 == 1)
    def _():
        step(hb_ref, ha_ref)


def _in_projection(x2d, ln, mod, w_in_b, w_tail, layer, row_of_tile, tm):
    rows = x2d.shape[0]
    n_main = IN_MAIN // IN_TN
    n_tiles = rows // tm
    per_tile = tm // IN_PRE_ROWS
    assert U_W // IN_TN - 1 == per_tile and U_W - IN_MAIN == IN_TN
    nxt = lambda i, *_: jnp.minimum(i + 1, n_tiles - 1)
    once = pl.Buffered(1)
    return pl.pallas_call(
        _inproj_kernel,
        grid=(n_tiles, U_W // IN_TN),
        in_specs=[pl.BlockSpec((tm, D_MODEL), lambda i, j: (0, 0), pipeline_mode=once),
                  _mod_spec(0, row_of_tile, lambda *_: 0),
                  _mod_spec(1, row_of_tile, lambda *_: 0),
                  pl.BlockSpec((IN_PRE_ROWS, D_MODEL),
                               lambda i, j: (nxt(i) * per_tile + jnp.clip(j - 1, 0, per_tile - 1), 0)),
                  _mod_spec(0, row_of_tile, nxt),
                  _mod_spec(1, row_of_tile, nxt),
                  pl.BlockSpec((1, D_MODEL), lambda i, j: (0, 0)),
                  pl.BlockSpec((None, D_MODEL, IN_TN), lambda i, j: (layer, 0, jnp.minimum(j, n_main - 1))),
                  pl.BlockSpec((None, D_MODEL, U_W - IN_MAIN), lambda i, j: (layer, 0, 0))],
        out_specs=pl.BlockSpec((tm, IN_TN), lambda i, j: (i, j)),
        out_shape=jax.ShapeDtypeStruct((rows, U_W), F32),
        scratch_shapes=[pltpu.VMEM((tm, D_MODEL), BF16), pltpu.VMEM((tm, D_MODEL), BF16)],
        compiler_params=_cparams(("arbitrary", "arbitrary")),
        name="in_projection",
    )(x2d, mod, mod, x2d, mod, mod, ln, w_in_b, w_tail)


def _filter_kernel(z_ref, w1_ref, b1_ref, fr_ref, w2_ref, b2_ref, w3f_ref, w3b_ref, dec_ref,
                   chi_ref, clo_ref, shi_ref, slo_ref, kr_ref, ki_ref, kn_ref, *, L):
    h = jnp.sin(fr_ref[0:1, :] * (_dot3(z_ref[...], w1_ref[...]) + b1_ref[...]))
    h = jnp.sin(fr_ref[1:2, :] * (_dot3(h, w2_ref[...]) + b2_ref[...]))
    dec = dec_ref[...]
    row = lax.broadcasted_iota(jnp.int32, (L, 1), 0)
    kf = _dot3(h, w3f_ref[...]) * dec
    kb = jnp.where(row == 0, 0.0, _dot3(h, w3b_ref[...]) * dec)
    p_hi, p_lo = _split(kf + kb)
    m_hi, m_lo = _split(kf - kb)
    chi, clo = chi_ref[...], clo_ref[...]
    shi, slo = shi_ref[...], slo_ref[...]
    kc = _dot(chi, p_hi) + (_dot(clo, p_hi) + _dot(chi, p_lo))
    ks = _dot(shi, m_hi) + (_dot(slo, m_hi) + _dot(shi, m_lo))
    sign = jnp.where((row & 1) == 0, 1.0, -1.0)
    kn = jnp.sum((kf + kb) * sign, axis=0, keepdims=True)
    kr_ref[...] = kc * jnp.where(row == 0, 0.5 / L, 1.0 / L)
    ki_ref[...] = ks * (-1.0 / L)
    kn_ref[...] = kn * (0.5 / L)


def _hyena_filters(L, consts, w1p, b1, freq, w2, b2, w3):
    zpos, decay, c_hi, c_lo, s_hi, s_lo = consts
    full = lambda shape: pl.BlockSpec(shape, lambda o: (0,) * len(shape))
    return pl.pallas_call(
        functools.partial(_filter_kernel, L=L),
        grid=(HY_ORDER,),
        in_specs=[full((L, 64)), full((64, HY_FF)), full((1, HY_FF)), full((2, HY_FF)),
                  full((HY_FF, HY_FF)), full((1, HY_FF)),
                  pl.BlockSpec((HY_FF, HY_CH), lambda o: (0, o)),
                  pl.BlockSpec((HY_FF, HY_CH), lambda o: (0, HY_ORDER + o)),
                  full((L, HY_CH)), full((L, L)), full((L, L)), full((L, L)), full((L, L))],
        out_specs=[pl.BlockSpec((None, L, HY_CH), lambda o: (o, 0, 0)),
                   pl.BlockSpec((None, L, HY_CH), lambda o: (o, 0, 0)),
                   pl.BlockSpec((None, 1, HY_CH), lambda o: (o, 0, 0))],
        out_shape=[jax.ShapeDtypeStruct((HY_ORDER, L, HY_CH), F32),
                   jax.ShapeDtypeStruct((HY_ORDER, L, HY_CH), F32),
                   jax.ShapeDtypeStruct((HY_ORDER, 1, HY_CH), F32)],
        compiler_params=_cparams(("arbitrary",)),
        name="hyena_filters",
    )(zpos, w1p, b1, freq, w2, b2, w3, w3, decay, c_hi, c_lo, s_hi, s_lo)


HY_CG = 256
HY_STEP_ROWS = 1024


def _hyena_kernel(v_ref, x1_ref, x2_ref, wv_ref, w1_ref, w2_ref, bv_ref, b1_ref, b2_ref, skip_ref,
                  kr_ref, ki_ref, kn_ref, c_ref, sp_ref, spt_ref, o_ref, *, L):
    row = lax.broadcasted_iota(jnp.int32, (L, 1), 0)
    first = row == 0
    last = row == L - 1
    chains = [(b, slice(g * HY_CG, (g + 1) * HY_CG)) for b in range(v_ref.shape[0]) for g in range(HY_CH // HY_CG)]

    def short_conv(u_ref, w_ref, b_ref, b, cols):
        u = u_ref[b, :, cols]
        prev = jnp.where(first, 0.0, pltpu.roll(u, 1, axis=0))
        nxt = jnp.where(last, 0.0, pltpu.roll(u, L - 1, axis=0))
        return prev * w_ref[0:1, cols] + u * w_ref[1:2, cols] + nxt * w_ref[2:3, cols] + b_ref[:, cols]

    gate_refs = ((x1_ref, w1_ref, b1_ref), (x2_ref, w2_ref, b2_ref))
    z = [short_conv(v_ref, wv_ref, bv_ref, b, cols) for b, cols in chains]
    for o in range(HY_ORDER):
        zb = [zz.astype(BF16) for zz in z]
        a = [_dot(c_ref[...], x) for x in zb]
        s = [_dot(sp_ref[...], x) for x in zb]
        conv = []
        for n, (b, cols) in enumerate(chains):
            kr = kr_ref[o, :, cols]
            ki = ki_ref[o, :, cols]
            qt = a[n] * kr + s[n] * ki
            qb = s[n] * jnp.where(first, kn_ref[o, :, cols], kr) - a[n] * ki
            conv.append(_dot(c_ref[...], qt.astype(BF16)) + _dot(spt_ref[...], qb.astype(BF16)))
        z = [short_conv(*gate_refs[o], b, cols) * (conv[n] + z[n] * skip_ref[o:o + 1, cols])
             for n, (b, cols) in enumerate(chains)]
    for n, (b, cols) in enumerate(chains):
        o_ref[b, :, cols] = z[n].astype(BF16)


def _hyena_mixer(u3, conv_w, conv_b, skip, filt, dft, L):
    B = u3.shape[0]
    nb = max(1, HY_STEP_ROWS // L)
    kr, ki, kn = filt
    c_b, sp_b, spt_b = dft
    once = pl.Buffered(1)
    ublk = lambda part: pl.BlockSpec((nb, L, HY_CH), lambda b: (b, 0, part))
    wblk = lambda part: pl.BlockSpec((3, HY_CH), lambda b: (0, part))
    bblk = lambda part: pl.BlockSpec((1, HY_CH), lambda b: (0, part))
    fblk = lambda rows: pl.BlockSpec((HY_ORDER, rows, HY_CH), lambda b: (0, 0, 0), pipeline_mode=once)
    mat = pl.BlockSpec((L, L), lambda b: (0, 0), pipeline_mode=once)
    return pl.pallas_call(
        functools.partial(_hyena_kernel, L=L),
        grid=(B // nb,),
        in_specs=[ublk(0), ublk(1), ublk(2), wblk(0), wblk(1), wblk(2), bblk(0), bblk(1), bblk(2),
                  pl.BlockSpec((HY_ORDER, HY_CH), lambda b: (0, 0)),
                  fblk(L), fblk(L), fblk(1), mat, mat, mat],
        out_specs=pl.BlockSpec((nb, L, HY_CH), lambda b: (b, 0, 0)),
        out_shape=jax.ShapeDtypeStruct((B, L, HY_CH), BF16),
        compiler_params=_cparams(("arbitrary",)),
        name="hyena_mixer",
    )(u3, u3, u3, conv_w, conv_w, conv_w, conv_b, conv_b, conv_b, skip, kr, ki, kn, c_b, sp_b, spt_b)


GLA_PAIR = 2 * GLA_CHUNK
GLA_STATES = 2 * GLA_HEADS
GLA_SCAN = 256


def _gla_kernel(*refs, L, has_state, emit_state, out_slot, n_alias):
    it = iter(refs)
    qkvg_ref, a_ref, wa_ref, ba_ref, ng_ref, scan_ref = (next(it) for _ in range(6))
    st0_refs = (next(it), next(it)) if has_state else None
    for _ in range(n_alias):
        next(it)
    y_ref = next(it)
    stout_refs = (next(it), next(it)) if emit_state else None
    qd_s, ki_s, ke_s, qi_s, ku_s, et_s, vb_s, vt_s, o_s, st_s = (next(it) for _ in range(10))

    npair = L // GLA_PAIR
    pre = _dot3(a_ref[...], wa_ref[...]) + ba_ref[...]
    la = (jnp.minimum(pre, 0.0) - jnp.log(1.0 + jnp.exp(-jnp.abs(pre)))) * (1.0 / GLA_TAU)

    scan = scan_ref[...]
    pfx, tot = [], []
    for r in range(0, L, GLA_SCAN):
        x = la[r:r + GLA_SCAN]
        hi = x.astype(BF16)
        r1 = x - hi.astype(F32)
        mid = r1.astype(BF16)
        lo = (r1 - mid.astype(F32)).astype(BF16)
        res = _dot(scan, hi) + (_dot(scan, mid) + _dot(scan, lo))
        pfx.append(res[:GLA_SCAN])
        tot.append(res[GLA_SCAN:])
    pfx = jnp.concatenate(pfx, axis=0) if len(pfx) > 1 else pfx[0]
    tot = jnp.concatenate(tot, axis=0) if len(tot) > 1 else tot[0]

    upper = (lax.broadcasted_iota(jnp.int32, (L, 1), 0) & GLA_CHUNK) != 0
    q = qkvg_ref[:, 0:GLA_QK] * (GLA_DK ** -0.5)
    k = qkvg_ref[:, GLA_QK:2 * GLA_QK]
    for dirn in range(2):
        cols = slice(dirn * GLA_QK, (dirn + 1) * GLA_QK)
        if dirn == 0:
            cum = pfx[:, cols]
            rest = tot[:, cols] - cum
            first = jnp.logical_not(upper)
        else:
            rest = pfx[:, cols] - la[:, cols]
            cum = tot[:, cols] - rest
            first = upper
        et = jnp.exp(tot[:, cols])
        et_other = jnp.where(upper, pltpu.roll(et, GLA_CHUNK, axis=0), pltpu.roll(et, L - GLA_CHUNK, axis=0))
        qd = q * jnp.exp(cum)
        ke = k * jnp.exp(rest)
        et_s[dirn] = et
        qd_s[dirn] = qd.astype(BF16)
        ki_s[dirn] = (k * jnp.exp(-cum)).astype(BF16)
        ke_s[dirn] = ke.astype(BF16)
        qi_s[dirn] = jnp.where(first, qd, qd * et_other).astype(BF16)
        ku_s[dirn] = jnp.where(first, ke * et_other, ke).astype(BF16)
    vb_s[...] = qkvg_ref[:, 2 * GLA_QK:2 * GLA_QK + GLA_VW].astype(BF16)

    def transpose_v(n2, carry):
        r0 = pl.multiple_of(n2 * GLA_PAIR, GLA_PAIR)
        for h in range(GLA_HEADS):
            vp = qkvg_ref[pl.ds(r0, GLA_PAIR), 2 * GLA_QK + h * GLA_DV:2 * GLA_QK + (h + 1) * GLA_DV]
            vt_s[n2, h * GLA_DV:(h + 1) * GLA_DV, :] = vp.T.astype(BF16)
        return carry

    lax.fori_loop(0, npair, transpose_v, 0)

    if has_state:
        zero = jnp.zeros((GLA_DK, GLA_DV), F32)
        for dirn in range(2):
            for h in range(GLA_HEADS):
                s0 = st0_refs[dirn][h]
                both = jnp.concatenate([s0, zero] if h % 2 == 0 else [zero, s0], axis=0)
                idx = dirn * GLA_HEADS + h
                st_s[idx * GLA_DV:(idx + 1) * GLA_DV, :] = both.T
    else:
        st_s[...] = jnp.zeros_like(st_s)

    lane = lax.broadcasted_iota(jnp.int32, (1, 2 * GLA_DK), 1)
    head_lanes = (lane < GLA_DK, lane >= GLA_DK)
    ri = lax.broadcasted_iota(jnp.int32, (GLA_PAIR, GLA_PAIR), 0)
    ci = lax.broadcasted_iota(jnp.int32, (GLA_PAIR, GLA_PAIR), 1)
    same_chunk = ((ri ^ ci) & GLA_CHUNK) == 0
    keep = (ci <= ri, ci >= ri)

    def pair_step(n, carry):
        for dirn in range(2):
            n2 = n if dirn == 0 else npair - 1 - n
            r0 = pl.multiple_of(n2 * GLA_PAIR, GLA_PAIR)
            rows = pl.ds(r0, GLA_PAIR)
            for p in range(GLA_HEADS // 2):
                lanes = slice(p * 2 * GLA_DK, (p + 1) * 2 * GLA_DK)
                qd = qd_s[dirn, rows, lanes]
                qi = qi_s[dirn, rows, lanes]
                zeros = jnp.zeros_like(qd)
                dec = (et_s[dirn, pl.ds(r0, 8), lanes][0:1, :]
                       * et_s[dirn, pl.ds(r0 + GLA_CHUNK, 8), lanes][0:1, :])
                q2 = jnp.concatenate([jnp.where(head_lanes[0], qd, zeros), jnp.where(head_lanes[1], qd, zeros)],
                                     axis=0)
                att_d = _dot_nt(q2, ki_s[dirn, rows, lanes])
                att_o = _dot_nt(q2, ke_s[dirn, rows, lanes])
                s0 = (dirn * GLA_HEADS + 2 * p) * GLA_DV
                st2 = st_s[s0:s0 + 2 * GLA_DV, :]
                vt2 = vt_s[n2, 2 * p * GLA_DV:(2 * p + 2) * GLA_DV, :]
                st_s[s0:s0 + 2 * GLA_DV, :] = st2 * dec + _dot(vt2, ku_s[dirn, rows, lanes])
                for hh in range(2):
                    h = 2 * p + hh
                    hr = slice(hh * GLA_PAIR, (hh + 1) * GLA_PAIR)
                    att = jnp.where(keep[dirn], jnp.where(same_chunk, att_d[hr], att_o[hr]), 0.0)
                    qm = jnp.where(head_lanes[hh], qi, zeros)
                    o = (_dot(att.astype(BF16), vb_s[rows, h * GLA_DV:(h + 1) * GLA_DV])
                         + _dot_nt(qm, st2[hh * GLA_DV:(hh + 1) * GLA_DV].astype(BF16)))
                    o_s[dirn, rows, h * GLA_DV:(h + 1) * GLA_DV] = o
        return carry

    lax.fori_loop(0, npair, pair_step, 0, unroll=2)

    for h in range(GLA_HEADS):
        cols = slice(h * GLA_DV, (h + 1) * GLA_DV)
        o = _rms(o_s[0, :, cols] + o_s[1, :, cols], ng_ref[...])
        g = qkvg_ref[:, 2 * GLA_QK + GLA_VW + h * GLA_DV:2 * GLA_QK + GLA_VW + (h + 1) * GLA_DV]
        y_ref[:, cols] = (o * (g * jax.nn.sigmoid(g))).astype(BF16)
    if emit_state:
        for dirn in range(2):
            dst = stout_refs[dirn]
            if out_slot is not None:
                for d in range(DEPTH):
                    if d != out_slot:
                        dst[d] = jnp.zeros(dst.shape[1:], F32)
                dst = dst.at[out_slot]
            for h in range(GLA_HEADS):
                idx = dirn * GLA_HEADS + h
                half = (h % 2) * GLA_DK
                dst[h] = st_s[idx * GLA_DV:(idx + 1) * GLA_DV, :].T[half:half + GLA_DK, :]


def _gla_scan_matrix():
    i = np.arange(GLA_SCAN)
    same = (i[:, None] // GLA_CHUNK) == (i[None, :] // GLA_CHUNK)
    tri = same & (i[None, :] <= i[:, None])
    return jnp.asarray(np.concatenate([tri, same], axis=0).astype(np.float32)).astype(BF16)


def _gla_mixer(u3, wa_p, ba_p, norm_g, st0, layer, L, emit_state):
    B = u3.shape[0]
    has_state = st0 is not None
    head_state = (GLA_HEADS, GLA_DK, GLA_DV)
    in_specs = [pl.BlockSpec((None, L, 1536), lambda b: (b, 0, U_GLA // 1536)),
                pl.BlockSpec((None, L, 128), lambda b: (b, 0, U_GLA_A // 128)),
                pl.BlockSpec((128, 2 * GLA_QK), lambda b: (0, 0)),
                pl.BlockSpec((1, 2 * GLA_QK), lambda b: (0, 0)),
                pl.BlockSpec((1, GLA_DV), lambda b: (0, 0)),
                pl.BlockSpec((2 * GLA_SCAN, GLA_SCAN), lambda b: (0, 0))]
    args = [u3, u3, wa_p, ba_p, norm_g, _gla_scan_matrix()]
    if has_state:
        in_specs += [pl.BlockSpec((None, None) + head_state, lambda b: (b, layer, 0, 0, 0))] * 2
        args += list(st0)
    out_specs = [pl.BlockSpec((None, L, GLA_VW), lambda b: (b, 0, 0))]
    out_shape = [jax.ShapeDtypeStruct((B, L, GLA_VW), BF16)]
    aliases = {}
    fresh = emit_state is not None and all(prev is None for prev in emit_state)
    if emit_state is not None:
        slot_spec = (pl.BlockSpec((None, DEPTH) + head_state, lambda b: (b, 0, 0, 0, 0)) if fresh else
                     pl.BlockSpec((None, None) + head_state, lambda b: (b, layer, 0, 0, 0)))
        out_specs += [slot_spec] * 2
        out_shape += [jax.ShapeDtypeStruct((B, DEPTH) + head_state, F32)] * 2
        for k, prev in enumerate(emit_state):
            if prev is not None:
                aliases[len(args)] = 1 + k
                in_specs.append(pl.BlockSpec(memory_space=pl.ANY))
                args.append(prev)
    npair = L // GLA_PAIR
    qk_bf16 = pltpu.VMEM((2, L, GLA_QK), BF16)
    res = pl.pallas_call(
        functools.partial(_gla_kernel, L=L, has_state=has_state, emit_state=emit_state is not None,
                          out_slot=layer if fresh else None, n_alias=len(aliases)),
        grid=(B,),
        in_specs=in_specs,
        out_specs=out_specs,
        out_shape=out_shape,
        input_output_aliases=aliases,
        scratch_shapes=[qk_bf16,
                        qk_bf16,
                        qk_bf16,
                        qk_bf16,
                        qk_bf16,
                        pltpu.VMEM((2, L, GLA_QK), F32),
                        pltpu.VMEM((L, GLA_VW), BF16),
                        pltpu.VMEM((npair, GLA_VW, GLA_PAIR), BF16),
                        pltpu.VMEM((2, L, GLA_VW), F32),
                        pltpu.VMEM((GLA_STATES * GLA_DV, 2 * GLA_DK), F32)],
        compiler_params=_cparams(("arbitrary",)),
        name="gla_mixer",
    )(*args)
    return (res[0], res[1], res[2]) if emit_state is not None else (res[0], None, None)


MLA_HW = 256
MLA_TQ = 512
MLA_KCH = 256


def _rope(x, cos, sin):
    lane = lax.broadcasted_iota(jnp.int32, x.shape, 1)
    partner = jnp.where((lane & 16) == 0, pltpu.roll(x, 112, axis=1), pltpu.roll(x, 16, axis=1))
    return x * cos + partner * sin


def _mla_kernel(*refs, L, Lk, tq, rope, emit, out_slot, n_alias):
    it = iter(refs)
    u_ref, qn_ref, wq_ref, kvn_ref, wkv_ref = (next(it) for _ in range(5))
    cos_ref, sin_ref, cckv_ref, ckr_ref = (next(it) for _ in range(4)) if rope else (None,) * 4
    for _ in range(n_alias):
        next(it)
    y_ref = next(it)
    ckv_out, kr_out = (next(it), next(it)) if emit else (None, None)
    kf_s, v_s = next(it), next(it)

    qi = pl.program_id(1)

    def project(keys, rope_part, r0):
        kv = _dot(keys, wkv_ref[...])
        for h in range(MLA_HEADS):
            kf_s[h, r0:r0 + MLA_KCH, 0:MLA_NOPE] = kv[:, h * 256:h * 256 + MLA_NOPE].astype(BF16)
            kf_s[h, r0:r0 + MLA_KCH, MLA_NOPE:MLA_HW] = rope_part
            v_s[h, r0:r0 + MLA_KCH, :] = kv[:, h * 256 + MLA_NOPE:(h + 1) * 256].astype(BF16)

    @pl.when(qi == 0)
    def _():
        if emit and out_slot is not None:
            for dst in (ckv_out, kr_out):
                for d in range(DEPTH):
                    if d != out_slot:
                        dst[d] = jnp.zeros(dst.shape[1:], F32)
        ckv_dst, kr_dst = (ckv_out, kr_out) if out_slot is None else (ckv_out.at[out_slot], kr_out.at[out_slot])
        for r0 in range(0, L, MLA_KCH):
            ckvn = _rms(u_ref[r0:r0 + MLA_KCH, MLA_Q_RANK:MLA_Q_RANK + MLA_KV_RANK], kvn_ref[...])
            kr = u_ref[r0:r0 + MLA_KCH, 640:768]
            kr = jnp.where(lax.broadcasted_iota(jnp.int32, kr.shape, 1) < MLA_ROPE, kr, 0.0)
            if emit:
                ckv_dst[r0:r0 + MLA_KCH, :] = ckvn
                kr_dst[r0:r0 + MLA_KCH, :] = kr[:, 0:MLA_ROPE]
            if rope:
                kr = _rope(kr, cos_ref[r0:r0 + MLA_KCH, :], sin_ref[r0:r0 + MLA_KCH, :])
            project(ckvn.astype(BF16), kr.astype(BF16), r0)
        if rope:
            for r0 in range(0, Lk - L, MLA_KCH):
                ckr = ckr_ref[r0:r0 + MLA_KCH, :]
                ckr = jnp.concatenate([ckr, jnp.zeros_like(ckr)], axis=1)
                project(cckv_ref[r0:r0 + MLA_KCH, :].astype(BF16), ckr.astype(BF16), L + r0)

    q0 = pl.multiple_of(qi * tq, tq)
    cqn = _rms(u_ref[pl.ds(q0, tq), 0:MLA_Q_RANK], qn_ref[...]).astype(BF16)
    q = _dot(cqn, wq_ref[...]) * ((MLA_NOPE + MLA_ROPE) ** -0.5)
    if rope:
        cos = cos_ref[pl.ds(q0, tq), :]
        sin = sin_ref[pl.ds(q0, tq), :]
    for h in range(MLA_HEADS):
        q_nope = q[:, h * MLA_HW:h * MLA_HW + MLA_NOPE]
        q_rope = q[:, h * MLA_HW + MLA_NOPE:(h + 1) * MLA_HW]
        if rope:
            q_rope = _rope(q_rope, cos, sin)
        qh = jnp.concatenate([q_nope, q_rope], axis=1).astype(BF16)
        s = _dot_nt(qh, kf_s[h])
        p = jnp.exp(s - jnp.max(s, axis=-1, keepdims=True))
        denom = jnp.sum(p, axis=-1, keepdims=True)
        o = _dot(p.astype(BF16), v_s[h])
        y_ref[:, h * MLA_V:(h + 1) * MLA_V] = (o / denom).astype(BF16)


def _mla_mixer(u3, q_norm, wq_p, kv_norm, wkv, rope_args, layer, L, emit):
    B = u3.shape[0]
    rope = rope_args is not None
    Lk = L + (rope_args[2].shape[2] if rope else 0)
    tq = min(MLA_TQ, L)
    const = lambda shape: pl.BlockSpec(shape, lambda b, i: (0,) * len(shape))
    in_specs = [pl.BlockSpec((None, L, 768), lambda b, i: (b, 0, U_MLA // 768)),
                const((1, MLA_Q_RANK)), const((MLA_Q_RANK, MLA_HEADS * MLA_HW)),
                const((1, MLA_KV_RANK)), const((MLA_KV_RANK, MLA_HEADS * 256))]
    args = [u3, q_norm, wq_p, kv_norm, wkv]
    if rope:
        cos, sin, cckv, ckr = rope_args
        in_specs += [const((L, 128)), const((L, 128)),
                     pl.BlockSpec((None, None, Lk - L, MLA_KV_RANK), lambda b, i: (b, layer, 0, 0)),
                     pl.BlockSpec((None, None, Lk - L, MLA_ROPE), lambda b, i: (b, layer, 0, 0))]
        args += [cos, sin, cckv, ckr]
    out_specs = [pl.BlockSpec((None, tq, MLA_HEADS * MLA_V), lambda b, i: (b, i, 0))]
    out_shape = [jax.ShapeDtypeStruct((B, L, MLA_HEADS * MLA_V), BF16)]
    aliases = {}
    fresh = emit is not None and all(prev is None for prev in emit)
    if emit is not None:
        slot = ((None, DEPTH), lambda b, i: (b, 0, 0, 0)) if fresh else ((None, None), lambda b, i: (b, layer, 0, 0))
        out_specs += [pl.BlockSpec(slot[0] + (L, MLA_KV_RANK), slot[1]),
                      pl.BlockSpec(slot[0] + (L, MLA_ROPE), slot[1])]
        out_shape += [jax.ShapeDtypeStruct((B, DEPTH, L, MLA_KV_RANK), F32),
                      jax.ShapeDtypeStruct((B, DEPTH, L, MLA_ROPE), F32)]
        for k, prev in enumerate(emit):
            if prev is not None:
                aliases[len(args)] = 1 + k
                in_specs.append(pl.BlockSpec(memory_space=pl.ANY))
                args.append(prev)
    res = pl.pallas_call(
        functools.partial(_mla_kernel, L=L, Lk=Lk, tq=tq, rope=rope, emit=emit is not None,
                          out_slot=layer if fresh else None, n_alias=len(aliases)),
        grid=(B, L // tq),
        in_specs=in_specs,
        out_specs=out_specs,
        out_shape=out_shape,
        input_output_aliases=aliases,
        scratch_shapes=[pltpu.VMEM((MLA_HEADS, Lk, MLA_HW), BF16),
                        pltpu.VMEM((MLA_HEADS, Lk, MLA_V), BF16)],
        compiler_params=_cparams(("arbitrary", "arbitrary")),
        name="mla_mixer",
    )(*args)
    return res if emit is not None else (res[0], None, None)


def _outproj_kernel(x_ref, g_ref, yh_ref, yg_ref, ym_ref, wh_ref, wg_ref, wm_ref, o_ref):
    y = _dot(yh_ref[...], wh_ref[...]) + _dot(yg_ref[...], wg_ref[...]) + _dot(ym_ref[...], wm_ref[...])
    o_ref[...] = x_ref[...] + g_ref[...] * y


def _out_projection(x2d, mod, y_hy, y_gla, y_mla, w_out_b, layer, row_of_tile, tm):
    rows = x2d.shape[0]
    once = pl.Buffered(1)
    return pl.pallas_call(
        _outproj_kernel,
        grid=(rows // tm,),
        in_specs=[pl.BlockSpec((tm, D_MODEL), lambda i: (i, 0)),
                  _mod_spec(2, row_of_tile),
                  pl.BlockSpec((tm, HY_CH), lambda i: (i, 0)),
                  pl.BlockSpec((tm, GLA_VW), lambda i: (i, 0)),
                  pl.BlockSpec((tm, MLA_HEADS * MLA_V), lambda i: (i, 0)),
                  pl.BlockSpec((None, HY_CH, D_MODEL), lambda i: (layer, 0, 0), pipeline_mode=once),
                  pl.BlockSpec((None, GLA_VW, D_MODEL), lambda i: (layer, 1, 0), pipeline_mode=once),
                  pl.BlockSpec((None, MLA_HEADS * MLA_V, D_MODEL), lambda i: (layer, 1, 0), pipeline_mode=once)],
        out_specs=pl.BlockSpec((tm, D_MODEL), lambda i: (i, 0)),
        out_shape=jax.ShapeDtypeStruct((rows, D_MODEL), F32),
        compiler_params=_cparams(("arbitrary",)),
        name="out_projection",
    )(x2d, mod, y_hy, y_gla, y_mla, w_out_b, w_out_b, w_out_b)


FFN_TF = 512


def _ffn_kernel(x_ref, ln_ref, sh_ref, sc_ref, g_ref, wg_ref, wu_ref, wo_ref, lnf_ref, o_ref,
                h_ref, acc_ref, *, final):
    f = pl.program_id(1)

    @pl.when(f == 0)
    def _():
        _adaln_all(x_ref, ln_ref[...] * (1.0 + sc_ref[...]), sh_ref[...], h_ref)
        acc_ref[...] = jnp.zeros_like(acc_ref)

    h = h_ref[...]
    gate = _dot(h, wg_ref[...])
    up = _dot(h, wu_ref[...])
    act = (gate * jax.nn.sigmoid(gate) * up).astype(BF16)
    acc_ref[...] += _dot(act, wo_ref[...])

    @pl.when(f == pl.num_programs(1) - 1)
    def _():
        def body(i, carry):
            rows = pl.ds(pl.multiple_of(i * ADALN_ROWS, ADALN_ROWS), ADALN_ROWS)
            x = x_ref[rows, :] + g_ref[...] * acc_ref[rows, :]
            o_ref[rows, :] = _rms(x, lnf_ref[...]) if final else x
            return carry

        lax.fori_loop(0, x_ref.shape[0] // ADALN_ROWS, body, 0, unroll=4)


def _ffn(x2d, ln, mod, w_in_b, w_out_b, layer, ln_final, row_of_tile, tm, final):
    rows = x2d.shape[0]
    nf = D_FF // FFN_TF
    return pl.pallas_call(
        functools.partial(_ffn_kernel, final=final),
        grid=(rows // tm, nf),
        in_specs=[pl.BlockSpec((tm, D_MODEL), lambda i, f: (i, 0)),
                  pl.BlockSpec((1, D_MODEL), lambda i, f: (0, 0)),
                  _mod_spec(3, row_of_tile),
                  _mod_spec(4, row_of_tile),
                  _mod_spec(5, row_of_tile),
                  pl.BlockSpec((None, D_MODEL, FFN_TF), lambda i, f: (layer, 0, f)),
                  pl.BlockSpec((None, D_MODEL, FFN_TF), lambda i, f: (layer, 0, nf + f)),
                  pl.BlockSpec((None, FFN_TF, D_MODEL), lambda i, f: (layer, f, 0)),
                  pl.BlockSpec((1, D_MODEL), lambda i, f: (0, 0))],
        out_specs=pl.BlockSpec((tm, D_MODEL), lambda i, f: (i, 0)),
        out_shape=jax.ShapeDtypeStruct((rows, D_MODEL), F32),
        scratch_shapes=[pltpu.VMEM((tm, D_MODEL), BF16), pltpu.VMEM((tm, D_MODEL), F32)],
        compiler_params=_cparams(("arbitrary", "arbitrary")),
        name="ffn",
    )(x2d, ln, mod, mod, mod, w_in_b, w_in_b, w_out_b, ln_final)


def _dft_tables(L):
    k = np.arange(L)
    ang = np.pi * ((k[:, None] * k[None, :]) % (2 * L)) / L
    c = np.cos(ang)
    s = np.sin(ang)
    sp = s.copy()
    sp[0, :] = 1.0 - 2.0 * (k % 2)
    as32 = lambda a: jnp.asarray(a.astype(np.float32))
    return as32(c), as32(s), as32(sp), as32(sp.T.copy())


def _hyena_consts(L):
    c, s, sp, spt = _dft_tables(L)
    c_hi, c_lo = _split(c)
    s_hi, s_lo = _split(s)
    f32 = np.float32
    t = np.linspace(0.0, 1.0, L, dtype=f32)[:, None]
    w = (f32(2.0 * math.pi) * np.arange(L, dtype=f32)[:, None] / f32(L)).astype(f32)
    f = np.linspace(1e-4, HY_BANDS - 1, HY_BANDS, dtype=f32)
    zpos = np.concatenate([t, np.cos(f * w), -np.sin(f * w)], axis=-1).astype(f32)
    zpos = np.pad(zpos, ((0, 0), (0, 64 - HY_EMB)))
    min_decay = math.log(HY_TARGET) / HY_SLOW_DECAY
    max_decay = math.log(HY_TARGET) / HY_FAST_DECAY
    delta = np.abs(np.linspace(min_decay, max_decay, HY_CH, dtype=f32))
    decay = np.exp(-t * delta).astype(f32)
    filt_consts = (jnp.asarray(zpos), jnp.asarray(decay), c_hi, c_lo, s_hi, s_lo)
    main_consts = (c_hi, sp.astype(BF16), spt.astype(BF16))
    return filt_consts, main_consts


def _rope_tables(L):
    t = jnp.arange(L)
    half = MLA_ROPE // 2
    inv = ROPE_THETA ** (-jnp.arange(0, half, 2, dtype=F32) / half)
    ang_r = (t // GRID_W).astype(F32)[:, None] * inv
    ang_c = (t % GRID_W).astype(F32)[:, None] * inv
    cr, sr, cc, sc = jnp.cos(ang_r), jnp.sin(ang_r), jnp.cos(ang_c), jnp.sin(ang_c)
    cos = jnp.concatenate([cr, cr, cc, cc, jnp.ones((L, 128 - MLA_ROPE), F32)], axis=1)
    sin = jnp.concatenate([-sr, sr, -sc, sc, jnp.zeros((L, 128 - MLA_ROPE), F32)], axis=1)
    return cos, sin


def _prep_w_in_tail(w):
    mla0 = IN_MAIN + 2 * GLA_LOWRANK
    pad = jnp.zeros((DEPTH, D_MODEL, IN_TN - (w.shape[2] - IN_MAIN)), w.dtype)
    return jnp.concatenate([w[:, :, mla0:], w[:, :, IN_MAIN:mla0], pad], axis=2).astype(BF16)


def _prep_w_uq(w):
    w = w.reshape(MLA_Q_RANK, MLA_HEADS, MLA_NOPE + MLA_ROPE)
    w = jnp.pad(w, ((0, 0), (0, 0), (0, MLA_HW - MLA_NOPE - MLA_ROPE)))
    return w.reshape(MLA_Q_RANK, MLA_HEADS * MLA_HW).astype(BF16)


def _prep_gla_decay(wa_f, ba_f, wa_b, ba_b):
    wa = jnp.zeros((128, 2 * GLA_QK), F32)
    wa = wa.at[U_GLA_A_LANE:U_GLA_A_LANE + GLA_LOWRANK, 0:GLA_QK].set(wa_f)
    wa = wa.at[U_GLA_A_LANE + GLA_LOWRANK:U_GLA_A_LANE + 2 * GLA_LOWRANK, GLA_QK:].set(wa_b)
    return wa, jnp.concatenate([ba_f, ba_b])[None, :]


def _trunk_layer(x2d, B, L, mod, row_of_tile, lw, consts, ctx, caches, final, ln_final):
    tm = 1024
    tm_out = 512
    u = _in_projection(x2d, lw['ln_mix'], mod, lw['w_in'], lw['w_in_tail'], lw['layer'], row_of_tile(tm), tm)
    u3 = u.reshape(B, L, U_W)
    filt_consts, main_consts = consts['hyena']
    filt = _hyena_filters(L, filt_consts, lw['hy_w1'], lw['hy_b1'], lw['hy_freq'], lw['hy_w2'],
                          lw['hy_b2'], lw['hy_w3'])
    y_hy = _hyena_mixer(u3, lw['hy_conv_w'], lw['hy_conv_b'], lw['hy_skip'], filt, main_consts, L)
    if ctx is None:
        y_gla, s_f, s_b = _gla_mixer(u3, lw['gla_wa'], lw['gla_ba'], lw['gla_norm'], None, lw['layer'], L, caches[2:])
        y_mla, ckv, krope = _mla_mixer(u3, lw['mla_q_norm'], lw['mla_w_uq'], lw['mla_kv_norm'],
                                       lw['mla_w_ukv'], None, lw['layer'], L, caches[:2])
        extras = (ckv, krope, s_f, s_b)
    else:
        cache_ckv, cache_krope, s0f, s0b = ctx
        y_gla, _, _ = _gla_mixer(u3, lw['gla_wa'], lw['gla_ba'], lw['gla_norm'], (s0f, s0b), lw['layer'], L, None)
        cos, sin = consts['rope']
        y_mla, _, _ = _mla_mixer(u3, lw['mla_q_norm'], lw['mla_w_uq'], lw['mla_kv_norm'], lw['mla_w_ukv'],
                                 (cos, sin, cache_ckv, cache_krope), lw['layer'], L, None)
        extras = None
    rows = B * L
    x2d = _out_projection(x2d, mod, y_hy.reshape(rows, -1), y_gla.reshape(rows, -1), y_mla.reshape(rows, -1),
                          lw['w_out'], lw['layer'], row_of_tile(tm_out), tm_out)
    x2d = _ffn(x2d, lw['ln_ffn'], mod, lw['w_ffn_in'], lw['w_ffn_out'], lw['layer'], ln_final, row_of_tile(tm_out),
               tm_out, final)
    return x2d, extras


def kernel(x_prompt, x_sample, cache_mla_ckv, cache_mla_krope, state_gla_fwd, state_gla_bwd, c, c_ctx, w_mod, b_mod, ln_mix, w_in, hy_conv_w, hy_conv_b, hy_filt_w1, hy_filt_b1, hy_filt_freq, hy_filt_w2, hy_filt_b2, hy_filt_w3, hy_skip, gla_wa_f, gla_ba_f, gla_wa_b, gla_ba_b, gla_norm, mla_q_norm, mla_w_uq, mla_kv_norm, mla_w_ukv, w_out, ln_ffn, w_ffn_in, w_ffn_out, ln_final):
    Bc, Lc, _ = x_prompt.shape
    Bl, Ll, _ = x_sample.shape
    assert 1 + Bl <= MOD_ROWS

    c_all = jnp.concatenate([c_ctx[None, :], c, jnp.zeros((MOD_ROWS - 1 - Bl, D_MODEL), F32)], axis=0)
    mod_all = _modulation(c_all, w_mod, b_mod).reshape(DEPTH, MOD_ROWS, 6, 1, D_MODEL)

    consts_ctx = {'hyena': _hyena_consts(Lc)}
    consts_lat = {'hyena': _hyena_consts(Ll), 'rope': _rope_tables(Ll)}
    ctx_rows = lambda tm: (lambda i: 0)
    lat_rows = lambda tm: (lambda i: 1 + (i * tm) // Ll)
    lnf = ln_final[None, :]

    x_ctx = x_prompt.reshape(Bc * Lc, D_MODEL)
    x_lat = x_sample.reshape(Bl * Ll, D_MODEL)
    w_in_b = w_in.astype(BF16)
    w_in_tail = _prep_w_in_tail(w_in)
    w_out_b = w_out.astype(BF16)
    w_ffn_in_b = w_ffn_in.astype(BF16)
    w_ffn_out_b = w_ffn_out.astype(BF16)
    caches = (None, None, None, None)
    for l in range(DEPTH):
        wa, ba = _prep_gla_decay(gla_wa_f[l], gla_ba_f[l], gla_wa_b[l], gla_ba_b[l])
        lw = {
            'layer': l, 'ln_mix': ln_mix[l][None, :], 'w_in': w_in_b, 'w_in_tail': w_in_tail,
            'hy_conv_w': hy_conv_w[l], 'hy_conv_b': hy_conv_b[l][None, :],
            'hy_w1': jnp.pad(hy_filt_w1[l], ((0, 64 - HY_EMB), (0, 0))), 'hy_b1': hy_filt_b1[l][None, :],
            'hy_freq': hy_filt_freq[l], 'hy_w2': hy_filt_w2[l], 'hy_b2': hy_filt_b2[l][None, :],
            'hy_w3': hy_filt_w3[l], 'hy_skip': hy_skip[l],
            'gla_wa': wa, 'gla_ba': ba, 'gla_norm': gla_norm[l][None, :],
            'mla_q_norm': mla_q_norm[l][None, :], 'mla_w_uq': _prep_w_uq(mla_w_uq[l]),
            'mla_kv_norm': mla_kv_norm[l][None, :], 'mla_w_ukv': mla_w_ukv[l].astype(BF16),
            'w_out': w_out_b, 'ln_ffn': ln_ffn[l][None, :],
            'w_ffn_in': w_ffn_in_b, 'w_ffn_out': w_ffn_out_b,
        }
        final = l == DEPTH - 1
        mod = mod_all[l]
        x_ctx, caches = _trunk_layer(x_ctx, Bc, Lc, mod, ctx_rows, lw, consts_ctx, None, caches, final, lnf)
        ctx = (cache_mla_ckv, cache_mla_krope, state_gla_fwd, state_gla_bwd)
        x_lat, _ = _trunk_layer(x_lat, Bl, Ll, mod, lat_rows, lw, consts_lat, ctx, None, final, lnf)
    return (x_ctx.reshape(Bc, Lc, D_MODEL), x_lat.reshape(Bl, Ll, D_MODEL)) + tuple(caches)
```

```python
import functools
import math

import numpy as np
import jax
import jax.numpy as jnp
from jax import lax
from jax.experimental import pallas as pl
from jax.experimental.pallas import tpu as pltpu

F32 = jnp.float32
BF16 = jnp.bfloat16

D_MODEL = 2048
DEPTH = 2
GRID_W = 64
EPS = 1e-6
HY_CH = 512
HY_ORDER = 2
HY_BANDS = 16
HY_EMB = 1 + 2 * HY_BANDS
HY_FF = 64
HY_FAST_DECAY = 0.3
HY_SLOW_DECAY = 1.5
HY_TARGET = 1e-2
HY_IN = 3 * HY_CH
GLA_HEADS = 4
GLA_DK = 64
GLA_DV = 128
GLA_LOWRANK = 16
GLA_TAU = 16.0
GLA_CHUNK = 64
GLA_QK = GLA_HEADS * GLA_DK
GLA_VW = GLA_HEADS * GLA_DV
MLA_HEADS = 8
MLA_Q_RANK = 384
MLA_KV_RANK = 256
MLA_NOPE = 128
MLA_ROPE = 64
MLA_V = 128
ROPE_THETA = 10000.0
D_FF = -(-8 * D_MODEL // (3 * 256)) * 256

U_HY = 0
U_GLA = 1536
U_MLA = 3072
U_GLA_A = 3712
U_GLA_A_LANE = 64
U_W = 3840
MOD_ROWS = 16

VMEM_LIMIT_V7X = 48 * 1024 * 1024


def _cparams(sem):
    return pltpu.CompilerParams(dimension_semantics=sem, vmem_limit_bytes=VMEM_LIMIT_V7X)


def _dot(a, b):
    return jnp.dot(a, b, preferred_element_type=F32)


def _dot_nt(a, b):
    return lax.dot_general(a, b, (((1,), (1,)), ((), ())), preferred_element_type=F32)


def _split(x):
    hi = x.astype(BF16)
    lo = (x - hi.astype(F32)).astype(BF16)
    return hi, lo


def _dot3(a, b):
    a_hi, a_lo = _split(a)
    b_hi, b_lo = _split(b)
    return _dot(a_hi, b_hi) + (_dot(a_lo, b_hi) + _dot(a_hi, b_lo))


def _rms(x, g):
    ms = jnp.mean(x * x, axis=-1, keepdims=True)
    return x * lax.rsqrt(ms + EPS) * g


ADALN_ROWS = 16


def _adaln_group(x_ref, gain, shift, h_ref, r):
    rows = pl.ds(pl.multiple_of(r, ADALN_ROWS), ADALN_ROWS)
    x = x_ref[rows, :]
    ms = jnp.mean(x * x, axis=-1, keepdims=True)
    h_ref[rows, :] = (x * lax.rsqrt(ms + EPS) * gain + shift).astype(BF16)


def _adaln_all(x_ref, gain, shift, h_ref):
    def body(i, carry):
        _adaln_group(x_ref, gain, shift, h_ref, i * ADALN_ROWS)
        return carry

    lax.fori_loop(0, x_ref.shape[0] // ADALN_ROWS, body, 0, unroll=4)


def _mod_kernel(c_ref, w_ref, b_ref, o_ref):
    c = c_ref[...]
    s = (c * jax.nn.sigmoid(c)).astype(BF16)
    o_ref[...] = _dot(s, w_ref[...].astype(BF16)) + b_ref[...]


def _modulation(c_all, w_mod, b_mod):
    tn = 1024
    n6 = 6 * D_MODEL
    return pl.pallas_call(
        _mod_kernel,
        grid=(DEPTH, n6 // tn),
        in_specs=[pl.BlockSpec((MOD_ROWS, D_MODEL), lambda l, j: (0, 0)),
                  pl.BlockSpec((None, D_MODEL, tn), lambda l, j: (l, 0, j)),
                  pl.BlockSpec((None, 1, tn), lambda l, j: (l, 0, j))],
        out_specs=pl.BlockSpec((None, MOD_ROWS, tn), lambda l, j: (l, 0, j)),
        out_shape=jax.ShapeDtypeStruct((DEPTH, MOD_ROWS, n6), F32),
        compiler_params=_cparams(("arbitrary", "arbitrary")),
        name="modulation",
    )(c_all, w_mod, b_mod.reshape(DEPTH, 1, n6))


def _mod_spec(chunk, row_of_tile, tile_of_step=lambda i, *_: i):
    return pl.BlockSpec((None, None, 1, D_MODEL), lambda *g: (row_of_tile(tile_of_step(*g)), chunk, 0, 0))


IN_TN = 768
IN_MAIN = 3072
IN_PRE_ROWS = 256


def _inproj_kernel(x0_ref, sh0_ref, sc0_ref, xn_ref, shn_ref, scn_ref, ln_ref, w_ref, wt_ref, o_ref, ha_ref, hb_ref):
    i = pl.program_id(0)
    j = pl.program_id(1)

    @pl.when((i == 0) & (j == 0))
    def _():
        _adaln_all(x0_ref, ln_ref[...] * (1.0 + sc0_ref[...]), sh0_ref[...], ha_ref)

    gain = ln_ref[...] * (1.0 + scn_ref[...])
    shift = shn_ref[...]
    base = jnp.clip(j - 1, 0, 3) * IN_PRE_ROWS

    def step(cur_ref, nxt_ref):
        def prepare_next():
            for t in range(IN_PRE_ROWS // ADALN_ROWS):
                rows = pl.ds(t * ADALN_ROWS, ADALN_ROWS)
                x = xn_ref[rows, :]
                ms = jnp.mean(x * x, axis=-1, keepdims=True)
                out_rows = pl.ds(pl.multiple_of(base + t * ADALN_ROWS, ADALN_ROWS), ADALN_ROWS)
                nxt_ref[out_rows, :] = (x * lax.rsqrt(ms + EPS) * gain + shift).astype(BF16)

        @pl.when(j < IN_MAIN // IN_TN)
        def _():
            prepare_next()
            o_ref[...] = _dot(cur_ref[...], w_ref[...])

        @pl.when(j >= IN_MAIN // IN_TN)
        def _():
            prepare_next()
            o_ref[...] = _dot(cur_ref[...], wt_ref[...])

    @pl.when(i % 2 == 0)
    def _():
        step(ha_ref, hb_ref)

    @pl.when(i % 2 == 1)
    def _():
        step(hb_ref, ha_ref)


def _in_projection(x2d, ln, mod, w_in_b, w_tail, layer, row_of_tile, tm):
    rows = x2d.shape[0]
    n_main = IN_MAIN // IN_TN
    n_tiles = rows // tm
    per_tile = tm // IN_PRE_ROWS
    assert U_W // IN_TN - 1 == per_tile and U_W - IN_MAIN == IN_TN
    nxt = lambda i, *_: jnp.minimum(i + 1, n_tiles - 1)
    once = pl.Buffered(1)
    return pl.pallas_call(
        _inproj_kernel,
        grid=(n_tiles, U_W // IN_TN),
        in_specs=[pl.BlockSpec((tm, D_MODEL), lambda i, j: (0, 0), pipeline_mode=once),
                  _mod_spec(0, row_of_tile, lambda *_: 0),
                  _mod_spec(1, row_of_tile, lambda *_: 0),
                  pl.BlockSpec((IN_PRE_ROWS, D_MODEL),
                               lambda i, j: (nxt(i) * per_tile + jnp.clip(j - 1, 0, per_tile - 1), 0)),
                  _mod_spec(0, row_of_tile, nxt),
                  _mod_spec(1, row_of_tile, nxt),
                  pl.BlockSpec((1, D_MODEL), lambda i, j: (0, 0)),
                  pl.BlockSpec((None, D_MODEL, IN_TN), lambda i, j: (layer, 0, jnp.minimum(j, n_main - 1))),
                  pl.BlockSpec((None, D_MODEL, U_W - IN_MAIN), lambda i, j: (layer, 0, 0))],
        out_specs=pl.BlockSpec((tm, IN_TN), lambda i, j: (i, j)),
        out_shape=jax.ShapeDtypeStruct((rows, U_W), F32),
        scratch_shapes=[pltpu.VMEM((tm, D_MODEL), BF16), pltpu.VMEM((tm, D_MODEL), BF16)],
        compiler_params=_cparams(("arbitrary", "arbitrary")),
        name="in_projection",
    )(x2d, mod, mod, x2d, mod, mod, ln, w_in_b, w_tail)


def _filter_kernel(z_ref, w1_ref, b1_ref, fr_ref, w2_ref, b2_ref, w3f_ref, w3b_ref, dec_ref,
                   chi_ref, clo_ref, shi_ref, slo_ref, kr_ref, ki_ref, kn_ref, *, L):
    h = jnp.sin(fr_ref[0:1, :] * (_dot3(z_ref[...], w1_ref[...]) + b1_ref[...]))
    h = jnp.sin(fr_ref[1:2, :] * (_dot3(h, w2_ref[...]) + b2_ref[...]))
    dec = dec_ref[...]
    row = lax.broadcasted_iota(jnp.int32, (L, 1), 0)
    kf = _dot3(h, w3f_ref[...]) * dec
    kb = jnp.where(row == 0, 0.0, _dot3(h, w3b_ref[...]) * dec)
    p_hi, p_lo = _split(kf + kb)
    m_hi, m_lo = _split(kf - kb)
    chi, clo = chi_ref[...], clo_ref[...]
    shi, slo = shi_ref[...], slo_ref[...]
    kc = _dot(chi, p_hi) + (_dot(clo, p_hi) + _dot(chi, p_lo))
    ks = _dot(shi, m_hi) + (_dot(slo, m_hi) + _dot(shi, m_lo))
    sign = jnp.where((row & 1) == 0, 1.0, -1.0)
    kn = jnp.sum((kf + kb) * sign, axis=0, keepdims=True)
    kr_ref[...] = kc * jnp.where(row == 0, 0.5 / L, 1.0 / L)
    ki_ref[...] = ks * (-1.0 / L)
    kn_ref[...] = kn * (0.5 / L)


def _hyena_filters(L, consts, w1p, b1, freq, w2, b2, w3):
    zpos, decay, c_hi, c_lo, s_hi, s_lo = consts
    full = lambda shape: pl.BlockSpec(shape, lambda o: (0,) * len(shape))
    return pl.pallas_call(
        functools.partial(_filter_kernel, L=L),
        grid=(HY_ORDER,),
        in_specs=[full((L, 64)), full((64, HY_FF)), full((1, HY_FF)), full((2, HY_FF)),
                  full((HY_FF, HY_FF)), full((1, HY_FF)),
                  pl.BlockSpec((HY_FF, HY_CH), lambda o: (0, o)),
                  pl.BlockSpec((HY_FF, HY_CH), lambda o: (0, HY_ORDER + o)),
                  full((L, HY_CH)), full((L, L)), full((L, L)), full((L, L)), full((L, L))],
        out_specs=[pl.BlockSpec((None, L, HY_CH), lambda o: (o, 0, 0)),
                   pl.BlockSpec((None, L, HY_CH), lambda o: (o, 0, 0)),
                   pl.BlockSpec((None, 1, HY_CH), lambda o: (o, 0, 0))],
        out_shape=[jax.ShapeDtypeStruct((HY_ORDER, L, HY_CH), F32),
                   jax.ShapeDtypeStruct((HY_ORDER, L, HY_CH), F32),
                   jax.ShapeDtypeStruct((HY_ORDER, 1, HY_CH), F32)],
        compiler_params=_cparams(("arbitrary",)),
        name="hyena_filters",
    )(zpos, w1p, b1, freq, w2, b2, w3, w3, decay, c_hi, c_lo, s_hi, s_lo)


HY_CG = 256
HY_STEP_ROWS = 1024


def _hyena_kernel(v_ref, x1_ref, x2_ref, wv_ref, w1_ref, w2_ref, bv_ref, b1_ref, b2_ref, skip_ref,
                  kr_ref, ki_ref, kn_ref, c_ref, sp_ref, spt_ref, o_ref, *, L):
    row = lax.broadcasted_iota(jnp.int32, (L, 1), 0)
    first = row == 0
    last = row == L - 1
    chains = [(b, slice(g * HY_CG, (g + 1) * HY_CG)) for b in range(v_ref.shape[0]) for g in range(HY_CH // HY_CG)]

    def short_conv(u_ref, w_ref, b_ref, b, cols):
        u = u_ref[b, :, cols]
        prev = jnp.where(first, 0.0, pltpu.roll(u, 1, axis=0))
        nxt = jnp.where(last, 0.0, pltpu.roll(u, L - 1, axis=0))
        return prev * w_ref[0:1, cols] + u * w_ref[1:2, cols] + nxt * w_ref[2:3, cols] + b_ref[:, cols]

    gate_refs = ((x1_ref, w1_ref, b1_ref), (x2_ref, w2_ref, b2_ref))
    z = [short_conv(v_ref, wv_ref, bv_ref, b, cols) for b, cols in chains]
    for o in range(HY_ORDER):
        zb = [zz.astype(BF16) for zz in z]
        a = [_dot(c_ref[...], x) for x in zb]
        s = [_dot(sp_ref[...], x) for x in zb]
        conv = []
        for n, (b, cols) in enumerate(chains):
            kr = kr_ref[o, :, cols]
            ki = ki_ref[o, :, cols]
            qt = a[n] * kr + s[n] * ki
            qb = s[n] * jnp.where(first, kn_ref[o, :, cols], kr) - a[n] * ki
            conv.append(_dot(c_ref[...], qt.astype(BF16)) + _dot(spt_ref[...], qb.astype(BF16)))
        z = [short_conv(*gate_refs[o], b, cols) * (conv[n] + z[n] * skip_ref[o:o + 1, cols])
             for n, (b, cols) in enumerate(chains)]
    for n, (b, cols) in enumerate(chains):
        o_ref[b, :, cols] = z[n].astype(BF16)


def _hyena_mixer(u3, conv_w, conv_b, skip, filt, dft, L):
    B = u3.shape[0]
    nb = max(1, HY_STEP_ROWS // L)
    kr, ki, kn = filt
    c_b, sp_b, spt_b = dft
    once = pl.Buffered(1)
    ublk = lambda part: pl.BlockSpec((nb, L, HY_CH), lambda b: (b, 0, part))
    wblk = lambda part: pl.BlockSpec((3, HY_CH), lambda b: (0, part))
    bblk = lambda part: pl.BlockSpec((1, HY_CH), lambda b: (0, part))
    fblk = lambda rows: pl.BlockSpec((HY_ORDER, rows, HY_CH), lambda b: (0, 0, 0), pipeline_mode=once)
    mat = pl.BlockSpec((L, L), lambda b: (0, 0), pipeline_mode=once)
    return pl.pallas_call(
        functools.partial(_hyena_kernel, L=L),
        grid=(B // nb,),
        in_specs=[ublk(0), ublk(1), ublk(2), wblk(0), wblk(1), wblk(2), bblk(0), bblk(1), bblk(2),
                  pl.BlockSpec((HY_ORDER, HY_CH), lambda b: (0, 0)),
                  fblk(L), fblk(L), fblk(1), mat, mat, mat],
        out_specs=pl.BlockSpec((nb, L, HY_CH), lambda b: (b, 0, 0)),
        out_shape=jax.ShapeDtypeStruct((B, L, HY_CH), BF16),
        compiler_params=_cparams(("arbitrary",)),
        name="hyena_mixer",
    )(u3, u3, u3, conv_w, conv_w, conv_w, conv_b, conv_b, conv_b, skip, kr, ki, kn, c_b, sp_b, spt_b)


GLA_PAIR = 2 * GLA_CHUNK
GLA_STATES = 2 * GLA_HEADS
GLA_SCAN = 256


def _gla_kernel(*refs, L, has_state, emit_state, out_slot, n_alias):
    it = iter(refs)
    qkvg_ref, a_ref, wa_ref, ba_ref, ng_ref, scan_ref = (next(it) for _ in range(6))
    st0_refs = (next(it), next(it)) if has_state else None
    for _ in range(n_alias):
        next(it)
    y_ref = next(it)
    stout_refs = (next(it), next(it)) if emit_state else None
    qd_s, ki_s, ke_s, qi_s, ku_s, et_s, vb_s, vt_s, o_s, st_s = (next(it) for _ in range(10))

    npair = L // GLA_PAIR
    pre = _dot3(a_ref[...], wa_ref[...]) + ba_ref[...]
    la = (jnp.minimum(pre, 0.0) - jnp.log(1.0 + jnp.exp(-jnp.abs(pre)))) * (1.0 / GLA_TAU)

    scan = scan_ref[...]
    pfx, tot = [], []
    for r in range(0, L, GLA_SCAN):
        x = la[r:r + GLA_SCAN]
        hi = x.astype(BF16)
        r1 = x - hi.astype(F32)
        mid = r1.astype(BF16)
        lo = (r1 - mid.astype(F32)).astype(BF16)
        res = _dot(scan, hi) + (_dot(scan, mid) + _dot(scan, lo))
        pfx.append(res[:GLA_SCAN])
        tot.append(res[GLA_SCAN:])
    pfx = jnp.concatenate(pfx, axis=0) if len(pfx) > 1 else pfx[0]
    tot = jnp.concatenate(tot, axis=0) if len(tot) > 1 else tot[0]

    upper = (lax.broadcasted_iota(jnp.int32, (L, 1), 0) & GLA_CHUNK) != 0
    q = qkvg_ref[:, 0:GLA_QK] * (GLA_DK ** -0.5)
    k = qkvg_ref[:, GLA_QK:2 * GLA_QK]
    for dirn in range(2):
        cols = slice(dirn * GLA_QK, (dirn + 1) * GLA_QK)
        if dirn == 0:
            cum = pfx[:, cols]
            rest = tot[:, cols] - cum
            first = jnp.logical_not(upper)
        else:
            rest = pfx[:, cols] - la[:, cols]
            cum = tot[:, cols] - rest
            first = upper
        et = jnp.exp(tot[:, cols])
        et_other = jnp.where(upper, pltpu.roll(et, GLA_CHUNK, axis=0), pltpu.roll(et, L - GLA_CHUNK, axis=0))
        qd = q * jnp.exp(cum)
        ke = k * jnp.exp(rest)
        et_s[dirn] = et
        qd_s[dirn] = qd.astype(BF16)
        ki_s[dirn] = (k * jnp.exp(-cum)).astype(BF16)
        ke_s[dirn] = ke.astype(BF16)
        qi_s[dirn] = jnp.where(first, qd, qd * et_other).astype(BF16)
        ku_s[dirn] = jnp.where(first, ke * et_other, ke).astype(BF16)
    vb_s[...] = qkvg_ref[:, 2 * GLA_QK:2 * GLA_QK + GLA_VW].astype(BF16)

    def transpose_v(n2, carry):
        r0 = pl.multiple_of(n2 * GLA_PAIR, GLA_PAIR)
        for h in range(GLA_HEADS):
            vp = qkvg_ref[pl.ds(r0, GLA_PAIR), 2 * GLA_QK + h * GLA_DV:2 * GLA_QK + (h + 1) * GLA_DV]
            vt_s[n2, h * GLA_DV:(h + 1) * GLA_DV, :] = vp.T.astype(BF16)
        return carry

    lax.fori_loop(0, npair, transpose_v, 0)

    if has_state:
        zero = jnp.zeros((GLA_DK, GLA_DV), F32)
        for dirn in range(2):
            for h in range(GLA_HEADS):
                s0 = st0_refs[dirn][h]
                both = jnp.concatenate([s0, zero] if h % 2 == 0 else [zero, s0], axis=0)
                idx = dirn * GLA_HEADS + h
                st_s[idx * GLA_DV:(idx + 1) * GLA_DV, :] = both.T
    else:
        st_s[...] = jnp.zeros_like(st_s)

    lane = lax.broadcasted_iota(jnp.int32, (1, 2 * GLA_DK), 1)
    head_lanes = (lane < GLA_DK, lane >= GLA_DK)
    ri = lax.broadcasted_iota(jnp.int32, (GLA_PAIR, GLA_PAIR), 0)
    ci = lax.broadcasted_iota(jnp.int32, (GLA_PAIR, GLA_PAIR), 1)
    same_chunk = ((ri ^ ci) & GLA_CHUNK) == 0
    keep = (ci <= ri, ci >= ri)

    def pair_step(n, carry):
        for dirn in range(2):
            n2 = n if dirn == 0 else npair - 1 - n
            r0 = pl.multiple_of(n2 * GLA_PAIR, GLA_PAIR)
            rows = pl.ds(r0, GLA_PAIR)
            for p in range(GLA_HEADS // 2):
                lanes = slice(p * 2 * GLA_DK, (p + 1) * 2 * GLA_DK)
                qd = qd_s[dirn, rows, lanes]
                qi = qi_s[dirn, rows, lanes]
                zeros = jnp.zeros_like(qd)
                dec = (et_s[dirn, pl.ds(r0, 8), lanes][0:1, :]
                       * et_s[dirn, pl.ds(r0 + GLA_CHUNK, 8), lanes][0:1, :])
                q2 = jnp.concatenate([jnp.where(head_lanes[0], qd, zeros), jnp.where(head_lanes[1], qd, zeros)],
                                     axis=0)
                att_d = _dot_nt(q2, ki_s[dirn, rows, lanes])
                att_o = _dot_nt(q2, ke_s[dirn, rows, lanes])
                s0 = (dirn * GLA_HEADS + 2 * p) * GLA_DV
                st2 = st_s[s0:s0 + 2 * GLA_DV, :]
                vt2 = vt_s[n2, 2 * p * GLA_DV:(2 * p + 2) * GLA_DV, :]
                st_s[s0:s0 + 2 * GLA_DV, :] = st2 * dec + _dot(vt2, ku_s[dirn, rows, lanes])
                for hh in range(2):
                    h = 2 * p + hh
                    hr = slice(hh * GLA_PAIR, (hh + 1) * GLA_PAIR)
                    att = jnp.where(keep[dirn], jnp.where(same_chunk, att_d[hr], att_o[hr]), 0.0)
                    qm = jnp.where(head_lanes[hh], qi, zeros)
                    o = (_dot(att.astype(BF16), vb_s[rows, h * GLA_DV:(h + 1) * GLA_DV])
                         + _dot_nt(qm, st2[hh * GLA_DV:(hh + 1) * GLA_DV].astype(BF16)))
                    o_s[dirn, rows, h * GLA_DV:(h + 1) * GLA_DV] = o
        return carry

    lax.fori_loop(0, npair, pair_step, 0, unroll=2)

    for h in range(GLA_HEADS):
        cols = slice(h * GLA_DV, (h + 1) * GLA_DV)
        o = _rms(o_s[0, :, cols] + o_s[1, :, cols], ng_ref[...])
        g = qkvg_ref[:, 2 * GLA_QK + GLA_VW + h * GLA_DV:2 * GLA_QK + GLA_VW + (h + 1) * GLA_DV]
        y_ref[:, cols] = (o * (g * jax.nn.sigmoid(g))).astype(BF16)
    if emit_state:
        for dirn in range(2):
            dst = stout_refs[dirn]
            if out_slot is not None:
                for d in range(DEPTH):
                    if d != out_slot:
                        dst[d] = jnp.zeros(dst.shape[1:], F32)
                dst = dst.at[out_slot]
            for h in range(GLA_HEADS):
                idx = dirn * GLA_HEADS + h
                half = (h % 2) * GLA_DK
                dst[h] = st_s[idx * GLA_DV:(idx + 1) * GLA_DV, :].T[half:half + GLA_DK, :]


def _gla_scan_matrix():
    i = np.arange(GLA_SCAN)
    same = (i[:, None] // GLA_CHUNK) == (i[None, :] // GLA_CHUNK)
    tri = same & (i[None, :] <= i[:, None])
    return jnp.asarray(np.concatenate([tri, same], axis=0).astype(np.float32)).astype(BF16)


def _gla_mixer(u3, wa_p, ba_p, norm_g, st0, layer, L, emit_state):
    B = u3.shape[0]
    has_state = st0 is not None
    head_state = (GLA_HEADS, GLA_DK, GLA_DV)
    in_specs = [pl.BlockSpec((None, L, 1536), lambda b: (b, 0, U_GLA // 1536)),
                pl.BlockSpec((None, L, 128), lambda b: (b, 0, U_GLA_A // 128)),
                pl.BlockSpec((128, 2 * GLA_QK), lambda b: (0, 0)),
                pl.BlockSpec((1, 2 * GLA_QK), lambda b: (0, 0)),
                pl.BlockSpec((1, GLA_DV), lambda b: (0, 0)),
                pl.BlockSpec((2 * GLA_SCAN, GLA_SCAN), lambda b: (0, 0))]
    args = [u3, u3, wa_p, ba_p, norm_g, _gla_scan_matrix()]
    if has_state:
        in_specs += [pl.BlockSpec((None, None) + head_state, lambda b: (b, layer, 0, 0, 0))] * 2
        args += list(st0)
    out_specs = [pl.BlockSpec((None, L, GLA_VW), lambda b: (b, 0, 0))]
    out_shape = [jax.ShapeDtypeStruct((B, L, GLA_VW), BF16)]
    aliases = {}
    fresh = emit_state is not None and all(prev is None for prev in emit_state)
    if emit_state is not None:
        slot_spec = (pl.BlockSpec((None, DEPTH) + head_state, lambda b: (b, 0, 0, 0, 0)) if fresh else
                     pl.BlockSpec((None, None) + head_state, lambda b: (b, layer, 0, 0, 0)))
        out_specs += [slot_spec] * 2
        out_shape += [jax.ShapeDtypeStruct((B, DEPTH) + head_state, F32)] * 2
        for k, prev in enumerate(emit_state):
            if prev is not None:
                aliases[len(args)] = 1 + k
                in_specs.append(pl.BlockSpec(memory_space=pl.ANY))
                args.append(prev)
    npair = L // GLA_PAIR
    qk_bf16 = pltpu.VMEM((2, L, GLA_QK), BF16)
    res = pl.pallas_call(
        functools.partial(_gla_kernel, L=L, has_state=has_state, emit_state=emit_state is not None,
                          out_slot=layer if fresh else None, n_alias=len(aliases)),
        grid=(B,),
        in_specs=in_specs,
        out_specs=out_specs,
        out_shape=out_shape,
        input_output_aliases=aliases,
        scratch_shapes=[qk_bf16,
                        qk_bf16,
                        qk_bf16,
                        qk_bf16,
                        qk_bf16,
                        pltpu.VMEM((2, L, GLA_QK), F32),
                        pltpu.VMEM((L, GLA_VW), BF16),
                        pltpu.VMEM((npair, GLA_VW, GLA_PAIR), BF16),
                        pltpu.VMEM((2, L, GLA_VW), F32),
                        pltpu.VMEM((GLA_STATES * GLA_DV, 2 * GLA_DK), F32)],
        compiler_params=_cparams(("arbitrary",)),
        name="gla_mixer",
    )(*args)
    return (res[0], res[1], res[2]) if emit_state is not None else (res[0], None, None)


MLA_HW = 256
MLA_TQ = 512
MLA_KCH = 256


def _rope(x, cos, sin):
    lane = lax.broadcasted_iota(jnp.int32, x.shape, 1)
    partner = jnp.where((lane & 16) == 0, pltpu.roll(x, 112, axis=1), pltpu.roll(x, 16, axis=1))
    return x * cos + partner * sin


def _mla_kernel(*refs, L, Lk, tq, rope, emit, out_slot, n_alias):
    it = iter(refs)
    u_ref, qn_ref, wq_ref, kvn_ref, wkv_ref = (next(it) for _ in range(5))
    cos_ref, sin_ref, cckv_ref, ckr_ref = (next(it) for _ in range(4)) if rope else (None,) * 4
    for _ in range(n_alias):
        next(it)
    y_ref = next(it)
    ckv_out, kr_out = (next(it), next(it)) if emit else (None, None)
    kf_s, v_s = next(it), next(it)

    qi = pl.program_id(1)

    def project(keys, rope_part, r0):
        kv = _dot(keys, wkv_ref[...])
        for h in range(MLA_HEADS):
            kf_s[h, r0:r0 + MLA_KCH, 0:MLA_NOPE] = kv[:, h * 256:h * 256 + MLA_NOPE].astype(BF16)
            kf_s[h, r0:r0 + MLA_KCH, MLA_NOPE:MLA_HW] = rope_part
            v_s[h, r0:r0 + MLA_KCH, :] = kv[:, h * 256 + MLA_NOPE:(h + 1) * 256].astype(BF16)

    @pl.when(qi == 0)
    def _():
        if emit and out_slot is not None:
            for dst in (ckv_out, kr_out):
                for d in range(DEPTH):
                    if d != out_slot:
                        dst[d] = jnp.zeros(dst.shape[1:], F32)
        ckv_dst, kr_dst = (ckv_out, kr_out) if out_slot is None else (ckv_out.at[out_slot], kr_out.at[out_slot])
        for r0 in range(0, L, MLA_KCH):
            ckvn = _rms(u_ref[r0:r0 + MLA_KCH, MLA_Q_RANK:MLA_Q_RANK + MLA_KV_RANK], kvn_ref[...])
            kr = u_ref[r0:r0 + MLA_KCH, 640:768]
            kr = jnp.where(lax.broadcasted_iota(jnp.int32, kr.shape, 1) < MLA_ROPE, kr, 0.0)
            if emit:
                ckv_dst[r0:r0 + MLA_KCH, :] = ckvn
                kr_dst[r0:r0 + MLA_KCH, :] = kr[:, 0:MLA_ROPE]
            if rope:
                kr = _rope(kr, cos_ref[r0:r0 + MLA_KCH, :], sin_ref[r0:r0 + MLA_KCH, :])
            project(ckvn.astype(BF16), kr.astype(BF16), r0)
        if rope:
            for r0 in range(0, Lk - L, MLA_KCH):
                ckr = ckr_ref[r0:r0 + MLA_KCH, :]
                ckr = jnp.concatenate([ckr, jnp.zeros_like(ckr)], axis=1)
                project(cckv_ref[r0:r0 + MLA_KCH, :].astype(BF16), ckr.astype(BF16), L + r0)

    q0 = pl.multiple_of(qi * tq, tq)
    cqn = _rms(u_ref[pl.ds(q0, tq), 0:MLA_Q_RANK], qn_ref[...]).astype(BF16)
    q = _dot(cqn, wq_ref[...]) * ((MLA_NOPE + MLA_ROPE) ** -0.5)
    if rope:
        cos = cos_ref[pl.ds(q0, tq), :]
        sin = sin_ref[pl.ds(q0, tq), :]
    for h in range(MLA_HEADS):
        q_nope = q[:, h * MLA_HW:h * MLA_HW + MLA_NOPE]
        q_rope = q[:, h * MLA_HW + MLA_NOPE:(h + 1) * MLA_HW]
        if rope:
            q_rope = _rope(q_rope, cos, sin)
        qh = jnp.concatenate([q_nope, q_rope], axis=1).astype(BF16)
        s = _dot_nt(qh, kf_s[h])
        p = jnp.exp(s - jnp.max(s, axis=-1, keepdims=True))
        denom = jnp.sum(p, axis=-1, keepdims=True)
        o = _dot(p.astype(BF16), v_s[h])
        y_ref[:, h * MLA_V:(h + 1) * MLA_V] = (o / denom).astype(BF16)


def _mla_mixer(u3, q_norm, wq_p, kv_norm, wkv, rope_args, layer, L, emit):
    B = u3.shape[0]
    rope = rope_args is not None
    Lk = L + (rope_args[2].shape[2] if rope else 0)
    tq = min(MLA_TQ, L)
    const = lambda shape: pl.BlockSpec(shape, lambda b, i: (0,) * len(shape))
    in_specs = [pl.BlockSpec((None, L, 768), lambda b, i: (b, 0, U_MLA // 768)),
                const((1, MLA_Q_RANK)), const((MLA_Q_RANK, MLA_HEADS * MLA_HW)),
                const((1, MLA_KV_RANK)), const((MLA_KV_RANK, MLA_HEADS * 256))]
    args = [u3, q_norm, wq_p, kv_norm, wkv]
    if rope:
        cos, sin, cckv, ckr = rope_args
        in_specs += [const((L, 128)), const((L, 128)),
                     pl.BlockSpec((None, None, Lk - L, MLA_KV_RANK), lambda b, i: (b, layer, 0, 0)),
                     pl.BlockSpec((None, None, Lk - L, MLA_ROPE), lambda b, i: (b, layer, 0, 0))]
        args += [cos, sin, cckv, ckr]
    out_specs = [pl.BlockSpec((None, tq, MLA_HEADS * MLA_V), lambda b, i: (b, i, 0))]
    out_shape = [jax.ShapeDtypeStruct((B, L, MLA_HEADS * MLA_V), BF16)]
    aliases = {}
    fresh = emit is not None and all(prev is None for prev in emit)
    if emit is not None:
        slot = ((None, DEPTH), lambda b, i: (b, 0, 0, 0)) if fresh else ((None, None), lambda b, i: (b, layer, 0, 0))
        out_specs += [pl.BlockSpec(slot[0] + (L, MLA_KV_RANK), slot[1]),
                      pl.BlockSpec(slot[0] + (L, MLA_ROPE), slot[1])]
        out_shape += [jax.ShapeDtypeStruct((B, DEPTH, L, MLA_KV_RANK), F32),
                      jax.ShapeDtypeStruct((B, DEPTH, L, MLA_ROPE), F32)]
        for k, prev in enumerate(emit):
            if prev is not None:
                aliases[len(args)] = 1 + k
                in_specs.append(pl.BlockSpec(memory_space=pl.ANY))
                args.append(prev)
    res = pl.pallas_call(
        functools.partial(_mla_kernel, L=L, Lk=Lk, tq=tq, rope=rope, emit=emit is not None,
                          out_slot=layer if fresh else None, n_alias=len(aliases)),
        grid=(B, L // tq),
        in_specs=in_specs,
        out_specs=out_specs,
        out_shape=out_shape,
        input_output_aliases=aliases,
        scratch_shapes=[pltpu.VMEM((MLA_HEADS, Lk, MLA_HW), BF16),
                        pltpu.VMEM((MLA_HEADS, Lk, MLA_V), BF16)],
        compiler_params=_cparams(("arbitrary", "arbitrary")),
        name="mla_mixer",
    )(*args)
    return res if emit is not None else (res[0], None, None)


def _outproj_kernel(x_ref, g_ref, yh_ref, yg_ref, ym_ref, wh_ref, wg_ref, wm_ref, o_ref):
    y = _dot(yh_ref[...], wh_ref[...]) + _dot(yg_ref[...], wg_ref[...]) + _dot(ym_ref[...], wm_ref[...])
    o_ref[...] = x_ref[...] + g_ref[...] * y


def _out_projection(x2d, mod, y_hy, y_gla, y_mla, w_out_b, layer, row_of_tile, tm):
    rows = x2d.shape[0]
    once = pl.Buffered(1)
    return pl.pallas_call(
        _outproj_kernel,
        grid=(rows // tm,),
        in_specs=[pl.BlockSpec((tm, D_MODEL), lambda i: (i, 0)),
                  _mod_spec(2, row_of_tile),
                  pl.BlockSpec((tm, HY_CH), lambda i: (i, 0)),
                  pl.BlockSpec((tm, GLA_VW), lambda i: (i, 0)),
                  pl.BlockSpec((tm, MLA_HEADS * MLA_V), lambda i: (i, 0)),
                  pl.BlockSpec((None, HY_CH, D_MODEL), lambda i: (layer, 0, 0), pipeline_mode=once),
                  pl.BlockSpec((None, GLA_VW, D_MODEL), lambda i: (layer, 1, 0), pipeline_mode=once),
                  pl.BlockSpec((None, MLA_HEADS * MLA_V, D_MODEL), lambda i: (layer, 1, 0), pipeline_mode=once)],
        out_specs=pl.BlockSpec((tm, D_MODEL), lambda i: (i, 0)),
        out_shape=jax.ShapeDtypeStruct((rows, D_MODEL), F32),
        compiler_params=_cparams(("arbitrary",)),
        name="out_projection",
    )(x2d, mod, y_hy, y_gla, y_mla, w_out_b, w_out_b, w_out_b)


FFN_TF = 512


FFN_X_PARTS = 4


def _ffn_kernel(*refs, final):
    x_refs = refs[:FFN_X_PARTS]
    ln_ref, sh_ref, sc_ref, g_ref, wg_ref, wu_ref, wo_ref, lnf_ref, o_ref, h_ref, acc_ref = refs[FFN_X_PARTS:]
    f = pl.program_id(1)
    part = x_refs[0].shape[0]

    @pl.when(f == 0)
    def _():
        gain = ln_ref[...] * (1.0 + sc_ref[...])
        shift = sh_ref[...]
        for c, x_ref in enumerate(x_refs):
            def body(t, carry):
                rows = pl.ds(pl.multiple_of(t * ADALN_ROWS, ADALN_ROWS), ADALN_ROWS)
                out_rows = pl.ds(pl.multiple_of(c * part + t * ADALN_ROWS, ADALN_ROWS), ADALN_ROWS)
                x = x_ref[rows, :]
                ms = jnp.mean(x * x, axis=-1, keepdims=True)
                h_ref[out_rows, :] = (x * lax.rsqrt(ms + EPS) * gain + shift).astype(BF16)
                o_ref[out_rows, :] = x
                return carry

            lax.fori_loop(0, part // ADALN_ROWS, body, 0, unroll=4)
        acc_ref[...] = jnp.zeros_like(acc_ref)

    h = h_ref[...]
    gate = _dot(h, wg_ref[...])
    up = _dot(h, wu_ref[...])
    act = (gate * jax.nn.sigmoid(gate) * up).astype(BF16)
    acc_ref[...] += _dot(act, wo_ref[...])

    @pl.when(f == pl.num_programs(1) - 1)
    def _():
        def body(i, carry):
            rows = pl.ds(pl.multiple_of(i * ADALN_ROWS, ADALN_ROWS), ADALN_ROWS)
            x = o_ref[rows, :] + g_ref[...] * acc_ref[rows, :]
            o_ref[rows, :] = _rms(x, lnf_ref[...]) if final else x
            return carry

        lax.fori_loop(0, o_ref.shape[0] // ADALN_ROWS, body, 0, unroll=4)


def _ffn(x2d, ln, mod, w_in_b, w_out_b, layer, ln_final, row_of_tile, tm, final):
    rows = x2d.shape[0]
    nf = D_FF // FFN_TF
    n_tiles = rows // tm
    switch = nf - FFN_X_PARTS

    def x_part(c):
        return pl.BlockSpec((tm // FFN_X_PARTS, D_MODEL),
                            lambda i, f: (jnp.minimum(i + jnp.where(f >= switch + c, 1, 0), n_tiles - 1) * FFN_X_PARTS + c, 0))

    return pl.pallas_call(
        functools.partial(_ffn_kernel, final=final),
        grid=(n_tiles, nf),
        in_specs=[x_part(c) for c in range(FFN_X_PARTS)] + [
                  pl.BlockSpec((1, D_MODEL), lambda i, f: (0, 0)),
                  _mod_spec(3, row_of_tile),
                  _mod_spec(4, row_of_tile),
                  _mod_spec(5, row_of_tile),
                  pl.BlockSpec((None, D_MODEL, FFN_TF), lambda i, f: (layer, 0, f)),
                  pl.BlockSpec((None, D_MODEL, FFN_TF), lambda i, f: (layer, 0, nf + f)),
                  pl.BlockSpec((None, FFN_TF, D_MODEL), lambda i, f: (layer, f, 0)),
                  pl.BlockSpec((1, D_MODEL), lambda i, f: (0, 0))],
        out_specs=pl.BlockSpec((tm, D_MODEL), lambda i, f: (i, 0)),
        out_shape=jax.ShapeDtypeStruct((rows, D_MODEL), F32),
        scratch_shapes=[pltpu.VMEM((tm, D_MODEL), BF16), pltpu.VMEM((tm, D_MODEL), F32)],
        compiler_params=_cparams(("arbitrary", "arbitrary")),
        name="ffn",
    )(*([x2d] * FFN_X_PARTS), ln, mod, mod, mod, w_in_b, w_in_b, w_out_b, ln_final)


def _dft_tables(L):
    k = np.arange(L)
    ang = np.pi * ((k[:, None] * k[None, :]) % (2 * L)) / L
    c = np.cos(ang)
    s = np.sin(ang)
    sp = s.copy()
    sp[0, :] = 1.0 - 2.0 * (k % 2)
    as32 = lambda a: jnp.asarray(a.astype(np.float32))
    return as32(c), as32(s), as32(sp), as32(sp.T.copy())


def _hyena_consts(L):
    c, s, sp, spt = _dft_tables(L)
    c_hi, c_lo = _split(c)
    s_hi, s_lo = _split(s)
    f32 = np.float32
    t = np.linspace(0.0, 1.0, L, dtype=f32)[:, None]
    w = (f32(2.0 * math.pi) * np.arange(L, dtype=f32)[:, None] / f32(L)).astype(f32)
    f = np.linspace(1e-4, HY_BANDS - 1, HY_BANDS, dtype=f32)
    zpos = np.concatenate([t, np.cos(f * w), -np.sin(f * w)], axis=-1).astype(f32)
    zpos = np.pad(zpos, ((0, 0), (0, 64 - HY_EMB)))
    min_decay = math.log(HY_TARGET) / HY_SLOW_DECAY
    max_decay = math.log(HY_TARGET) / HY_FAST_DECAY
    delta = np.abs(np.linspace(min_decay, max_decay, HY_CH, dtype=f32))
    decay = np.exp(-t * delta).astype(f32)
    filt_consts = (jnp.asarray(zpos), jnp.asarray(decay), c_hi, c_lo, s_hi, s_lo)
    main_consts = (c_hi, sp.astype(BF16), spt.astype(BF16))
    return filt_consts, main_consts


def _rope_tables(L):
    t = jnp.arange(L)
    half = MLA_ROPE // 2
    inv = ROPE_THETA ** (-jnp.arange(0, half, 2, dtype=F32) / half)
    ang_r = (t // GRID_W).astype(F32)[:, None] * inv
    ang_c = (t % GRID_W).astype(F32)[:, None] * inv
    cr, sr, cc, sc = jnp.cos(ang_r), jnp.sin(ang_r), jnp.cos(ang_c), jnp.sin(ang_c)
    cos = jnp.concatenate([cr, cr, cc, cc, jnp.ones((L, 128 - MLA_ROPE), F32)], axis=1)
    sin = jnp.concatenate([-sr, sr, -sc, sc, jnp.zeros((L, 128 - MLA_ROPE), F32)], axis=1)
    return cos, sin


def _prep_w_in_tail(w):
    mla0 = IN_MAIN + 2 * GLA_LOWRANK
    pad = jnp.zeros((DEPTH, D_MODEL, IN_TN - (w.shape[2] - IN_MAIN)), w.dtype)
    return jnp.concatenate([w[:, :, mla0:], w[:, :, IN_MAIN:mla0], pad], axis=2).astype(BF16)


def _prep_w_uq(w):
    w = w.reshape(MLA_Q_RANK, MLA_HEADS, MLA_NOPE + MLA_ROPE)
    w = jnp.pad(w, ((0, 0), (0, 0), (0, MLA_HW - MLA_NOPE - MLA_ROPE)))
    return w.reshape(MLA_Q_RANK, MLA_HEADS * MLA_HW).astype(BF16)


def _prep_gla_decay(wa_f, ba_f, wa_b, ba_b):
    wa = jnp.zeros((128, 2 * GLA_QK), F32)
    wa = wa.at[U_GLA_A_LANE:U_GLA_A_LANE + GLA_LOWRANK, 0:GLA_QK].set(wa_f)
    wa = wa.at[U_GLA_A_LANE + GLA_LOWRANK:U_GLA_A_LANE + 2 * GLA_LOWRANK, GLA_QK:].set(wa_b)
    return wa, jnp.concatenate([ba_f, ba_b])[None, :]


def _trunk_layer(x2d, B, L, mod, row_of_tile, lw, consts, ctx, caches, final, ln_final):
    tm = 1024
    tm_out = 512
    u = _in_projection(x2d, lw['ln_mix'], mod, lw['w_in'], lw['w_in_tail'], lw['layer'], row_of_tile(tm), tm)
    u3 = u.reshape(B, L, U_W)
    filt_consts, main_consts = consts['hyena']
    filt = _hyena_filters(L, filt_consts, lw['hy_w1'], lw['hy_b1'], lw['hy_freq'], lw['hy_w2'],
                          lw['hy_b2'], lw['hy_w3'])
    y_hy = _hyena_mixer(u3, lw['hy_conv_w'], lw['hy_conv_b'], lw['hy_skip'], filt, main_consts, L)
    if ctx is None:
        y_gla, s_f, s_b = _gla_mixer(u3, lw['gla_wa'], lw['gla_ba'], lw['gla_norm'], None, lw['layer'], L, caches[2:])
        y_mla, ckv, krope = _mla_mixer(u3, lw['mla_q_norm'], lw['mla_w_uq'], lw['mla_kv_norm'],
                                       lw['mla_w_ukv'], None, lw['layer'], L, caches[:2])
        extras = (ckv, krope, s_f, s_b)
    else:
        cache_ckv, cache_krope, s0f, s0b = ctx
        y_gla, _, _ = _gla_mixer(u3, lw['gla_wa'], lw['gla_ba'], lw['gla_norm'], (s0f, s0b), lw['layer'], L, None)
        cos, sin = consts['rope']
        y_mla, _, _ = _mla_mixer(u3, lw['mla_q_norm'], lw['mla_w_uq'], lw['mla_kv_norm'], lw['mla_w_ukv'],
                                 (cos, sin, cache_ckv, cache_krope), lw['layer'], L, None)
        extras = None
    rows = B * L
    x2d = _out_projection(x2d, mod, y_hy.reshape(rows, -1), y_gla.reshape(rows, -1), y_mla.reshape(rows, -1),
                          lw['w_out'], lw['layer'], row_of_tile(tm_out), tm_out)
    x2d = _ffn(x2d, lw['ln_ffn'], mod, lw['w_ffn_in'], lw['w_ffn_out'], lw['layer'], ln_final, row_of_tile(tm_out),
               tm_out, final)
    return x2d, extras


def kernel(x_prompt, x_sample, cache_mla_ckv, cache_mla_krope, state_gla_fwd, state_gla_bwd, c, c_ctx, w_mod, b_mod, ln_mix, w_in, hy_conv_w, hy_conv_b, hy_filt_w1, hy_filt_b1, hy_filt_freq, hy_filt_w2, hy_filt_b2, hy_filt_w3, hy_skip, gla_wa_f, gla_ba_f, gla_wa_b, gla_ba_b, gla_norm, mla_q_norm, mla_w_uq, mla_kv_norm, mla_w_ukv, w_out, ln_ffn, w_ffn_in, w_ffn_out, ln_final):
    Bc, Lc, _ = x_prompt.shape
    Bl, Ll, _ = x_sample.shape
    assert 1 + Bl <= MOD_ROWS

    c_all = jnp.concatenate([c_ctx[None, :], c, jnp.zeros((MOD_ROWS - 1 - Bl, D_MODEL), F32)], axis=0)
    mod_all = _modulation(c_all, w_mod, b_mod).reshape(DEPTH, MOD_ROWS, 6, 1, D_MODEL)

    consts_ctx = {'hyena': _hyena_consts(Lc)}
    consts_lat = {'hyena': _hyena_consts(Ll), 'rope': _rope_tables(Ll)}
    ctx_rows = lambda tm: (lambda i: 0)
    lat_rows = lambda tm: (lambda i: 1 + (i * tm) // Ll)
    lnf = ln_final[None, :]

    x_ctx = x_prompt.reshape(Bc * Lc, D_MODEL)
    x_lat = x_sample.reshape(Bl * Ll, D_MODEL)
    w_in_b = w_in.astype(BF16)
    w_in_tail = _prep_w_in_tail(w_in)
    w_out_b = w_out.astype(BF16)
    w_ffn_in_b = w_ffn_in.astype(BF16)
    w_ffn_out_b = w_ffn_out.astype(BF16)
    caches = (None, None, None, None)
    for l in range(DEPTH):
        wa, ba = _prep_gla_decay(gla_wa_f[l], gla_ba_f[l], gla_wa_b[l], gla_ba_b[l])
        lw = {
            'layer': l, 'ln_mix': ln_mix[l][None, :], 'w_in': w_in_b, 'w_in_tail': w_in_tail,
            'hy_conv_w': hy_conv_w[l], 'hy_conv_b': hy_conv_b[l][None, :],
            'hy_w1': jnp.pad(hy_filt_w1[l], ((0, 64 - HY_EMB), (0, 0))), 'hy_b1': hy_filt_b1[l][None, :],
            'hy_freq': hy_filt_freq[l], 'hy_w2': hy_filt_w2[l], 'hy_b2': hy_filt_b2[l][None, :],
            'hy_w3': hy_filt_w3[l], 'hy_skip': hy_skip[l],
            'gla_wa': wa, 'gla_ba': ba, 'gla_norm': gla_norm[l][None, :],
            'mla_q_norm': mla_q_norm[l][None, :], 'mla_w_uq': _prep_w_uq(mla_w_uq[l]),
            'mla_kv_norm': mla_kv_norm[l][None, :], 'mla_w_ukv': mla_w_ukv[l].astype(BF16),
            'w_out': w_out_b, 'ln_ffn': ln_ffn[l][None, :],
            'w_ffn_in': w_ffn_in_b, 'w_ffn_out': w_ffn_out_b,
        }
        final = l == DEPTH - 1
        mod = mod_all[l]
        x_ctx, caches = _trunk_layer(x_ctx, Bc, Lc, mod, ctx_rows, lw, consts_ctx, None, caches, final, lnf)
        ctx = (cache_mla_ckv, cache_mla_krope, state_gla_fwd, state_gla_bwd)
        x_lat, _ = _trunk_layer(x_lat, Bl, Ll, mod, lat_rows, lw, consts_lat, ctx, None, final, lnf)
    return (x_ctx.reshape(Bc, Lc, D_MODEL), x_lat.reshape(Bl, Ll, D_MODEL)) + tuple(caches)
```

```python
import functools
import math

import numpy as np
import jax
import jax.numpy as jnp
from jax import lax
from jax.experimental import pallas as pl
from jax.experimental.pallas import tpu as pltpu

F32 = jnp.float32
BF16 = jnp.bfloat16

D_MODEL = 2048
DEPTH = 2
GRID_W = 64
EPS = 1e-6
HY_CH = 512
HY_ORDER = 2
HY_BANDS = 16
HY_EMB = 1 + 2 * HY_BANDS
HY_FF = 64
HY_FAST_DECAY = 0.3
HY_SLOW_DECAY = 1.5
HY_TARGET = 1e-2
HY_IN = 3 * HY_CH
GLA_HEADS = 4
GLA_DK = 64
GLA_DV = 128
GLA_LOWRANK = 16
GLA_TAU = 16.0
GLA_CHUNK = 64
GLA_QK = GLA_HEADS * GLA_DK
GLA_VW = GLA_HEADS * GLA_DV
MLA_HEADS = 8
MLA_Q_RANK = 384
MLA_KV_RANK = 256
MLA_NOPE = 128
MLA_ROPE = 64
MLA_V = 128
ROPE_THETA = 10000.0
D_FF = -(-8 * D_MODEL // (3 * 256)) * 256

U_HY = 0
U_GLA = 1536
U_MLA = 3072
U_GLA_A = 3712
U_GLA_A_LANE = 64
U_W = 3840
MOD_ROWS = 16

VMEM_LIMIT_V7X = 48 * 1024 * 1024


def _cparams(sem):
    return pltpu.CompilerParams(dimension_semantics=sem, vmem_limit_bytes=VMEM_LIMIT_V7X)


def _dot(a, b):
    return jnp.dot(a, b, preferred_element_type=F32)


def _dot_nt(a, b):
    return lax.dot_general(a, b, (((1,), (1,)), ((), ())), preferred_element_type=F32)


def _split(x):
    hi = x.astype(BF16)
    lo = (x - hi.astype(F32)).astype(BF16)
    return hi, lo


def _dot3(a, b):
    a_hi, a_lo = _split(a)
    b_hi, b_lo = _split(b)
    return _dot(a_hi, b_hi) + (_dot(a_lo, b_hi) + _dot(a_hi, b_lo))


def _rms(x, g):
    ms = jnp.mean(x * x, axis=-1, keepdims=True)
    return x * lax.rsqrt(ms + EPS) * g


ADALN_ROWS = 16


def _adaln_group(x_ref, gain, shift, h_ref, r):
    rows = pl.ds(pl.multiple_of(r, ADALN_ROWS), ADALN_ROWS)
    x = x_ref[rows, :]
    ms = jnp.mean(x * x, axis=-1, keepdims=True)
    h_ref[rows, :] = (x * lax.rsqrt(ms + EPS) * gain + shift).astype(BF16)


def _adaln_all(x_ref, gain, shift, h_ref):
    def body(i, carry):
        _adaln_group(x_ref, gain, shift, h_ref, i * ADALN_ROWS)
        return carry

    lax.fori_loop(0, x_ref.shape[0] // ADALN_ROWS, body, 0, unroll=4)


def _mod_kernel(c_ref, w_ref, b_ref, o_ref):
    c = c_ref[...]
    s = (c * jax.nn.sigmoid(c)).astype(BF16)
    o_ref[...] = _dot(s, w_ref[...].astype(BF16)) + b_ref[...]


def _modulation(c_all, w_mod, b_mod):
    tn = 1024
    n6 = 6 * D_MODEL
    return pl.pallas_call(
        _mod_kernel,
        grid=(DEPTH, n6 // tn),
        in_specs=[pl.BlockSpec((MOD_ROWS, D_MODEL), lambda l, j: (0, 0)),
                  pl.BlockSpec((None, D_MODEL, tn), lambda l, j: (l, 0, j)),
                  pl.BlockSpec((None, 1, tn), lambda l, j: (l, 0, j))],
        out_specs=pl.BlockSpec((None, MOD_ROWS, tn), lambda l, j: (l, 0, j)),
        out_shape=jax.ShapeDtypeStruct((DEPTH, MOD_ROWS, n6), F32),
        compiler_params=_cparams(("arbitrary", "arbitrary")),
        name="modulation",
    )(c_all, w_mod, b_mod.reshape(DEPTH, 1, n6))


def _mod_spec(chunk, row_of_tile, tile_of_step=lambda i, *_: i):
    return pl.BlockSpec((None, None, 1, D_MODEL), lambda *g: (row_of_tile(tile_of_step(*g)), chunk, 0, 0))


IN_TN = 768
IN_MAIN = 3072
IN_PRE_ROWS = 256


def _inproj_kernel(x0_ref, sh0_ref, sc0_ref, xn_ref, shn_ref, scn_ref, ln_ref, w_ref, wt_ref, o_ref, ha_ref, hb_ref):
    i = pl.program_id(0)
    j = pl.program_id(1)

    @pl.when((i == 0) & (j == 0))
    def _():
        _adaln_all(x0_ref, ln_ref[...] * (1.0 + sc0_ref[...]), sh0_ref[...], ha_ref)

    gain = ln_ref[...] * (1.0 + scn_ref[...])
    shift = shn_ref[...]
    base = jnp.clip(j - 1, 0, 3) * IN_PRE_ROWS

    def step(cur_ref, nxt_ref):
        def prepare_next():
            for t in range(IN_PRE_ROWS // ADALN_ROWS):
                rows = pl.ds(t * ADALN_ROWS, ADALN_ROWS)
                x = xn_ref[rows, :]
                ms = jnp.mean(x * x, axis=-1, keepdims=True)
                out_rows = pl.ds(pl.multiple_of(base + t * ADALN_ROWS, ADALN_ROWS), ADALN_ROWS)
                nxt_ref[out_rows, :] = (x * lax.rsqrt(ms + EPS) * gain + shift).astype(BF16)

        @pl.when(j < IN_MAIN // IN_TN)
        def _():
            prepare_next()
            o_ref[...] = _dot_nt(cur_ref[...], w_ref[...])

        @pl.when(j >= IN_MAIN // IN_TN)
        def _():
            prepare_next()
            o_ref[...] = _dot_nt(cur_ref[...], wt_ref[...])

    @pl.when(i % 2 == 0)
    def _():
        step(ha_ref, hb_ref)

    @pl.when(i % 2 == 1)
    def _():
        step(hb_ref, ha_ref)


def _in_projection(x2d, ln, mod, w_in_b, w_tail, layer, row_of_tile, tm):
    rows = x2d.shape[0]
    n_main = IN_MAIN // IN_TN
    n_tiles = rows // tm
    per_tile = tm // IN_PRE_ROWS
    assert U_W // IN_TN - 1 == per_tile and U_W - IN_MAIN == IN_TN
    nxt = lambda i, *_: jnp.minimum(i + 1, n_tiles - 1)
    once = pl.Buffered(1)
    return pl.pallas_call(
        _inproj_kernel,
        grid=(n_tiles, U_W // IN_TN),
        in_specs=[pl.BlockSpec((tm, D_MODEL), lambda i, j: (0, 0), pipeline_mode=once),
                  _mod_spec(0, row_of_tile, lambda *_: 0),
                  _mod_spec(1, row_of_tile, lambda *_: 0),
                  pl.BlockSpec((IN_PRE_ROWS, D_MODEL),
                               lambda i, j: (nxt(i) * per_tile + jnp.clip(j - 1, 0, per_tile - 1), 0)),
                  _mod_spec(0, row_of_tile, nxt),
                  _mod_spec(1, row_of_tile, nxt),
                  pl.BlockSpec((1, D_MODEL), lambda i, j: (0, 0)),
                  pl.BlockSpec((None, IN_TN, D_MODEL), lambda i, j: (layer, jnp.minimum(j, n_main - 1), 0)),
                  pl.BlockSpec((None, U_W - IN_MAIN, D_MODEL), lambda i, j: (layer, 0, 0))],
        out_specs=pl.BlockSpec((tm, IN_TN), lambda i, j: (i, j)),
        out_shape=jax.ShapeDtypeStruct((rows, U_W), F32),
        scratch_shapes=[pltpu.VMEM((tm, D_MODEL), BF16), pltpu.VMEM((tm, D_MODEL), BF16)],
        compiler_params=_cparams(("arbitrary", "arbitrary")),
        name="in_projection",
    )(x2d, mod, mod, x2d, mod, mod, ln, w_in_b, w_tail)


def _filter_kernel(z_ref, w1_ref, b1_ref, fr_ref, w2_ref, b2_ref, w3f_ref, w3b_ref, dec_ref,
                   chi_ref, clo_ref, shi_ref, slo_ref, kr_ref, ki_ref, kn_ref, *, L):
    h = jnp.sin(fr_ref[0:1, :] * (_dot3(z_ref[...], w1_ref[...]) + b1_ref[...]))
    h = jnp.sin(fr_ref[1:2, :] * (_dot3(h, w2_ref[...]) + b2_ref[...]))
    dec = dec_ref[...]
    row = lax.broadcasted_iota(jnp.int32, (L, 1), 0)
    kf = _dot3(h, w3f_ref[...]) * dec
    kb = jnp.where(row == 0, 0.0, _dot3(h, w3b_ref[...]) * dec)
    p_hi, p_lo = _split(kf + kb)
    m_hi, m_lo = _split(kf - kb)
    chi, clo = chi_ref[...], clo_ref[...]
    shi, slo = shi_ref[...], slo_ref[...]
    kc = _dot(chi, p_hi) + (_dot(clo, p_hi) + _dot(chi, p_lo))
    ks = _dot(shi, m_hi) + (_dot(slo, m_hi) + _dot(shi, m_lo))
    sign = jnp.where((row & 1) == 0, 1.0, -1.0)
    kn = jnp.sum((kf + kb) * sign, axis=0, keepdims=True)
    kr_ref[...] = kc * jnp.where(row == 0, 0.5 / L, 1.0 / L)
    ki_ref[...] = ks * (-1.0 / L)
    kn_ref[...] = kn * (0.5 / L)


def _hyena_filters(L, consts, w1p, b1, freq, w2, b2, w3):
    zpos, decay, c_hi, c_lo, s_hi, s_lo = consts
    full = lambda shape: pl.BlockSpec(shape, lambda o: (0,) * len(shape))
    return pl.pallas_call(
        functools.partial(_filter_kernel, L=L),
        grid=(HY_ORDER,),
        in_specs=[full((L, 64)), full((64, HY_FF)), full((1, HY_FF)), full((2, HY_FF)),
                  full((HY_FF, HY_FF)), full((1, HY_FF)),
                  pl.BlockSpec((HY_FF, HY_CH), lambda o: (0, o)),
                  pl.BlockSpec((HY_FF, HY_CH), lambda o: (0, HY_ORDER + o)),
                  full((L, HY_CH)), full((L, L)), full((L, L)), full((L, L)), full((L, L))],
        out_specs=[pl.BlockSpec((None, L, HY_CH), lambda o: (o, 0, 0)),
                   pl.BlockSpec((None, L, HY_CH), lambda o: (o, 0, 0)),
                   pl.BlockSpec((None, 1, HY_CH), lambda o: (o, 0, 0))],
        out_shape=[jax.ShapeDtypeStruct((HY_ORDER, L, HY_CH), F32),
                   jax.ShapeDtypeStruct((HY_ORDER, L, HY_CH), F32),
                   jax.ShapeDtypeStruct((HY_ORDER, 1, HY_CH), F32)],
        compiler_params=_cparams(("arbitrary",)),
        name="hyena_filters",
    )(zpos, w1p, b1, freq, w2, b2, w3, w3, decay, c_hi, c_lo, s_hi, s_lo)


HY_CG = 256
HY_STEP_ROWS = 1024


def _hyena_kernel(v_ref, x1_ref, x2_ref, wv_ref, w1_ref, w2_ref, bv_ref, b1_ref, b2_ref, skip_ref,
                  kr_ref, ki_ref, kn_ref, c_ref, sp_ref, spt_ref, o_ref, *, L):
    row = lax.broadcasted_iota(jnp.int32, (L, 1), 0)
    first = row == 0
    last = row == L - 1
    chains = [(b, slice(g * HY_CG, (g + 1) * HY_CG)) for b in range(v_ref.shape[0]) for g in range(HY_CH // HY_CG)]

    def short_conv(u_ref, w_ref, b_ref, b, cols):
        u = u_ref[b, :, cols]
        prev = jnp.where(first, 0.0, pltpu.roll(u, 1, axis=0))
        nxt = jnp.where(last, 0.0, pltpu.roll(u, L - 1, axis=0))
        return prev * w_ref[0:1, cols] + u * w_ref[1:2, cols] + nxt * w_ref[2:3, cols] + b_ref[:, cols]

    gate_refs = ((x1_ref, w1_ref, b1_ref), (x2_ref, w2_ref, b2_ref))
    z = [short_conv(v_ref, wv_ref, bv_ref, b, cols) for b, cols in chains]
    for o in range(HY_ORDER):
        zb = [zz.astype(BF16) for zz in z]
        a = [_dot(c_ref[...], x) for x in zb]
        s = [_dot(sp_ref[...], x) for x in zb]
        conv = []
        for n, (b, cols) in enumerate(chains):
            kr = kr_ref[o, :, cols]
            ki = ki_ref[o, :, cols]
            qt = a[n] * kr + s[n] * ki
            qb = s[n] * jnp.where(first, kn_ref[o, :, cols], kr) - a[n] * ki
            conv.append(_dot(c_ref[...], qt.astype(BF16)) + _dot(spt_ref[...], qb.astype(BF16)))
        z = [short_conv(*gate_refs[o], b, cols) * (conv[n] + z[n] * skip_ref[o:o + 1, cols])
             for n, (b, cols) in enumerate(chains)]
    for n, (b, cols) in enumerate(chains):
        o_ref[b, :, cols] = z[n].astype(BF16)


def _hyena_mixer(u3, conv_w, conv_b, skip, filt, dft, L):
    B = u3.shape[0]
    nb = max(1, HY_STEP_ROWS // L)
    kr, ki, kn = filt
    c_b, sp_b, spt_b = dft
    once = pl.Buffered(1)
    ublk = lambda part: pl.BlockSpec((nb, L, HY_CH), lambda b: (b, 0, part))
    wblk = lambda part: pl.BlockSpec((3, HY_CH), lambda b: (0, part))
    bblk = lambda part: pl.BlockSpec((1, HY_CH), lambda b: (0, part))
    fblk = lambda rows: pl.BlockSpec((HY_ORDER, rows, HY_CH), lambda b: (0, 0, 0), pipeline_mode=once)
    mat = pl.BlockSpec((L, L), lambda b: (0, 0), pipeline_mode=once)
    return pl.pallas_call(
        functools.partial(_hyena_kernel, L=L),
        grid=(B // nb,),
        in_specs=[ublk(0), ublk(1), ublk(2), wblk(0), wblk(1), wblk(2), bblk(0), bblk(1), bblk(2),
                  pl.BlockSpec((HY_ORDER, HY_CH), lambda b: (0, 0)),
                  fblk(L), fblk(L), fblk(1), mat, mat, mat],
        out_specs=pl.BlockSpec((nb, L, HY_CH), lambda b: (b, 0, 0)),
        out_shape=jax.ShapeDtypeStruct((B, L, HY_CH), BF16),
        compiler_params=_cparams(("arbitrary",)),
        name="hyena_mixer",
    )(u3, u3, u3, conv_w, conv_w, conv_w, conv_b, conv_b, conv_b, skip, kr, ki, kn, c_b, sp_b, spt_b)


GLA_PAIR = 2 * GLA_CHUNK
GLA_STATES = 2 * GLA_HEADS
GLA_SCAN = 256


def _gla_kernel(*refs, L, has_state, emit_state, out_slot, n_alias):
    it = iter(refs)
    qkvg_ref, a_ref, wa_ref, ba_ref, ng_ref, scan_ref = (next(it) for _ in range(6))
    st0_refs = (next(it), next(it)) if has_state else None
    for _ in range(n_alias):
        next(it)
    y_ref = next(it)
    stout_refs = (next(it), next(it)) if emit_state else None
    qd_s, ki_s, ke_s, qi_s, ku_s, et_s, vb_s, vt_s, o_s, st_s = (next(it) for _ in range(10))

    npair = L // GLA_PAIR
    pre = _dot3(a_ref[...], wa_ref[...]) + ba_ref[...]
    la = (jnp.minimum(pre, 0.0) - jnp.log(1.0 + jnp.exp(-jnp.abs(pre)))) * (1.0 / GLA_TAU)

    scan = scan_ref[...]
    pfx, tot = [], []
    for r in range(0, L, GLA_SCAN):
        x = la[r:r + GLA_SCAN]
        hi = x.astype(BF16)
        r1 = x - hi.astype(F32)
        mid = r1.astype(BF16)
        lo = (r1 - mid.astype(F32)).astype(BF16)
        res = _dot(scan, hi) + (_dot(scan, mid) + _dot(scan, lo))
        pfx.append(res[:GLA_SCAN])
        tot.append(res[GLA_SCAN:])
    pfx = jnp.concatenate(pfx, axis=0) if len(pfx) > 1 else pfx[0]
    tot = jnp.concatenate(tot, axis=0) if len(tot) > 1 else tot[0]

    upper = (lax.broadcasted_iota(jnp.int32, (L, 1), 0) & GLA_CHUNK) != 0
    q = qkvg_ref[:, 0:GLA_QK] * (GLA_DK ** -0.5)
    k = qkvg_ref[:, GLA_QK:2 * GLA_QK]
    for dirn in range(2):
        cols = slice(dirn * GLA_QK, (dirn + 1) * GLA_QK)
        if dirn == 0:
            cum = pfx[:, cols]
            rest = tot[:, cols] - cum
            first = jnp.logical_not(upper)
        else:
            rest = pfx[:, cols] - la[:, cols]
            cum = tot[:, cols] - rest
            first = upper
        et = jnp.exp(tot[:, cols])
        et_other = jnp.where(upper, pltpu.roll(et, GLA_CHUNK, axis=0), pltpu.roll(et, L - GLA_CHUNK, axis=0))
        qd = q * jnp.exp(cum)
        ke = k * jnp.exp(rest)
        et_s[dirn] = et
        qd_s[dirn] = qd.astype(BF16)
        ki_s[dirn] = (k * jnp.exp(-cum)).astype(BF16)
        ke_s[dirn] = ke.astype(BF16)
        qi_s[dirn] = jnp.where(first, qd, qd * et_other).astype(BF16)
        ku_s[dirn] = jnp.where(first, ke * et_other, ke).astype(BF16)
    vb_s[...] = qkvg_ref[:, 2 * GLA_QK:2 * GLA_QK + GLA_VW].astype(BF16)

    def transpose_v(n2, carry):
        r0 = pl.multiple_of(n2 * GLA_PAIR, GLA_PAIR)
        for h in range(GLA_HEADS):
            vp = qkvg_ref[pl.ds(r0, GLA_PAIR), 2 * GLA_QK + h * GLA_DV:2 * GLA_QK + (h + 1) * GLA_DV]
            vt_s[n2, h * GLA_DV:(h + 1) * GLA_DV, :] = vp.T.astype(BF16)
        return carry

    lax.fori_loop(0, npair, transpose_v, 0)

    if has_state:
        zero = jnp.zeros((GLA_DK, GLA_DV), F32)
        for dirn in range(2):
            for h in range(GLA_HEADS):
                s0 = st0_refs[dirn][h]
                both = jnp.concatenate([s0, zero] if h % 2 == 0 else [zero, s0], axis=0)
                idx = dirn * GLA_HEADS + h
                st_s[idx * GLA_DV:(idx + 1) * GLA_DV, :] = both.T
    else:
        st_s[...] = jnp.zeros_like(st_s)

    lane = lax.broadcasted_iota(jnp.int32, (1, 2 * GLA_DK), 1)
    head_lanes = (lane < GLA_DK, lane >= GLA_DK)
    ri = lax.broadcasted_iota(jnp.int32, (GLA_PAIR, GLA_PAIR), 0)
    ci = lax.broadcasted_iota(jnp.int32, (GLA_PAIR, GLA_PAIR), 1)
    same_chunk = ((ri ^ ci) & GLA_CHUNK) == 0
    keep = (ci <= ri, ci >= ri)

    def pair_step(n, carry):
        for dirn in range(2):
            n2 = n if dirn == 0 else npair - 1 - n
            r0 = pl.multiple_of(n2 * GLA_PAIR, GLA_PAIR)
            rows = pl.ds(r0, GLA_PAIR)
            for p in range(GLA_HEADS // 2):
                lanes = slice(p * 2 * GLA_DK, (p + 1) * 2 * GLA_DK)
                qd = qd_s[dirn, rows, lanes]
                qi = qi_s[dirn, rows, lanes]
                zeros = jnp.zeros_like(qd)
                dec = (et_s[dirn, pl.ds(r0, 8), lanes][0:1, :]
                       * et_s[dirn, pl.ds(r0 + GLA_CHUNK, 8), lanes][0:1, :])
                q2 = jnp.concatenate([jnp.where(head_lanes[0], qd, zeros), jnp.where(head_lanes[1], qd, zeros)],
                                     axis=0)
                att_d = _dot_nt(q2, ki_s[dirn, rows, lanes])
                att_o = _dot_nt(q2, ke_s[dirn, rows, lanes])
                s0 = (dirn * GLA_HEADS + 2 * p) * GLA_DV
                st2 = st_s[s0:s0 + 2 * GLA_DV, :]
                vt2 = vt_s[n2, 2 * p * GLA_DV:(2 * p + 2) * GLA_DV, :]
                st_s[s0:s0 + 2 * GLA_DV, :] = st2 * dec + _dot(vt2, ku_s[dirn, rows, lanes])
                for hh in range(2):
                    h = 2 * p + hh
                    hr = slice(hh * GLA_PAIR, (hh + 1) * GLA_PAIR)
                    att = jnp.where(keep[dirn], jnp.where(same_chunk, att_d[hr], att_o[hr]), 0.0)
                    qm = jnp.where(head_lanes[hh], qi, zeros)
                    o = (_dot(att.astype(BF16), vb_s[rows, h * GLA_DV:(h + 1) * GLA_DV])
                         + _dot_nt(qm, st2[hh * GLA_DV:(hh + 1) * GLA_DV].astype(BF16)))
                    o_s[dirn, rows, h * GLA_DV:(h + 1) * GLA_DV] = o
        return carry

    lax.fori_loop(0, npair, pair_step, 0, unroll=2)

    for h in range(GLA_HEADS):
        cols = slice(h * GLA_DV, (h + 1) * GLA_DV)
        o = _rms(o_s[0, :, cols] + o_s[1, :, cols], ng_ref[...])
        g = qkvg_ref[:, 2 * GLA_QK + GLA_VW + h * GLA_DV:2 * GLA_QK + GLA_VW + (h + 1) * GLA_DV]
        y_ref[:, cols] = (o * (g * jax.nn.sigmoid(g))).astype(BF16)
    if emit_state:
        for dirn in range(2):
            dst = stout_refs[dirn]
            if out_slot is not None:
                for d in range(DEPTH):
                    if d != out_slot:
                        dst[d] = jnp.zeros(dst.shape[1:], F32)
                dst = dst.at[out_slot]
            for h in range(GLA_HEADS):
                idx = dirn * GLA_HEADS + h
                half = (h % 2) * GLA_DK
                dst[h] = st_s[idx * GLA_DV:(idx + 1) * GLA_DV, :].T[half:half + GLA_DK, :]


def _gla_scan_matrix():
    i = np.arange(GLA_SCAN)
    same = (i[:, None] // GLA_CHUNK) == (i[None, :] // GLA_CHUNK)
    tri = same & (i[None, :] <= i[:, None])
    return jnp.asarray(np.concatenate([tri, same], axis=0).astype(np.float32)).astype(BF16)


def _gla_mixer(u3, wa_p, ba_p, norm_g, st0, layer, L, emit_state):
    B = u3.shape[0]
    has_state = st0 is not None
    head_state = (GLA_HEADS, GLA_DK, GLA_DV)
    in_specs = [pl.BlockSpec((None, L, 1536), lambda b: (b, 0, U_GLA // 1536)),
                pl.BlockSpec((None, L, 128), lambda b: (b, 0, U_GLA_A // 128)),
                pl.BlockSpec((128, 2 * GLA_QK), lambda b: (0, 0)),
                pl.BlockSpec((1, 2 * GLA_QK), lambda b: (0, 0)),
                pl.BlockSpec((1, GLA_DV), lambda b: (0, 0)),
                pl.BlockSpec((2 * GLA_SCAN, GLA_SCAN), lambda b: (0, 0))]
    args = [u3, u3, wa_p, ba_p, norm_g, _gla_scan_matrix()]
    if has_state:
        in_specs += [pl.BlockSpec((None, None) + head_state, lambda b: (b, layer, 0, 0, 0))] * 2
        args += list(st0)
    out_specs = [pl.BlockSpec((None, L, GLA_VW), lambda b: (b, 0, 0))]
    out_shape = [jax.ShapeDtypeStruct((B, L, GLA_VW), BF16)]
    aliases = {}
    fresh = emit_state is not None and all(prev is None for prev in emit_state)
    if emit_state is not None:
        slot_spec = (pl.BlockSpec((None, DEPTH) + head_state, lambda b: (b, 0, 0, 0, 0)) if fresh else
                     pl.BlockSpec((None, None) + head_state, lambda b: (b, layer, 0, 0, 0)))
        out_specs += [slot_spec] * 2
        out_shape += [jax.ShapeDtypeStruct((B, DEPTH) + head_state, F32)] * 2
        for k, prev in enumerate(emit_state):
            if prev is not None:
                aliases[len(args)] = 1 + k
                in_specs.append(pl.BlockSpec(memory_space=pl.ANY))
                args.append(prev)
    npair = L // GLA_PAIR
    qk_bf16 = pltpu.VMEM((2, L, GLA_QK), BF16)
    res = pl.pallas_call(
        functools.partial(_gla_kernel, L=L, has_state=has_state, emit_state=emit_state is not None,
                          out_slot=layer if fresh else None, n_alias=len(aliases)),
        grid=(B,),
        in_specs=in_specs,
        out_specs=out_specs,
        out_shape=out_shape,
        input_output_aliases=aliases,
        scratch_shapes=[qk_bf16,
                        qk_bf16,
                        qk_bf16,
                        qk_bf16,
                        qk_bf16,
                        pltpu.VMEM((2, L, GLA_QK), F32),
                        pltpu.VMEM((L, GLA_VW), BF16),
                        pltpu.VMEM((npair, GLA_VW, GLA_PAIR), BF16),
                        pltpu.VMEM((2, L, GLA_VW), F32),
                        pltpu.VMEM((GLA_STATES * GLA_DV, 2 * GLA_DK), F32)],
        compiler_params=_cparams(("arbitrary",)),
        name="gla_mixer",
    )(*args)
    return (res[0], res[1], res[2]) if emit_state is not None else (res[0], None, None)


MLA_HW = 256
MLA_TQ = 512
MLA_KCH = 256


def _rope(x, cos, sin):
    lane = lax.broadcasted_iota(jnp.int32, x.shape, 1)
    partner = jnp.where((lane & 16) == 0, pltpu.roll(x, 112, axis=1), pltpu.roll(x, 16, axis=1))
    return x * cos + partner * sin


def _mla_kernel(*refs, L, Lk, tq, rope, emit, out_slot, n_alias):
    it = iter(refs)
    u_ref, qn_ref, wq_ref, kvn_ref, wkv_ref = (next(it) for _ in range(5))
    cos_ref, sin_ref, cckv_ref, ckr_ref = (next(it) for _ in range(4)) if rope else (None,) * 4
    for _ in range(n_alias):
        next(it)
    y_ref = next(it)
    ckv_out, kr_out = (next(it), next(it)) if emit else (None, None)
    kf_s, v_s = next(it), next(it)

    qi = pl.program_id(1)

    def project(keys, rope_part, r0):
        kv = _dot(keys, wkv_ref[...])
        for h in range(MLA_HEADS):
            kf_s[h, r0:r0 + MLA_KCH, 0:MLA_NOPE] = kv[:, h * 256:h * 256 + MLA_NOPE].astype(BF16)
            kf_s[h, r0:r0 + MLA_KCH, MLA_NOPE:MLA_HW] = rope_part
            v_s[h, r0:r0 + MLA_KCH, :] = kv[:, h * 256 + MLA_NOPE:(h + 1) * 256].astype(BF16)

    @pl.when(qi == 0)
    def _():
        if emit and out_slot is not None:
            for dst in (ckv_out, kr_out):
                for d in range(DEPTH):
                    if d != out_slot:
                        dst[d] = jnp.zeros(dst.shape[1:], F32)
        ckv_dst, kr_dst = (ckv_out, kr_out) if out_slot is None else (ckv_out.at[out_slot], kr_out.at[out_slot])
        for r0 in range(0, L, MLA_KCH):
            ckvn = _rms(u_ref[r0:r0 + MLA_KCH, MLA_Q_RANK:MLA_Q_RANK + MLA_KV_RANK], kvn_ref[...])
            kr = u_ref[r0:r0 + MLA_KCH, 640:768]
            kr = jnp.where(lax.broadcasted_iota(jnp.int32, kr.shape, 1) < MLA_ROPE, kr, 0.0)
            if emit:
                ckv_dst[r0:r0 + MLA_KCH, :] = ckvn
                kr_dst[r0:r0 + MLA_KCH, :] = kr[:, 0:MLA_ROPE]
            if rope:
                kr = _rope(kr, cos_ref[r0:r0 + MLA_KCH, :], sin_ref[r0:r0 + MLA_KCH, :])
            project(ckvn.astype(BF16), kr.astype(BF16), r0)
        if rope:
            for r0 in range(0, Lk - L, MLA_KCH):
                ckr = ckr_ref[r0:r0 + MLA_KCH, :]
                ckr = jnp.concatenate([ckr, jnp.zeros_like(ckr)], axis=1)
                project(cckv_ref[r0:r0 + MLA_KCH, :].astype(BF16), ckr.astype(BF16), L + r0)

    q0 = pl.multiple_of(qi * tq, tq)
    cqn = _rms(u_ref[pl.ds(q0, tq), 0:MLA_Q_RANK], qn_ref[...]).astype(BF16)
    q = _dot(cqn, wq_ref[...]) * ((MLA_NOPE + MLA_ROPE) ** -0.5)
    if rope:
        cos = cos_ref[pl.ds(q0, tq), :]
        sin = sin_ref[pl.ds(q0, tq), :]
    for h in range(MLA_HEADS):
        q_nope = q[:, h * MLA_HW:h * MLA_HW + MLA_NOPE]
        q_rope = q[:, h * MLA_HW + MLA_NOPE:(h + 1) * MLA_HW]
        if rope:
            q_rope = _rope(q_rope, cos, sin)
        qh = jnp.concatenate([q_nope, q_rope], axis=1).astype(BF16)
        s = _dot_nt(qh, kf_s[h])
        p = jnp.exp(s - jnp.max(s, axis=-1, keepdims=True))
        denom = jnp.sum(p, axis=-1, keepdims=True)
        o = _dot(p.astype(BF16), v_s[h])
        y_ref[:, h * MLA_V:(h + 1) * MLA_V] = (o / denom).astype(BF16)


def _mla_mixer(u3, q_norm, wq_p, kv_norm, wkv, rope_args, layer, L, emit):
    B = u3.shape[0]
    rope = rope_args is not None
    Lk = L + (rope_args[2].shape[2] if rope else 0)
    tq = min(MLA_TQ, L)
    const = lambda shape: pl.BlockSpec(shape, lambda b, i: (0,) * len(shape))
    in_specs = [pl.BlockSpec((None, L, 768), lambda b, i: (b, 0, U_MLA // 768)),
                const((1, MLA_Q_RANK)), const((MLA_Q_RANK, MLA_HEADS * MLA_HW)),
                const((1, MLA_KV_RANK)), const((MLA_KV_RANK, MLA_HEADS * 256))]
    args = [u3, q_norm, wq_p, kv_norm, wkv]
    if rope:
        cos, sin, cckv, ckr = rope_args
        in_specs += [const((L, 128)), const((L, 128)),
                     pl.BlockSpec((None, None, Lk - L, MLA_KV_RANK), lambda b, i: (b, layer, 0, 0)),
                     pl.BlockSpec((None, None, Lk - L, MLA_ROPE), lambda b, i: (b, layer, 0, 0))]
        args += [cos, sin, cckv, ckr]
    out_specs = [pl.BlockSpec((None, tq, MLA_HEADS * MLA_V), lambda b, i: (b, i, 0))]
    out_shape = [jax.ShapeDtypeStruct((B, L, MLA_HEADS * MLA_V), BF16)]
    aliases = {}
    fresh = emit is not None and all(prev is None for prev in emit)
    if emit is not None:
        slot = ((None, DEPTH), lambda b, i: (b, 0, 0, 0)) if fresh else ((None, None), lambda b, i: (b, layer, 0, 0))
        out_specs += [pl.BlockSpec(slot[0] + (L, MLA_KV_RANK), slot[1]),
                      pl.BlockSpec(slot[0] + (L, MLA_ROPE), slot[1])]
        out_shape += [jax.ShapeDtypeStruct((B, DEPTH, L, MLA_KV_RANK), F32),
                      jax.ShapeDtypeStruct((B, DEPTH, L, MLA_ROPE), F32)]
        for k, prev in enumerate(emit):
            if prev is not None:
                aliases[len(args)] = 1 + k
                in_specs.append(pl.BlockSpec(memory_space=pl.ANY))
                args.append(prev)
    res = pl.pallas_call(
        functools.partial(_mla_kernel, L=L, Lk=Lk, tq=tq, rope=rope, emit=emit is not None,
                          out_slot=layer if fresh else None, n_alias=len(aliases)),
        grid=(B, L // tq),
        in_specs=in_specs,
        out_specs=out_specs,
        out_shape=out_shape,
        input_output_aliases=aliases,
        scratch_shapes=[pltpu.VMEM((MLA_HEADS, Lk, MLA_HW), BF16),
                        pltpu.VMEM((MLA_HEADS, Lk, MLA_V), BF16)],
        compiler_params=_cparams(("arbitrary", "arbitrary")),
        name="mla_mixer",
    )(*args)
    return res if emit is not None else (res[0], None, None)


def _outproj_kernel(x_ref, g_ref, yh_ref, yg_ref, ym_ref, wh_ref, wg_ref, wm_ref, o_ref):
    y = _dot(yh_ref[...], wh_ref[...]) + _dot(yg_ref[...], wg_ref[...]) + _dot(ym_ref[...], wm_ref[...])
    o_ref[...] = x_ref[...] + g_ref[...] * y


def _out_projection(x2d, mod, y_hy, y_gla, y_mla, w_out_b, layer, row_of_tile, tm):
    rows = x2d.shape[0]
    once = pl.Buffered(1)
    return pl.pallas_call(
        _outproj_kernel,
        grid=(rows // tm,),
        in_specs=[pl.BlockSpec((tm, D_MODEL), lambda i: (i, 0)),
                  _mod_spec(2, row_of_tile),
                  pl.BlockSpec((tm, HY_CH), lambda i: (i, 0)),
                  pl.BlockSpec((tm, GLA_VW), lambda i: (i, 0)),
                  pl.BlockSpec((tm, MLA_HEADS * MLA_V), lambda i: (i, 0)),
                  pl.BlockSpec((None, HY_CH, D_MODEL), lambda i: (layer, 0, 0), pipeline_mode=once),
                  pl.BlockSpec((None, GLA_VW, D_MODEL), lambda i: (layer, 1, 0), pipeline_mode=once),
                  pl.BlockSpec((None, MLA_HEADS * MLA_V, D_MODEL), lambda i: (layer, 1, 0), pipeline_mode=once)],
        out_specs=pl.BlockSpec((tm, D_MODEL), lambda i: (i, 0)),
        out_shape=jax.ShapeDtypeStruct((rows, D_MODEL), F32),
        compiler_params=_cparams(("arbitrary",)),
        name="out_projection",
    )(x2d, mod, y_hy, y_gla, y_mla, w_out_b, w_out_b, w_out_b)


FFN_TF = 512


def _ffn_kernel(x_ref, ln_ref, sh_ref, sc_ref, g_ref, wg_ref, wu_ref, wo_ref, lnf_ref, o_ref,
                h_ref, acc_ref, *, final):
    f = pl.program_id(1)

    @pl.when(f == 0)
    def _():
        _adaln_all(x_ref, ln_ref[...] * (1.0 + sc_ref[...]), sh_ref[...], h_ref)
        acc_ref[...] = jnp.zeros_like(acc_ref)

    h = h_ref[...]
    gate = _dot(h, wg_ref[...])
    up = _dot(h, wu_ref[...])
    act = (gate * jax.nn.sigmoid(gate) * up).astype(BF16)
    acc_ref[...] += _dot(act, wo_ref[...])

    @pl.when(f == pl.num_programs(1) - 1)
    def _():
        def body(i, carry):
            rows = pl.ds(pl.multiple_of(i * ADALN_ROWS, ADALN_ROWS), ADALN_ROWS)
            x = x_ref[rows, :] + g_ref[...] * acc_ref[rows, :]
            o_ref[rows, :] = _rms(x, lnf_ref[...]) if final else x
            return carry

        lax.fori_loop(0, x_ref.shape[0] // ADALN_ROWS, body, 0, unroll=4)


def _ffn(x2d, ln, mod, w_in_b, w_out_b, layer, ln_final, row_of_tile, tm, final):
    rows = x2d.shape[0]
    nf = D_FF // FFN_TF
    return pl.pallas_call(
        functools.partial(_ffn_kernel, final=final),
        grid=(rows // tm, nf),
        in_specs=[pl.BlockSpec((tm, D_MODEL), lambda i, f: (i, 0)),
                  pl.BlockSpec((1, D_MODEL), lambda i, f: (0, 0)),
                  _mod_spec(3, row_of_tile),
                  _mod_spec(4, row_of_tile),
                  _mod_spec(5, row_of_tile),
                  pl.BlockSpec((None, D_MODEL, FFN_TF), lambda i, f: (layer, 0, f)),
                  pl.BlockSpec((None, D_MODEL, FFN_TF), lambda i, f: (layer, 0, nf + f)),
                  pl.BlockSpec((None, FFN_TF, D_MODEL), lambda i, f: (layer, f, 0)),
                  pl.BlockSpec((1, D_MODEL), lambda i, f: (0, 0))],
        out_specs=pl.BlockSpec((tm, D_MODEL), lambda i, f: (i, 0)),
        out_shape=jax.ShapeDtypeStruct((rows, D_MODEL), F32),
        scratch_shapes=[pltpu.VMEM((tm, D_MODEL), BF16), pltpu.VMEM((tm, D_MODEL), F32)],
        compiler_params=_cparams(("arbitrary", "arbitrary")),
        name="ffn",
    )(x2d, ln, mod, mod, mod, w_in_b, w_in_b, w_out_b, ln_final)


def _dft_tables(L):
    k = np.arange(L)
    ang = np.pi * ((k[:, None] * k[None, :]) % (2 * L)) / L
    c = np.cos(ang)
    s = np.sin(ang)
    sp = s.copy()
    sp[0, :] = 1.0 - 2.0 * (k % 2)
    as32 = lambda a: jnp.asarray(a.astype(np.float32))
    return as32(c), as32(s), as32(sp), as32(sp.T.copy())


def _hyena_consts(L):
    c, s, sp, spt = _dft_tables(L)
    c_hi, c_lo = _split(c)
    s_hi, s_lo = _split(s)
    f32 = np.float32
    t = np.linspace(0.0, 1.0, L, dtype=f32)[:, None]
    w = (f32(2.0 * math.pi) * np.arange(L, dtype=f32)[:, None] / f32(L)).astype(f32)
    f = np.linspace(1e-4, HY_BANDS - 1, HY_BANDS, dtype=f32)
    zpos = np.concatenate([t, np.cos(f * w), -np.sin(f * w)], axis=-1).astype(f32)
    zpos = np.pad(zpos, ((0, 0), (0, 64 - HY_EMB)))
    min_decay = math.log(HY_TARGET) / HY_SLOW_DECAY
    max_decay = math.log(HY_TARGET) / HY_FAST_DECAY
    delta = np.abs(np.linspace(min_decay, max_decay, HY_CH, dtype=f32))
    decay = np.exp(-t * delta).astype(f32)
    filt_consts = (jnp.asarray(zpos), jnp.asarray(decay), c_hi, c_lo, s_hi, s_lo)
    main_consts = (c_hi, sp.astype(BF16), spt.astype(BF16))
    return filt_consts, main_consts


def _rope_tables(L):
    t = jnp.arange(L)
    half = MLA_ROPE // 2
    inv = ROPE_THETA ** (-jnp.arange(0, half, 2, dtype=F32) / half)
    ang_r = (t // GRID_W).astype(F32)[:, None] * inv
    ang_c = (t % GRID_W).astype(F32)[:, None] * inv
    cr, sr, cc, sc = jnp.cos(ang_r), jnp.sin(ang_r), jnp.cos(ang_c), jnp.sin(ang_c)
    cos = jnp.concatenate([cr, cr, cc, cc, jnp.ones((L, 128 - MLA_ROPE), F32)], axis=1)
    sin = jnp.concatenate([-sr, sr, -sc, sc, jnp.zeros((L, 128 - MLA_ROPE), F32)], axis=1)
    return cos, sin


def _prep_w_in_tail(w):
    mla0 = IN_MAIN + 2 * GLA_LOWRANK
    pad = jnp.zeros((DEPTH, IN_TN - (w.shape[1] - IN_MAIN), D_MODEL), w.dtype)
    return jnp.concatenate([w[:, mla0:], w[:, IN_MAIN:mla0], pad], axis=1).astype(BF16)


def _prep_w_uq(w):
    w = w.reshape(MLA_Q_RANK, MLA_HEADS, MLA_NOPE + MLA_ROPE)
    w = jnp.pad(w, ((0, 0), (0, 0), (0, MLA_HW - MLA_NOPE - MLA_ROPE)))
    return w.reshape(MLA_Q_RANK, MLA_HEADS * MLA_HW).astype(BF16)


def _prep_gla_decay(wa_f, ba_f, wa_b, ba_b):
    wa = jnp.zeros((128, 2 * GLA_QK), F32)
    wa = wa.at[U_GLA_A_LANE:U_GLA_A_LANE + GLA_LOWRANK, 0:GLA_QK].set(wa_f)
    wa = wa.at[U_GLA_A_LANE + GLA_LOWRANK:U_GLA_A_LANE + 2 * GLA_LOWRANK, GLA_QK:].set(wa_b)
    return wa, jnp.concatenate([ba_f, ba_b])[None, :]


def _trunk_layer(x2d, B, L, mod, row_of_tile, lw, consts, ctx, caches, final, ln_final):
    tm = 1024
    tm_out = 512
    u = _in_projection(x2d, lw['ln_mix'], mod, lw['w_in'], lw['w_in_tail'], lw['layer'], row_of_tile(tm), tm)
    u3 = u.reshape(B, L, U_W)
    filt_consts, main_consts = consts['hyena']
    filt = _hyena_filters(L, filt_consts, lw['hy_w1'], lw['hy_b1'], lw['hy_freq'], lw['hy_w2'],
                          lw['hy_b2'], lw['hy_w3'])
    y_hy = _hyena_mixer(u3, lw['hy_conv_w'], lw['hy_conv_b'], lw['hy_skip'], filt, main_consts, L)
    if ctx is None:
        y_gla, s_f, s_b = _gla_mixer(u3, lw['gla_wa'], lw['gla_ba'], lw['gla_norm'], None, lw['layer'], L, caches[2:])
        y_mla, ckv, krope = _mla_mixer(u3, lw['mla_q_norm'], lw['mla_w_uq'], lw['mla_kv_norm'],
                                       lw['mla_w_ukv'], None, lw['layer'], L, caches[:2])
        extras = (ckv, krope, s_f, s_b)
    else:
        cache_ckv, cache_krope, s0f, s0b = ctx
        y_gla, _, _ = _gla_mixer(u3, lw['gla_wa'], lw['gla_ba'], lw['gla_norm'], (s0f, s0b), lw['layer'], L, None)
        cos, sin = consts['rope']
        y_mla, _, _ = _mla_mixer(u3, lw['mla_q_norm'], lw['mla_w_uq'], lw['mla_kv_norm'], lw['mla_w_ukv'],
                                 (cos, sin, cache_ckv, cache_krope), lw['layer'], L, None)
        extras = None
    rows = B * L
    x2d = _out_projection(x2d, mod, y_hy.reshape(rows, -1), y_gla.reshape(rows, -1), y_mla.reshape(rows, -1),
                          lw['w_out'], lw['layer'], row_of_tile(tm_out), tm_out)
    x2d = _ffn(x2d, lw['ln_ffn'], mod, lw['w_ffn_in'], lw['w_ffn_out'], lw['layer'], ln_final, row_of_tile(tm_out),
               tm_out, final)
    return x2d, extras


def kernel(x_prompt, x_sample, cache_mla_ckv, cache_mla_krope, state_gla_fwd, state_gla_bwd, c, c_ctx, w_mod, b_mod, ln_mix, w_in, hy_conv_w, hy_conv_b, hy_filt_w1, hy_filt_b1, hy_filt_freq, hy_filt_w2, hy_filt_b2, hy_filt_w3, hy_skip, gla_wa_f, gla_ba_f, gla_wa_b, gla_ba_b, gla_norm, mla_q_norm, mla_w_uq, mla_kv_norm, mla_w_ukv, w_out, ln_ffn, w_ffn_in, w_ffn_out, ln_final):
    Bc, Lc, _ = x_prompt.shape
    Bl, Ll, _ = x_sample.shape
    assert 1 + Bl <= MOD_ROWS

    c_all = jnp.concatenate([c_ctx[None, :], c, jnp.zeros((MOD_ROWS - 1 - Bl, D_MODEL), F32)], axis=0)
    mod_all = _modulation(c_all, w_mod, b_mod).reshape(DEPTH, MOD_ROWS, 6, 1, D_MODEL)

    consts_ctx = {'hyena': _hyena_consts(Lc)}
    consts_lat = {'hyena': _hyena_consts(Ll), 'rope': _rope_tables(Ll)}
    ctx_rows = lambda tm: (lambda i: 0)
    lat_rows = lambda tm: (lambda i: 1 + (i * tm) // Ll)
    lnf = ln_final[None, :]

    x_ctx = x_prompt.reshape(Bc * Lc, D_MODEL)
    x_lat = x_sample.reshape(Bl * Ll, D_MODEL)
    w_in_t = jnp.swapaxes(w_in, 1, 2)
    w_in_b = w_in_t.astype(BF16)
    w_in_tail = _prep_w_in_tail(w_in_t)
    w_out_b = w_out.astype(BF16)
    w_ffn_in_b = w_ffn_in.astype(BF16)
    w_ffn_out_b = w_ffn_out.astype(BF16)
    caches = (None, None, None, None)
    for l in range(DEPTH):
        wa, ba = _prep_gla_decay(gla_wa_f[l], gla_ba_f[l], gla_wa_b[l], gla_ba_b[l])
        lw = {
            'layer': l, 'ln_mix': ln_mix[l][None, :], 'w_in': w_in_b, 'w_in_tail': w_in_tail,
            'hy_conv_w': hy_conv_w[l], 'hy_conv_b': hy_conv_b[l][None, :],
            'hy_w1': jnp.pad(hy_filt_w1[l], ((0, 64 - HY_EMB), (0, 0))), 'hy_b1': hy_filt_b1[l][None, :],
            'hy_freq': hy_filt_freq[l], 'hy_w2': hy_filt_w2[l], 'hy_b2': hy_filt_b2[l][None, :],
            'hy_w3': hy_filt_w3[l], 'hy_skip': hy_skip[l],
            'gla_wa': wa, 'gla_ba': ba, 'gla_norm': gla_norm[l][None, :],
            'mla_q_norm': mla_q_norm[l][None, :], 'mla_w_uq': _prep_w_uq(mla_w_uq[l]),
            'mla_kv_norm': mla_kv_norm[l][None, :], 'mla_w_ukv': mla_w_ukv[l].astype(BF16),
            'w_out': w_out_b, 'ln_ffn': ln_ffn[l][None, :],
            'w_ffn_in': w_ffn_in_b, 'w_ffn_out': w_ffn_out_b,
        }
        final = l == DEPTH - 1
        mod = mod_all[l]
        x_ctx, caches = _trunk_layer(x_ctx, Bc, Lc, mod, ctx_rows, lw, consts_ctx, None, caches, final, lnf)
        ctx = (cache_mla_ckv, cache_mla_krope, state_gla_fwd, state_gla_bwd)
        x_lat, _ = _trunk_layer(x_lat, Bl, Ll, mod, lat_rows, lw, consts_lat, ctx, None, final, lnf)
    return (x_ctx.reshape(Bc, Lc, D_MODEL), x_lat.reshape(Bl, Ll, D_MODEL)) + tuple(caches)
```

```python
import functools
import math

import numpy as np
import jax
import jax.numpy as jnp
from jax import lax
from jax.experimental import pallas as pl
from jax.experimental.pallas import tpu as pltpu

F32 = jnp.float32
BF16 = jnp.bfloat16

D_MODEL = 2048
DEPTH = 2
GRID_W = 64
EPS = 1e-6
HY_CH = 512
HY_ORDER = 2
HY_BANDS = 16
HY_EMB = 1 + 2 * HY_BANDS
HY_FF = 64
HY_FAST_DECAY = 0.3
HY_SLOW_DECAY = 1.5
HY_TARGET = 1e-2
HY_IN = 3 * HY_CH
GLA_HEADS = 4
GLA_DK = 64
GLA_DV = 128
GLA_LOWRANK = 16
GLA_TAU = 16.0
GLA_CHUNK = 64
GLA_QK = GLA_HEADS * GLA_DK
GLA_VW = GLA_HEADS * GLA_DV
MLA_HEADS = 8
MLA_Q_RANK = 384
MLA_KV_RANK = 256
MLA_NOPE = 128
MLA_ROPE = 64
MLA_V = 128
ROPE_THETA = 10000.0
D_FF = -(-8 * D_MODEL // (3 * 256)) * 256

U_HY = 0
U_GLA = 1536
U_MLA = 3072
U_GLA_A = 3712
U_GLA_A_LANE = 64
U_W = 3840
MOD_ROWS = 16

VMEM_LIMIT_V7X = 48 * 1024 * 1024


def _cparams(sem):
    return pltpu.CompilerParams(dimension_semantics=sem, vmem_limit_bytes=VMEM_LIMIT_V7X)


def _dot(a, b):
    return jnp.dot(a, b, preferred_element_type=F32)


def _dot_nt(a, b):
    return lax.dot_general(a, b, (((1,), (1,)), ((), ())), preferred_element_type=F32)


def _split(x):
    hi = x.astype(BF16)
    lo = (x - hi.astype(F32)).astype(BF16)
    return hi, lo


def _dot3(a, b):
    a_hi, a_lo = _split(a)
    b_hi, b_lo = _split(b)
    return _dot(a_hi, b_hi) + (_dot(a_lo, b_hi) + _dot(a_hi, b_lo))


def _rms(x, g):
    ms = jnp.mean(x * x, axis=-1, keepdims=True)
    return x * lax.rsqrt(ms + EPS) * g


ADALN_ROWS = 16


def _adaln_group(x_ref, gain, shift, h_ref, r):
    rows = pl.ds(pl.multiple_of(r, ADALN_ROWS), ADALN_ROWS)
    x = x_ref[rows, :]
    ms = jnp.mean(x * x, axis=-1, keepdims=True)
    h_ref[rows, :] = (x * lax.rsqrt(ms + EPS) * gain + shift).astype(BF16)


def _adaln_all(x_ref, gain, shift, h_ref):
    def body(i, carry):
        _adaln_group(x_ref, gain, shift, h_ref, i * ADALN_ROWS)
        return carry

    lax.fori_loop(0, x_ref.shape[0] // ADALN_ROWS, body, 0, unroll=8)


def _mod_kernel(c_ref, w_ref, b_ref, o_ref):
    c = c_ref[...]
    s = (c * jax.nn.sigmoid(c)).astype(BF16)
    o_ref[...] = _dot(s, w_ref[...].astype(BF16)) + b_ref[...]


def _modulation(c_all, w_mod, b_mod):
    tn = 1024
    n6 = 6 * D_MODEL
    return pl.pallas_call(
        _mod_kernel,
        grid=(DEPTH, n6 // tn),
        in_specs=[pl.BlockSpec((MOD_ROWS, D_MODEL), lambda l, j: (0, 0)),
                  pl.BlockSpec((None, D_MODEL, tn), lambda l, j: (l, 0, j)),
                  pl.BlockSpec((None, 1, tn), lambda l, j: (l, 0, j))],
        out_specs=pl.BlockSpec((None, MOD_ROWS, tn), lambda l, j: (l, 0, j)),
        out_shape=jax.ShapeDtypeStruct((DEPTH, MOD_ROWS, n6), F32),
        compiler_params=_cparams(("arbitrary", "arbitrary")),
        name="modulation",
    )(c_all, w_mod, b_mod.reshape(DEPTH, 1, n6))


def _mod_spec(chunk, row_of_tile, tile_of_step=lambda i, *_: i):
    return pl.BlockSpec((None, None, 1, D_MODEL), lambda *g: (row_of_tile(tile_of_step(*g)), chunk, 0, 0))


IN_TN = 768
IN_MAIN = 3072
IN_PRE_ROWS = 256


def _inproj_kernel(x0_ref, sh0_ref, sc0_ref, xn_ref, shn_ref, scn_ref, ln_ref, w_ref, wt_ref, o_ref, ha_ref, hb_ref):
    i = pl.program_id(0)
    j = pl.program_id(1)

    @pl.when((i == 0) & (j == 0))
    def _():
        _adaln_all(x0_ref, ln_ref[...] * (1.0 + sc0_ref[...]), sh0_ref[...], ha_ref)

    gain = ln_ref[...] * (1.0 + scn_ref[...])
    shift = shn_ref[...]
    base = jnp.clip(j - 1, 0, 3) * IN_PRE_ROWS

    def step(cur_ref, nxt_ref):
        def prepare_next():
            for t in range(IN_PRE_ROWS // ADALN_ROWS):
                rows = pl.ds(t * ADALN_ROWS, ADALN_ROWS)
                x = xn_ref[rows, :]
                ms = jnp.mean(x * x, axis=-1, keepdims=True)
                out_rows = pl.ds(pl.multiple_of(base + t * ADALN_ROWS, ADALN_ROWS), ADALN_ROWS)
                nxt_ref[out_rows, :] = (x * lax.rsqrt(ms + EPS) * gain + shift).astype(BF16)

        @pl.when(j < IN_MAIN // IN_TN)
        def _():
            prepare_next()
            o_ref[...] = _dot_nt(cur_ref[...], w_ref[...])

        @pl.when(j >= IN_MAIN // IN_TN)
        def _():
            prepare_next()
            o_ref[...] = _dot_nt(cur_ref[...], wt_ref[...])

    @pl.when(i % 2 == 0)
    def _():
        step(ha_ref, hb_ref)

    @pl.when(i % 2 == 1)
    def _():
        step(hb_ref, ha_ref)


def _in_projection(x2d, ln, mod, w_in_b, w_tail, layer, row_of_tile, tm):
    rows = x2d.shape[0]
    n_main = IN_MAIN // IN_TN
    n_tiles = rows // tm
    per_tile = tm // IN_PRE_ROWS
    assert U_W // IN_TN - 1 == per_tile and U_W - IN_MAIN == IN_TN
    nxt = lambda i, *_: jnp.minimum(i + 1, n_tiles - 1)
    once = pl.Buffered(1)
    return pl.pallas_call(
        _inproj_kernel,
        grid=(n_tiles, U_W // IN_TN),
        in_specs=[pl.BlockSpec((tm, D_MODEL), lambda i, j: (0, 0), pipeline_mode=once),
                  _mod_spec(0, row_of_tile, lambda *_: 0),
                  _mod_spec(1, row_of_tile, lambda *_: 0),
                  pl.BlockSpec((IN_PRE_ROWS, D_MODEL),
                               lambda i, j: (nxt(i) * per_tile + jnp.clip(j - 1, 0, per_tile - 1), 0)),
                  _mod_spec(0, row_of_tile, nxt),
                  _mod_spec(1, row_of_tile, nxt),
                  pl.BlockSpec((1, D_MODEL), lambda i, j: (0, 0)),
                  pl.BlockSpec((None, IN_TN, D_MODEL), lambda i, j: (layer, jnp.minimum(j, n_main - 1), 0)),
                  pl.BlockSpec((None, U_W - IN_MAIN, D_MODEL), lambda i, j: (layer, 0, 0))],
        out_specs=pl.BlockSpec((tm, IN_TN), lambda i, j: (i, j)),
        out_shape=jax.ShapeDtypeStruct((rows, U_W), F32),
        scratch_shapes=[pltpu.VMEM((tm, D_MODEL), BF16), pltpu.VMEM((tm, D_MODEL), BF16)],
        compiler_params=_cparams(("arbitrary", "arbitrary")),
        name="in_projection",
    )(x2d, mod, mod, x2d, mod, mod, ln, w_in_b, w_tail)


def _filter_kernel(z_ref, w1_ref, b1_ref, fr_ref, w2_ref, b2_ref, w3f_ref, w3b_ref, dec_ref,
                   chi_ref, clo_ref, shi_ref, slo_ref, kr_ref, ki_ref, kn_ref, *, L):
    h = jnp.sin(fr_ref[0:1, :] * (_dot3(z_ref[...], w1_ref[...]) + b1_ref[...]))
    h = jnp.sin(fr_ref[1:2, :] * (_dot3(h, w2_ref[...]) + b2_ref[...]))
    dec = dec_ref[...]
    row = lax.broadcasted_iota(jnp.int32, (L, 1), 0)
    kf = _dot3(h, w3f_ref[...]) * dec
    kb = jnp.where(row == 0, 0.0, _dot3(h, w3b_ref[...]) * dec)
    p_hi, p_lo = _split(kf + kb)
    m_hi, m_lo = _split(kf - kb)
    chi, clo = chi_ref[...], clo_ref[...]
    shi, slo = shi_ref[...], slo_ref[...]
    kc = _dot(chi, p_hi) + (_dot(clo, p_hi) + _dot(chi, p_lo))
    ks = _dot(shi, m_hi) + (_dot(slo, m_hi) + _dot(shi, m_lo))
    sign = jnp.where((row & 1) == 0, 1.0, -1.0)
    kn = jnp.sum((kf + kb) * sign, axis=0, keepdims=True)
    kr_ref[...] = kc * jnp.where(row == 0, 0.5 / L, 1.0 / L)
    ki_ref[...] = ks * (-1.0 / L)
    kn_ref[...] = kn * (0.5 / L)


def _hyena_filters(L, consts, w1p, b1, freq, w2, b2, w3):
    zpos, decay, c_hi, c_lo, s_hi, s_lo = consts
    full = lambda shape: pl.BlockSpec(shape, lambda o: (0,) * len(shape))
    return pl.pallas_call(
        functools.partial(_filter_kernel, L=L),
        grid=(HY_ORDER,),
        in_specs=[full((L, 64)), full((64, HY_FF)), full((1, HY_FF)), full((2, HY_FF)),
                  full((HY_FF, HY_FF)), full((1, HY_FF)),
                  pl.BlockSpec((HY_FF, HY_CH), lambda o: (0, o)),
                  pl.BlockSpec((HY_FF, HY_CH), lambda o: (0, HY_ORDER + o)),
                  full((L, HY_CH)), full((L, L)), full((L, L)), full((L, L)), full((L, L))],
        out_specs=[pl.BlockSpec((None, L, HY_CH), lambda o: (o, 0, 0)),
                   pl.BlockSpec((None, L, HY_CH), lambda o: (o, 0, 0)),
                   pl.BlockSpec((None, 1, HY_CH), lambda o: (o, 0, 0))],
        out_shape=[jax.ShapeDtypeStruct((HY_ORDER, L, HY_CH), F32),
                   jax.ShapeDtypeStruct((HY_ORDER, L, HY_CH), F32),
                   jax.ShapeDtypeStruct((HY_ORDER, 1, HY_CH), F32)],
        compiler_params=_cparams(("arbitrary",)),
        name="hyena_filters",
    )(zpos, w1p, b1, freq, w2, b2, w3, w3, decay, c_hi, c_lo, s_hi, s_lo)


HY_CG = 256
HY_STEP_ROWS = 1024


def _hyena_kernel(v_ref, x1_ref, x2_ref, wv_ref, w1_ref, w2_ref, bv_ref, b1_ref, b2_ref, skip_ref,
                  kr_ref, ki_ref, kn_ref, c_ref, sp_ref, spt_ref, o_ref, *, L):
    row = lax.broadcasted_iota(jnp.int32, (L, 1), 0)
    first = row == 0
    last = row == L - 1
    chains = [(b, slice(g * HY_CG, (g + 1) * HY_CG)) for b in range(v_ref.shape[0]) for g in range(HY_CH // HY_CG)]

    def short_conv(u_ref, w_ref, b_ref, b, cols):
        u = u_ref[b, :, cols]
        prev = jnp.where(first, 0.0, pltpu.roll(u, 1, axis=0))
        nxt = jnp.where(last, 0.0, pltpu.roll(u, L - 1, axis=0))
        return prev * w_ref[0:1, cols] + u * w_ref[1:2, cols] + nxt * w_ref[2:3, cols] + b_ref[:, cols]

    gate_refs = ((x1_ref, w1_ref, b1_ref), (x2_ref, w2_ref, b2_ref))
    z = [short_conv(v_ref, wv_ref, bv_ref, b, cols) for b, cols in chains]
    for o in range(HY_ORDER):
        zb = [zz.astype(BF16) for zz in z]
        a = [_dot(c_ref[...], x) for x in zb]
        s = [_dot(sp_ref[...], x) for x in zb]
        conv = []
        for n, (b, cols) in enumerate(chains):
            kr = kr_ref[o, :, cols]
            ki = ki_ref[o, :, cols]
            qt = a[n] * kr + s[n] * ki
            qb = s[n] * jnp.where(first, kn_ref[o, :, cols], kr) - a[n] * ki
            conv.append(_dot(c_ref[...], qt.astype(BF16)) + _dot(spt_ref[...], qb.astype(BF16)))
        z = [short_conv(*gate_refs[o], b, cols) * (conv[n] + z[n] * skip_ref[o:o + 1, cols])
             for n, (b, cols) in enumerate(chains)]
    for n, (b, cols) in enumerate(chains):
        o_ref[b, :, cols] = z[n].astype(BF16)


def _hyena_mixer(u3, conv_w, conv_b, skip, filt, dft, L):
    B = u3.shape[0]
    nb = max(1, HY_STEP_ROWS // L)
    kr, ki, kn = filt
    c_b, sp_b, spt_b = dft
    once = pl.Buffered(1)
    ublk = lambda part: pl.BlockSpec((nb, L, HY_CH), lambda b: (b, 0, part))
    wblk = lambda part: pl.BlockSpec((3, HY_CH), lambda b: (0, part))
    bblk = lambda part: pl.BlockSpec((1, HY_CH), lambda b: (0, part))
    fblk = lambda rows: pl.BlockSpec((HY_ORDER, rows, HY_CH), lambda b: (0, 0, 0), pipeline_mode=once)
    mat = pl.BlockSpec((L, L), lambda b: (0, 0), pipeline_mode=once)
    return pl.pallas_call(
        functools.partial(_hyena_kernel, L=L),
        grid=(B // nb,),
        in_specs=[ublk(0), ublk(1), ublk(2), wblk(0), wblk(1), wblk(2), bblk(0), bblk(1), bblk(2),
                  pl.BlockSpec((HY_ORDER, HY_CH), lambda b: (0, 0)),
                  fblk(L), fblk(L), fblk(1), mat, mat, mat],
        out_specs=pl.BlockSpec((nb, L, HY_CH), lambda b: (b, 0, 0)),
        out_shape=jax.ShapeDtypeStruct((B, L, HY_CH), BF16),
        compiler_params=_cparams(("arbitrary",)),
        name="hyena_mixer",
    )(u3, u3, u3, conv_w, conv_w, conv_w, conv_b, conv_b, conv_b, skip, kr, ki, kn, c_b, sp_b, spt_b)


GLA_PAIR = 2 * GLA_CHUNK
GLA_STATES = 2 * GLA_HEADS
GLA_SCAN = 256


def _gla_kernel(*refs, L, has_state, emit_state, out_slot, n_alias):
    it = iter(refs)
    qkvg_ref, a_ref, wa_ref, ba_ref, ng_ref, scan_ref = (next(it) for _ in range(6))
    st0_refs = (next(it), next(it)) if has_state else None
    for _ in range(n_alias):
        next(it)
    y_ref = next(it)
    stout_refs = (next(it), next(it)) if emit_state else None
    qd_s, ki_s, ke_s, qi_s, ku_s, et_s, vb_s, vt_s, o_s, st_s = (next(it) for _ in range(10))

    npair = L // GLA_PAIR
    pre = _dot3(a_ref[...], wa_ref[...]) + ba_ref[...]
    la = (jnp.minimum(pre, 0.0) - jnp.log(1.0 + jnp.exp(-jnp.abs(pre)))) * (1.0 / GLA_TAU)

    scan = scan_ref[...]
    pfx, tot = [], []
    for r in range(0, L, GLA_SCAN):
        x = la[r:r + GLA_SCAN]
        hi = x.astype(BF16)
        r1 = x - hi.astype(F32)
        mid = r1.astype(BF16)
        lo = (r1 - mid.astype(F32)).astype(BF16)
        res = _dot(scan, hi) + (_dot(scan, mid) + _dot(scan, lo))
        pfx.append(res[:GLA_SCAN])
        tot.append(res[GLA_SCAN:])
    pfx = jnp.concatenate(pfx, axis=0) if len(pfx) > 1 else pfx[0]
    tot = jnp.concatenate(tot, axis=0) if len(tot) > 1 else tot[0]

    upper = (lax.broadcasted_iota(jnp.int32, (L, 1), 0) & GLA_CHUNK) != 0
    q = qkvg_ref[:, 0:GLA_QK] * (GLA_DK ** -0.5)
    k = qkvg_ref[:, GLA_QK:2 * GLA_QK]
    for dirn in range(2):
        cols = slice(dirn * GLA_QK, (dirn + 1) * GLA_QK)
        if dirn == 0:
            cum = pfx[:, cols]
            rest = tot[:, cols] - cum
            first = jnp.logical_not(upper)
        else:
            rest = pfx[:, cols] - la[:, cols]
            cum = tot[:, cols] - rest
            first = upper
        et = jnp.exp(tot[:, cols])
        et_other = jnp.where(upper, pltpu.roll(et, GLA_CHUNK, axis=0), pltpu.roll(et, L - GLA_CHUNK, axis=0))
        qd = q * jnp.exp(cum)
        ke = k * jnp.exp(rest)
        et_s[dirn] = et
        qd_s[dirn] = qd.astype(BF16)
        ki_s[dirn] = (k * jnp.exp(-cum)).astype(BF16)
        ke_s[dirn] = ke.astype(BF16)
        qi_s[dirn] = jnp.where(first, qd, qd * et_other).astype(BF16)
        ku_s[dirn] = jnp.where(first, ke * et_other, ke).astype(BF16)
    vb_s[...] = qkvg_ref[:, 2 * GLA_QK:2 * GLA_QK + GLA_VW].astype(BF16)

    def transpose_v(n2, carry):
        r0 = pl.multiple_of(n2 * GLA_PAIR, GLA_PAIR)
        for h in range(GLA_HEADS):
            vp = qkvg_ref[pl.ds(r0, GLA_PAIR), 2 * GLA_QK + h * GLA_DV:2 * GLA_QK + (h + 1) * GLA_DV]
            vt_s[n2, h * GLA_DV:(h + 1) * GLA_DV, :] = vp.T.astype(BF16)
        return carry

    lax.fori_loop(0, npair, transpose_v, 0)

    if has_state:
        zero = jnp.zeros((GLA_DK, GLA_DV), F32)
        for dirn in range(2):
            for h in range(GLA_HEADS):
                s0 = st0_refs[dirn][h]
                both = jnp.concatenate([s0, zero] if h % 2 == 0 else [zero, s0], axis=0)
                idx = dirn * GLA_HEADS + h
                st_s[idx * GLA_DV:(idx + 1) * GLA_DV, :] = both.T
    else:
        st_s[...] = jnp.zeros_like(st_s)

    lane = lax.broadcasted_iota(jnp.int32, (1, 2 * GLA_DK), 1)
    head_lanes = (lane < GLA_DK, lane >= GLA_DK)
    ri = lax.broadcasted_iota(jnp.int32, (GLA_PAIR, GLA_PAIR), 0)
    ci = lax.broadcasted_iota(jnp.int32, (GLA_PAIR, GLA_PAIR), 1)
    same_chunk = ((ri ^ ci) & GLA_CHUNK) == 0
    keep = (ci <= ri, ci >= ri)

    def pair_step(n, carry):
        for dirn in range(2):
            n2 = n if dirn == 0 else npair - 1 - n
            r0 = pl.multiple_of(n2 * GLA_PAIR, GLA_PAIR)
            rows = pl.ds(r0, GLA_PAIR)
            for p in range(GLA_HEADS // 2):
                lanes = slice(p * 2 * GLA_DK, (p + 1) * 2 * GLA_DK)
                qd = qd_s[dirn, rows, lanes]
                qi = qi_s[dirn, rows, lanes]
                zeros = jnp.zeros_like(qd)
                dec = (et_s[dirn, pl.ds(r0, 8), lanes][0:1, :]
                       * et_s[dirn, pl.ds(r0 + GLA_CHUNK, 8), lanes][0:1, :])
                q2 = jnp.concatenate([jnp.where(head_lanes[0], qd, zeros), jnp.where(head_lanes[1], qd, zeros)],
                                     axis=0)
                att_d = _dot_nt(q2, ki_s[dirn, rows, lanes])
                att_o = _dot_nt(q2, ke_s[dirn, rows, lanes])
                s0 = (dirn * GLA_HEADS + 2 * p) * GLA_DV
                st2 = st_s[s0:s0 + 2 * GLA_DV, :]
                vt2 = vt_s[n2, 2 * p * GLA_DV:(2 * p + 2) * GLA_DV, :]
                st_s[s0:s0 + 2 * GLA_DV, :] = st2 * dec + _dot(vt2, ku_s[dirn, rows, lanes])
                for hh in range(2):
                    h = 2 * p + hh
                    hr = slice(hh * GLA_PAIR, (hh + 1) * GLA_PAIR)
                    att = jnp.where(keep[dirn], jnp.where(same_chunk, att_d[hr], att_o[hr]), 0.0)
                    qm = jnp.where(head_lanes[hh], qi, zeros)
                    o = (_dot(att.astype(BF16), vb_s[rows, h * GLA_DV:(h + 1) * GLA_DV])
                         + _dot_nt(qm, st2[hh * GLA_DV:(hh + 1) * GLA_DV].astype(BF16)))
                    o_s[dirn, rows, h * GLA_DV:(h + 1) * GLA_DV] = o
        return carry

    lax.fori_loop(0, npair, pair_step, 0, unroll=2)

    for h in range(GLA_HEADS):
        cols = slice(h * GLA_DV, (h + 1) * GLA_DV)
        o = _rms(o_s[0, :, cols] + o_s[1, :, cols], ng_ref[...])
        g = qkvg_ref[:, 2 * GLA_QK + GLA_VW + h * GLA_DV:2 * GLA_QK + GLA_VW + (h + 1) * GLA_DV]
        y_ref[:, cols] = (o * (g * jax.nn.sigmoid(g))).astype(BF16)
    if emit_state:
        for dirn in range(2):
            dst = stout_refs[dirn]
            if out_slot is not None:
                for d in range(DEPTH):
                    if d != out_slot:
                        dst[d] = jnp.zeros(dst.shape[1:], F32)
                dst = dst.at[out_slot]
            for h in range(GLA_HEADS):
                idx = dirn * GLA_HEADS + h
                half = (h % 2) * GLA_DK
                dst[h] = st_s[idx * GLA_DV:(idx + 1) * GLA_DV, :].T[half:half + GLA_DK, :]


def _gla_scan_matrix():
    i = np.arange(GLA_SCAN)
    same = (i[:, None] // GLA_CHUNK) == (i[None, :] // GLA_CHUNK)
    tri = same & (i[None, :] <= i[:, None])
    return jnp.asarray(np.concatenate([tri, same], axis=0).astype(np.float32)).astype(BF16)


def _gla_mixer(u3, wa_p, ba_p, norm_g, st0, layer, L, emit_state):
    B = u3.shape[0]
    has_state = st0 is not None
    head_state = (GLA_HEADS, GLA_DK, GLA_DV)
    in_specs = [pl.BlockSpec((None, L, 1536), lambda b: (b, 0, U_GLA // 1536)),
                pl.BlockSpec((None, L, 128), lambda b: (b, 0, U_GLA_A // 128)),
                pl.BlockSpec((128, 2 * GLA_QK), lambda b: (0, 0)),
                pl.BlockSpec((1, 2 * GLA_QK), lambda b: (0, 0)),
                pl.BlockSpec((1, GLA_DV), lambda b: (0, 0)),
                pl.BlockSpec((2 * GLA_SCAN, GLA_SCAN), lambda b: (0, 0))]
    args = [u3, u3, wa_p, ba_p, norm_g, _gla_scan_matrix()]
    if has_state:
        in_specs += [pl.BlockSpec((None, None) + head_state, lambda b: (b, layer, 0, 0, 0))] * 2
        args += list(st0)
    out_specs = [pl.BlockSpec((None, L, GLA_VW), lambda b: (b, 0, 0))]
    out_shape = [jax.ShapeDtypeStruct((B, L, GLA_VW), BF16)]
    aliases = {}
    fresh = emit_state is not None and all(prev is None for prev in emit_state)
    if emit_state is not None:
        slot_spec = (pl.BlockSpec((None, DEPTH) + head_state, lambda b: (b, 0, 0, 0, 0)) if fresh else
                     pl.BlockSpec((None, None) + head_state, lambda b: (b, layer, 0, 0, 0)))
        out_specs += [slot_spec] * 2
        out_shape += [jax.ShapeDtypeStruct((B, DEPTH) + head_state, F32)] * 2
        for k, prev in enumerate(emit_state):
            if prev is not None:
                aliases[len(args)] = 1 + k
                in_specs.append(pl.BlockSpec(memory_space=pl.ANY))
                args.append(prev)
    npair = L // GLA_PAIR
    qk_bf16 = pltpu.VMEM((2, L, GLA_QK), BF16)
    res = pl.pallas_call(
        functools.partial(_gla_kernel, L=L, has_state=has_state, emit_state=emit_state is not None,
                          out_slot=layer if fresh else None, n_alias=len(aliases)),
        grid=(B,),
        in_specs=in_specs,
        out_specs=out_specs,
        out_shape=out_shape,
        input_output_aliases=aliases,
        scratch_shapes=[qk_bf16,
                        qk_bf16,
                        qk_bf16,
                        qk_bf16,
                        qk_bf16,
                        pltpu.VMEM((2, L, GLA_QK), F32),
                        pltpu.VMEM((L, GLA_VW), BF16),
                        pltpu.VMEM((npair, GLA_VW, GLA_PAIR), BF16),
                        pltpu.VMEM((2, L, GLA_VW), F32),
                        pltpu.VMEM((GLA_STATES * GLA_DV, 2 * GLA_DK), F32)],
        compiler_params=_cparams(("arbitrary",)),
        name="gla_mixer",
    )(*args)
    return (res[0], res[1], res[2]) if emit_state is not None else (res[0], None, None)


MLA_HW = 256
MLA_TQ = 512
MLA_KCH = 256


def _rope(x, cos, sin):
    lane = lax.broadcasted_iota(jnp.int32, x.shape, 1)
    partner = jnp.where((lane & 16) == 0, pltpu.roll(x, 112, axis=1), pltpu.roll(x, 16, axis=1))
    return x * cos + partner * sin


def _mla_kernel(*refs, L, Lk, tq, rope, emit, out_slot, n_alias):
    it = iter(refs)
    u_ref, qn_ref, wq_ref, kvn_ref, wkv_ref = (next(it) for _ in range(5))
    cos_ref, sin_ref, cckv_ref, ckr_ref = (next(it) for _ in range(4)) if rope else (None,) * 4
    for _ in range(n_alias):
        next(it)
    y_ref = next(it)
    ckv_out, kr_out = (next(it), next(it)) if emit else (None, None)
    kf_s, v_s = next(it), next(it)

    qi = pl.program_id(1)

    def project(keys, rope_part, r0):
        kv = _dot(keys, wkv_ref[...])
        for h in range(MLA_HEADS):
            kf_s[h, r0:r0 + MLA_KCH, 0:MLA_NOPE] = kv[:, h * 256:h * 256 + MLA_NOPE].astype(BF16)
            kf_s[h, r0:r0 + MLA_KCH, MLA_NOPE:MLA_HW] = rope_part
            v_s[h, r0:r0 + MLA_KCH, :] = kv[:, h * 256 + MLA_NOPE:(h + 1) * 256].astype(BF16)

    @pl.when(qi == 0)
    def _():
        if emit and out_slot is not None:
            for dst in (ckv_out, kr_out):
                for d in range(DEPTH):
                    if d != out_slot:
                        dst[d] = jnp.zeros(dst.shape[1:], F32)
        ckv_dst, kr_dst = (ckv_out, kr_out) if out_slot is None else (ckv_out.at[out_slot], kr_out.at[out_slot])
        for r0 in range(0, L, MLA_KCH):
            ckvn = _rms(u_ref[r0:r0 + MLA_KCH, MLA_Q_RANK:MLA_Q_RANK + MLA_KV_RANK], kvn_ref[...])
            kr = u_ref[r0:r0 + MLA_KCH, 640:768]
            kr = jnp.where(lax.broadcasted_iota(jnp.int32, kr.shape, 1) < MLA_ROPE, kr, 0.0)
            if emit:
                ckv_dst[r0:r0 + MLA_KCH, :] = ckvn
                kr_dst[r0:r0 + MLA_KCH, :] = kr[:, 0:MLA_ROPE]
            if rope:
                kr = _rope(kr, cos_ref[r0:r0 + MLA_KCH, :], sin_ref[r0:r0 + MLA_KCH, :])
            project(ckvn.astype(BF16), kr.astype(BF16), r0)
        if rope:
            for r0 in range(0, Lk - L, MLA_KCH):
                ckr = ckr_ref[r0:r0 + MLA_KCH, :]
                ckr = jnp.concatenate([ckr, jnp.zeros_like(ckr)], axis=1)
                project(cckv_ref[r0:r0 + MLA_KCH, :].astype(BF16), ckr.astype(BF16), L + r0)

    q0 = pl.multiple_of(qi * tq, tq)
    cqn = _rms(u_ref[pl.ds(q0, tq), 0:MLA_Q_RANK], qn_ref[...]).astype(BF16)
    q = _dot(cqn, wq_ref[...]) * ((MLA_NOPE + MLA_ROPE) ** -0.5)
    if rope:
        cos = cos_ref[pl.ds(q0, tq), :]
        sin = sin_ref[pl.ds(q0, tq), :]
    for h in range(MLA_HEADS):
        q_nope = q[:, h * MLA_HW:h * MLA_HW + MLA_NOPE]
        q_rope = q[:, h * MLA_HW + MLA_NOPE:(h + 1) * MLA_HW]
        if rope:
            q_rope = _rope(q_rope, cos, sin)
        qh = jnp.concatenate([q_nope, q_rope], axis=1).astype(BF16)
        s = _dot_nt(qh, kf_s[h])
        p = jnp.exp(s - jnp.max(s, axis=-1, keepdims=True))
        denom = jnp.sum(p, axis=-1, keepdims=True)
        o = _dot(p.astype(BF16), v_s[h])
        y_ref[:, h * MLA_V:(h + 1) * MLA_V] = (o / denom).astype(BF16)


def _mla_mixer(u3, q_norm, wq_p, kv_norm, wkv, rope_args, layer, L, emit):
    B = u3.shape[0]
    rope = rope_args is not None
    Lk = L + (rope_args[2].shape[2] if rope else 0)
    tq = min(MLA_TQ, L)
    const = lambda shape: pl.BlockSpec(shape, lambda b, i: (0,) * len(shape))
    in_specs = [pl.BlockSpec((None, L, 768), lambda b, i: (b, 0, U_MLA // 768)),
                const((1, MLA_Q_RANK)), const((MLA_Q_RANK, MLA_HEADS * MLA_HW)),
                const((1, MLA_KV_RANK)), const((MLA_KV_RANK, MLA_HEADS * 256))]
    args = [u3, q_norm, wq_p, kv_norm, wkv]
    if rope:
        cos, sin, cckv, ckr = rope_args
        in_specs += [const((L, 128)), const((L, 128)),
                     pl.BlockSpec((None, None, Lk - L, MLA_KV_RANK), lambda b, i: (b, layer, 0, 0)),
                     pl.BlockSpec((None, None, Lk - L, MLA_ROPE), lambda b, i: (b, layer, 0, 0))]
        args += [cos, sin, cckv, ckr]
    out_specs = [pl.BlockSpec((None, tq, MLA_HEADS * MLA_V), lambda b, i: (b, i, 0))]
    out_shape = [jax.ShapeDtypeStruct((B, L, MLA_HEADS * MLA_V), BF16)]
    aliases = {}
    fresh = emit is not None and all(prev is None for prev in emit)
    if emit is not None:
        slot = ((None, DEPTH), lambda b, i: (b, 0, 0, 0)) if fresh else ((None, None), lambda b, i: (b, layer, 0, 0))
        out_specs += [pl.BlockSpec(slot[0] + (L, MLA_KV_RANK), slot[1]),
                      pl.BlockSpec(slot[0] + (L, MLA_ROPE), slot[1])]
        out_shape += [jax.ShapeDtypeStruct((B, DEPTH, L, MLA_KV_RANK), F32),
                      jax.ShapeDtypeStruct((B, DEPTH, L, MLA_ROPE), F32)]
        for k, prev in enumerate(emit):
            if prev is not None:
                aliases[len(args)] = 1 + k
                in_specs.append(pl.BlockSpec(memory_space=pl.ANY))
                args.append(prev)
    res = pl.pallas_call(
        functools.partial(_mla_kernel, L=L, Lk=Lk, tq=tq, rope=rope, emit=emit is not None,
                          out_slot=layer if fresh else None, n_alias=len(aliases)),
        grid=(B, L // tq),
        in_specs=in_specs,
        out_specs=out_specs,
        out_shape=out_shape,
        input_output_aliases=aliases,
        scratch_shapes=[pltpu.VMEM((MLA_HEADS, Lk, MLA_HW), BF16),
                        pltpu.VMEM((MLA_HEADS, Lk, MLA_V), BF16)],
        compiler_params=_cparams(("arbitrary", "arbitrary")),
        name="mla_mixer",
    )(*args)
    return res if emit is not None else (res[0], None, None)


def _outproj_kernel(x_ref, g_ref, yh_ref, yg_ref, ym_ref, wh_ref, wg_ref, wm_ref, o_ref):
    y = _dot(yh_ref[...], wh_ref[...]) + _dot(yg_ref[...], wg_ref[...]) + _dot(ym_ref[...], wm_ref[...])
    o_ref[...] = x_ref[...] + g_ref[...] * y


def _out_projection(x2d, mod, y_hy, y_gla, y_mla, w_out_b, layer, row_of_tile, tm):
    rows = x2d.shape[0]
    once = pl.Buffered(1)
    return pl.pallas_call(
        _outproj_kernel,
        grid=(rows // tm,),
        in_specs=[pl.BlockSpec((tm, D_MODEL), lambda i: (i, 0)),
                  _mod_spec(2, row_of_tile),
                  pl.BlockSpec((tm, HY_CH), lambda i: (i, 0)),
                  pl.BlockSpec((tm, GLA_VW), lambda i: (i, 0)),
                  pl.BlockSpec((tm, MLA_HEADS * MLA_V), lambda i: (i, 0)),
                  pl.BlockSpec((None, HY_CH, D_MODEL), lambda i: (layer, 0, 0), pipeline_mode=once),
                  pl.BlockSpec((None, GLA_VW, D_MODEL), lambda i: (layer, 1, 0), pipeline_mode=once),
                  pl.BlockSpec((None, MLA_HEADS * MLA_V, D_MODEL), lambda i: (layer, 1, 0), pipeline_mode=once)],
        out_specs=pl.BlockSpec((tm, D_MODEL), lambda i: (i, 0)),
        out_shape=jax.ShapeDtypeStruct((rows, D_MODEL), F32),
        compiler_params=_cparams(("arbitrary",)),
        name="out_projection",
    )(x2d, mod, y_hy, y_gla, y_mla, w_out_b, w_out_b, w_out_b)


FFN_TF = 512


def _ffn_kernel(x_ref, ln_ref, sh_ref, sc_ref, g_ref, wg_ref, wu_ref, wo_ref, lnf_ref, o_ref,
                h_ref, acc_ref, *, final):
    f = pl.program_id(1)

    @pl.when(f == 0)
    def _():
        _adaln_all(x_ref, ln_ref[...] * (1.0 + sc_ref[...]), sh_ref[...], h_ref)
        acc_ref[...] = jnp.zeros_like(acc_ref)

    h = h_ref[...]
    gate = _dot(h, wg_ref[...])
    up = _dot(h, wu_ref[...])
    act = (gate * jax.nn.sigmoid(gate) * up).astype(BF16)
    acc_ref[...] += _dot(act, wo_ref[...])

    @pl.when(f == pl.num_programs(1) - 1)
    def _():
        def body(i, carry):
            rows = pl.ds(pl.multiple_of(i * ADALN_ROWS, ADALN_ROWS), ADALN_ROWS)
            x = x_ref[rows, :] + g_ref[...] * acc_ref[rows, :]
            o_ref[rows, :] = _rms(x, lnf_ref[...]) if final else x
            return carry

        lax.fori_loop(0, x_ref.shape[0] // ADALN_ROWS, body, 0, unroll=4)


def _ffn(x2d, ln, mod, w_in_b, w_out_b, layer, ln_final, row_of_tile, tm, final):
    rows = x2d.shape[0]
    nf = D_FF // FFN_TF
    return pl.pallas_call(
        functools.partial(_ffn_kernel, final=final),
        grid=(rows // tm, nf),
        in_specs=[pl.BlockSpec((tm, D_MODEL), lambda i, f: (i, 0)),
                  pl.BlockSpec((1, D_MODEL), lambda i, f: (0, 0)),
                  _mod_spec(3, row_of_tile),
                  _mod_spec(4, row_of_tile),
                  _mod_spec(5, row_of_tile),
                  pl.BlockSpec((None, D_MODEL, FFN_TF), lambda i, f: (layer, 0, f)),
                  pl.BlockSpec((None, D_MODEL, FFN_TF), lambda i, f: (layer, 0, nf + f)),
                  pl.BlockSpec((None, FFN_TF, D_MODEL), lambda i, f: (layer, f, 0)),
                  pl.BlockSpec((1, D_MODEL), lambda i, f: (0, 0))],
        out_specs=pl.BlockSpec((tm, D_MODEL), lambda i, f: (i, 0)),
        out_shape=jax.ShapeDtypeStruct((rows, D_MODEL), F32),
        scratch_shapes=[pltpu.VMEM((tm, D_MODEL), BF16), pltpu.VMEM((tm, D_MODEL), F32)],
        compiler_params=_cparams(("arbitrary", "arbitrary")),
        name="ffn",
    )(x2d, ln, mod, mod, mod, w_in_b, w_in_b, w_out_b, ln_final)


def _dft_tables(L):
    k = np.arange(L)
    ang = np.pi * ((k[:, None] * k[None, :]) % (2 * L)) / L
    c = np.cos(ang)
    s = np.sin(ang)
    sp = s.copy()
    sp[0, :] = 1.0 - 2.0 * (k % 2)
    as32 = lambda a: jnp.asarray(a.astype(np.float32))
    return as32(c), as32(s), as32(sp), as32(sp.T.copy())


def _hyena_consts(L):
    c, s, sp, spt = _dft_tables(L)
    c_hi, c_lo = _split(c)
    s_hi, s_lo = _split(s)
    f32 = np.float32
    t = np.linspace(0.0, 1.0, L, dtype=f32)[:, None]
    w = (f32(2.0 * math.pi) * np.arange(L, dtype=f32)[:, None] / f32(L)).astype(f32)
    f = np.linspace(1e-4, HY_BANDS - 1, HY_BANDS, dtype=f32)
    zpos = np.concatenate([t, np.cos(f * w), -np.sin(f * w)], axis=-1).astype(f32)
    zpos = np.pad(zpos, ((0, 0), (0, 64 - HY_EMB)))
    min_decay = math.log(HY_TARGET) / HY_SLOW_DECAY
    max_decay = math.log(HY_TARGET) / HY_FAST_DECAY
    delta = np.abs(np.linspace(min_decay, max_decay, HY_CH, dtype=f32))
    decay = np.exp(-t * delta).astype(f32)
    filt_consts = (jnp.asarray(zpos), jnp.asarray(decay), c_hi, c_lo, s_hi, s_lo)
    main_consts = (c_hi, sp.astype(BF16), spt.astype(BF16))
    return filt_consts, main_consts


def _rope_tables(L):
    t = jnp.arange(L)
    half = MLA_ROPE // 2
    inv = ROPE_THETA ** (-jnp.arange(0, half, 2, dtype=F32) / half)
    ang_r = (t // GRID_W).astype(F32)[:, None] * inv
    ang_c = (t % GRID_W).astype(F32)[:, None] * inv
    cr, sr, cc, sc = jnp.cos(ang_r), jnp.sin(ang_r), jnp.cos(ang_c), jnp.sin(ang_c)
    cos = jnp.concatenate([cr, cr, cc, cc, jnp.ones((L, 128 - MLA_ROPE), F32)], axis=1)
    sin = jnp.concatenate([-sr, sr, -sc, sc, jnp.zeros((L, 128 - MLA_ROPE), F32)], axis=1)
    return cos, sin


def _prep_w_in_tail(w):
    mla0 = IN_MAIN + 2 * GLA_LOWRANK
    pad = jnp.zeros((DEPTH, IN_TN - (w.shape[1] - IN_MAIN), D_MODEL), w.dtype)
    return jnp.concatenate([w[:, mla0:], w[:, IN_MAIN:mla0], pad], axis=1).astype(BF16)


def _prep_w_uq(w):
    w = w.reshape(MLA_Q_RANK, MLA_HEADS, MLA_NOPE + MLA_ROPE)
    w = jnp.pad(w, ((0, 0), (0, 0), (0, MLA_HW - MLA_NOPE - MLA_ROPE)))
    return w.reshape(MLA_Q_RANK, MLA_HEADS * MLA_HW).astype(BF16)


def _prep_gla_decay(wa_f, ba_f, wa_b, ba_b):
    wa = jnp.zeros((128, 2 * GLA_QK), F32)
    wa = wa.at[U_GLA_A_LANE:U_GLA_A_LANE + GLA_LOWRANK, 0:GLA_QK].set(wa_f)
    wa = wa.at[U_GLA_A_LANE + GLA_LOWRANK:U_GLA_A_LANE + 2 * GLA_LOWRANK, GLA_QK:].set(wa_b)
    return wa, jnp.concatenate([ba_f, ba_b])[None, :]


def _trunk_layer(x2d, B, L, mod, row_of_tile, lw, consts, ctx, caches, final, ln_final):
    tm = 1024
    tm_out = 512
    u = _in_projection(x2d, lw['ln_mix'], mod, lw['w_in'], lw['w_in_tail'], lw['layer'], row_of_tile(tm), tm)
    u3 = u.reshape(B, L, U_W)
    filt_consts, main_consts = consts['hyena']
    filt = _hyena_filters(L, filt_consts, lw['hy_w1'], lw['hy_b1'], lw['hy_freq'], lw['hy_w2'],
                          lw['hy_b2'], lw['hy_w3'])
    y_hy = _hyena_mixer(u3, lw['hy_conv_w'], lw['hy_conv_b'], lw['hy_skip'], filt, main_consts, L)
    if ctx is None:
        y_gla, s_f, s_b = _gla_mixer(u3, lw['gla_wa'], lw['gla_ba'], lw['gla_norm'], None, lw['layer'], L, caches[2:])
        y_mla, ckv, krope = _mla_mixer(u3, lw['mla_q_norm'], lw['mla_w_uq'], lw['mla_kv_norm'],
                                       lw['mla_w_ukv'], None, lw['layer'], L, caches[:2])
        extras = (ckv, krope, s_f, s_b)
    else:
        cache_ckv, cache_krope, s0f, s0b = ctx
        y_gla, _, _ = _gla_mixer(u3, lw['gla_wa'], lw['gla_ba'], lw['gla_norm'], (s0f, s0b), lw['layer'], L, None)
        cos, sin = consts['rope']
        y_mla, _, _ = _mla_mixer(u3, lw['mla_q_norm'], lw['mla_w_uq'], lw['mla_kv_norm'], lw['mla_w_ukv'],
                                 (cos, sin, cache_ckv, cache_krope), lw['layer'], L, None)
        extras = None
    rows = B * L
    x2d = _out_projection(x2d, mod, y_hy.reshape(rows, -1), y_gla.reshape(rows, -1), y_mla.reshape(rows, -1),
                          lw['w_out'], lw['layer'], row_of_tile(tm_out), tm_out)
    x2d = _ffn(x2d, lw['ln_ffn'], mod, lw['w_ffn_in'], lw['w_ffn_out'], lw['layer'], ln_final, row_of_tile(tm_out),
               tm_out, final)
    return x2d, extras


def kernel(x_prompt, x_sample, cache_mla_ckv, cache_mla_krope, state_gla_fwd, state_gla_bwd, c, c_ctx, w_mod, b_mod, ln_mix, w_in, hy_conv_w, hy_conv_b, hy_filt_w1, hy_filt_b1, hy_filt_freq, hy_filt_w2, hy_filt_b2, hy_filt_w3, hy_skip, gla_wa_f, gla_ba_f, gla_wa_b, gla_ba_b, gla_norm, mla_q_norm, mla_w_uq, mla_kv_norm, mla_w_ukv, w_out, ln_ffn, w_ffn_in, w_ffn_out, ln_final):
    Bc, Lc, _ = x_prompt.shape
    Bl, Ll, _ = x_sample.shape
    assert 1 + Bl <= MOD_ROWS

    c_all = jnp.concatenate([c_ctx[None, :], c, jnp.zeros((MOD_ROWS - 1 - Bl, D_MODEL), F32)], axis=0)
    mod_all = _modulation(c_all, w_mod, b_mod).reshape(DEPTH, MOD_ROWS, 6, 1, D_MODEL)

    consts_ctx = {'hyena': _hyena_consts(Lc)}
    consts_lat = {'hyena': _hyena_consts(Ll), 'rope': _rope_tables(Ll)}
    ctx_rows = lambda tm: (lambda i: 0)
    lat_rows = lambda tm: (lambda i: 1 + (i * tm) // Ll)
    lnf = ln_final[None, :]

    x_ctx = x_prompt.reshape(Bc * Lc, D_MODEL)
    x_lat = x_sample.reshape(Bl * Ll, D_MODEL)
    w_in_t = jnp.swapaxes(w_in, 1, 2)
    w_in_b = w_in_t.astype(BF16)
    w_in_tail = _prep_w_in_tail(w_in_t)
    w_out_b = w_out.astype(BF16)
    w_ffn_in_b = w_ffn_in.astype(BF16)
    w_ffn_out_b = w_ffn_out.astype(BF16)
    caches = (None, None, None, None)
    for l in range(DEPTH):
        wa, ba = _prep_gla_decay(gla_wa_f[l], gla_ba_f[l], gla_wa_b[l], gla_ba_b[l])
        lw = {
            'layer': l, 'ln_mix': ln_mix[l][None, :], 'w_in': w_in_b, 'w_in_tail': w_in_tail,
            'hy_conv_w': hy_conv_w[l], 'hy_conv_b': hy_conv_b[l][None, :],
            'hy_w1': jnp.pad(hy_filt_w1[l], ((0, 64 - HY_EMB), (0, 0))), 'hy_b1': hy_filt_b1[l][None, :],
            'hy_freq': hy_filt_freq[l], 'hy_w2': hy_filt_w2[l], 'hy_b2': hy_filt_b2[l][None, :],
            'hy_w3': hy_filt_w3[l], 'hy_skip': hy_skip[l],
            'gla_wa': wa, 'gla_ba': ba, 'gla_norm': gla_norm[l][None, :],
            'mla_q_norm': mla_q_norm[l][None, :], 'mla_w_uq': _prep_w_uq(mla_w_uq[l]),
            'mla_kv_norm': mla_kv_norm[l][None, :], 'mla_w_ukv': mla_w_ukv[l].astype(BF16),
            'w_out': w_out_b, 'ln_ffn': ln_ffn[l][None, :],
            'w_ffn_in': w_ffn_in_b, 'w_ffn_out': w_ffn_out_b,
        }
        final = l == DEPTH - 1
        mod = mod_all[l]
        x_ctx, caches = _trunk_layer(x_ctx, Bc, Lc, mod, ctx_rows, lw, consts_ctx, None, caches, final, lnf)
        ctx = (cache_mla_ckv, cache_mla_krope, state_gla_fwd, state_gla_bwd)
        x_lat, _ = _trunk_layer(x_lat, Bl, Ll, mod, lat_rows, lw, consts_lat, ctx, None, final, lnf)
    return (x_ctx.reshape(Bc, Lc, D_MODEL), x_lat.reshape(Bl, Ll, D_MODEL)) + tuple(caches)
```
